```python
import jax, jax.numpy as jnp
from jax import lax
import numpy as np

D_MODEL = 2048
BATCH = 2
SEQ = 8192
DEPTH = 1
DEC_BATCH = 128
DEC_SEQ = 1
PAST_LEN = 16384
PAGE_SIZE = 128

HEAD_DIM = 64
N_Q_HEADS = 16
N_KV_HEADS = 4
GROUP = N_Q_HEADS // N_KV_HEADS
ATTN_WIDTH = N_Q_HEADS * HEAD_DIM
KV_WIDTH = N_KV_HEADS * HEAD_DIM
WINDOW = 128
BLOCK = 128
ROPE_THETA = 500000.0
ROT_DIM = HEAD_DIM // 4
N_RWKV_HEADS = 16
RWKV_WIDTH = N_RWKV_HEADS * HEAD_DIM
DECAY_LORA = 64
ICLR_LORA = 64
GATE_LORA = 160
RWKV_PROJ_WIDTH = 3 * RWKV_WIDTH + DECAY_LORA + ICLR_LORA + GATE_LORA
ATTN_PROJ_WIDTH = ATTN_WIDTH + 2 * KV_WIDTH
IN_WIDTH = ATTN_PROJ_WIDTH + RWKV_PROJ_WIDTH + 2 * D_MODEL
D_FF = 4 * D_MODEL
DEEPNORM_ALPHA = (2.0 * DEPTH) ** 0.25
DEEPNORM_BETA = (8.0 * DEPTH) ** -0.25
LN_EPS = 1e-5
GN_EPS = HEAD_DIM * 1e-5

kernel_name = "hybrid_swa_sink_rwkv7_step"


def _split(t, sizes):
    idx = np.cumsum(sizes)[:-1].tolist()
    return jnp.split(t, idx, axis=-1)


def _layer_norm(x, g, b):
    xf = x.astype(jnp.float32)
    mu = jnp.mean(xf, axis=-1, keepdims=True)
    var = jnp.mean(jnp.square(xf - mu), axis=-1, keepdims=True)
    return ((xf - mu) * lax.rsqrt(var + LN_EPS) * g.astype(jnp.float32) + b.astype(jnp.float32)).astype(x.dtype)


def _rope(x, pos):
    inv = ROPE_THETA ** (-jnp.arange(0, ROT_DIM, 2, dtype=jnp.float32) / ROT_DIM)
    ang = pos.astype(jnp.float32)[:, None] * inv[None, :]
    cos = jnp.cos(ang)[:, None, :]
    sin = jnp.sin(ang)[:, None, :]
    xr = x[..., :ROT_DIM].astype(jnp.float32)
    x1, x2 = xr[..., :ROT_DIM // 2], xr[..., ROT_DIM // 2:]
    rot = jnp.concatenate([x1 * cos - x2 * sin, x2 * cos + x1 * sin], axis=-1).astype(x.dtype)
    return jnp.concatenate([rot, x[..., ROT_DIM:]], axis=-1)


def _sink_attention(q, k, v, q_pos, k_pos, sinks):
    qg = q.reshape(*q.shape[:-2], N_KV_HEADS, GROUP, HEAD_DIM)
    s = jnp.einsum('...qkgd,...skd->...kgqs', qg, k).astype(jnp.float32) * (HEAD_DIM ** -0.5)
    diff = q_pos[..., :, None] - k_pos[..., None, :]
    ok = (diff >= 0) & (diff < WINDOW) & (k_pos[..., None, :] >= 0)
    s = jnp.where(ok[..., None, None, :, :], s, -jnp.inf)
    sink = jnp.broadcast_to(sinks.astype(jnp.float32).reshape(N_KV_HEADS, GROUP, 1, 1), s.shape[:-1] + (1,))
    p = jax.nn.softmax(jnp.concatenate([s, sink], axis=-1), axis=-1)[..., :-1]
    o = jnp.einsum('...kgqs,...skd->...qkgd', p.astype(v.dtype), v)
    return o.reshape(*o.shape[:-3], ATTN_WIDTH)


def _attn_prompt(q, k, v, sinks):
    b, t = q.shape[0], q.shape[1]
    nb = t // BLOCK
    pos = jnp.arange(t, dtype=jnp.int32)
    q = _rope(q, pos)
    k = _rope(k, pos)

    def band(z):
        zb = z.reshape(b, nb, BLOCK, N_KV_HEADS, HEAD_DIM)
        zp = jnp.concatenate([jnp.zeros_like(zb[:, :1]), zb], axis=1)
        return jnp.concatenate([zp[:, :-1], zp[:, 1:]], axis=2)

    qb = q.reshape(b, nb, BLOCK, N_Q_HEADS, HEAD_DIM)
    q_pos = pos.reshape(nb, BLOCK)
    k_pos = (jnp.arange(nb, dtype=jnp.int32)[:, None] - 1) * BLOCK + jnp.arange(2 * BLOCK, dtype=jnp.int32)[None, :]
    o = _sink_attention(qb, band(k), band(v), q_pos, k_pos, sinks)
    win = min(WINDOW, t)
    return o.reshape(b, t, ATTN_WIDTH), k[:, t - win:], v[:, t - win:]


def _attn_sample(q, k, v, k_cache, v_cache, sinks):
    b, t = q.shape[0], q.shape[1]
    win = k_cache.shape[1]
    q_pos = PAST_LEN + jnp.arange(t, dtype=jnp.int32)
    q = _rope(q, q_pos)
    k = _rope(k, q_pos)
    kk = jnp.concatenate([k_cache, k], axis=1)
    vv = jnp.concatenate([v_cache, v], axis=1)
    k_pos = jnp.concatenate([PAST_LEN - win + jnp.arange(win, dtype=jnp.int32), q_pos])
    o = _sink_attention(q, kk, vv, q_pos, k_pos, sinks)
    return o.reshape(b, t, ATTN_WIDTH), kk[:, t:], vv[:, t:]


def _wkv7_scan(r, w, k, v, a, bb, s0):
    xs = tuple(jnp.moveaxis(z, 1, 0) for z in (r, w, k, v, a, bb))

    def step(S, inp):
        r_t, w_t, k_t, v_t, a_t, b_t = inp
        sa = jnp.einsum('bhij,bhj->bhi', S, a_t)
        S = S * w_t[:, :, None, :] + sa[..., None] * b_t[:, :, None, :] + v_t[..., None] * k_t[:, :, None, :]
        return S, jnp.einsum('bhij,bhj->bhi', S, r_t)

    S, ys = lax.scan(step, s0.astype(jnp.float32), xs)
    return jnp.moveaxis(ys, 0, 1), S


def _rwkv7_branch(p, p_last, wkv0, w):
    b, t, _ = p.shape
    prev = jnp.concatenate([p_last[:, None, :].astype(p.dtype), p[:, :-1]], axis=1)
    ps = (p + (prev - p) * w['mu_shift']).astype(jnp.float32)
    r, k, v, wd, ad, gd = _split(ps, (RWKV_WIDTH, RWKV_WIDTH, RWKV_WIDTH, DECAY_LORA, ICLR_LORA, GATE_LORA))
    f32 = lambda z: z.astype(jnp.float32)
    w_log = -jax.nn.softplus(-(f32(w['w0']) + jnp.tanh(wd) @ f32(w['w_decay_up']))) - 0.5
    decay = jnp.exp(-jnp.exp(w_log))
    a = jax.nn.sigmoid(f32(w['a0']) + ad @ f32(w['w_iclr_up']))
    g = jax.nn.sigmoid(gd) @ f32(w['w_gate_up'])
    heads = lambda z: z.reshape(b, t, N_RWKV_HEADS, HEAD_DIM)
    kk = heads(k * f32(w['k_k']))
    kk = kk / jnp.maximum(jnp.sqrt(jnp.sum(kk * kk, axis=-1, keepdims=True)), 1e-12)
    k = k * (1.0 + (a - 1.0) * f32(w['k_a']))
    rh, kh, vh, ah = heads(r), heads(k), heads(v), heads(a)
    y, s_new = _wkv7_scan(rh, heads(decay), kh, vh, -kk, kk * ah, wkv0)
    mu = jnp.mean(y, axis=-1, keepdims=True)
    var = jnp.mean(jnp.square(y - mu), axis=-1, keepdims=True)
    yn = ((y - mu) * lax.rsqrt(var + GN_EPS)).reshape(b, t, RWKV_WIDTH) * f32(w['gn_w']) + f32(w['gn_b'])
    bonus = jnp.sum(rh * kh * f32(w['r_k']).reshape(N_RWKV_HEADS, HEAD_DIM), axis=-1, keepdims=True) * vh
    out = (yn + bonus.reshape(b, t, RWKV_WIDTH)) * g
    return out.astype(p.dtype), p[:, -1], s_new.astype(wkv0.dtype)


def _trunk_layer(x, attn_fn, p_last, wkv0, w):
    b, t, _ = x.shape
    p = x @ w['w_in']
    p_attn, p_rwkv, p_gate = _split(p, (ATTN_PROJ_WIDTH, RWKV_PROJ_WIDTH, 2 * D_MODEL))
    q, k, v = _split(p_attn, (ATTN_WIDTH, KV_WIDTH, KV_WIDTH))
    gate_a, gate_b = _split(p_gate, (D_MODEL, D_MODEL))
    attn_o, k_buf, v_buf = attn_fn(q.reshape(b, t, N_Q_HEADS, HEAD_DIM),
                                   k.reshape(b, t, N_KV_HEADS, HEAD_DIM),
                                   v.reshape(b, t, N_KV_HEADS, HEAD_DIM))
    rwkv_o, shift_new, wkv_new = _rwkv7_branch(p_rwkv, p_last, wkv0, w)
    merged = (jax.nn.sigmoid(gate_a) * (attn_o @ w['w_proj_attn'])
              + jax.nn.sigmoid(gate_b) * (rwkv_o @ w['w_proj_rwkv'])) @ w['w_out']
    h = _layer_norm(DEEPNORM_ALPHA * x + merged, w['ln1_g'], w['ln1_b'])
    f = jnp.square(jax.nn.relu(h @ w['w_up'])) @ w['w_down']
    y = _layer_norm(DEEPNORM_ALPHA * h + f, w['ln2_g'], w['ln2_b'])
    return y, k_buf, v_buf, shift_new, wkv_new


def setup_inputs(seed: int = 0) -> dict:
    key = jax.random.key(seed)
    ks = jax.random.split(key, 32)
    nrm = lambda kk, shape, sc: jax.random.normal(kk, shape, jnp.float32) * sc
    L = DEPTH
    win_buf = min(WINDOW, PAST_LEN)
    return {
        'x_prompt': nrm(ks[0], (BATCH, SEQ, D_MODEL), 1.0),
        'x_sample': nrm(ks[1], (DEC_BATCH, DEC_SEQ, D_MODEL), 1.0),
        'cache_k_win': nrm(ks[2], (L, DEC_BATCH, win_buf, N_KV_HEADS, HEAD_DIM), 1.0),
        'cache_v_win': nrm(ks[3], (L, DEC_BATCH, win_buf, N_KV_HEADS, HEAD_DIM), 1.0),
        'state_shift': nrm(ks[4], (L, DEC_BATCH, RWKV_PROJ_WIDTH), 1.0),
        'state_wkv': nrm(ks[5], (L, DEC_BATCH, N_RWKV_HEADS, HEAD_DIM, HEAD_DIM), 1.0),
        'w_in': nrm(ks[6], (L, D_MODEL, IN_WIDTH), D_MODEL ** -0.5),
        'attn_sinks': nrm(ks[7], (L, N_Q_HEADS), 0.5),
        'mu_shift': jax.random.uniform(ks[8], (L, RWKV_PROJ_WIDTH), jnp.float32, 0.0, 1.0),
        'w0': jax.random.uniform(ks[9], (L, RWKV_WIDTH), jnp.float32, -2.0, 1.0),
        'w_decay_up': nrm(ks[10], (L, DECAY_LORA, RWKV_WIDTH), 0.1 * DECAY_LORA ** -0.5),
        'a0': nrm(ks[11], (L, RWKV_WIDTH), 0.1),
        'w_iclr_up': nrm(ks[12], (L, ICLR_LORA, RWKV_WIDTH), 0.1 * ICLR_LORA ** -0.5),
        'w_gate_up': nrm(ks[13], (L, GATE_LORA, RWKV_WIDTH), GATE_LORA ** -0.5),
        'k_k': 0.85 + nrm(ks[14], (L, RWKV_WIDTH), 0.05),
        'k_a': 1.0 + nrm(ks[15], (L, RWKV_WIDTH), 0.05),
        'r_k': nrm(ks[16], (L, RWKV_WIDTH), 0.1),
        'gn_w': 1.0 + nrm(ks[17], (L, RWKV_WIDTH), 0.05),
        'gn_b': nrm(ks[18], (L, RWKV_WIDTH), 0.01),
        'w_proj_attn': nrm(ks[19], (L, ATTN_WIDTH, D_MODEL), DEEPNORM_BETA * ATTN_WIDTH ** -0.5),
        'w_proj_rwkv': nrm(ks[20], (L, RWKV_WIDTH, D_MODEL), DEEPNORM_BETA * RWKV_WIDTH ** -0.5),
        'w_out': nrm(ks[21], (L, D_MODEL, D_MODEL), DEEPNORM_BETA * D_MODEL ** -0.5),
        'ln1_g': 1.0 + nrm(ks[22], (L, D_MODEL), 0.05),
        'ln1_b': nrm(ks[23], (L, D_MODEL), 0.01),
        'w_up': nrm(ks[24], (L, D_MODEL, D_FF), D_MODEL ** -0.5),
        'w_down': nrm(ks[25], (L, D_FF, D_MODEL), DEEPNORM_BETA * D_FF ** -0.5),
        'ln2_g': 1.0 + nrm(ks[26], (L, D_MODEL), 0.05),
        'ln2_b': nrm(ks[27], (L, D_MODEL), 0.01),
    }


def reference(x_prompt, x_sample, cache_k_win, cache_v_win, state_shift, state_wkv,
              w_in, attn_sinks, mu_shift, w0, w_decay_up, a0, w_iclr_up, w_gate_up,
              k_k, k_a, r_k, gn_w, gn_b, w_proj_attn, w_proj_rwkv, w_out,
              ln1_g, ln1_b, w_up, w_down, ln2_g, ln2_b):
    params = {
        'w_in': w_in, 'attn_sinks': attn_sinks, 'mu_shift': mu_shift, 'w0': w0,
        'w_decay_up': w_decay_up, 'a0': a0, 'w_iclr_up': w_iclr_up, 'w_gate_up': w_gate_up,
        'k_k': k_k, 'k_a': k_a, 'r_k': r_k, 'gn_w': gn_w, 'gn_b': gn_b,
        'w_proj_attn': w_proj_attn, 'w_proj_rwkv': w_proj_rwkv, 'w_out': w_out,
        'ln1_g': ln1_g, 'ln1_b': ln1_b, 'w_up': w_up, 'w_down': w_down,
        'ln2_g': ln2_g, 'ln2_b': ln2_b,
    }
    b = x_prompt.shape[0]
    hp, hs = x_prompt, x_sample
    kp_l, vp_l, sp_l, wp_l, ks_l, vs_l, ss_l, ws_l = [], [], [], [], [], [], [], []
    for l in range(DEPTH):
        w = {name: arr[l] for name, arr in params.items()}
        sinks = w['attn_sinks']
        hp, kp, vp, sp, wp = _trunk_layer(
            hp, lambda q, k, v: _attn_prompt(q, k, v, sinks),
            jnp.zeros((b, RWKV_PROJ_WIDTH), x_prompt.dtype),
            jnp.zeros((b, N_RWKV_HEADS, HEAD_DIM, HEAD_DIM), state_wkv.dtype), w)
        kc, vc = cache_k_win[l], cache_v_win[l]
        hs, kn, vn, sn, wn = _trunk_layer(
            hs, lambda q, k, v: _attn_sample(q, k, v, kc, vc, sinks),
            state_shift[l], state_wkv[l], w)
        kp_l.append(kp); vp_l.append(vp); sp_l.append(sp); wp_l.append(wp)
        ks_l.append(kn); vs_l.append(vn); ss_l.append(sn); ws_l.append(wn)
    return (hp, hs,
            jnp.stack(kp_l), jnp.stack(vp_l), jnp.stack(sp_l), jnp.stack(wp_l),
            jnp.stack(ks_l), jnp.stack(vs_l), jnp.stack(ss_l), jnp.stack(ws_l))
```

```python
import functools

import jax
import jax.numpy as jnp
import numpy as np
from jax import lax
from jax.experimental import pallas as pl
from jax.experimental.pallas import tpu as pltpu

F32 = jnp.float32
BF16 = jnp.bfloat16

D_MODEL = 2048
HEAD_DIM = 64
N_Q_HEADS = 16
N_KV_HEADS = 4
ATTN_WIDTH = N_Q_HEADS * HEAD_DIM
KV_WIDTH = N_KV_HEADS * HEAD_DIM
WINDOW = 128
ROPE_THETA = 500000.0
ROT_DIM = HEAD_DIM // 4
N_RWKV_HEADS = 16
RWKV_WIDTH = N_RWKV_HEADS * HEAD_DIM
DECAY_LORA = 64
ICLR_LORA = 64
GATE_LORA = 160
LORA_WIDTH = DECAY_LORA + ICLR_LORA + GATE_LORA
RWKV_PROJ_WIDTH = 3 * RWKV_WIDTH + LORA_WIDTH
D_FF = 4 * D_MODEL
PAST_LEN = 16384
DEEPNORM_ALPHA = 2.0 ** 0.25
LN_EPS = 1e-5
GN_EPS = HEAD_DIM * 1e-5
NEG_BIG = -1e30

LANES = 128
CHUNK = 64
VMEM_LIMIT = 56 * 1024 * 1024

COL_Q = 0
COL_R = 1024
COL_KR = 2048
COL_VR = 3072
COL_GA = 4096
COL_GB = 6144
COL_K = 8192
COL_V = 8448
COL_LORA = 8704
LORA_PAD = 512
PACK_WIDTH = COL_LORA + LORA_PAD


def _cparams(n_axes):
    return pltpu.CompilerParams(dimension_semantics=("arbitrary",) * n_axes, vmem_limit_bytes=VMEM_LIMIT)


def _sigmoid(x):
    return 1.0 / (1.0 + jnp.exp(-x))


def _softplus(x):
    return jnp.maximum(x, 0.0) + jnp.log(1.0 + jnp.exp(-jnp.abs(x)))


def _layer_norm_rows(z, g, b):
    mu = jnp.mean(z, axis=-1, keepdims=True)
    d = z - mu
    var = jnp.mean(d * d, axis=-1, keepdims=True)
    return d * lax.rsqrt(var + LN_EPS) * g + b


def _split_bf16(x):
    hi = x.astype(BF16)
    lo = (x - hi.astype(F32)).astype(BF16)
    return hi, lo


def _head_block_ones():
    r = lax.broadcasted_iota(jnp.int32, (2 * LANES, LANES), 0)
    c = lax.broadcasted_iota(jnp.int32, (2 * LANES, LANES), 1)
    return jnp.where((r % LANES) // HEAD_DIM == c // HEAD_DIM, 1.0, 0.0).astype(BF16)


def _head_sum(x, bd2):
    outs = []
    for t in range(x.shape[1] // LANES):
        hi, lo = _split_bf16(x[:, LANES * t:LANES * (t + 1)])
        outs.append(jnp.dot(jnp.concatenate([hi, lo], axis=1), bd2, preferred_element_type=F32))
    return outs[0] if len(outs) == 1 else jnp.concatenate(outs, axis=1)


def _inproj_kernel(x_ref, w_ref, o_ref, xb_ref):
    @pl.when(pl.program_id(1) == 0)
    def _():
        xb_ref[...] = x_ref[...].astype(BF16)

    o_ref[...] = jnp.dot(xb_ref[...], w_ref[...], preferred_element_type=F32)


def _inproj(x, w_packed, tm, tn):
    m, k = x.shape
    n = w_packed.shape[1]
    return pl.pallas_call(
        _inproj_kernel,
        grid=(m // tm, n // tn),
        in_specs=[pl.BlockSpec((tm, k), lambda i, j: (i, 0)),
                  pl.BlockSpec((k, tn), lambda i, j: (0, j))],
        out_specs=pl.BlockSpec((tm, tn), lambda i, j: (i, j)),
        out_shape=jax.ShapeDtypeStruct((m, n), F32),
        scratch_shapes=[pltpu.VMEM((tm, k), BF16)],
        compiler_params=_cparams(2),
        name="inproj",
    )(x, w_packed)


def _rope_tables(pos):
    inv = ROPE_THETA ** (-jnp.arange(0, ROT_DIM, 2, dtype=F32) / ROT_DIM)
    ang = pos.astype(F32)[:, None] * inv[None, :]
    cos, sin = jnp.cos(ang), jnp.sin(ang)
    t = pos.shape[0]
    half = ROT_DIM // 2
    pad = HEAD_DIM - ROT_DIM
    c_head = jnp.concatenate([cos, cos, jnp.ones((t, pad), F32)], axis=1)
    sp_head = jnp.concatenate([-sin, jnp.zeros((t, half + pad), F32)], axis=1)
    sm_head = jnp.concatenate([jnp.zeros((t, half), F32), sin, jnp.zeros((t, pad), F32)], axis=1)
    rep = LANES // HEAD_DIM
    return tuple(jnp.tile(z, (1, rep)) for z in (c_head, sp_head, sm_head))


def _rope_tile(x, c, sp, sm):
    half = ROT_DIM // 2
    return x * c + pltpu.roll(x, LANES - half, 1) * sp + pltpu.roll(x, half, 1) * sm


def _attn_prompt_kernel(q_ref, k_ref, v_ref, c_ref, sp_ref, sm_ref, sink_ref,
                        o_ref, kwin_ref, vwin_ref, kp_ref, vp_ref):
    i = pl.program_id(1)
    blk = WINDOW

    @pl.when(i == 0)
    def _():
        kp_ref[...] = jnp.zeros_like(kp_ref)
        vp_ref[...] = jnp.zeros_like(vp_ref)

    c, sp, sm = c_ref[...], sp_ref[...], sm_ref[...]
    lane = lax.broadcasted_iota(jnp.int32, (2 * blk, LANES), 1)
    lo2 = lane < HEAD_DIM
    lo1 = lax.broadcasted_iota(jnp.int32, (blk, LANES), 1) < HEAD_DIM

    qi = lax.broadcasted_iota(jnp.int32, (blk, 2 * blk), 0)
    kj = lax.broadcasted_iota(jnp.int32, (blk, 2 * blk), 1)
    kmin = jnp.where(i == 0, blk, 0)
    valid = (kj > qi) & (kj <= qi + blk) & (kj >= kmin)

    n_kv_tiles = KV_WIDTH // LANES
    kk_g, va_g, vb_g = [], [], []
    for t in range(n_kv_tiles):
        sl = slice(LANES * t, LANES * (t + 1))
        kcur = _rope_tile(k_ref[:, sl], c, sp, sm)
        vcur = v_ref[:, sl]
        kwin_ref[0, :, sl] = kcur
        vwin_ref[0, :, sl] = vcur
        kall = jnp.concatenate([kp_ref[:, sl], kcur], axis=0)
        vall = jnp.concatenate([vp_ref[:, sl], vcur], axis=0)
        kp_ref[:, sl] = kcur
        vp_ref[:, sl] = vcur
        kswap = pltpu.roll(kall, HEAD_DIM, 1)
        vswap = pltpu.roll(vall, HEAD_DIM, 1)
        kk_g += [jnp.where(lo2, kall, kswap).astype(BF16), jnp.where(lo2, kswap, kall).astype(BF16)]
        va_g += [jnp.where(lo2, vall, 0.0).astype(BF16), jnp.where(lo2, vswap, 0.0).astype(BF16)]
        vb_g += [jnp.where(lo2, 0.0, vswap).astype(BF16), jnp.where(lo2, 0.0, vall).astype(BF16)]

    group = N_Q_HEADS // N_KV_HEADS
    for j in range(ATTN_WIDTH // LANES):
        sl = slice(LANES * j, LANES * (j + 1))
        g = (2 * j) // group
        qt = _rope_tile(q_ref[:, sl], c, sp, sm) * (HEAD_DIM ** -0.5)
        o_tile = None
        for hh in range(2):
            sink = sink_ref[2 * j + hh]
            qm = (jnp.where(lo1, qt, 0.0) if hh == 0 else jnp.where(lo1, 0.0, qt)).astype(BF16)
            s = lax.dot_general(qm, kk_g[g], (((1,), (1,)), ((), ())), preferred_element_type=F32)
            s = jnp.where(valid, s, NEG_BIG)
            m = jnp.maximum(jnp.max(s, axis=1, keepdims=True), sink)
            p = jnp.exp(s - m)
            den = jnp.sum(p, axis=1, keepdims=True) + jnp.exp(sink - m)
            vmat = va_g[g] if hh == 0 else vb_g[g]
            o_h = jnp.dot(p.astype(BF16), vmat, preferred_element_type=F32) / den
            o_tile = o_h if o_tile is None else o_tile + o_h
        o_ref[:, sl] = o_tile.astype(o_ref.dtype)


def _attn_prompt(p_all, sinks, b, t):
    blk = WINDOW
    nb = t // blk
    c, sp, sm = _rope_tables(jnp.arange(t, dtype=jnp.int32))
    tab_spec = pl.BlockSpec((blk, LANES), lambda bi, i: (i, 0))
    row = lambda bi, i: bi * nb + i
    return pl.pallas_call(
        _attn_prompt_kernel,
        grid=(b, nb),
        in_specs=[pl.BlockSpec((blk, ATTN_WIDTH), lambda bi, i: (row(bi, i), COL_Q // ATTN_WIDTH)),
                  pl.BlockSpec((blk, KV_WIDTH), lambda bi, i: (row(bi, i), COL_K // KV_WIDTH)),
                  pl.BlockSpec((blk, KV_WIDTH), lambda bi, i: (row(bi, i), COL_V // KV_WIDTH)),
                  tab_spec, tab_spec, tab_spec,
                  pl.BlockSpec(memory_space=pltpu.SMEM)],
        out_specs=[pl.BlockSpec((blk, ATTN_WIDTH), lambda bi, i: (row(bi, i), 0)),
                   pl.BlockSpec((1, blk, KV_WIDTH), lambda bi, i: (bi, 0, 0)),
                   pl.BlockSpec((1, blk, KV_WIDTH), lambda bi, i: (bi, 0, 0))],
        out_shape=[jax.ShapeDtypeStruct((b * t, ATTN_WIDTH), BF16),
                   jax.ShapeDtypeStruct((b, blk, KV_WIDTH), F32),
                   jax.ShapeDtypeStruct((b, blk, KV_WIDTH), F32)],
        scratch_shapes=[pltpu.VMEM((blk, KV_WIDTH), F32), pltpu.VMEM((blk, KV_WIDTH), F32)],
        compiler_params=_cparams(2),
        name="attn_prompt",
    )(p_all, p_all, p_all, c, sp, sm, sinks)


def _rwkv_prepare(r_in, k_in, v_in, l_in, pr, pk, pv, plr, prm, wd_ref, wi_ref, wg_ref, bd2):
    mur, muk, muv, mul, w0, a0, k_k, k_a = prm
    r = r_in + (pr - r_in) * mur
    k = k_in + (pk - k_in) * muk
    v = v_in + (pv - v_in) * muv
    ls = l_in + (plr - l_in) * mul
    l0 = ls[:, :LANES]
    lane = lax.broadcasted_iota(jnp.int32, l0.shape, 1)
    z0 = jnp.where(lane < DECAY_LORA, jnp.tanh(l0), l0).astype(BF16)
    zg = _sigmoid(ls[:, LANES:3 * LANES]).astype(BF16)
    dec_up = jnp.dot(z0, wd_ref[...], preferred_element_type=F32)
    icl_up = jnp.dot(z0, wi_ref[...], preferred_element_type=F32)
    gate = jnp.dot(zg, wg_ref[...], preferred_element_type=F32)
    w_log = -_softplus(-(w0 + dec_up)) - 0.5
    logw = -jnp.exp(w_log)
    a_sig = _sigmoid(a0 + icl_up)
    kk = k * k_k
    nrm = jnp.maximum(jnp.sqrt(_head_sum(kk * kk, bd2)), 1e-12)
    kk = kk / nrm
    k = k * (1.0 + (a_sig - 1.0) * k_a)
    return r, k, v, logw, -kk, kk * a_sig, gate


def _rwkv_finish(y, r, k, v, gate, r_k, gn_w, gn_b, bd2):
    mu = _head_sum(y, bd2) * (1.0 / HEAD_DIM)
    d = y - mu
    var = _head_sum(d * d, bd2) * (1.0 / HEAD_DIM)
    yn = d * lax.rsqrt(var + GN_EPS) * gn_w + gn_b
    bonus = _head_sum(r * k * r_k, bd2) * v
    return (yn + bonus) * gate


def _rwkv_prompt_kernel(r_ref, k_ref, v_ref, l_ref,
                        mur_ref, muk_ref, muv_ref, mul_ref, w0_ref, a0_ref, kk_ref, ka_ref,
                        rk_ref, gnw_ref, gnb_ref, wd_ref, wi_ref, wg_ref,
                        o_ref, sout_ref,
                        cr_ref, ck_ref, cv_ref, cl_ref, st_ref, y_ref):
    ti = pl.program_id(2)
    tb, lw = r_ref.shape
    n_pair = lw // LANES
    n_chunk = tb // CHUNK

    @pl.when(ti == 0)
    def _():
        for ref in (cr_ref, ck_ref, cv_ref, cl_ref, st_ref):
            ref[...] = jnp.zeros_like(ref)

    def shifted(x, carry_ref):
        rolled = pltpu.roll(x, 1, 0)
        row = lax.broadcasted_iota(jnp.int32, x.shape, 0)
        prev = jnp.where(row == 0, carry_ref[0:1, :], rolled)
        carry_ref[0:1, :] = x[tb - 1:tb, :]
        return prev

    bd2 = _head_block_ones()
    r_in, k_in, v_in, l_in = r_ref[...], k_ref[...], v_ref[...], l_ref[...]
    prm = tuple(ref[...] for ref in (mur_ref, muk_ref, muv_ref, mul_ref, w0_ref, a0_ref, kk_ref, ka_ref))
    r, k, v, logw, a_s, b_s, gate = _rwkv_prepare(
        r_in, k_in, v_in, l_in,
        shifted(r_in, cr_ref), shifted(k_in, ck_ref), shifted(v_in, cv_ref), shifted(l_in, cl_ref),
        prm, wd_ref, wi_ref, wg_ref, bd2)

    c = CHUNK
    ri = lax.broadcasted_iota(jnp.int32, (c, 3 * c), 0)
    ci = lax.broadcasted_iota(jnp.int32, (c, 3 * c), 1) % c
    tri3 = jnp.where(ri >= ci, 1.0, 0.0).astype(BF16)
    r2 = lax.broadcasted_iota(jnp.int32, (2 * c, LANES), 0)
    l2 = lax.broadcasted_iota(jnp.int32, (2 * c, LANES), 1)
    tt, ss = r2 % c, l2 % c
    causal = ss < tt + r2 // c
    lo1 = lax.broadcasted_iota(jnp.int32, (c, LANES), 1) < HEAD_DIM
    lo2 = l2 < HEAD_DIM
    diag_blocks = (r2 // HEAD_DIM) == (l2 // HEAD_DIM)
    eye = jnp.where(r2 == l2, 1.0, 0.0)

    def stack_heads(x):
        return jnp.concatenate([jnp.where(lo1, x, 0.0), jnp.where(lo1, 0.0, x)], axis=0)

    def dot_t(a, b):
        return lax.dot_general(a, b, (((1,), (1,)), ((), ())), preferred_element_type=F32)

    for ch in range(n_chunk):
        rows = slice(c * ch, c * (ch + 1))
        lw_c = logw[rows]
        hi = lw_c.astype(BF16)
        rem = lw_c - hi.astype(F32)
        mid = rem.astype(BF16)
        low = (rem - mid.astype(F32)).astype(BF16)
        lcum = jnp.dot(tri3, jnp.concatenate([hi, mid, low], axis=0), preferred_element_type=F32)
        ltot = lcum[c - 1:c, :]
        p_inc = jnp.exp(lcum)
        p_inv = jnp.exp(-lcum)
        p_tail = jnp.exp(ltot - lcum)
        aq = a_s[rows] * jnp.exp(lcum - lw_c)
        rq = r[rows] * p_inc
        bk = b_s[rows] * p_inv
        kq = k[rows] * p_inv
        bt = b_s[rows] * p_tail
        kt = k[rows] * p_tail
        p_end = jnp.exp(ltot)
        v_c = v[rows]
        for pi in range(n_pair):
            sl = slice(LANES * pi, LANES * (pi + 1))
            ar = jnp.concatenate([aq[:, sl], rq[:, sl]], axis=0).astype(BF16)
            gb = jnp.where(causal, dot_t(ar, stack_heads(bk[:, sl]).astype(BF16)), 0.0)
            gk = jnp.where(causal, dot_t(ar, stack_heads(kq[:, sl]).astype(BF16)), 0.0)
            nn = gb[:c]
            pw = jnp.concatenate([jnp.where(lo1, nn, 0.0), jnp.where(lo1, 0.0, nn)], axis=0)
            tm = eye + pw
            for _ in range(5):
                pwb = pw.astype(BF16)
                pw = jnp.dot(pwb, pwb, preferred_element_type=F32)
                tm = tm + jnp.dot(tm.astype(BF16), pw.astype(BF16), preferred_element_type=F32)
            t_side = (tm[:c] + tm[c:]).astype(BF16)
            v_st = stack_heads(v_c[:, sl]).astype(BF16)
            gv = jnp.dot(gk.astype(BF16), v_st, preferred_element_type=F32)
            s_old = st_ref[pi]
            gs = dot_t(ar, s_old.astype(BF16))
            x = gs[:c] + gv[:c]
            u = jnp.dot(t_side, stack_heads(x).astype(BF16), preferred_element_type=F32)
            y = gs[c:] + gv[c:] + jnp.dot(gb[c:].astype(BF16), stack_heads(u).astype(BF16),
                                          preferred_element_type=F32)
            y_ref[rows, sl] = y
            uv = jnp.concatenate([u, v_c[:, sl]], axis=0).astype(BF16)
            bkt = jnp.concatenate([bt[:, sl], kt[:, sl]], axis=0).astype(BF16)
            upd = lax.dot_general(uv, bkt, (((0,), (0,)), ((), ())), preferred_element_type=F32)
            st_ref[pi] = s_old * p_end[:, sl] + jnp.where(diag_blocks, upd, 0.0)

    out = _rwkv_finish(y_ref[...], r, k, v, gate, rk_ref[...], gnw_ref[...], gnb_ref[...], bd2)
    o_ref[...] = out.astype(o_ref.dtype)

    @pl.when(ti == pl.num_programs(2) - 1)
    def _():
        for pi in range(n_pair):
            s = st_ref[pi]
            sout_ref[0, 2 * pi] = s[:HEAD_DIM, :HEAD_DIM]
            sout_ref[0, 2 * pi + 1] = s[HEAD_DIM:, HEAD_DIM:]


def _rwkv_lora_weights(w_decay_up, w_iclr_up, w_gate_up):
    z64 = jnp.zeros((DECAY_LORA, RWKV_WIDTH), F32)
    wd = jnp.concatenate([w_decay_up, z64], axis=0).astype(BF16)
    wi = jnp.concatenate([z64, w_iclr_up], axis=0).astype(BF16)
    wg = jnp.concatenate([w_gate_up, jnp.zeros((2 * LANES - GATE_LORA, RWKV_WIDTH), F32)], axis=0).astype(BF16)
    return wd, wi, wg


def _rwkv_prompt(p_all, prm, b, t, tb, lw):
    nt = t // tb
    ns = RWKV_WIDTH // lw
    row = lambda bi, si, ti: bi * nt + ti
    col_spec = lambda col0: pl.BlockSpec((tb, lw), lambda bi, si, ti: (row(bi, si, ti), col0 // lw + si))
    vec = pl.BlockSpec((1, lw), lambda bi, si, ti: (0, si))
    vec_l = pl.BlockSpec((1, LORA_PAD), lambda bi, si, ti: (0, 0))
    return pl.pallas_call(
        _rwkv_prompt_kernel,
        grid=(b, ns, nt),
        in_specs=[col_spec(COL_R), col_spec(COL_KR), col_spec(COL_VR),
                  pl.BlockSpec((tb, LORA_PAD), lambda bi, si, ti: (row(bi, si, ti), COL_LORA // LORA_PAD)),
                  vec, vec, vec, vec_l, vec, vec, vec, vec, vec, vec, vec,
                  pl.BlockSpec((LANES, lw), lambda bi, si, ti: (0, si)),
                  pl.BlockSpec((LANES, lw), lambda bi, si, ti: (0, si)),
                  pl.BlockSpec((2 * LANES, lw), lambda bi, si, ti: (0, si))],
        out_specs=[pl.BlockSpec((tb, lw), lambda bi, si, ti: (row(bi, si, ti), si)),
                   pl.BlockSpec((1, 2 * (lw // LANES), HEAD_DIM, HEAD_DIM), lambda bi, si, ti: (bi, si, 0, 0))],
        out_shape=[jax.ShapeDtypeStruct((b * t, RWKV_WIDTH), BF16),
                   jax.ShapeDtypeStruct((b, N_RWKV_HEADS, HEAD_DIM, HEAD_DIM), F32)],
        scratch_shapes=[pltpu.VMEM((8, lw), F32), pltpu.VMEM((8, lw), F32), pltpu.VMEM((8, lw), F32),
                        pltpu.VMEM((8, LORA_PAD), F32),
                        pltpu.VMEM((lw // LANES, LANES, LANES), F32),
                        pltpu.VMEM((tb, lw), F32)],
        compiler_params=_cparams(3),
        name="rwkv_prompt",
    )(p_all, p_all, p_all, p_all,
      prm["mu_r"], prm["mu_k"], prm["mu_v"], prm["mu_l"], prm["w0"], prm["a0"], prm["k_k"], prm["k_a"],
      prm["r_k"], prm["gn_w"], prm["gn_b"], prm["wd"], prm["wi"], prm["wg"])


def _merge_kernel(x_ref, ao_ref, ro_ref, ga_ref, gb_ref, wpa_ref, wpr_ref, wo_ref, g_ref, b_ref, h_ref):
    a = jnp.dot(ao_ref[...], wpa_ref[...], preferred_element_type=F32)
    r = jnp.dot(ro_ref[...], wpr_ref[...], preferred_element_type=F32)
    m = _sigmoid(ga_ref[...]) * a + _sigmoid(gb_ref[...]) * r
    merged = jnp.dot(m.astype(BF16), wo_ref[...], preferred_element_type=F32)
    z = DEEPNORM_ALPHA * x_ref[...] + merged
    h_ref[...] = _layer_norm_rows(z, g_ref[...], b_ref[...])


def _merge(x, attn_o, rwkv_o, p_all, wpa, wpr, wo, ln_g, ln_b, tm):
    m = x.shape[0]
    full = lambda shape: pl.BlockSpec(shape, lambda i: (0, 0), pipeline_mode=pl.Buffered(1))
    return pl.pallas_call(
        _merge_kernel,
        grid=(m // tm,),
        in_specs=[pl.BlockSpec((tm, D_MODEL), lambda i: (i, 0)),
                  pl.BlockSpec((tm, ATTN_WIDTH), lambda i: (i, 0)),
                  pl.BlockSpec((tm, RWKV_WIDTH), lambda i: (i, 0)),
                  pl.BlockSpec((tm, D_MODEL), lambda i: (i, COL_GA // D_MODEL)),
                  pl.BlockSpec((tm, D_MODEL), lambda i: (i, COL_GB // D_MODEL)),
                  full((ATTN_WIDTH, D_MODEL)), full((RWKV_WIDTH, D_MODEL)), full((D_MODEL, D_MODEL)),
                  full((1, D_MODEL)), full((1, D_MODEL))],
        out_specs=pl.BlockSpec((tm, D_MODEL), lambda i: (i, 0)),
        out_shape=jax.ShapeDtypeStruct((m, D_MODEL), F32),
        compiler_params=_cparams(1),
        name="merge_ln1",
    )(x, attn_o, rwkv_o, p_all, p_all, wpa, wpr, wo, ln_g, ln_b)


def _ffn_kernel(h_ref, wu_ref, wd_ref, g_ref, b_ref, y_ref, hb_ref, acc_ref):
    f = pl.program_id(1)

    @pl.when(f == 0)
    def _():
        hb_ref[...] = h_ref[...].astype(BF16)
        acc_ref[...] = jnp.zeros_like(acc_ref)

    u = jnp.dot(hb_ref[...], wu_ref[...], preferred_element_type=F32)
    u = jnp.square(jnp.maximum(u, 0.0))
    acc_ref[...] += jnp.dot(u.astype(BF16), wd_ref[...], preferred_element_type=F32)

    @pl.when(f == pl.num_programs(1) - 1)
    def _():
        z = DEEPNORM_ALPHA * h_ref[...] + acc_ref[...]
        y_ref[...] = _layer_norm_rows(z, g_ref[...], b_ref[...])


def _ffn(h, wu, wd, ln_g, ln_b, tm, tf):
    m = h.shape[0]
    return pl.pallas_call(
        _ffn_kernel,
        grid=(m // tm, D_FF // tf),
        in_specs=[pl.BlockSpec((tm, D_MODEL), lambda i, f: (i, 0)),
                  pl.BlockSpec((D_MODEL, tf), lambda i, f: (0, f)),
                  pl.BlockSpec((tf, D_MODEL), lambda i, f: (f, 0)),
                  pl.BlockSpec((1, D_MODEL), lambda i, f: (0, 0)),
                  pl.BlockSpec((1, D_MODEL), lambda i, f: (0, 0))],
        out_specs=pl.BlockSpec((tm, D_MODEL), lambda i, f: (i, 0)),
        out_shape=jax.ShapeDtypeStruct((m, D_MODEL), F32),
        scratch_shapes=[pltpu.VMEM((tm, D_MODEL), BF16), pltpu.VMEM((tm, D_MODEL), F32)],
        compiler_params=_cparams(2),
        name="ffn_ln2",
    )(h, wu, wd, ln_g, ln_b)


def _attn_sample_kernel(q_ref, k_ref, v_ref, ck_ref, cv_ref, c_ref, sp_ref, sm_ref, sink_ref,
                        o_ref, nk_ref, nv_ref):
    bt = q_ref.shape[0]
    win = ck_ref.shape[1]
    group = N_Q_HEADS // N_KV_HEADS
    c, sp, sm = c_ref[0:1, :], sp_ref[0:1, :], sm_ref[0:1, :]
    sink = sink_ref[:, 0:1]
    row16 = lax.broadcasted_iota(jnp.int32, (N_Q_HEADS, KV_WIDTH), 0)
    lane16 = lax.broadcasted_iota(jnp.int32, (N_Q_HEADS, KV_WIDTH), 1)
    own_kv = (lane16 // HEAD_DIM) == (row16 % N_KV_HEADS)
    urow = row16 // N_KV_HEADS
    keyi = lax.broadcasted_iota(jnp.int32, (N_Q_HEADS, win), 1)
    rowk = lax.broadcasted_iota(jnp.int32, (win, KV_WIDTH), 0)

    def rope_row(x):
        return jnp.concatenate([_rope_tile(x[:, LANES * t:LANES * (t + 1)], c, sp, sm)
                                for t in range(x.shape[1] // LANES)], axis=1)

    out_rows = []
    for b in range(bt):
        q = rope_row(q_ref[b:b + 1, :]) * (HEAD_DIM ** -0.5)
        knew = rope_row(k_ref[b:b + 1, :])
        vnew = v_ref[b:b + 1, :]
        qb = [jnp.broadcast_to(q[:, KV_WIDTH * u:KV_WIDTH * (u + 1)], (N_Q_HEADS, KV_WIDTH)) for u in range(group)]
        qsel = jnp.where(urow == 0, qb[0], jnp.where(urow == 1, qb[1], jnp.where(urow == 2, qb[2], qb[3])))
        qmat = jnp.where(own_kv, qsel, 0.0)
        kc = ck_ref[b]
        vc = cv_ref[b]
        s = lax.dot_general(qmat.astype(BF16), kc.astype(BF16), (((1,), (1,)), ((), ())), preferred_element_type=F32)
        s = jnp.where(keyi >= 1, s, NEG_BIG)
        snew = jnp.sum(qmat * knew, axis=1, keepdims=True)
        m = jnp.maximum(jnp.maximum(jnp.max(s, axis=1, keepdims=True), snew), sink)
        p = jnp.exp(s - m)
        pn = jnp.exp(snew - m)
        den = jnp.sum(p, axis=1, keepdims=True) + pn + jnp.exp(sink - m)
        o = (jnp.dot(p.astype(BF16), vc.astype(BF16), preferred_element_type=F32) + pn * vnew) / den
        o = jnp.where(own_kv, o, 0.0)
        chunks = [jnp.sum(jnp.where(urow == u, o, 0.0), axis=0, keepdims=True) for u in range(group)]
        out_rows.append(jnp.concatenate(chunks, axis=1))
        nk_ref[b] = jnp.where(rowk == win - 1, knew, pltpu.roll(kc, win - 1, 0))
        nv_ref[b] = jnp.where(rowk == win - 1, vnew, pltpu.roll(vc, win - 1, 0))
    o_ref[...] = jnp.concatenate(out_rows, axis=0).astype(o_ref.dtype)


def _attn_sample(p_all, cache_k, cache_v, sink_mat, bt):
    n, win = cache_k.shape[0], cache_k.shape[1]
    c, sp, sm = (jnp.broadcast_to(z, (8, LANES)) for z in _rope_tables(jnp.full((1,), PAST_LEN, jnp.int32)))
    small = lambda shape: pl.BlockSpec(shape, lambda i: (0, 0))
    cache_spec = pl.BlockSpec((bt, win, KV_WIDTH), lambda i: (i, 0, 0))
    return pl.pallas_call(
        _attn_sample_kernel,
        grid=(n // bt,),
        in_specs=[pl.BlockSpec((bt, ATTN_WIDTH), lambda i: (i, COL_Q // ATTN_WIDTH)),
                  pl.BlockSpec((bt, KV_WIDTH), lambda i: (i, COL_K // KV_WIDTH)),
                  pl.BlockSpec((bt, KV_WIDTH), lambda i: (i, COL_V // KV_WIDTH)),
                  cache_spec, cache_spec,
                  small((8, LANES)), small((8, LANES)), small((8, LANES)), small((N_Q_HEADS, LANES))],
        out_specs=[pl.BlockSpec((bt, ATTN_WIDTH), lambda i: (i, 0)), cache_spec, cache_spec],
        out_shape=[jax.ShapeDtypeStruct((n, ATTN_WIDTH), BF16),
                   jax.ShapeDtypeStruct(cache_k.shape, F32),
                   jax.ShapeDtypeStruct(cache_v.shape, F32)],
        compiler_params=_cparams(1),
        name="attn_sample",
    )(p_all, p_all, p_all, cache_k, cache_v, c, sp, sm, sink_mat)


def _rwkv_sample_kernel(r_ref, k_ref, v_ref, l_ref, pr_ref, pk_ref, pv_ref, pl_ref, st_ref,
                        mur_ref, muk_ref, muv_ref, mul_ref, w0_ref, a0_ref, kk_ref, ka_ref,
                        rk_ref, gnw_ref, gnb_ref, wd_ref, wi_ref, wg_ref,
                        o_ref, ns_ref,
                        w_s, r_s, k_s, v_s, a_s_ref, b_s_ref, y_s):
    bt = r_ref.shape[0]
    bd2 = _head_block_ones()
    prm = tuple(ref[...] for ref in (mur_ref, muk_ref, muv_ref, mul_ref, w0_ref, a0_ref, kk_ref, ka_ref))
    r, k, v, logw, a_s, b_s, gate = _rwkv_prepare(
        r_ref[...], k_ref[...], v_ref[...], l_ref[...], pr_ref[...], pk_ref[...], pv_ref[...], pl_ref[...],
        prm, wd_ref, wi_ref, wg_ref, bd2)
    w_s[...] = jnp.exp(logw)
    r_s[...] = r
    k_s[...] = k
    v_s[...] = v
    a_s_ref[...] = a_s
    b_s_ref[...] = b_s
    hd = HEAD_DIM
    eye = lax.broadcasted_iota(jnp.int32, (hd, hd), 0) == lax.broadcasted_iota(jnp.int32, (hd, hd), 1)

    for b in range(bt):
        row = slice(b, b + 1)
        for h in range(N_RWKV_HEADS):
            sl = slice(hd * h, hd * (h + 1))
            s = st_ref[b, h]
            sa = jnp.sum(s * a_s_ref[row, sl], axis=1, keepdims=True)
            v_col = jnp.sum(jnp.where(eye, v_s[row, sl], 0.0), axis=1, keepdims=True)
            s_new = s * w_s[row, sl] + sa * b_s_ref[row, sl] + v_col * k_s[row, sl]
            ns_ref[b, h] = s_new
            y_col = jnp.sum(s_new * r_s[row, sl], axis=1, keepdims=True)
            y_s[row, sl] = jnp.sum(jnp.where(eye, y_col, 0.0), axis=0, keepdims=True)
    out = _rwkv_finish(y_s[...], r, k, v, gate, rk_ref[...], gnw_ref[...], gnb_ref[...], bd2)
    o_ref[...] = out.astype(o_ref.dtype)


def _rwkv_sample(p_all, shift, shift_l, state, prm, bt):
    n = state.shape[0]
    wide = lambda col0: pl.BlockSpec((bt, RWKV_WIDTH), lambda i: (i, col0 // RWKV_WIDTH))
    vec = pl.BlockSpec((1, RWKV_WIDTH), lambda i: (0, 0))
    vec_l = pl.BlockSpec((1, LORA_PAD), lambda i: (0, 0))
    st_spec = pl.BlockSpec((bt, N_RWKV_HEADS, HEAD_DIM, HEAD_DIM), lambda i: (i, 0, 0, 0))
    row_scr = pltpu.VMEM((bt, RWKV_WIDTH), F32)
    return pl.pallas_call(
        _rwkv_sample_kernel,
        grid=(n // bt,),
        in_specs=[wide(COL_R), wide(COL_KR), wide(COL_VR),
                  pl.BlockSpec((bt, LORA_PAD), lambda i: (i, COL_LORA // LORA_PAD)),
                  wide(0), wide(RWKV_WIDTH), wide(2 * RWKV_WIDTH),
                  pl.BlockSpec((bt, LORA_PAD), lambda i: (i, 0)),
                  st_spec,
                  vec, vec, vec, vec_l, vec, vec, vec, vec, vec, vec, vec,
                  pl.BlockSpec((LANES, RWKV_WIDTH), lambda i: (0, 0)),
                  pl.BlockSpec((LANES, RWKV_WIDTH), lambda i: (0, 0)),
                  pl.BlockSpec((2 * LANES, RWKV_WIDTH), lambda i: (0, 0))],
        out_specs=[pl.BlockSpec((bt, RWKV_WIDTH), lambda i: (i, 0)), st_spec],
        out_shape=[jax.ShapeDtypeStruct((n, RWKV_WIDTH), BF16), jax.ShapeDtypeStruct(state.shape, F32)],
        scratch_shapes=[row_scr] * 7,
        compiler_params=_cparams(1),
        name="rwkv_sample",
    )(p_all, p_all, p_all, p_all, shift, shift, shift, shift_l, state,
      prm["mu_r"], prm["mu_k"], prm["mu_v"], prm["mu_l"], prm["w0"], prm["a0"], prm["k_k"], prm["k_a"],
      prm["r_k"], prm["gn_w"], prm["gn_b"], prm["wd"], prm["wi"], prm["wg"])


def _pack_w_in(w_in, permute_q):
    c0 = ATTN_WIDTH
    q = w_in[:, :c0]
    if permute_q:
        group = N_Q_HEADS // N_KV_HEADS
        q = q.reshape(D_MODEL, N_KV_HEADS, group, HEAD_DIM).transpose(0, 2, 1, 3).reshape(D_MODEL, ATTN_WIDTH)
    k = w_in[:, c0:c0 + KV_WIDTH]
    v = w_in[:, c0 + KV_WIDTH:c0 + 2 * KV_WIDTH]
    c1 = c0 + 2 * KV_WIDTH
    rkv = w_in[:, c1:c1 + 3 * RWKV_WIDTH]
    lora = w_in[:, c1 + 3 * RWKV_WIDTH:c1 + RWKV_PROJ_WIDTH]
    c2 = c1 + RWKV_PROJ_WIDTH
    gates = w_in[:, c2:c2 + 2 * D_MODEL]
    pad = jnp.zeros((D_MODEL, LORA_PAD - LORA_WIDTH), w_in.dtype)
    return jnp.concatenate([q, rkv, gates, k, v, lora, pad], axis=1).astype(BF16)


def _shift_columns(p_rows):
    return jnp.concatenate([p_rows[:, COL_R:COL_R + 3 * RWKV_WIDTH], p_rows[:, COL_LORA:COL_LORA + LORA_WIDTH]], axis=1)


def _forward(x_prompt, x_sample, cache_k_win, cache_v_win, state_shift, state_wkv, w, cfg):
    b, t, _ = x_prompt.shape
    n_s = x_sample.shape[0]
    row = lambda z: z.reshape(1, -1).astype(F32)
    mu = w["mu_shift"]
    wd, wi, wg = _rwkv_lora_weights(w["w_decay_up"], w["w_iclr_up"], w["w_gate_up"])
    prm = dict(
        mu_r=row(mu[:RWKV_WIDTH]), mu_k=row(mu[RWKV_WIDTH:2 * RWKV_WIDTH]), mu_v=row(mu[2 * RWKV_WIDTH:3 * RWKV_WIDTH]),
        mu_l=row(jnp.pad(mu[3 * RWKV_WIDTH:], (0, LORA_PAD - LORA_WIDTH))),
        w0=row(w["w0"]), a0=row(w["a0"]), k_k=row(w["k_k"]), k_a=row(w["k_a"]), r_k=row(w["r_k"]),
        gn_w=row(w["gn_w"]), gn_b=row(w["gn_b"]), wd=wd, wi=wi, wg=wg)
    wpa = w["w_proj_attn"].astype(BF16)
    wpr = w["w_proj_rwkv"].astype(BF16)
    wo = w["w_out"].astype(BF16)
    wu = w["w_up"].astype(BF16)
    wdn = w["w_down"].astype(BF16)
    ln1g, ln1b, ln2g, ln2b = row(w["ln1_g"]), row(w["ln1_b"]), row(w["ln2_g"]), row(w["ln2_b"])
    sinks = w["attn_sinks"].astype(F32)

    xp = x_prompt.reshape(b * t, D_MODEL)
    pp = _inproj(xp, _pack_w_in(w["w_in"], False), cfg["tm_in"], cfg["tn_in"])
    attn_p, kwin_p, vwin_p = _attn_prompt(pp, sinks, b, t)
    rwkv_p, wkv_p = _rwkv_prompt(pp, prm, b, t, cfg["tb_rwkv"], cfg["lw_rwkv"])
    hp = _merge(xp, attn_p, rwkv_p, pp, wpa, wpr, wo, ln1g, ln1b, cfg["tm_merge"])
    yp = _ffn(hp, wu, wdn, ln2g, ln2b, cfg["tm_ffn"], cfg["tf_ffn"])
    shift_p = _shift_columns(pp.reshape(b, t, PACK_WIDTH)[:, t - 1])

    group = N_Q_HEADS // N_KV_HEADS
    xs = x_sample.reshape(n_s, D_MODEL)
    ps = _inproj(xs, _pack_w_in(w["w_in"], True), n_s, cfg["tn_in"])
    sink_mat = jnp.broadcast_to(sinks.reshape(N_KV_HEADS, group).T.reshape(N_Q_HEADS, 1), (N_Q_HEADS, LANES))
    win = cache_k_win.shape[1]
    attn_s, nk_s, nv_s = _attn_sample(ps, cache_k_win.reshape(n_s, win, KV_WIDTH),
                                      cache_v_win.reshape(n_s, win, KV_WIDTH), sink_mat, cfg["bt_sample"])
    shift_l = jnp.pad(state_shift[:, 3 * RWKV_WIDTH:], ((0, 0), (0, LORA_PAD - LORA_WIDTH)))
    rwkv_s, wkv_s = _rwkv_sample(ps, state_shift, shift_l, state_wkv, prm, cfg["bt_sample"])
    wpa_s = wpa.reshape(N_KV_HEADS, group, HEAD_DIM, D_MODEL).transpose(1, 0, 2, 3).reshape(ATTN_WIDTH, D_MODEL)
    hs = _merge(xs, attn_s, rwkv_s, ps, wpa_s, wpr, wo, ln1g, ln1b, n_s)
    ys = _ffn(hs, wu, wdn, ln2g, ln2b, n_s, cfg["tf_ffn"])
    shift_s = _shift_columns(ps)

    kv5 = lambda z: z.reshape(1, z.shape[0], z.shape[1], N_KV_HEADS, HEAD_DIM)
    return (yp.reshape(b, t, D_MODEL), ys.reshape(n_s, 1, D_MODEL),
            kv5(kwin_p), kv5(vwin_p), shift_p[None], wkv_p[None],
            kv5(nk_s), kv5(nv_s), shift_s[None], wkv_s[None])


_CFG = dict(tm_in=1024, tn_in=1024, tb_rwkv=256, lw_rwkv=256, tm_merge=256, tm_ffn=512, tf_ffn=1024, bt_sample=8)


def kernel(x_prompt, x_sample, cache_k_win, cache_v_win, state_shift, state_wkv, w_in, attn_sinks, mu_shift, w0,
           w_decay_up, a0, w_iclr_up, w_gate_up, k_k, k_a, r_k, gn_w, gn_b, w_proj_attn, w_proj_rwkv, w_out,
           ln1_g, ln1_b, w_up, w_down, ln2_g, ln2_b):
    w = dict(w_in=w_in[0], attn_sinks=attn_sinks[0], mu_shift=mu_shift[0], w0=w0[0], w_decay_up=w_decay_up[0],
             a0=a0[0], w_iclr_up=w_iclr_up[0], w_gate_up=w_gate_up[0], k_k=k_k[0], k_a=k_a[0], r_k=r_k[0],
             gn_w=gn_w[0], gn_b=gn_b[0], w_proj_attn=w_proj_attn[0], w_proj_rwkv=w_proj_rwkv[0], w_out=w_out[0],
             ln1_g=ln1_g[0], ln1_b=ln1_b[0], w_up=w_up[0], w_down=w_down[0], ln2_g=ln2_g[0], ln2_b=ln2_b[0])
    return _forward(x_prompt, x_sample, cache_k_win[0], cache_v_win[0], state_shift[0], state_wkv[0], w, _CFG)
```

```python
import functools

import jax
import jax.numpy as jnp
import numpy as np
from jax import lax
from jax.experimental import pallas as pl
from jax.experimental.pallas import tpu as pltpu

F32 = jnp.float32
BF16 = jnp.bfloat16

D_MODEL = 2048
HEAD_DIM = 64
N_Q_HEADS = 16
N_KV_HEADS = 4
ATTN_WIDTH = N_Q_HEADS * HEAD_DIM
KV_WIDTH = N_KV_HEADS * HEAD_DIM
WINDOW = 128
ROPE_THETA = 500000.0
ROT_DIM = HEAD_DIM // 4
N_RWKV_HEADS = 16
RWKV_WIDTH = N_RWKV_HEADS * HEAD_DIM
DECAY_LORA = 64
ICLR_LORA = 64
GATE_LORA = 160
LORA_WIDTH = DECAY_LORA + ICLR_LORA + GATE_LORA
RWKV_PROJ_WIDTH = 3 * RWKV_WIDTH + LORA_WIDTH
D_FF = 4 * D_MODEL
PAST_LEN = 16384
DEEPNORM_ALPHA = 2.0 ** 0.25
LN_EPS = 1e-5
GN_EPS = HEAD_DIM * 1e-5
NEG_BIG = -1e30

LANES = 128
CHUNK = 64
VMEM_LIMIT = 56 * 1024 * 1024

COL_Q = 0
COL_R = 1024
COL_KR = 2048
COL_VR = 3072
COL_GA = 4096
COL_GB = 6144
COL_K = 8192
COL_V = 8448
COL_LORA = 8704
LORA_PAD = 512
PACK_WIDTH = COL_LORA + LORA_PAD


def _cparams(n_axes):
    return pltpu.CompilerParams(dimension_semantics=("arbitrary",) * n_axes, vmem_limit_bytes=VMEM_LIMIT)


def _sigmoid(x):
    return 1.0 / (1.0 + jnp.exp(-x))


def _softplus(x):
    return jnp.maximum(x, 0.0) + jnp.log(1.0 + jnp.exp(-jnp.abs(x)))


def _layer_norm_rows(z, g, b):
    mu = jnp.mean(z, axis=-1, keepdims=True)
    d = z - mu
    var = jnp.mean(d * d, axis=-1, keepdims=True)
    return d * lax.rsqrt(var + LN_EPS) * g + b


def _split_bf16(x):
    hi = x.astype(BF16)
    lo = (x - hi.astype(F32)).astype(BF16)
    return hi, lo


def _head_block_ones():
    r = lax.broadcasted_iota(jnp.int32, (2 * LANES, LANES), 0)
    c = lax.broadcasted_iota(jnp.int32, (2 * LANES, LANES), 1)
    return jnp.where((r % LANES) // HEAD_DIM == c // HEAD_DIM, 1.0, 0.0).astype(BF16)


def _head_sum(x, bd2):
    outs = []
    for t in range(x.shape[1] // LANES):
        hi, lo = _split_bf16(x[:, LANES * t:LANES * (t + 1)])
        outs.append(jnp.dot(jnp.concatenate([hi, lo], axis=1), bd2, preferred_element_type=F32))
    return outs[0] if len(outs) == 1 else jnp.concatenate(outs, axis=1)


def _inproj_kernel(x_ref, w_ref, o_ref, xb_ref):
    @pl.when(pl.program_id(1) == 0)
    def _():
        xb_ref[...] = x_ref[...].astype(BF16)

    o_ref[...] = jnp.dot(xb_ref[...], w_ref[...], preferred_element_type=F32)


def _inproj(x, w_packed, tm, tn):
    m, k = x.shape
    n = w_packed.shape[1]
    return pl.pallas_call(
        _inproj_kernel,
        grid=(m // tm, n // tn),
        in_specs=[pl.BlockSpec((tm, k), lambda i, j: (i, 0)),
                  pl.BlockSpec((k, tn), lambda i, j: (0, j))],
        out_specs=pl.BlockSpec((tm, tn), lambda i, j: (i, j)),
        out_shape=jax.ShapeDtypeStruct((m, n), F32),
        scratch_shapes=[pltpu.VMEM((tm, k), BF16)],
        compiler_params=_cparams(2),
        name="inproj",
    )(x, w_packed)


def _rope_tables(pos):
    inv = ROPE_THETA ** (-jnp.arange(0, ROT_DIM, 2, dtype=F32) / ROT_DIM)
    ang = pos.astype(F32)[:, None] * inv[None, :]
    cos, sin = jnp.cos(ang), jnp.sin(ang)
    t = pos.shape[0]
    half = ROT_DIM // 2
    pad = HEAD_DIM - ROT_DIM
    c_head = jnp.concatenate([cos, cos, jnp.ones((t, pad), F32)], axis=1)
    sp_head = jnp.concatenate([-sin, jnp.zeros((t, half + pad), F32)], axis=1)
    sm_head = jnp.concatenate([jnp.zeros((t, half), F32), sin, jnp.zeros((t, pad), F32)], axis=1)
    rep = LANES // HEAD_DIM
    return tuple(jnp.tile(z, (1, rep)) for z in (c_head, sp_head, sm_head))


def _rope_tile(x, c, sp, sm):
    half = ROT_DIM // 2
    return x * c + pltpu.roll(x, LANES - half, 1) * sp + pltpu.roll(x, half, 1) * sm


def _attn_prompt_kernel(q_ref, k_ref, v_ref, c_ref, sp_ref, sm_ref, sink_ref,
                        o_ref, kwin_ref, vwin_ref, kp_ref, vp_ref):
    i = pl.program_id(1)
    blk = WINDOW

    @pl.when(i == 0)
    def _():
        kp_ref[...] = jnp.zeros_like(kp_ref)
        vp_ref[...] = jnp.zeros_like(vp_ref)

    c, sp, sm = c_ref[...], sp_ref[...], sm_ref[...]
    lane = lax.broadcasted_iota(jnp.int32, (2 * blk, LANES), 1)
    lo2 = lane < HEAD_DIM
    lo1 = lax.broadcasted_iota(jnp.int32, (blk, LANES), 1) < HEAD_DIM

    qi = lax.broadcasted_iota(jnp.int32, (blk, 2 * blk), 0)
    kj = lax.broadcasted_iota(jnp.int32, (blk, 2 * blk), 1)
    kmin = jnp.where(i == 0, blk, 0)
    valid = (kj > qi) & (kj <= qi + blk) & (kj >= kmin)

    n_kv_tiles = KV_WIDTH // LANES
    kk_g, va_g, vb_g = [], [], []
    for t in range(n_kv_tiles):
        sl = slice(LANES * t, LANES * (t + 1))
        kcur = _rope_tile(k_ref[:, sl], c, sp, sm)
        vcur = v_ref[:, sl]
        kwin_ref[0, :, sl] = kcur
        vwin_ref[0, :, sl] = vcur
        kall = jnp.concatenate([kp_ref[:, sl], kcur], axis=0)
        vall = jnp.concatenate([vp_ref[:, sl], vcur], axis=0)
        kp_ref[:, sl] = kcur
        vp_ref[:, sl] = vcur
        kswap = pltpu.roll(kall, HEAD_DIM, 1)
        vswap = pltpu.roll(vall, HEAD_DIM, 1)
        kk_g += [jnp.where(lo2, kall, kswap).astype(BF16), jnp.where(lo2, kswap, kall).astype(BF16)]
        va_g += [jnp.where(lo2, vall, 0.0).astype(BF16), jnp.where(lo2, vswap, 0.0).astype(BF16)]
        vb_g += [jnp.where(lo2, 0.0, vswap).astype(BF16), jnp.where(lo2, 0.0, vall).astype(BF16)]

    group = N_Q_HEADS // N_KV_HEADS
    for j in range(ATTN_WIDTH // LANES):
        sl = slice(LANES * j, LANES * (j + 1))
        g = (2 * j) // group
        qt = _rope_tile(q_ref[:, sl], c, sp, sm) * (HEAD_DIM ** -0.5)
        o_tile = None
        for hh in range(2):
            sink = sink_ref[2 * j + hh]
            qm = (jnp.where(lo1, qt, 0.0) if hh == 0 else jnp.where(lo1, 0.0, qt)).astype(BF16)
            s = lax.dot_general(qm, kk_g[g], (((1,), (1,)), ((), ())), preferred_element_type=F32)
            s = jnp.where(valid, s, NEG_BIG)
            m = jnp.maximum(jnp.max(s, axis=1, keepdims=True), sink)
            p = jnp.exp(s - m)
            den = jnp.sum(p, axis=1, keepdims=True) + jnp.exp(sink - m)
            vmat = va_g[g] if hh == 0 else vb_g[g]
            o_h = jnp.dot(p.astype(BF16), vmat, preferred_element_type=F32) / den
            o_tile = o_h if o_tile is None else o_tile + o_h
        o_ref[:, sl] = o_tile.astype(o_ref.dtype)


def _attn_prompt(p_all, sinks, b, t):
    blk = WINDOW
    nb = t // blk
    c, sp, sm = _rope_tables(jnp.arange(t, dtype=jnp.int32))
    tab_spec = pl.BlockSpec((blk, LANES), lambda bi, i: (i, 0))
    row = lambda bi, i: bi * nb + i
    return pl.pallas_call(
        _attn_prompt_kernel,
        grid=(b, nb),
        in_specs=[pl.BlockSpec((blk, ATTN_WIDTH), lambda bi, i: (row(bi, i), COL_Q // ATTN_WIDTH)),
                  pl.BlockSpec((blk, KV_WIDTH), lambda bi, i: (row(bi, i), COL_K // KV_WIDTH)),
                  pl.BlockSpec((blk, KV_WIDTH), lambda bi, i: (row(bi, i), COL_V // KV_WIDTH)),
                  tab_spec, tab_spec, tab_spec,
                  pl.BlockSpec(memory_space=pltpu.SMEM)],
        out_specs=[pl.BlockSpec((blk, ATTN_WIDTH), lambda bi, i: (row(bi, i), 0)),
                   pl.BlockSpec((1, blk, KV_WIDTH), lambda bi, i: (bi, 0, 0)),
                   pl.BlockSpec((1, blk, KV_WIDTH), lambda bi, i: (bi, 0, 0))],
        out_shape=[jax.ShapeDtypeStruct((b * t, ATTN_WIDTH), BF16),
                   jax.ShapeDtypeStruct((b, blk, KV_WIDTH), F32),
                   jax.ShapeDtypeStruct((b, blk, KV_WIDTH), F32)],
        scratch_shapes=[pltpu.VMEM((blk, KV_WIDTH), F32), pltpu.VMEM((blk, KV_WIDTH), F32)],
        compiler_params=_cparams(2),
        name="attn_prompt",
    )(p_all, p_all, p_all, c, sp, sm, sinks)


def _rwkv_prepare(r_in, k_in, v_in, l_in, pr, pk, pv, plr, prm, wd_ref, wi_ref, wg_ref, bd2):
    mur, muk, muv, mul, w0, a0, k_k, k_a = prm
    r = r_in + (pr - r_in) * mur
    k = k_in + (pk - k_in) * muk
    v = v_in + (pv - v_in) * muv
    ls = l_in + (plr - l_in) * mul
    l0 = ls[:, :LANES]
    lane = lax.broadcasted_iota(jnp.int32, l0.shape, 1)
    z0 = jnp.where(lane < DECAY_LORA, jnp.tanh(l0), l0).astype(BF16)
    zg = _sigmoid(ls[:, LANES:3 * LANES]).astype(BF16)
    dec_up = jnp.dot(z0, wd_ref[...], preferred_element_type=F32)
    icl_up = jnp.dot(z0, wi_ref[...], preferred_element_type=F32)
    gate = jnp.dot(zg, wg_ref[...], preferred_element_type=F32)
    w_log = -_softplus(-(w0 + dec_up)) - 0.5
    logw = -jnp.exp(w_log)
    a_sig = _sigmoid(a0 + icl_up)
    kk = k * k_k
    nrm = jnp.maximum(jnp.sqrt(_head_sum(kk * kk, bd2)), 1e-12)
    kk = kk / nrm
    k = k * (1.0 + (a_sig - 1.0) * k_a)
    return r, k, v, logw, -kk, kk * a_sig, gate


def _rwkv_finish(y, r, k, v, gate, r_k, gn_w, gn_b, bd2):
    mu = _head_sum(y, bd2) * (1.0 / HEAD_DIM)
    d = y - mu
    var = _head_sum(d * d, bd2) * (1.0 / HEAD_DIM)
    yn = d * lax.rsqrt(var + GN_EPS) * gn_w + gn_b
    bonus = _head_sum(r * k * r_k, bd2) * v
    return (yn + bonus) * gate


def _rwkv_prompt_kernel(r_ref, k_ref, v_ref, l_ref,
                        mur_ref, muk_ref, muv_ref, mul_ref, w0_ref, a0_ref, kk_ref, ka_ref,
                        rk_ref, gnw_ref, gnb_ref, wd_ref, wi_ref, wg_ref,
                        o_ref, sout_ref,
                        cr_ref, ck_ref, cv_ref, cl_ref, st_ref, y_ref):
    ti = pl.program_id(2)
    tb, lw = r_ref.shape
    n_pair = lw // LANES
    n_chunk = tb // CHUNK

    @pl.when(ti == 0)
    def _():
        for ref in (cr_ref, ck_ref, cv_ref, cl_ref, st_ref):
            ref[...] = jnp.zeros_like(ref)

    def shifted(x, carry_ref):
        rolled = pltpu.roll(x, 1, 0)
        row = lax.broadcasted_iota(jnp.int32, x.shape, 0)
        prev = jnp.where(row == 0, carry_ref[0:1, :], rolled)
        carry_ref[0:1, :] = x[tb - 1:tb, :]
        return prev

    bd2 = _head_block_ones()
    r_in, k_in, v_in, l_in = r_ref[...], k_ref[...], v_ref[...], l_ref[...]
    prm = tuple(ref[...] for ref in (mur_ref, muk_ref, muv_ref, mul_ref, w0_ref, a0_ref, kk_ref, ka_ref))
    r, k, v, logw, a_s, b_s, gate = _rwkv_prepare(
        r_in, k_in, v_in, l_in,
        shifted(r_in, cr_ref), shifted(k_in, ck_ref), shifted(v_in, cv_ref), shifted(l_in, cl_ref),
        prm, wd_ref, wi_ref, wg_ref, bd2)

    c = CHUNK
    ri = lax.broadcasted_iota(jnp.int32, (c, 3 * c), 0)
    ci = lax.broadcasted_iota(jnp.int32, (c, 3 * c), 1) % c
    tri3 = jnp.where(ri >= ci, 1.0, 0.0).astype(BF16)
    r2 = lax.broadcasted_iota(jnp.int32, (2 * c, LANES), 0)
    l2 = lax.broadcasted_iota(jnp.int32, (2 * c, LANES), 1)
    tt, ss = r2 % c, l2 % c
    causal = ss < tt + r2 // c
    lo1 = lax.broadcasted_iota(jnp.int32, (c, LANES), 1) < HEAD_DIM
    lo2 = l2 < HEAD_DIM
    diag_blocks = (r2 // HEAD_DIM) == (l2 // HEAD_DIM)
    eye = jnp.where(r2 == l2, 1.0, 0.0)

    def stack_heads(x):
        return jnp.concatenate([jnp.where(lo1, x, 0.0), jnp.where(lo1, 0.0, x)], axis=0)

    def dot_t(a, b):
        return lax.dot_general(a, b, (((1,), (1,)), ((), ())), preferred_element_type=F32)

    def dot_tt(a, b):
        return lax.dot_general(a, b, (((0,), (0,)), ((), ())), preferred_element_type=F32)

    def dot(a, b):
        return jnp.dot(a.astype(BF16), b.astype(BF16), preferred_element_type=F32)

    pairs = range(n_pair)
    lanes = [slice(LANES * pi, LANES * (pi + 1)) for pi in pairs]
    for ch in range(n_chunk):
        rows = slice(c * ch, c * (ch + 1))
        lw_c = logw[rows]
        hi = lw_c.astype(BF16)
        rem = lw_c - hi.astype(F32)
        mid = rem.astype(BF16)
        low = (rem - mid.astype(F32)).astype(BF16)
        lcum = jnp.dot(tri3, jnp.concatenate([hi, mid, low], axis=0), preferred_element_type=F32)
        ltot = lcum[c - 1:c, :]
        p_inv = jnp.exp(-lcum)
        p_tail = jnp.exp(ltot - lcum)
        aq = a_s[rows] * jnp.exp(lcum - lw_c)
        rq = r[rows] * jnp.exp(lcum)
        bk = b_s[rows] * p_inv
        kq = k[rows] * p_inv
        bt = b_s[rows] * p_tail
        kt = k[rows] * p_tail
        p_end = jnp.exp(ltot)
        v_c = v[rows]

        ar = [jnp.concatenate([aq[:, sl], rq[:, sl]], axis=0).astype(BF16) for sl in lanes]
        gb = [jnp.where(causal, dot_t(ar[pi], stack_heads(bk[:, lanes[pi]]).astype(BF16)), 0.0) for pi in pairs]
        gk = [jnp.where(causal, dot_t(ar[pi], stack_heads(kq[:, lanes[pi]]).astype(BF16)), 0.0) for pi in pairs]
        gv = [dot(gk[pi], stack_heads(v_c[:, lanes[pi]])) for pi in pairs]
        pw = [stack_heads(gb[pi][:c]) for pi in pairs]
        tm = [eye + pw[pi] for pi in pairs]
        for _ in range(5):
            pw = [dot(pw[pi], pw[pi]) for pi in pairs]
            tm = [tm[pi] + dot(tm[pi], pw[pi]) for pi in pairs]
        t_side = [tm[pi][:c] + tm[pi][c:] for pi in pairs]
        tax = [dot(t_side[pi], jnp.concatenate([stack_heads(aq[:, lanes[pi]]), stack_heads(gv[pi][:c])], axis=1))
               for pi in pairs]
        taq = [tax[pi][:, :LANES] for pi in pairs]
        txv = [tax[pi][:, LANES:] for pi in pairs]
        arx = [dot(gb[pi][c:], jnp.concatenate([stack_heads(taq[pi]), stack_heads(txv[pi])], axis=1)) for pi in pairs]
        mb = [jnp.where(diag_blocks, dot_tt(bt[:, lanes[pi]].astype(BF16), taq[pi].astype(BF16)), 0.0) for pi in pairs]
        cct = [jnp.where(diag_blocks,
                         dot_tt(jnp.concatenate([txv[pi], v_c[:, lanes[pi]]], axis=0).astype(BF16),
                                jnp.concatenate([bt[:, lanes[pi]], kt[:, lanes[pi]]], axis=0).astype(BF16)), 0.0)
               for pi in pairs]
        for pi in pairs:
            sl = lanes[pi]
            s_old = st_ref[pi]
            sb = s_old.astype(BF16)
            rqp = (rq[:, sl] + arx[pi][:, :LANES]).astype(BF16)
            y_ref[rows, sl] = dot_t(rqp, sb) + gv[pi][c:] + arx[pi][:, LANES:]
            st_ref[pi] = s_old * p_end[:, sl] + dot_t(sb, mb[pi].astype(BF16)) + cct[pi]

    out = _rwkv_finish(y_ref[...], r, k, v, gate, rk_ref[...], gnw_ref[...], gnb_ref[...], bd2)
    o_ref[...] = out.astype(o_ref.dtype)

    @pl.when(ti == pl.num_programs(2) - 1)
    def _():
        for pi in range(n_pair):
            s = st_ref[pi]
            sout_ref[0, 2 * pi] = s[:HEAD_DIM, :HEAD_DIM]
            sout_ref[0, 2 * pi + 1] = s[HEAD_DIM:, HEAD_DIM:]


def _rwkv_lora_weights(w_decay_up, w_iclr_up, w_gate_up):
    z64 = jnp.zeros((DECAY_LORA, RWKV_WIDTH), F32)
    wd = jnp.concatenate([w_decay_up, z64], axis=0).astype(BF16)
    wi = jnp.concatenate([z64, w_iclr_up], axis=0).astype(BF16)
    wg = jnp.concatenate([w_gate_up, jnp.zeros((2 * LANES - GATE_LORA, RWKV_WIDTH), F32)], axis=0).astype(BF16)
    return wd, wi, wg


def _rwkv_prompt(p_all, prm, b, t, tb, lw):
    nt = t // tb
    ns = RWKV_WIDTH // lw
    row = lambda bi, si, ti: bi * nt + ti
    col_spec = lambda col0: pl.BlockSpec((tb, lw), lambda bi, si, ti: (row(bi, si, ti), col0 // lw + si))
    vec = pl.BlockSpec((1, lw), lambda bi, si, ti: (0, si))
    vec_l = pl.BlockSpec((1, LORA_PAD), lambda bi, si, ti: (0, 0))
    return pl.pallas_call(
        _rwkv_prompt_kernel,
        grid=(b, ns, nt),
        in_specs=[col_spec(COL_R), col_spec(COL_KR), col_spec(COL_VR),
                  pl.BlockSpec((tb, LORA_PAD), lambda bi, si, ti: (row(bi, si, ti), COL_LORA // LORA_PAD)),
                  vec, vec, vec, vec_l, vec, vec, vec, vec, vec, vec, vec,
                  pl.BlockSpec((LANES, lw), lambda bi, si, ti: (0, si)),
                  pl.BlockSpec((LANES, lw), lambda bi, si, ti: (0, si)),
                  pl.BlockSpec((2 * LANES, lw), lambda bi, si, ti: (0, si))],
        out_specs=[pl.BlockSpec((tb, lw), lambda bi, si, ti: (row(bi, si, ti), si)),
                   pl.BlockSpec((1, 2 * (lw // LANES), HEAD_DIM, HEAD_DIM), lambda bi, si, ti: (bi, si, 0, 0))],
        out_shape=[jax.ShapeDtypeStruct((b * t, RWKV_WIDTH), BF16),
                   jax.ShapeDtypeStruct((b, N_RWKV_HEADS, HEAD_DIM, HEAD_DIM), F32)],
        scratch_shapes=[pltpu.VMEM((8, lw), F32), pltpu.VMEM((8, lw), F32), pltpu.VMEM((8, lw), F32),
                        pltpu.VMEM((8, LORA_PAD), F32),
                        pltpu.VMEM((lw // LANES, LANES, LANES), F32),
                        pltpu.VMEM((tb, lw), F32)],
        compiler_params=_cparams(3),
        name="rwkv_prompt",
    )(p_all, p_all, p_all, p_all,
      prm["mu_r"], prm["mu_k"], prm["mu_v"], prm["mu_l"], prm["w0"], prm["a0"], prm["k_k"], prm["k_a"],
      prm["r_k"], prm["gn_w"], prm["gn_b"], prm["wd"], prm["wi"], prm["wg"])


def _merge_kernel(x_ref, ao_ref, ro_ref, ga_ref, gb_ref, wpa_ref, wpr_ref, wo_ref, g_ref, b_ref, h_ref):
    a = jnp.dot(ao_ref[...], wpa_ref[...], preferred_element_type=F32)
    r = jnp.dot(ro_ref[...], wpr_ref[...], preferred_element_type=F32)
    m = _sigmoid(ga_ref[...]) * a + _sigmoid(gb_ref[...]) * r
    merged = jnp.dot(m.astype(BF16), wo_ref[...], preferred_element_type=F32)
    z = DEEPNORM_ALPHA * x_ref[...] + merged
    h_ref[...] = _layer_norm_rows(z, g_ref[...], b_ref[...])


def _merge(x, attn_o, rwkv_o, p_all, wpa, wpr, wo, ln_g, ln_b, tm):
    m = x.shape[0]
    full = lambda shape: pl.BlockSpec(shape, lambda i: (0, 0), pipeline_mode=pl.Buffered(1))
    return pl.pallas_call(
        _merge_kernel,
        grid=(m // tm,),
        in_specs=[pl.BlockSpec((tm, D_MODEL), lambda i: (i, 0)),
                  pl.BlockSpec((tm, ATTN_WIDTH), lambda i: (i, 0)),
                  pl.BlockSpec((tm, RWKV_WIDTH), lambda i: (i, 0)),
                  pl.BlockSpec((tm, D_MODEL), lambda i: (i, COL_GA // D_MODEL)),
                  pl.BlockSpec((tm, D_MODEL), lambda i: (i, COL_GB // D_MODEL)),
                  full((ATTN_WIDTH, D_MODEL)), full((RWKV_WIDTH, D_MODEL)), full((D_MODEL, D_MODEL)),
                  full((1, D_MODEL)), full((1, D_MODEL))],
        out_specs=pl.BlockSpec((tm, D_MODEL), lambda i: (i, 0)),
        out_shape=jax.ShapeDtypeStruct((m, D_MODEL), F32),
        compiler_params=_cparams(1),
        name="merge_ln1",
    )(x, attn_o, rwkv_o, p_all, p_all, wpa, wpr, wo, ln_g, ln_b)


def _ffn_kernel(h_ref, wu_ref, wd_ref, g_ref, b_ref, y_ref, hb_ref, acc_ref):
    f = pl.program_id(1)

    @pl.when(f == 0)
    def _():
        hb_ref[...] = h_ref[...].astype(BF16)
        acc_ref[...] = jnp.zeros_like(acc_ref)

    u = jnp.dot(hb_ref[...], wu_ref[...], preferred_element_type=F32)
    u = jnp.square(jnp.maximum(u, 0.0))
    acc_ref[...] += jnp.dot(u.astype(BF16), wd_ref[...], preferred_element_type=F32)

    @pl.when(f == pl.num_programs(1) - 1)
    def _():
        z = DEEPNORM_ALPHA * h_ref[...] + acc_ref[...]
        y_ref[...] = _layer_norm_rows(z, g_ref[...], b_ref[...])


def _ffn(h, wu, wd, ln_g, ln_b, tm, tf):
    m = h.shape[0]
    return pl.pallas_call(
        _ffn_kernel,
        grid=(m // tm, D_FF // tf),
        in_specs=[pl.BlockSpec((tm, D_MODEL), lambda i, f: (i, 0)),
                  pl.BlockSpec((D_MODEL, tf), lambda i, f: (0, f)),
                  pl.BlockSpec((tf, D_MODEL), lambda i, f: (f, 0)),
                  pl.BlockSpec((1, D_MODEL), lambda i, f: (0, 0)),
                  pl.BlockSpec((1, D_MODEL), lambda i, f: (0, 0))],
        out_specs=pl.BlockSpec((tm, D_MODEL), lambda i, f: (i, 0)),
        out_shape=jax.ShapeDtypeStruct((m, D_MODEL), F32),
        scratch_shapes=[pltpu.VMEM((tm, D_MODEL), BF16), pltpu.VMEM((tm, D_MODEL), F32)],
        compiler_params=_cparams(2),
        name="ffn_ln2",
    )(h, wu, wd, ln_g, ln_b)


def _attn_sample_kernel(q_ref, k_ref, v_ref, ck_ref, cv_ref, c_ref, sp_ref, sm_ref, sink_ref,
                        o_ref, nk_ref, nv_ref):
    bt = q_ref.shape[0]
    win = ck_ref.shape[1]
    group = N_Q_HEADS // N_KV_HEADS
    c, sp, sm = c_ref[0:1, :], sp_ref[0:1, :], sm_ref[0:1, :]
    sink = sink_ref[:, 0:1]
    row16 = lax.broadcasted_iota(jnp.int32, (N_Q_HEADS, KV_WIDTH), 0)
    lane16 = lax.broadcasted_iota(jnp.int32, (N_Q_HEADS, KV_WIDTH), 1)
    own_kv = (lane16 // HEAD_DIM) == (row16 % N_KV_HEADS)
    urow = row16 // N_KV_HEADS
    keyi = lax.broadcasted_iota(jnp.int32, (N_Q_HEADS, win), 1)
    rowk = lax.broadcasted_iota(jnp.int32, (win, KV_WIDTH), 0)

    def rope_row(x):
        return jnp.concatenate([_rope_tile(x[:, LANES * t:LANES * (t + 1)], c, sp, sm)
                                for t in range(x.shape[1] // LANES)], axis=1)

    out_rows = []
    for b in range(bt):
        q = rope_row(q_ref[b:b + 1, :]) * (HEAD_DIM ** -0.5)
        knew = rope_row(k_ref[b:b + 1, :])
        vnew = v_ref[b:b + 1, :]
        qb = [jnp.broadcast_to(q[:, KV_WIDTH * u:KV_WIDTH * (u + 1)], (N_Q_HEADS, KV_WIDTH)) for u in range(group)]
        qsel = jnp.where(urow == 0, qb[0], jnp.where(urow == 1, qb[1], jnp.where(urow == 2, qb[2], qb[3])))
        qmat = jnp.where(own_kv, qsel, 0.0)
        kc = ck_ref[b]
        vc = cv_ref[b]
        s = lax.dot_general(qmat.astype(BF16), kc.astype(BF16), (((1,), (1,)), ((), ())), preferred_element_type=F32)
        s = jnp.where(keyi >= 1, s, NEG_BIG)
        snew = jnp.sum(qmat * knew, axis=1, keepdims=True)
        m = jnp.maximum(jnp.maximum(jnp.max(s, axis=1, keepdims=True), snew), sink)
        p = jnp.exp(s - m)
        pn = jnp.exp(snew - m)
        den = jnp.sum(p, axis=1, keepdims=True) + pn + jnp.exp(sink - m)
        o = (jnp.dot(p.astype(BF16), vc.astype(BF16), preferred_element_type=F32) + pn * vnew) / den
        o = jnp.where(own_kv, o, 0.0)
        chunks = [jnp.sum(jnp.where(urow == u, o, 0.0), axis=0, keepdims=True) for u in range(group)]
        out_rows.append(jnp.concatenate(chunks, axis=1))
        nk_ref[b] = jnp.where(rowk == win - 1, knew, pltpu.roll(kc, win - 1, 0))
        nv_ref[b] = jnp.where(rowk == win - 1, vnew, pltpu.roll(vc, win - 1, 0))
    o_ref[...] = jnp.concatenate(out_rows, axis=0).astype(o_ref.dtype)


def _attn_sample(p_all, cache_k, cache_v, sink_mat, bt):
    n, win = cache_k.shape[0], cache_k.shape[1]
    c, sp, sm = (jnp.broadcast_to(z, (8, LANES)) for z in _rope_tables(jnp.full((1,), PAST_LEN, jnp.int32)))
    small = lambda shape: pl.BlockSpec(shape, lambda i: (0, 0))
    cache_spec = pl.BlockSpec((bt, win, KV_WIDTH), lambda i: (i, 0, 0))
    return pl.pallas_call(
        _attn_sample_kernel,
        grid=(n // bt,),
        in_specs=[pl.BlockSpec((bt, ATTN_WIDTH), lambda i: (i, COL_Q // ATTN_WIDTH)),
                  pl.BlockSpec((bt, KV_WIDTH), lambda i: (i, COL_K // KV_WIDTH)),
                  pl.BlockSpec((bt, KV_WIDTH), lambda i: (i, COL_V // KV_WIDTH)),
                  cache_spec, cache_spec,
                  small((8, LANES)), small((8, LANES)), small((8, LANES)), small((N_Q_HEADS, LANES))],
        out_specs=[pl.BlockSpec((bt, ATTN_WIDTH), lambda i: (i, 0)), cache_spec, cache_spec],
        out_shape=[jax.ShapeDtypeStruct((n, ATTN_WIDTH), BF16),
                   jax.ShapeDtypeStruct(cache_k.shape, F32),
                   jax.ShapeDtypeStruct(cache_v.shape, F32)],
        compiler_params=_cparams(1),
        name="attn_sample",
    )(p_all, p_all, p_all, cache_k, cache_v, c, sp, sm, sink_mat)


def _rwkv_sample_kernel(r_ref, k_ref, v_ref, l_ref, pr_ref, pk_ref, pv_ref, pl_ref, st_ref,
                        mur_ref, muk_ref, muv_ref, mul_ref, w0_ref, a0_ref, kk_ref, ka_ref,
                        rk_ref, gnw_ref, gnb_ref, wd_ref, wi_ref, wg_ref,
                        o_ref, ns_ref,
                        w_s, r_s, k_s, v_s, a_s_ref, b_s_ref, y_s):
    bt = r_ref.shape[0]
    bd2 = _head_block_ones()
    prm = tuple(ref[...] for ref in (mur_ref, muk_ref, muv_ref, mul_ref, w0_ref, a0_ref, kk_ref, ka_ref))
    r, k, v, logw, a_s, b_s, gate = _rwkv_prepare(
        r_ref[...], k_ref[...], v_ref[...], l_ref[...], pr_ref[...], pk_ref[...], pv_ref[...], pl_ref[...],
        prm, wd_ref, wi_ref, wg_ref, bd2)
    w_s[...] = jnp.exp(logw)
    r_s[...] = r
    k_s[...] = k
    v_s[...] = v
    a_s_ref[...] = a_s
    b_s_ref[...] = b_s
    hd = HEAD_DIM
    eye = lax.broadcasted_iota(jnp.int32, (hd, hd), 0) == lax.broadcasted_iota(jnp.int32, (hd, hd), 1)

    for b in range(bt):
        row = slice(b, b + 1)
        for h in range(N_RWKV_HEADS):
            sl = slice(hd * h, hd * (h + 1))
            s = st_ref[b, h]
            sa = jnp.sum(s * a_s_ref[row, sl], axis=1, keepdims=True)
            v_col = jnp.sum(jnp.where(eye, v_s[row, sl], 0.0), axis=1, keepdims=True)
            s_new = s * w_s[row, sl] + sa * b_s_ref[row, sl] + v_col * k_s[row, sl]
            ns_ref[b, h] = s_new
            y_col = jnp.sum(s_new * r_s[row, sl], axis=1, keepdims=True)
            y_s[row, sl] = jnp.sum(jnp.where(eye, y_col, 0.0), axis=0, keepdims=True)
    out = _rwkv_finish(y_s[...], r, k, v, gate, rk_ref[...], gnw_ref[...], gnb_ref[...], bd2)
    o_ref[...] = out.astype(o_ref.dtype)


def _rwkv_sample(p_all, shift, shift_l, state, prm, bt):
    n = state.shape[0]
    wide = lambda col0: pl.BlockSpec((bt, RWKV_WIDTH), lambda i: (i, col0 // RWKV_WIDTH))
    vec = pl.BlockSpec((1, RWKV_WIDTH), lambda i: (0, 0))
    vec_l = pl.BlockSpec((1, LORA_PAD), lambda i: (0, 0))
    st_spec = pl.BlockSpec((bt, N_RWKV_HEADS, HEAD_DIM, HEAD_DIM), lambda i: (i, 0, 0, 0))
    row_scr = pltpu.VMEM((bt, RWKV_WIDTH), F32)
    return pl.pallas_call(
        _rwkv_sample_kernel,
        grid=(n // bt,),
        in_specs=[wide(COL_R), wide(COL_KR), wide(COL_VR),
                  pl.BlockSpec((bt, LORA_PAD), lambda i: (i, COL_LORA // LORA_PAD)),
                  wide(0), wide(RWKV_WIDTH), wide(2 * RWKV_WIDTH),
                  pl.BlockSpec((bt, LORA_PAD), lambda i: (i, 0)),
                  st_spec,
                  vec, vec, vec, vec_l, vec, vec, vec, vec, vec, vec, vec,
                  pl.BlockSpec((LANES, RWKV_WIDTH), lambda i: (0, 0)),
                  pl.BlockSpec((LANES, RWKV_WIDTH), lambda i: (0, 0)),
                  pl.BlockSpec((2 * LANES, RWKV_WIDTH), lambda i: (0, 0))],
        out_specs=[pl.BlockSpec((bt, RWKV_WIDTH), lambda i: (i, 0)), st_spec],
        out_shape=[jax.ShapeDtypeStruct((n, RWKV_WIDTH), BF16), jax.ShapeDtypeStruct(state.shape, F32)],
        scratch_shapes=[row_scr] * 7,
        compiler_params=_cparams(1),
        name="rwkv_sample",
    )(p_all, p_all, p_all, p_all, shift, shift, shift, shift_l, state,
      prm["mu_r"], prm["mu_k"], prm["mu_v"], prm["mu_l"], prm["w0"], prm["a0"], prm["k_k"], prm["k_a"],
      prm["r_k"], prm["gn_w"], prm["gn_b"], prm["wd"], prm["wi"], prm["wg"])


def _pack_w_in(w_in, permute_q):
    c0 = ATTN_WIDTH
    q = w_in[:, :c0]
    if permute_q:
        group = N_Q_HEADS // N_KV_HEADS
        q = q.reshape(D_MODEL, N_KV_HEADS, group, HEAD_DIM).transpose(0, 2, 1, 3).reshape(D_MODEL, ATTN_WIDTH)
    k = w_in[:, c0:c0 + KV_WIDTH]
    v = w_in[:, c0 + KV_WIDTH:c0 + 2 * KV_WIDTH]
    c1 = c0 + 2 * KV_WIDTH
    rkv = w_in[:, c1:c1 + 3 * RWKV_WIDTH]
    lora = w_in[:, c1 + 3 * RWKV_WIDTH:c1 + RWKV_PROJ_WIDTH]
    c2 = c1 + RWKV_PROJ_WIDTH
    gates = w_in[:, c2:c2 + 2 * D_MODEL]
    pad = jnp.zeros((D_MODEL, LORA_PAD - LORA_WIDTH), w_in.dtype)
    return jnp.concatenate([q, rkv, gates, k, v, lora, pad], axis=1).astype(BF16)


def _shift_columns(p_rows):
    return jnp.concatenate([p_rows[:, COL_R:COL_R + 3 * RWKV_WIDTH], p_rows[:, COL_LORA:COL_LORA + LORA_WIDTH]], axis=1)


def _forward(x_prompt, x_sample, cache_k_win, cache_v_win, state_shift, state_wkv, w, cfg):
    b, t, _ = x_prompt.shape
    n_s = x_sample.shape[0]
    row = lambda z: z.reshape(1, -1).astype(F32)
    mu = w["mu_shift"]
    wd, wi, wg = _rwkv_lora_weights(w["w_decay_up"], w["w_iclr_up"], w["w_gate_up"])
    prm = dict(
        mu_r=row(mu[:RWKV_WIDTH]), mu_k=row(mu[RWKV_WIDTH:2 * RWKV_WIDTH]), mu_v=row(mu[2 * RWKV_WIDTH:3 * RWKV_WIDTH]),
        mu_l=row(jnp.pad(mu[3 * RWKV_WIDTH:], (0, LORA_PAD - LORA_WIDTH))),
        w0=row(w["w0"]), a0=row(w["a0"]), k_k=row(w["k_k"]), k_a=row(w["k_a"]), r_k=row(w["r_k"]),
        gn_w=row(w["gn_w"]), gn_b=row(w["gn_b"]), wd=wd, wi=wi, wg=wg)
    wpa = w["w_proj_attn"].astype(BF16)
    wpr = w["w_proj_rwkv"].astype(BF16)
    wo = w["w_out"].astype(BF16)
    wu = w["w_up"].astype(BF16)
    wdn = w["w_down"].astype(BF16)
    ln1g, ln1b, ln2g, ln2b = row(w["ln1_g"]), row(w["ln1_b"]), row(w["ln2_g"]), row(w["ln2_b"])
    sinks = w["attn_sinks"].astype(F32)

    xp = x_prompt.reshape(b * t, D_MODEL)
    pp = _inproj(xp, _pack_w_in(w["w_in"], False), cfg["tm_in"], cfg["tn_in"])
    attn_p, kwin_p, vwin_p = _attn_prompt(pp, sinks, b, t)
    rwkv_p, wkv_p = _rwkv_prompt(pp, prm, b, t, cfg["tb_rwkv"], cfg["lw_rwkv"])
    hp = _merge(xp, attn_p, rwkv_p, pp, wpa, wpr, wo, ln1g, ln1b, cfg["tm_merge"])
    yp = _ffn(hp, wu, wdn, ln2g, ln2b, cfg["tm_ffn"], cfg["tf_ffn"])
    shift_p = _shift_columns(pp.reshape(b, t, PACK_WIDTH)[:, t - 1])

    group = N_Q_HEADS // N_KV_HEADS
    xs = x_sample.reshape(n_s, D_MODEL)
    ps = _inproj(xs, _pack_w_in(w["w_in"], True), n_s, cfg["tn_in"])
    sink_mat = jnp.broadcast_to(sinks.reshape(N_KV_HEADS, group).T.reshape(N_Q_HEADS, 1), (N_Q_HEADS, LANES))
    win = cache_k_win.shape[1]
    attn_s, nk_s, nv_s = _attn_sample(ps, cache_k_win.reshape(n_s, win, KV_WIDTH),
                                      cache_v_win.reshape(n_s, win, KV_WIDTH), sink_mat, cfg["bt_sample"])
    shift_l = jnp.pad(state_shift[:, 3 * RWKV_WIDTH:], ((0, 0), (0, LORA_PAD - LORA_WIDTH)))
    rwkv_s, wkv_s = _rwkv_sample(ps, state_shift, shift_l, state_wkv, prm, cfg["bt_sample"])
    wpa_s = wpa.reshape(N_KV_HEADS, group, HEAD_DIM, D_MODEL).transpose(1, 0, 2, 3).reshape(ATTN_WIDTH, D_MODEL)
    hs = _merge(xs, attn_s, rwkv_s, ps, wpa_s, wpr, wo, ln1g, ln1b, n_s)
    ys = _ffn(hs, wu, wdn, ln2g, ln2b, n_s, cfg["tf_ffn"])
    shift_s = _shift_columns(ps)

    kv5 = lambda z: z.reshape(1, z.shape[0], z.shape[1], N_KV_HEADS, HEAD_DIM)
    return (yp.reshape(b, t, D_MODEL), ys.reshape(n_s, 1, D_MODEL),
            kv5(kwin_p), kv5(vwin_p), shift_p[None], wkv_p[None],
            kv5(nk_s), kv5(nv_s), shift_s[None], wkv_s[None])


_CFG = dict(tm_in=1024, tn_in=1024, tb_rwkv=128, lw_rwkv=1024, tm_merge=256, tm_ffn=512, tf_ffn=1024, bt_sample=8)


def kernel(x_prompt, x_sample, cache_k_win, cache_v_win, state_shift, state_wkv, w_in, attn_sinks, mu_shift, w0,
           w_decay_up, a0, w_iclr_up, w_gate_up, k_k, k_a, r_k, gn_w, gn_b, w_proj_attn, w_proj_rwkv, w_out,
           ln1_g, ln1_b, w_up, w_down, ln2_g, ln2_b):
    w = dict(w_in=w_in[0], attn_sinks=attn_sinks[0], mu_shift=mu_shift[0], w0=w0[0], w_decay_up=w_decay_up[0],
             a0=a0[0], w_iclr_up=w_iclr_up[0], w_gate_up=w_gate_up[0], k_k=k_k[0], k_a=k_a[0], r_k=r_k[0],
             gn_w=gn_w[0], gn_b=gn_b[0], w_proj_attn=w_proj_attn[0], w_proj_rwkv=w_proj_rwkv[0], w_out=w_out[0],
             ln1_g=ln1_g[0], ln1_b=ln1_b[0], w_up=w_up[0], w_down=w_down[0], ln2_g=ln2_g[0], ln2_b=ln2_b[0])
    return _forward(x_prompt, x_sample, cache_k_win[0], cache_v_win[0], state_shift[0], state_wkv[0], w, _CFG)
```

```python
import functools

import jax
import jax.numpy as jnp
import numpy as np
from jax import lax
from jax.experimental import pallas as pl
from jax.experimental.pallas import tpu as pltpu

F32 = jnp.float32
BF16 = jnp.bfloat16

D_MODEL = 2048
HEAD_DIM = 64
N_Q_HEADS = 16
N_KV_HEADS = 4
ATTN_WIDTH = N_Q_HEADS * HEAD_DIM
KV_WIDTH = N_KV_HEADS * HEAD_DIM
WINDOW = 128
ROPE_THETA = 500000.0
ROT_DIM = HEAD_DIM // 4
N_RWKV_HEADS = 16
RWKV_WIDTH = N_RWKV_HEADS * HEAD_DIM
DECAY_LORA = 64
ICLR_LORA = 64
GATE_LORA = 160
LORA_WIDTH = DECAY_LORA + ICLR_LORA + GATE_LORA
RWKV_PROJ_WIDTH = 3 * RWKV_WIDTH + LORA_WIDTH
D_FF = 4 * D_MODEL
PAST_LEN = 16384
DEEPNORM_ALPHA = 2.0 ** 0.25
LN_EPS = 1e-5
GN_EPS = HEAD_DIM * 1e-5
NEG_BIG = -1e30

LANES = 128
UNITS_PER_GROUP = 32
CHUNK = 64
VMEM_LIMIT = 56 * 1024 * 1024

COL_Q = 0
COL_R = 1024
COL_KR = 2048
COL_VR = 3072
COL_K = 4096
COL_V = 4352
COL_LORA = 4608
LORA_PAD = 512
PACK_WIDTH = COL_LORA + LORA_PAD


def _cparams(n_axes):
    return pltpu.CompilerParams(dimension_semantics=("arbitrary",) * n_axes, vmem_limit_bytes=VMEM_LIMIT)


def _sigmoid(x):
    return 1.0 / (1.0 + jnp.exp(-x))


def _softplus(x):
    return jnp.maximum(x, 0.0) + jnp.log(1.0 + jnp.exp(-jnp.abs(x)))


def _layer_norm_rows(z, g, b):
    mu = jnp.mean(z, axis=-1, keepdims=True)
    d = z - mu
    var = jnp.mean(d * d, axis=-1, keepdims=True)
    return d * lax.rsqrt(var + LN_EPS) * g + b


def _split_bf16(x):
    hi = x.astype(BF16)
    lo = (x - hi.astype(F32)).astype(BF16)
    return hi, lo


def _head_block_ones():
    r = lax.broadcasted_iota(jnp.int32, (2 * LANES, LANES), 0)
    c = lax.broadcasted_iota(jnp.int32, (2 * LANES, LANES), 1)
    return jnp.where((r % LANES) // HEAD_DIM == c // HEAD_DIM, 1.0, 0.0).astype(BF16)


def _head_sum(x, bd2):
    outs = []
    for t in range(x.shape[1] // LANES):
        hi, lo = _split_bf16(x[:, LANES * t:LANES * (t + 1)])
        outs.append(jnp.dot(jnp.concatenate([hi, lo], axis=1), bd2, preferred_element_type=F32))
    return outs[0] if len(outs) == 1 else jnp.concatenate(outs, axis=1)


def _inproj_kernel(x_ref, w_ref, o_ref, xb_ref):
    @pl.when(pl.program_id(1) == 0)
    def _():
        xb_ref[...] = x_ref[...].astype(BF16)

    o_ref[...] = jnp.dot(xb_ref[...], w_ref[...], preferred_element_type=F32)


def _inproj(x, w_packed, tm, tn):
    m, k = x.shape
    n = w_packed.shape[1]
    return pl.pallas_call(
        _inproj_kernel,
        grid=(m // tm, n // tn),
        in_specs=[pl.BlockSpec((tm, k), lambda i, j: (i, 0)),
                  pl.BlockSpec((k, tn), lambda i, j: (0, j))],
        out_specs=pl.BlockSpec((tm, tn), lambda i, j: (i, j)),
        out_shape=jax.ShapeDtypeStruct((m, n), F32),
        scratch_shapes=[pltpu.VMEM((tm, k), BF16)],
        compiler_params=_cparams(2),
        name="inproj",
    )(x, w_packed)


def _rope_tables(pos):
    inv = ROPE_THETA ** (-jnp.arange(0, ROT_DIM, 2, dtype=F32) / ROT_DIM)
    ang = pos.astype(F32)[:, None] * inv[None, :]
    cos, sin = jnp.cos(ang), jnp.sin(ang)
    t = pos.shape[0]
    half = ROT_DIM // 2
    pad = HEAD_DIM - ROT_DIM
    c_head = jnp.concatenate([cos, cos, jnp.ones((t, pad), F32)], axis=1)
    sp_head = jnp.concatenate([-sin, jnp.zeros((t, half + pad), F32)], axis=1)
    sm_head = jnp.concatenate([jnp.zeros((t, half), F32), sin, jnp.zeros((t, pad), F32)], axis=1)
    rep = LANES // HEAD_DIM
    return tuple(jnp.tile(z, (1, rep)) for z in (c_head, sp_head, sm_head))


def _rope_tile(x, c, sp, sm):
    half = ROT_DIM // 2
    return x * c + pltpu.roll(x, LANES - half, 1) * sp + pltpu.roll(x, half, 1) * sm


def _attn_prompt_kernel(q_ref, k_ref, v_ref, c_ref, sp_ref, sm_ref, sink_ref,
                        o_ref, kwin_ref, vwin_ref, kp_ref, vp_ref):
    i = pl.program_id(1)
    blk = WINDOW

    @pl.when(i == 0)
    def _():
        kp_ref[...] = jnp.zeros_like(kp_ref)
        vp_ref[...] = jnp.zeros_like(vp_ref)

    c, sp, sm = c_ref[...], sp_ref[...], sm_ref[...]
    lane = lax.broadcasted_iota(jnp.int32, (2 * blk, LANES), 1)
    lo2 = lane < HEAD_DIM
    lo1 = lax.broadcasted_iota(jnp.int32, (blk, LANES), 1) < HEAD_DIM

    qi = lax.broadcasted_iota(jnp.int32, (blk, 2 * blk), 0)
    kj = lax.broadcasted_iota(jnp.int32, (blk, 2 * blk), 1)
    kmin = jnp.where(i == 0, blk, 0)
    valid = (kj > qi) & (kj <= qi + blk) & (kj >= kmin)

    n_kv_tiles = KV_WIDTH // LANES
    kk_g, va_g, vb_g = [], [], []
    for t in range(n_kv_tiles):
        sl = slice(LANES * t, LANES * (t + 1))
        kcur = _rope_tile(k_ref[:, sl], c, sp, sm)
        vcur = v_ref[:, sl]
        kwin_ref[0, :, sl] = kcur
        vwin_ref[0, :, sl] = vcur
        kall = jnp.concatenate([kp_ref[:, sl], kcur], axis=0)
        vall = jnp.concatenate([vp_ref[:, sl], vcur], axis=0)
        kp_ref[:, sl] = kcur
        vp_ref[:, sl] = vcur
        kswap = pltpu.roll(kall, HEAD_DIM, 1)
        vswap = pltpu.roll(vall, HEAD_DIM, 1)
        kk_g += [jnp.where(lo2, kall, kswap).astype(BF16), jnp.where(lo2, kswap, kall).astype(BF16)]
        va_g += [jnp.where(lo2, vall, 0.0).astype(BF16), jnp.where(lo2, vswap, 0.0).astype(BF16)]
        vb_g += [jnp.where(lo2, 0.0, vswap).astype(BF16), jnp.where(lo2, 0.0, vall).astype(BF16)]

    group = N_Q_HEADS // N_KV_HEADS
    for j in range(ATTN_WIDTH // LANES):
        sl = slice(LANES * j, LANES * (j + 1))
        g = (2 * j) // group
        qt = _rope_tile(q_ref[:, sl], c, sp, sm) * (HEAD_DIM ** -0.5)
        o_tile = None
        for hh in range(2):
            sink = sink_ref[2 * j + hh]
            qm = (jnp.where(lo1, qt, 0.0) if hh == 0 else jnp.where(lo1, 0.0, qt)).astype(BF16)
            s = lax.dot_general(qm, kk_g[g], (((1,), (1,)), ((), ())), preferred_element_type=F32)
            s = jnp.where(valid, s, NEG_BIG)
            m = jnp.maximum(jnp.max(s, axis=1, keepdims=True), sink)
            p = jnp.exp(s - m)
            den = jnp.sum(p, axis=1, keepdims=True) + jnp.exp(sink - m)
            vmat = va_g[g] if hh == 0 else vb_g[g]
            o_h = jnp.dot(p.astype(BF16), vmat, preferred_element_type=F32) / den
            o_tile = o_h if o_tile is None else o_tile + o_h
        o_ref[:, sl] = o_tile.astype(o_ref.dtype)


def _attn_prompt(p_all, sinks, b, t):
    blk = WINDOW
    nb = t // blk
    c, sp, sm = _rope_tables(jnp.arange(t, dtype=jnp.int32))
    tab_spec = pl.BlockSpec((blk, LANES), lambda bi, i: (i, 0))
    row = lambda bi, i: bi * nb + i
    return pl.pallas_call(
        _attn_prompt_kernel,
        grid=(b, nb),
        in_specs=[pl.BlockSpec((blk, ATTN_WIDTH), lambda bi, i: (row(bi, i), COL_Q // ATTN_WIDTH)),
                  pl.BlockSpec((blk, KV_WIDTH), lambda bi, i: (row(bi, i), COL_K // KV_WIDTH)),
                  pl.BlockSpec((blk, KV_WIDTH), lambda bi, i: (row(bi, i), COL_V // KV_WIDTH)),
                  tab_spec, tab_spec, tab_spec,
                  pl.BlockSpec(memory_space=pltpu.SMEM)],
        out_specs=[pl.BlockSpec((blk, ATTN_WIDTH), lambda bi, i: (row(bi, i), 0)),
                   pl.BlockSpec((1, blk, KV_WIDTH), lambda bi, i: (bi, 0, 0)),
                   pl.BlockSpec((1, blk, KV_WIDTH), lambda bi, i: (bi, 0, 0))],
        out_shape=[jax.ShapeDtypeStruct((b * t, ATTN_WIDTH), BF16),
                   jax.ShapeDtypeStruct((b, blk, KV_WIDTH), F32),
                   jax.ShapeDtypeStruct((b, blk, KV_WIDTH), F32)],
        scratch_shapes=[pltpu.VMEM((blk, KV_WIDTH), F32), pltpu.VMEM((blk, KV_WIDTH), F32)],
        compiler_params=_cparams(2),
        name="attn_prompt",
    )(p_all, p_all, p_all, c, sp, sm, sinks)


def _rwkv_prepare(r_in, k_in, v_in, l_in, pr, pk, pv, plr, prm, wd_ref, wi_ref, wg_ref, bd2):
    mur, muk, muv, mul, w0, a0, k_k, k_a = prm
    r = r_in + (pr - r_in) * mur
    k = k_in + (pk - k_in) * muk
    v = v_in + (pv - v_in) * muv
    ls = l_in + (plr - l_in) * mul
    l0 = ls[:, :LANES]
    lane = lax.broadcasted_iota(jnp.int32, l0.shape, 1)
    z0 = jnp.where(lane < DECAY_LORA, jnp.tanh(l0), l0).astype(BF16)
    zg = _sigmoid(ls[:, LANES:3 * LANES]).astype(BF16)
    dec_up = jnp.dot(z0, wd_ref[...], preferred_element_type=F32)
    icl_up = jnp.dot(z0, wi_ref[...], preferred_element_type=F32)
    gate = jnp.dot(zg, wg_ref[...], preferred_element_type=F32)
    w_log = -_softplus(-(w0 + dec_up)) - 0.5
    logw = -jnp.exp(w_log)
    a_sig = _sigmoid(a0 + icl_up)
    kk = k * k_k
    nrm = jnp.maximum(jnp.sqrt(_head_sum(kk * kk, bd2)), 1e-12)
    kk = kk / nrm
    k = k * (1.0 + (a_sig - 1.0) * k_a)
    return r, k, v, logw, -kk, kk * a_sig, gate


def _rwkv_finish(y, r, k, v, gate, r_k, gn_w, gn_b, bd2):
    mu = _head_sum(y, bd2) * (1.0 / HEAD_DIM)
    d = y - mu
    var = _head_sum(d * d, bd2) * (1.0 / HEAD_DIM)
    yn = d * lax.rsqrt(var + GN_EPS) * gn_w + gn_b
    bonus = _head_sum(r * k * r_k, bd2) * v
    return (yn + bonus) * gate


def _rwkv_prompt_kernel(r_ref, k_ref, v_ref, l_ref,
                        mur_ref, muk_ref, muv_ref, mul_ref, w0_ref, a0_ref, kk_ref, ka_ref,
                        rk_ref, gnw_ref, gnb_ref, wd_ref, wi_ref, wg_ref,
                        o_ref, sout_ref,
                        cr_ref, ck_ref, cv_ref, cl_ref, st_ref, y_ref):
    ti = pl.program_id(2)
    tb, lw = r_ref.shape
    n_pair = lw // LANES
    n_chunk = tb // CHUNK

    @pl.when(ti == 0)
    def _():
        for ref in (cr_ref, ck_ref, cv_ref, cl_ref, st_ref):
            ref[...] = jnp.zeros_like(ref)

    def shifted(x, carry_ref):
        rolled = pltpu.roll(x, 1, 0)
        row = lax.broadcasted_iota(jnp.int32, x.shape, 0)
        prev = jnp.where(row == 0, carry_ref[0:1, :], rolled)
        carry_ref[0:1, :] = x[tb - 1:tb, :]
        return prev

    bd2 = _head_block_ones()
    r_in, k_in, v_in, l_in = r_ref[...], k_ref[...], v_ref[...], l_ref[...]
    prm = tuple(ref[...] for ref in (mur_ref, muk_ref, muv_ref, mul_ref, w0_ref, a0_ref, kk_ref, ka_ref))
    r, k, v, logw, a_s, b_s, gate = _rwkv_prepare(
        r_in, k_in, v_in, l_in,
        shifted(r_in, cr_ref), shifted(k_in, ck_ref), shifted(v_in, cv_ref), shifted(l_in, cl_ref),
        prm, wd_ref, wi_ref, wg_ref, bd2)

    c = CHUNK
    ri = lax.broadcasted_iota(jnp.int32, (c, 3 * c), 0)
    ci = lax.broadcasted_iota(jnp.int32, (c, 3 * c), 1) % c
    tri3 = jnp.where(ri >= ci, 1.0, 0.0).astype(BF16)
    r2 = lax.broadcasted_iota(jnp.int32, (2 * c, LANES), 0)
    l2 = lax.broadcasted_iota(jnp.int32, (2 * c, LANES), 1)
    tt, ss = r2 % c, l2 % c
    causal = ss < tt + r2 // c
    lo1 = lax.broadcasted_iota(jnp.int32, (c, LANES), 1) < HEAD_DIM
    lo2 = l2 < HEAD_DIM
    diag_blocks = (r2 // HEAD_DIM) == (l2 // HEAD_DIM)
    eye = jnp.where(r2 == l2, 1.0, 0.0)

    def stack_heads(x):
        return jnp.concatenate([jnp.where(lo1, x, 0.0), jnp.where(lo1, 0.0, x)], axis=0)

    def dot_t(a, b):
        return lax.dot_general(a, b, (((1,), (1,)), ((), ())), preferred_element_type=F32)

    def dot_tt(a, b):
        return lax.dot_general(a, b, (((0,), (0,)), ((), ())), preferred_element_type=F32)

    def dot(a, b):
        return jnp.dot(a.astype(BF16), b.astype(BF16), preferred_element_type=F32)

    pairs = range(n_pair)
    lanes = [slice(LANES * pi, LANES * (pi + 1)) for pi in pairs]
    for ch in range(n_chunk):
        rows = slice(c * ch, c * (ch + 1))
        lw_c = logw[rows]
        hi = lw_c.astype(BF16)
        rem = lw_c - hi.astype(F32)
        mid = rem.astype(BF16)
        low = (rem - mid.astype(F32)).astype(BF16)
        lcum = jnp.dot(tri3, jnp.concatenate([hi, mid, low], axis=0), preferred_element_type=F32)
        ltot = lcum[c - 1:c, :]
        p_inv = jnp.exp(-lcum)
        p_tail = jnp.exp(ltot - lcum)
        aq = a_s[rows] * jnp.exp(lcum - lw_c)
        rq = r[rows] * jnp.exp(lcum)
        bk = b_s[rows] * p_inv
        kq = k[rows] * p_inv
        bt = b_s[rows] * p_tail
        kt = k[rows] * p_tail
        p_end = jnp.exp(ltot)
        v_c = v[rows]

        ar = [jnp.concatenate([aq[:, sl], rq[:, sl]], axis=0).astype(BF16) for sl in lanes]
        gb = [jnp.where(causal, dot_t(ar[pi], stack_heads(bk[:, lanes[pi]]).astype(BF16)), 0.0) for pi in pairs]
        gk = [jnp.where(causal, dot_t(ar[pi], stack_heads(kq[:, lanes[pi]]).astype(BF16)), 0.0) for pi in pairs]
        gv = [dot(gk[pi], stack_heads(v_c[:, lanes[pi]])) for pi in pairs]
        pw = [stack_heads(gb[pi][:c]) for pi in pairs]
        tm = [eye + pw[pi] for pi in pairs]
        for _ in range(5):
            pw = [dot(pw[pi], pw[pi]) for pi in pairs]
            tm = [tm[pi] + dot(tm[pi], pw[pi]) for pi in pairs]
        t_side = [tm[pi][:c] + tm[pi][c:] for pi in pairs]
        tax = [dot(t_side[pi], jnp.concatenate([stack_heads(aq[:, lanes[pi]]), stack_heads(gv[pi][:c])], axis=1))
               for pi in pairs]
        taq = [tax[pi][:, :LANES] for pi in pairs]
        txv = [tax[pi][:, LANES:] for pi in pairs]
        arx = [dot(gb[pi][c:], jnp.concatenate([stack_heads(taq[pi]), stack_heads(txv[pi])], axis=1)) for pi in pairs]
        mb = [jnp.where(diag_blocks, dot_tt(bt[:, lanes[pi]].astype(BF16), taq[pi].astype(BF16)), 0.0) for pi in pairs]
        cct = [jnp.where(diag_blocks,
                         dot_tt(jnp.concatenate([txv[pi], v_c[:, lanes[pi]]], axis=0).astype(BF16),
                                jnp.concatenate([bt[:, lanes[pi]], kt[:, lanes[pi]]], axis=0).astype(BF16)), 0.0)
               for pi in pairs]
        for pi in pairs:
            sl = lanes[pi]
            s_old = st_ref[pi]
            sb = s_old.astype(BF16)
            rqp = (rq[:, sl] + arx[pi][:, :LANES]).astype(BF16)
            y_ref[rows, sl] = dot_t(rqp, sb) + gv[pi][c:] + arx[pi][:, LANES:]
            st_ref[pi] = s_old * p_end[:, sl] + dot_t(sb, mb[pi].astype(BF16)) + cct[pi]

    out = _rwkv_finish(y_ref[...], r, k, v, gate, rk_ref[...], gnw_ref[...], gnb_ref[...], bd2)
    o_ref[...] = out.astype(o_ref.dtype)

    @pl.when(ti == pl.num_programs(2) - 1)
    def _():
        for pi in range(n_pair):
            s = st_ref[pi]
            sout_ref[0, 2 * pi] = s[:HEAD_DIM, :HEAD_DIM]
            sout_ref[0, 2 * pi + 1] = s[HEAD_DIM:, HEAD_DIM:]


def _rwkv_lora_weights(w_decay_up, w_iclr_up, w_gate_up):
    z64 = jnp.zeros((DECAY_LORA, RWKV_WIDTH), F32)
    wd = jnp.concatenate([w_decay_up, z64], axis=0).astype(BF16)
    wi = jnp.concatenate([z64, w_iclr_up], axis=0).astype(BF16)
    wg = jnp.concatenate([w_gate_up, jnp.zeros((2 * LANES - GATE_LORA, RWKV_WIDTH), F32)], axis=0).astype(BF16)
    return wd, wi, wg


def _rwkv_prompt(p_all, prm, b, t, tb, lw):
    nt = t // tb
    ns = RWKV_WIDTH // lw
    row = lambda bi, si, ti: bi * nt + ti
    col_spec = lambda col0: pl.BlockSpec((tb, lw), lambda bi, si, ti: (row(bi, si, ti), col0 // lw + si))
    vec = pl.BlockSpec((1, lw), lambda bi, si, ti: (0, si))
    vec_l = pl.BlockSpec((1, LORA_PAD), lambda bi, si, ti: (0, 0))
    return pl.pallas_call(
        _rwkv_prompt_kernel,
        grid=(b, ns, nt),
        in_specs=[col_spec(COL_R), col_spec(COL_KR), col_spec(COL_VR),
                  pl.BlockSpec((tb, LORA_PAD), lambda bi, si, ti: (row(bi, si, ti), COL_LORA // LORA_PAD)),
                  vec, vec, vec, vec_l, vec, vec, vec, vec, vec, vec, vec,
                  pl.BlockSpec((LANES, lw), lambda bi, si, ti: (0, si)),
                  pl.BlockSpec((LANES, lw), lambda bi, si, ti: (0, si)),
                  pl.BlockSpec((2 * LANES, lw), lambda bi, si, ti: (0, si))],
        out_specs=[pl.BlockSpec((tb, lw), lambda bi, si, ti: (row(bi, si, ti), si)),
                   pl.BlockSpec((1, 2 * (lw // LANES), HEAD_DIM, HEAD_DIM), lambda bi, si, ti: (bi, si, 0, 0))],
        out_shape=[jax.ShapeDtypeStruct((b * t, RWKV_WIDTH), BF16),
                   jax.ShapeDtypeStruct((b, N_RWKV_HEADS, HEAD_DIM, HEAD_DIM), F32)],
        scratch_shapes=[pltpu.VMEM((8, lw), F32), pltpu.VMEM((8, lw), F32), pltpu.VMEM((8, lw), F32),
                        pltpu.VMEM((8, LORA_PAD), F32),
                        pltpu.VMEM((lw // LANES, LANES, LANES), F32),
                        pltpu.VMEM((tb, lw), F32)],
        compiler_params=_cparams(3),
        name="rwkv_prompt",
    )(p_all, p_all, p_all, p_all,
      prm["mu_r"], prm["mu_k"], prm["mu_v"], prm["mu_l"], prm["w0"], prm["a0"], prm["k_k"], prm["k_a"],
      prm["r_k"], prm["gn_w"], prm["gn_b"], prm["wd"], prm["wi"], prm["wg"])


def _merge_kernel(x_ref, ao_ref, ro_ref, wga_ref, wgb_ref, wpa_ref, wpr_ref, wo_ref, g_ref, b_ref, h_ref,
                  xb_ref, acc_ref):
    j = pl.program_id(1)

    @pl.when(j == 0)
    def _():
        xb_ref[...] = x_ref[...].astype(BF16)
        acc_ref[...] = jnp.zeros_like(acc_ref)

    xb = xb_ref[...]
    ga = jnp.dot(xb, wga_ref[...], preferred_element_type=F32)
    gb = jnp.dot(xb, wgb_ref[...], preferred_element_type=F32)
    a = jnp.dot(ao_ref[...], wpa_ref[...], preferred_element_type=F32)
    r = jnp.dot(ro_ref[...], wpr_ref[...], preferred_element_type=F32)
    m = _sigmoid(ga) * a + _sigmoid(gb) * r
    acc_ref[...] += jnp.dot(m.astype(BF16), wo_ref[...], preferred_element_type=F32)

    @pl.when(j == pl.num_programs(1) - 1)
    def _():
        z = DEEPNORM_ALPHA * x_ref[...] + acc_ref[...]
        h_ref[...] = _layer_norm_rows(z, g_ref[...], b_ref[...])


def _merge(x, attn_o, rwkv_o, w_gates, wpa, wpr, wo, ln_g, ln_b, tm, tj):
    m = x.shape[0]
    nj = D_MODEL // tj
    return pl.pallas_call(
        _merge_kernel,
        grid=(m // tm, nj),
        in_specs=[pl.BlockSpec((tm, D_MODEL), lambda i, j: (i, 0)),
                  pl.BlockSpec((tm, ATTN_WIDTH), lambda i, j: (i, 0)),
                  pl.BlockSpec((tm, RWKV_WIDTH), lambda i, j: (i, 0)),
                  pl.BlockSpec((D_MODEL, tj), lambda i, j: (0, j)),
                  pl.BlockSpec((D_MODEL, tj), lambda i, j: (0, nj + j)),
                  pl.BlockSpec((ATTN_WIDTH, tj), lambda i, j: (0, j)),
                  pl.BlockSpec((RWKV_WIDTH, tj), lambda i, j: (0, j)),
                  pl.BlockSpec((tj, D_MODEL), lambda i, j: (j, 0)),
                  pl.BlockSpec((1, D_MODEL), lambda i, j: (0, 0)),
                  pl.BlockSpec((1, D_MODEL), lambda i, j: (0, 0))],
        out_specs=pl.BlockSpec((tm, D_MODEL), lambda i, j: (i, 0)),
        out_shape=jax.ShapeDtypeStruct((m, D_MODEL), F32),
        scratch_shapes=[pltpu.VMEM((tm, D_MODEL), BF16), pltpu.VMEM((tm, D_MODEL), F32)],
        compiler_params=_cparams(2),
        name="merge_ln1",
    )(x, attn_o, rwkv_o, w_gates, w_gates, wpa, wpr, wo, ln_g, ln_b)


def _ffn_kernel(h_ref, wu_ref, wd_ref, g_ref, b_ref, y_ref, hb_ref, acc_ref):
    f = pl.program_id(1)

    @pl.when(f == 0)
    def _():
        hb_ref[...] = h_ref[...].astype(BF16)
        acc_ref[...] = jnp.zeros_like(acc_ref)

    u = jnp.dot(hb_ref[...], wu_ref[...], preferred_element_type=F32)
    u = jnp.square(jnp.maximum(u, 0.0))
    acc_ref[...] += jnp.dot(u.astype(BF16), wd_ref[...], preferred_element_type=F32)

    @pl.when(f == pl.num_programs(1) - 1)
    def _():
        z = DEEPNORM_ALPHA * h_ref[...] + acc_ref[...]
        y_ref[...] = _layer_norm_rows(z, g_ref[...], b_ref[...])


def _ffn(h, wu, wd, ln_g, ln_b, tm, tf):
    m = h.shape[0]
    return pl.pallas_call(
        _ffn_kernel,
        grid=(m // tm, D_FF // tf),
        in_specs=[pl.BlockSpec((tm, D_MODEL), lambda i, f: (i, 0)),
                  pl.BlockSpec((D_MODEL, tf), lambda i, f: (0, f)),
                  pl.BlockSpec((tf, D_MODEL), lambda i, f: (f, 0)),
                  pl.BlockSpec((1, D_MODEL), lambda i, f: (0, 0)),
                  pl.BlockSpec((1, D_MODEL), lambda i, f: (0, 0))],
        out_specs=pl.BlockSpec((tm, D_MODEL), lambda i, f: (i, 0)),
        out_shape=jax.ShapeDtypeStruct((m, D_MODEL), F32),
        scratch_shapes=[pltpu.VMEM((tm, D_MODEL), BF16), pltpu.VMEM((tm, D_MODEL), F32)],
        compiler_params=_cparams(2),
        name="ffn_ln2",
    )(h, wu, wd, ln_g, ln_b)


def _attn_sample_kernel(q_ref, k_ref, v_ref, ck_ref, cv_ref, c_ref, sp_ref, sm_ref, sink_ref,
                        o_ref, nk_ref, nv_ref):
    bt = q_ref.shape[0]
    win = ck_ref.shape[1]
    group = N_Q_HEADS // N_KV_HEADS
    c, sp, sm = c_ref[0:1, :], sp_ref[0:1, :], sm_ref[0:1, :]
    sink = sink_ref[:, 0:1]
    row16 = lax.broadcasted_iota(jnp.int32, (N_Q_HEADS, KV_WIDTH), 0)
    lane16 = lax.broadcasted_iota(jnp.int32, (N_Q_HEADS, KV_WIDTH), 1)
    own_kv = (lane16 // HEAD_DIM) == (row16 % N_KV_HEADS)
    urow = row16 // N_KV_HEADS
    keyi = lax.broadcasted_iota(jnp.int32, (N_Q_HEADS, win), 1)
    rowk = lax.broadcasted_iota(jnp.int32, (win, KV_WIDTH), 0)

    def rope_row(x):
        return jnp.concatenate([_rope_tile(x[:, LANES * t:LANES * (t + 1)], c, sp, sm)
                                for t in range(x.shape[1] // LANES)], axis=1)

    out_rows = []
    for b in range(bt):
        q = rope_row(q_ref[b:b + 1, :]) * (HEAD_DIM ** -0.5)
        knew = rope_row(k_ref[b:b + 1, :])
        vnew = v_ref[b:b + 1, :]
        qb = [jnp.broadcast_to(q[:, KV_WIDTH * u:KV_WIDTH * (u + 1)], (N_Q_HEADS, KV_WIDTH)) for u in range(group)]
        qsel = jnp.where(urow == 0, qb[0], jnp.where(urow == 1, qb[1], jnp.where(urow == 2, qb[2], qb[3])))
        qmat = jnp.where(own_kv, qsel, 0.0)
        kc = ck_ref[b]
        vc = cv_ref[b]
        s = lax.dot_general(qmat.astype(BF16), kc.astype(BF16), (((1,), (1,)), ((), ())), preferred_element_type=F32)
        s = jnp.where(keyi >= 1, s, NEG_BIG)
        snew = jnp.sum(qmat * knew, axis=1, keepdims=True)
        m = jnp.maximum(jnp.maximum(jnp.max(s, axis=1, keepdims=True), snew), sink)
        p = jnp.exp(s - m)
        pn = jnp.exp(snew - m)
        den = jnp.sum(p, axis=1, keepdims=True) + pn + jnp.exp(sink - m)
        o = (jnp.dot(p.astype(BF16), vc.astype(BF16), preferred_element_type=F32) + pn * vnew) / den
        o = jnp.where(own_kv, o, 0.0)
        chunks = [jnp.sum(jnp.where(urow == u, o, 0.0), axis=0, keepdims=True) for u in range(group)]
        out_rows.append(jnp.concatenate(chunks, axis=1))
        nk_ref[b] = jnp.where(rowk == win - 1, knew, pltpu.roll(kc, win - 1, 0))
        nv_ref[b] = jnp.where(rowk == win - 1, vnew, pltpu.roll(vc, win - 1, 0))
    o_ref[...] = jnp.concatenate(out_rows, axis=0).astype(o_ref.dtype)


def _attn_sample(q_perm, p_all, cache_k, cache_v, sink_mat, bt):
    n, win = cache_k.shape[0], cache_k.shape[1]
    c, sp, sm = (jnp.broadcast_to(z, (8, LANES)) for z in _rope_tables(jnp.full((1,), PAST_LEN, jnp.int32)))
    small = lambda shape: pl.BlockSpec(shape, lambda i: (0, 0))
    cache_spec = pl.BlockSpec((bt, win, KV_WIDTH), lambda i: (i, 0, 0))
    return pl.pallas_call(
        _attn_sample_kernel,
        grid=(n // bt,),
        in_specs=[pl.BlockSpec((bt, ATTN_WIDTH), lambda i: (i, 0)),
                  pl.BlockSpec((bt, KV_WIDTH), lambda i: (i, COL_K // KV_WIDTH)),
                  pl.BlockSpec((bt, KV_WIDTH), lambda i: (i, COL_V // KV_WIDTH)),
                  cache_spec, cache_spec,
                  small((8, LANES)), small((8, LANES)), small((8, LANES)), small((N_Q_HEADS, LANES))],
        out_specs=[pl.BlockSpec((bt, ATTN_WIDTH), lambda i: (i, 0)), cache_spec, cache_spec],
        out_shape=[jax.ShapeDtypeStruct((n, ATTN_WIDTH), BF16),
                   jax.ShapeDtypeStruct(cache_k.shape, F32),
                   jax.ShapeDtypeStruct(cache_v.shape, F32)],
        compiler_params=_cparams(1),
        name="attn_sample",
    )(q_perm, p_all, p_all, cache_k, cache_v, c, sp, sm, sink_mat)


def _rwkv_sample_kernel(r_ref, k_ref, v_ref, l_ref, pr_ref, pk_ref, pv_ref, pl_ref, st_ref,
                        mur_ref, muk_ref, muv_ref, mul_ref, w0_ref, a0_ref, kk_ref, ka_ref,
                        rk_ref, gnw_ref, gnb_ref, wd_ref, wi_ref, wg_ref,
                        o_ref, ns_ref, rows_s, y_s):
    bt = r_ref.shape[0]
    bd2 = _head_block_ones()
    prm = tuple(ref[...] for ref in (mur_ref, muk_ref, muv_ref, mul_ref, w0_ref, a0_ref, kk_ref, ka_ref))
    r, k, v, logw, a_s, b_s, gate = _rwkv_prepare(
        r_ref[...], k_ref[...], v_ref[...], l_ref[...], pr_ref[...], pk_ref[...], pv_ref[...], pl_ref[...],
        prm, wd_ref, wi_ref, wg_ref, bd2)
    w = jnp.exp(logw)
    quantities = (a_s, w * r, v, w, b_s, k, _head_sum(b_s * r, bd2), _head_sum(k * r, bd2))
    for qi, x in enumerate(quantities):
        rows_s[qi, 0] = x
        rows_s[qi, 1] = pltpu.roll(x, x.shape[1] - HEAD_DIM, 1)
    y_s[...] = jnp.zeros_like(y_s)
    q_a, q_wr, q_v, q_w, q_b, q_k, q_beta, q_kappa = range(len(quantities))
    hd = HEAD_DIM
    eye = lax.broadcasted_iota(jnp.int32, (hd, hd), 0) == lax.broadcasted_iota(jnp.int32, (hd, hd), 1)

    def vec(qi, b, h):
        t, par = divmod(h, 2)
        return rows_s[qi, par, b:b + 1, LANES * t:LANES * t + hd]

    lane_sum = lambda x: jnp.sum(x, axis=1, keepdims=True)
    units = [(b, h) for b in range(bt) for h in range(N_RWKV_HEADS)]
    for g0 in range(0, len(units), UNITS_PER_GROUP):
        grp = units[g0:g0 + UNITS_PER_GROUP]
        s_old = [st_ref[b, h] for b, h in grp]
        sa = [lane_sum(s * vec(q_a, b, h)) for s, (b, h) in zip(s_old, grp)]
        swr = [lane_sum(s * vec(q_wr, b, h)) for s, (b, h) in zip(s_old, grp)]
        v_col = [lane_sum(jnp.where(eye, vec(q_v, b, h), 0.0)) for b, h in grp]
        for i, (b, h) in enumerate(grp):
            ns_ref[b, h] = s_old[i] * vec(q_w, b, h) + sa[i] * vec(q_b, b, h) + v_col[i] * vec(q_k, b, h)
            t_mat = swr[i] + sa[i] * vec(q_beta, b, h)
            y_row = jnp.sum(jnp.where(eye, t_mat, 0.0), axis=0, keepdims=True) + vec(q_v, b, h) * vec(q_kappa, b, h)
            t, par = divmod(h, 2)
            y_s[par, b:b + 1, LANES * t:LANES * t + hd] = y_row
    lane = lax.broadcasted_iota(jnp.int32, r.shape, 1)
    y = jnp.where(lane % LANES < hd, y_s[0], pltpu.roll(y_s[1], hd, 1))
    out = _rwkv_finish(y, r, k, v, gate, rk_ref[...], gnw_ref[...], gnb_ref[...], bd2)
    o_ref[...] = out.astype(o_ref.dtype)


def _rwkv_sample(p_all, shift, shift_l, state, prm, bt):
    n = state.shape[0]
    wide = lambda col0: pl.BlockSpec((bt, RWKV_WIDTH), lambda i: (i, col0 // RWKV_WIDTH))
    vec = pl.BlockSpec((1, RWKV_WIDTH), lambda i: (0, 0))
    vec_l = pl.BlockSpec((1, LORA_PAD), lambda i: (0, 0))
    st_spec = pl.BlockSpec((bt, N_RWKV_HEADS, HEAD_DIM, HEAD_DIM), lambda i: (i, 0, 0, 0))
    return pl.pallas_call(
        _rwkv_sample_kernel,
        grid=(n // bt,),
        in_specs=[wide(COL_R), wide(COL_KR), wide(COL_VR),
                  pl.BlockSpec((bt, LORA_PAD), lambda i: (i, COL_LORA // LORA_PAD)),
                  wide(0), wide(RWKV_WIDTH), wide(2 * RWKV_WIDTH),
                  pl.BlockSpec((bt, LORA_PAD), lambda i: (i, 0)),
                  st_spec,
                  vec, vec, vec, vec_l, vec, vec, vec, vec, vec, vec, vec,
                  pl.BlockSpec((LANES, RWKV_WIDTH), lambda i: (0, 0)),
                  pl.BlockSpec((LANES, RWKV_WIDTH), lambda i: (0, 0)),
                  pl.BlockSpec((2 * LANES, RWKV_WIDTH), lambda i: (0, 0))],
        out_specs=[pl.BlockSpec((bt, RWKV_WIDTH), lambda i: (i, 0)), st_spec],
        out_shape=[jax.ShapeDtypeStruct((n, RWKV_WIDTH), BF16), jax.ShapeDtypeStruct(state.shape, F32)],
        scratch_shapes=[pltpu.VMEM((8, 2, bt, RWKV_WIDTH), F32), pltpu.VMEM((2, bt, RWKV_WIDTH), F32)],
        compiler_params=_cparams(1),
        name="rwkv_sample",
    )(p_all, p_all, p_all, p_all, shift, shift, shift, shift_l, state,
      prm["mu_r"], prm["mu_k"], prm["mu_v"], prm["mu_l"], prm["w0"], prm["a0"], prm["k_k"], prm["k_a"],
      prm["r_k"], prm["gn_w"], prm["gn_b"], prm["wd"], prm["wi"], prm["wg"])


def _pack_w_in(w_in):
    c0 = ATTN_WIDTH
    q = w_in[:, :c0]
    k = w_in[:, c0:c0 + KV_WIDTH]
    v = w_in[:, c0 + KV_WIDTH:c0 + 2 * KV_WIDTH]
    c1 = c0 + 2 * KV_WIDTH
    rkv = w_in[:, c1:c1 + 3 * RWKV_WIDTH]
    lora = w_in[:, c1 + 3 * RWKV_WIDTH:c1 + RWKV_PROJ_WIDTH]
    c2 = c1 + RWKV_PROJ_WIDTH
    gates = w_in[:, c2:c2 + 2 * D_MODEL]
    pad = jnp.zeros((D_MODEL, LORA_PAD - LORA_WIDTH), w_in.dtype)
    return jnp.concatenate([q, rkv, k, v, lora, pad], axis=1).astype(BF16), gates.astype(BF16)


def _swap_head_order(z, outer, inner):
    n = z.shape[0]
    return z.reshape(n, outer, inner, HEAD_DIM).transpose(0, 2, 1, 3).reshape(n, outer * inner * HEAD_DIM)


def _shift_columns(p_rows):
    return jnp.concatenate([p_rows[:, COL_R:COL_R + 3 * RWKV_WIDTH], p_rows[:, COL_LORA:COL_LORA + LORA_WIDTH]], axis=1)


def _forward(x_prompt, x_sample, cache_k_win, cache_v_win, state_shift, state_wkv, w, cfg):
    b, t, _ = x_prompt.shape
    n_s = x_sample.shape[0]
    row = lambda z: z.reshape(1, -1).astype(F32)
    mu = w["mu_shift"]
    wd, wi, wg = _rwkv_lora_weights(w["w_decay_up"], w["w_iclr_up"], w["w_gate_up"])
    prm = dict(
        mu_r=row(mu[:RWKV_WIDTH]), mu_k=row(mu[RWKV_WIDTH:2 * RWKV_WIDTH]), mu_v=row(mu[2 * RWKV_WIDTH:3 * RWKV_WIDTH]),
        mu_l=row(jnp.pad(mu[3 * RWKV_WIDTH:], (0, LORA_PAD - LORA_WIDTH))),
        w0=row(w["w0"]), a0=row(w["a0"]), k_k=row(w["k_k"]), k_a=row(w["k_a"]), r_k=row(w["r_k"]),
        gn_w=row(w["gn_w"]), gn_b=row(w["gn_b"]), wd=wd, wi=wi, wg=wg)
    wpa = w["w_proj_attn"].astype(BF16)
    wpr = w["w_proj_rwkv"].astype(BF16)
    wo = w["w_out"].astype(BF16)
    wu = w["w_up"].astype(BF16)
    wdn = w["w_down"].astype(BF16)
    ln1g, ln1b, ln2g, ln2b = row(w["ln1_g"]), row(w["ln1_b"]), row(w["ln2_g"]), row(w["ln2_b"])
    sinks = w["attn_sinks"].astype(F32)

    xp = x_prompt.reshape(b * t, D_MODEL)
    w_packed, w_gates = _pack_w_in(w["w_in"])
    pp = _inproj(xp, w_packed, cfg["tm_in"], cfg["tn_in"])
    attn_p, kwin_p, vwin_p = _attn_prompt(pp, sinks, b, t)
    rwkv_p, wkv_p = _rwkv_prompt(pp, prm, b, t, cfg["tb_rwkv"], cfg["lw_rwkv"])
    hp = _merge(xp, attn_p, rwkv_p, w_gates, wpa, wpr, wo, ln1g, ln1b, cfg["tm_merge"], cfg["tj_merge"])
    yp = _ffn(hp, wu, wdn, ln2g, ln2b, cfg["tm_ffn"], cfg["tf_ffn"])
    shift_p = _shift_columns(pp.reshape(b, t, PACK_WIDTH)[:, t - 1])

    group = N_Q_HEADS // N_KV_HEADS
    xs = x_sample.reshape(n_s, D_MODEL)
    ps = _inproj(xs, w_packed, n_s, cfg["tn_in"])
    q_perm = _swap_head_order(ps[:, COL_Q:COL_Q + ATTN_WIDTH], N_KV_HEADS, group)
    sink_mat = jnp.broadcast_to(sinks.reshape(N_KV_HEADS, group).T.reshape(N_Q_HEADS, 1), (N_Q_HEADS, LANES))
    win = cache_k_win.shape[1]
    attn_s, nk_s, nv_s = _attn_sample(q_perm, ps, cache_k_win.reshape(n_s, win, KV_WIDTH),
                                      cache_v_win.reshape(n_s, win, KV_WIDTH), sink_mat, cfg["bt_sample"])
    attn_s = _swap_head_order(attn_s, group, N_KV_HEADS)
    shift_l = jnp.pad(state_shift[:, 3 * RWKV_WIDTH:], ((0, 0), (0, LORA_PAD - LORA_WIDTH)))
    rwkv_s, wkv_s = _rwkv_sample(ps, state_shift, shift_l, state_wkv, prm, cfg["bt_sample"])
    hs = _merge(xs, attn_s, rwkv_s, w_gates, wpa, wpr, wo, ln1g, ln1b, n_s, cfg["tj_merge"])
    ys = _ffn(hs, wu, wdn, ln2g, ln2b, n_s, cfg["tf_ffn"])
    shift_s = _shift_columns(ps)

    kv5 = lambda z: z.reshape(1, z.shape[0], z.shape[1], N_KV_HEADS, HEAD_DIM)
    return (yp.reshape(b, t, D_MODEL), ys.reshape(n_s, 1, D_MODEL),
            kv5(kwin_p), kv5(vwin_p), shift_p[None], wkv_p[None],
            kv5(nk_s), kv5(nv_s), shift_s[None], wkv_s[None])


_CFG = dict(tm_in=1024, tn_in=1024, tb_rwkv=128, lw_rwkv=1024, tm_merge=512, tj_merge=512, tm_ffn=512, tf_ffn=1024, bt_sample=8)


def kernel(x_prompt, x_sample, cache_k_win, cache_v_win, state_shift, state_wkv, w_in, attn_sinks, mu_shift, w0,
           w_decay_up, a0, w_iclr_up, w_gate_up, k_k, k_a, r_k, gn_w, gn_b, w_proj_attn, w_proj_rwkv, w_out,
           ln1_g, ln1_b, w_up, w_down, ln2_g, ln2_b):
    w = dict(w_in=w_in[0], attn_sinks=attn_sinks[0], mu_shift=mu_shift[0], w0=w0[0], w_decay_up=w_decay_up[0],
             a0=a0[0], w_iclr_up=w_iclr_up[0], w_gate_up=w_gate_up[0], k_k=k_k[0], k_a=k_a[0], r_k=r_k[0],
             gn_w=gn_w[0], gn_b=gn_b[0], w_proj_attn=w_proj_attn[0], w_proj_rwkv=w_proj_rwkv[0], w_out=w_out[0],
             ln1_g=ln1_g[0], ln1_b=ln1_b[0], w_up=w_up[0], w_down=w_down[0], ln2_g=ln2_g[0], ln2_b=ln2_b[0])
    return _forward(x_prompt, x_sample, cache_k_win[0], cache_v_win[0], state_shift[0], state_wkv[0], w, _CFG)
```

```python
import functools

import jax
import jax.numpy as jnp
import numpy as np
from jax import lax
from jax.experimental import pallas as pl
from jax.experimental.pallas import tpu as pltpu

F32 = jnp.float32
BF16 = jnp.bfloat16

D_MODEL = 2048
HEAD_DIM = 64
N_Q_HEADS = 16
N_KV_HEADS = 4
ATTN_WIDTH = N_Q_HEADS * HEAD_DIM
KV_WIDTH = N_KV_HEADS * HEAD_DIM
WINDOW = 128
ROPE_THETA = 500000.0
ROT_DIM = HEAD_DIM // 4
N_RWKV_HEADS = 16
RWKV_WIDTH = N_RWKV_HEADS * HEAD_DIM
DECAY_LORA = 64
ICLR_LORA = 64
GATE_LORA = 160
LORA_WIDTH = DECAY_LORA + ICLR_LORA + GATE_LORA
RWKV_PROJ_WIDTH = 3 * RWKV_WIDTH + LORA_WIDTH
D_FF = 4 * D_MODEL
PAST_LEN = 16384
DEEPNORM_ALPHA = 2.0 ** 0.25
LN_EPS = 1e-5
GN_EPS = HEAD_DIM * 1e-5
NEG_BIG = -1e30

LANES = 128
CHUNK = 64
VMEM_LIMIT = 56 * 1024 * 1024

COL_Q = 0
COL_R = 1024
COL_KR = 2048
COL_VR = 3072
COL_K = 4096
COL_V = 4352
COL_LORA = 4608
LORA_PAD = 512
PACK_WIDTH = COL_LORA + LORA_PAD


def _cparams(n_axes):
    return pltpu.CompilerParams(dimension_semantics=("arbitrary",) * n_axes, vmem_limit_bytes=VMEM_LIMIT)


def _sigmoid(x):
    return 1.0 / (1.0 + jnp.exp(-x))


def _softplus(x):
    return jnp.maximum(x, 0.0) + jnp.log(1.0 + jnp.exp(-jnp.abs(x)))


def _layer_norm_rows(z, g, b):
    mu = jnp.mean(z, axis=-1, keepdims=True)
    d = z - mu
    var = jnp.mean(d * d, axis=-1, keepdims=True)
    return d * lax.rsqrt(var + LN_EPS) * g + b


def _dot_t(a, b):
    return lax.dot_general(a, b, (((1,), (1,)), ((), ())), preferred_element_type=F32)


def _split_bf16(x):
    hi = x.astype(BF16)
    lo = (x - hi.astype(F32)).astype(BF16)
    return hi, lo


def _head_block_ones():
    r = lax.broadcasted_iota(jnp.int32, (2 * LANES, LANES), 0)
    c = lax.broadcasted_iota(jnp.int32, (2 * LANES, LANES), 1)
    return jnp.where((r % LANES) // HEAD_DIM == c // HEAD_DIM, 1.0, 0.0).astype(BF16)


def _head_sum(x, bd2):
    outs = []
    for t in range(x.shape[1] // LANES):
        hi, lo = _split_bf16(x[:, LANES * t:LANES * (t + 1)])
        outs.append(jnp.dot(jnp.concatenate([hi, lo], axis=1), bd2, preferred_element_type=F32))
    return outs[0] if len(outs) == 1 else jnp.concatenate(outs, axis=1)


def _inproj_kernel(x_ref, w_ref, o_ref, xb_ref):
    @pl.when(pl.program_id(1) == 0)
    def _():
        xb_ref[...] = x_ref[...].astype(BF16)

    o_ref[...] = _dot_t(xb_ref[...], w_ref[...])


def _inproj(x, w_packed_t, tm, tn):
    m, k = x.shape
    n = w_packed_t.shape[0]
    return pl.pallas_call(
        _inproj_kernel,
        grid=(m // tm, n // tn),
        in_specs=[pl.BlockSpec((tm, k), lambda i, j: (i, 0)),
                  pl.BlockSpec((tn, k), lambda i, j: (j, 0))],
        out_specs=pl.BlockSpec((tm, tn), lambda i, j: (i, j)),
        out_shape=jax.ShapeDtypeStruct((m, n), F32),
        scratch_shapes=[pltpu.VMEM((tm, k), BF16)],
        compiler_params=_cparams(2),
        name="inproj",
    )(x, w_packed_t)


def _rope_tables(pos):
    inv = ROPE_THETA ** (-jnp.arange(0, ROT_DIM, 2, dtype=F32) / ROT_DIM)
    ang = pos.astype(F32)[:, None] * inv[None, :]
    cos, sin = jnp.cos(ang), jnp.sin(ang)
    t = pos.shape[0]
    half = ROT_DIM // 2
    pad = HEAD_DIM - ROT_DIM
    c_head = jnp.concatenate([cos, cos, jnp.ones((t, pad), F32)], axis=1)
    sp_head = jnp.concatenate([-sin, jnp.zeros((t, half + pad), F32)], axis=1)
    sm_head = jnp.concatenate([jnp.zeros((t, half), F32), sin, jnp.zeros((t, pad), F32)], axis=1)
    rep = LANES // HEAD_DIM
    return tuple(jnp.tile(z, (1, rep)) for z in (c_head, sp_head, sm_head))


def _rope_tile(x, c, sp, sm):
    half = ROT_DIM // 2
    return x * c + pltpu.roll(x, LANES - half, 1) * sp + pltpu.roll(x, half, 1) * sm


def _attn_prompt_kernel(q_ref, k_ref, v_ref, c_ref, sp_ref, sm_ref, sink_ref,
                        o_ref, kwin_ref, vwin_ref, kp_ref, vp_ref):
    i = pl.program_id(1)
    blk = WINDOW

    @pl.when(i == 0)
    def _():
        kp_ref[...] = jnp.zeros_like(kp_ref)
        vp_ref[...] = jnp.zeros_like(vp_ref)

    c, sp, sm = c_ref[...], sp_ref[...], sm_ref[...]
    lane = lax.broadcasted_iota(jnp.int32, (2 * blk, LANES), 1)
    lo2 = lane < HEAD_DIM
    lo1 = lax.broadcasted_iota(jnp.int32, (blk, LANES), 1) < HEAD_DIM

    qi = lax.broadcasted_iota(jnp.int32, (blk, 2 * blk), 0)
    kj = lax.broadcasted_iota(jnp.int32, (blk, 2 * blk), 1)
    kmin = jnp.where(i == 0, blk, 0)
    valid = (kj > qi) & (kj <= qi + blk) & (kj >= kmin)

    n_kv_tiles = KV_WIDTH // LANES
    kk_g, va_g, vb_g = [], [], []
    for t in range(n_kv_tiles):
        sl = slice(LANES * t, LANES * (t + 1))
        kcur = _rope_tile(k_ref[:, sl], c, sp, sm)
        vcur = v_ref[:, sl]
        kwin_ref[0, :, sl] = kcur
        vwin_ref[0, :, sl] = vcur
        kall = jnp.concatenate([kp_ref[:, sl], kcur], axis=0)
        vall = jnp.concatenate([vp_ref[:, sl], vcur], axis=0)
        kp_ref[:, sl] = kcur
        vp_ref[:, sl] = vcur
        kswap = pltpu.roll(kall, HEAD_DIM, 1)
        vswap = pltpu.roll(vall, HEAD_DIM, 1)
        kk_g += [jnp.where(lo2, kall, kswap).astype(BF16), jnp.where(lo2, kswap, kall).astype(BF16)]
        va_g += [jnp.where(lo2, vall, 0.0).astype(BF16), jnp.where(lo2, vswap, 0.0).astype(BF16)]
        vb_g += [jnp.where(lo2, 0.0, vswap).astype(BF16), jnp.where(lo2, 0.0, vall).astype(BF16)]

    group = N_Q_HEADS // N_KV_HEADS
    for j in range(ATTN_WIDTH // LANES):
        sl = slice(LANES * j, LANES * (j + 1))
        g = (2 * j) // group
        qt = _rope_tile(q_ref[:, sl], c, sp, sm) * (HEAD_DIM ** -0.5)
        o_tile = None
        for hh in range(2):
            sink = sink_ref[2 * j + hh]
            qm = (jnp.where(lo1, qt, 0.0) if hh == 0 else jnp.where(lo1, 0.0, qt)).astype(BF16)
            s = lax.dot_general(qm, kk_g[g], (((1,), (1,)), ((), ())), preferred_element_type=F32)
            s = jnp.where(valid, s, NEG_BIG)
            m = jnp.maximum(jnp.max(s, axis=1, keepdims=True), sink)
            p = jnp.exp(s - m)
            den = jnp.sum(p, axis=1, keepdims=True) + jnp.exp(sink - m)
            vmat = va_g[g] if hh == 0 else vb_g[g]
            o_h = jnp.dot(p.astype(BF16), vmat, preferred_element_type=F32) / den
            o_tile = o_h if o_tile is None else o_tile + o_h
        o_ref[:, sl] = o_tile.astype(o_ref.dtype)


def _attn_prompt(p_all, sinks, b, t):
    blk = WINDOW
    nb = t // blk
    c, sp, sm = _rope_tables(jnp.arange(t, dtype=jnp.int32))
    tab_spec = pl.BlockSpec((blk, LANES), lambda bi, i: (i, 0))
    row = lambda bi, i: bi * nb + i
    return pl.pallas_call(
        _attn_prompt_kernel,
        grid=(b, nb),
        in_specs=[pl.BlockSpec((blk, ATTN_WIDTH), lambda bi, i: (row(bi, i), COL_Q // ATTN_WIDTH)),
                  pl.BlockSpec((blk, KV_WIDTH), lambda bi, i: (row(bi, i), COL_K // KV_WIDTH)),
                  pl.BlockSpec((blk, KV_WIDTH), lambda bi, i: (row(bi, i), COL_V // KV_WIDTH)),
                  tab_spec, tab_spec, tab_spec,
                  pl.BlockSpec(memory_space=pltpu.SMEM)],
        out_specs=[pl.BlockSpec((blk, ATTN_WIDTH), lambda bi, i: (row(bi, i), 0)),
                   pl.BlockSpec((1, blk, KV_WIDTH), lambda bi, i: (bi, 0, 0)),
                   pl.BlockSpec((1, blk, KV_WIDTH), lambda bi, i: (bi, 0, 0))],
        out_shape=[jax.ShapeDtypeStruct((b * t, ATTN_WIDTH), BF16),
                   jax.ShapeDtypeStruct((b, blk, KV_WIDTH), F32),
                   jax.ShapeDtypeStruct((b, blk, KV_WIDTH), F32)],
        scratch_shapes=[pltpu.VMEM((blk, KV_WIDTH), F32), pltpu.VMEM((blk, KV_WIDTH), F32)],
        compiler_params=_cparams(2),
        name="attn_prompt",
    )(p_all, p_all, p_all, c, sp, sm, sinks)


def _rwkv_prepare(r_in, k_in, v_in, l_in, pr, pk, pv, plr, prm, wd_ref, wi_ref, wg_ref, bd2):
    mur, muk, muv, mul, w0, a0, k_k, k_a = prm
    r = r_in + (pr - r_in) * mur
    k = k_in + (pk - k_in) * muk
    v = v_in + (pv - v_in) * muv
    ls = l_in + (plr - l_in) * mul
    l0 = ls[:, :LANES]
    lane = lax.broadcasted_iota(jnp.int32, l0.shape, 1)
    z0 = jnp.where(lane < DECAY_LORA, jnp.tanh(l0), l0).astype(BF16)
    zg = _sigmoid(ls[:, LANES:3 * LANES]).astype(BF16)
    dec_up = jnp.dot(z0, wd_ref[...], preferred_element_type=F32)
    icl_up = jnp.dot(z0, wi_ref[...], preferred_element_type=F32)
    gate = jnp.dot(zg, wg_ref[...], preferred_element_type=F32)
    w_log = -_softplus(-(w0 + dec_up)) - 0.5
    logw = -jnp.exp(w_log)
    a_sig = _sigmoid(a0 + icl_up)
    kk = k * k_k
    nrm = jnp.maximum(jnp.sqrt(_head_sum(kk * kk, bd2)), 1e-12)
    kk = kk / nrm
    k = k * (1.0 + (a_sig - 1.0) * k_a)
    return r, k, v, logw, -kk, kk * a_sig, gate


def _rwkv_finish(y, r, k, v, gate, r_k, gn_w, gn_b, bd2):
    mu = _head_sum(y, bd2) * (1.0 / HEAD_DIM)
    d = y - mu
    var = _head_sum(d * d, bd2) * (1.0 / HEAD_DIM)
    yn = d * lax.rsqrt(var + GN_EPS) * gn_w + gn_b
    bonus = _head_sum(r * k * r_k, bd2) * v
    return (yn + bonus) * gate


def _rwkv_prompt_kernel(r_ref, k_ref, v_ref, l_ref,
                        mur_ref, muk_ref, muv_ref, mul_ref, w0_ref, a0_ref, kk_ref, ka_ref,
                        rk_ref, gnw_ref, gnb_ref, wd_ref, wi_ref, wg_ref,
                        o_ref, sout_ref,
                        cr_ref, ck_ref, cv_ref, cl_ref, st_ref, y_ref):
    ti = pl.program_id(2)
    tb, lw = r_ref.shape
    n_pair = lw // LANES
    n_chunk = tb // CHUNK

    @pl.when(ti == 0)
    def _():
        for ref in (cr_ref, ck_ref, cv_ref, cl_ref, st_ref):
            ref[...] = jnp.zeros_like(ref)

    def shifted(x, carry_ref):
        rolled = pltpu.roll(x, 1, 0)
        row = lax.broadcasted_iota(jnp.int32, x.shape, 0)
        prev = jnp.where(row == 0, carry_ref[0:1, :], rolled)
        carry_ref[0:1, :] = x[tb - 1:tb, :]
        return prev

    bd2 = _head_block_ones()
    r_in, k_in, v_in, l_in = r_ref[...], k_ref[...], v_ref[...], l_ref[...]
    prm = tuple(ref[...] for ref in (mur_ref, muk_ref, muv_ref, mul_ref, w0_ref, a0_ref, kk_ref, ka_ref))
    r, k, v, logw, a_s, b_s, gate = _rwkv_prepare(
        r_in, k_in, v_in, l_in,
        shifted(r_in, cr_ref), shifted(k_in, ck_ref), shifted(v_in, cv_ref), shifted(l_in, cl_ref),
        prm, wd_ref, wi_ref, wg_ref, bd2)

    c = CHUNK
    ri = lax.broadcasted_iota(jnp.int32, (c, 3 * c), 0)
    ci = lax.broadcasted_iota(jnp.int32, (c, 3 * c), 1) % c
    tri3 = jnp.where(ri >= ci, 1.0, 0.0).astype(BF16)
    r2 = lax.broadcasted_iota(jnp.int32, (2 * c, LANES), 0)
    l2 = lax.broadcasted_iota(jnp.int32, (2 * c, LANES), 1)
    tt, ss = r2 % c, l2 % c
    causal = ss < tt + r2 // c
    lo1 = lax.broadcasted_iota(jnp.int32, (c, LANES), 1) < HEAD_DIM
    lo2 = l2 < HEAD_DIM
    diag_blocks = (r2 // HEAD_DIM) == (l2 // HEAD_DIM)
    eye = jnp.where(r2 == l2, 1.0, 0.0)

    def stack_heads(x):
        return jnp.concatenate([jnp.where(lo1, x, 0.0), jnp.where(lo1, 0.0, x)], axis=0)

    def dot_t(a, b):
        return lax.dot_general(a, b, (((1,), (1,)), ((), ())), preferred_element_type=F32)

    def dot_tt(a, b):
        return lax.dot_general(a, b, (((0,), (0,)), ((), ())), preferred_element_type=F32)

    def dot(a, b):
        return jnp.dot(a.astype(BF16), b.astype(BF16), preferred_element_type=F32)

    pairs = range(n_pair)
    lanes = [slice(LANES * pi, LANES * (pi + 1)) for pi in pairs]
    for ch in range(n_chunk):
        rows = slice(c * ch, c * (ch + 1))
        lw_c = logw[rows]
        hi = lw_c.astype(BF16)
        rem = lw_c - hi.astype(F32)
        mid = rem.astype(BF16)
        low = (rem - mid.astype(F32)).astype(BF16)
        lcum = jnp.dot(tri3, jnp.concatenate([hi, mid, low], axis=0), preferred_element_type=F32)
        ltot = lcum[c - 1:c, :]
        p_inv = jnp.exp(-lcum)
        p_tail = jnp.exp(ltot - lcum)
        aq = a_s[rows] * jnp.exp(lcum - lw_c)
        rq = r[rows] * jnp.exp(lcum)
        bk = b_s[rows] * p_inv
        kq = k[rows] * p_inv
        bt = b_s[rows] * p_tail
        kt = k[rows] * p_tail
        p_end = jnp.exp(ltot)
        v_c = v[rows]

        ar = [jnp.concatenate([aq[:, sl], rq[:, sl]], axis=0).astype(BF16) for sl in lanes]
        gb = [jnp.where(causal, dot_t(ar[pi], stack_heads(bk[:, lanes[pi]]).astype(BF16)), 0.0) for pi in pairs]
        gk = [jnp.where(causal, dot_t(ar[pi], stack_heads(kq[:, lanes[pi]]).astype(BF16)), 0.0) for pi in pairs]
        gv = [dot(gk[pi], stack_heads(v_c[:, lanes[pi]])) for pi in pairs]
        pw = [stack_heads(gb[pi][:c]) for pi in pairs]
        tm = [eye + pw[pi] for pi in pairs]
        for _ in range(5):
            pw = [dot(pw[pi], pw[pi]) for pi in pairs]
            tm = [tm[pi] + dot(tm[pi], pw[pi]) for pi in pairs]
        t_side = [tm[pi][:c] + tm[pi][c:] for pi in pairs]
        tax = [dot(t_side[pi], jnp.concatenate([stack_heads(aq[:, lanes[pi]]), stack_heads(gv[pi][:c])], axis=1))
               for pi in pairs]
        taq = [tax[pi][:, :LANES] for pi in pairs]
        txv = [tax[pi][:, LANES:] for pi in pairs]
        arx = [dot(gb[pi][c:], jnp.concatenate([stack_heads(taq[pi]), stack_heads(txv[pi])], axis=1)) for pi in pairs]
        mb = [jnp.where(diag_blocks, dot_tt(bt[:, lanes[pi]].astype(BF16), taq[pi].astype(BF16)), 0.0) for pi in pairs]
        cct = [jnp.where(diag_blocks,
                         dot_tt(jnp.concatenate([txv[pi], v_c[:, lanes[pi]]], axis=0).astype(BF16),
                                jnp.concatenate([bt[:, lanes[pi]], kt[:, lanes[pi]]], axis=0).astype(BF16)), 0.0)
               for pi in pairs]
        for pi in pairs:
            sl = lanes[pi]
            s_old = st_ref[pi]
            sb = s_old.astype(BF16)
            rqp = (rq[:, sl] + arx[pi][:, :LANES]).astype(BF16)
            y_ref[rows, sl] = dot_t(rqp, sb) + gv[pi][c:] + arx[pi][:, LANES:]
            st_ref[pi] = s_old * p_end[:, sl] + dot_t(sb, mb[pi].astype(BF16)) + cct[pi]

    out = _rwkv_finish(y_ref[...], r, k, v, gate, rk_ref[...], gnw_ref[...], gnb_ref[...], bd2)
    o_ref[...] = out.astype(o_ref.dtype)

    @pl.when(ti == pl.num_programs(2) - 1)
    def _():
        for pi in range(n_pair):
            s = st_ref[pi]
            sout_ref[0, 2 * pi] = s[:HEAD_DIM, :HEAD_DIM]
            sout_ref[0, 2 * pi + 1] = s[HEAD_DIM:, HEAD_DIM:]


def _rwkv_lora_weights(w_decay_up, w_iclr_up, w_gate_up):
    z64 = jnp.zeros((DECAY_LORA, RWKV_WIDTH), F32)
    wd = jnp.concatenate([w_decay_up, z64], axis=0).astype(BF16)
    wi = jnp.concatenate([z64, w_iclr_up], axis=0).astype(BF16)
    wg = jnp.concatenate([w_gate_up, jnp.zeros((2 * LANES - GATE_LORA, RWKV_WIDTH), F32)], axis=0).astype(BF16)
    return wd, wi, wg


def _rwkv_prompt(p_all, prm, b, t, tb, lw):
    nt = t // tb
    ns = RWKV_WIDTH // lw
    row = lambda bi, si, ti: bi * nt + ti
    col_spec = lambda col0: pl.BlockSpec((tb, lw), lambda bi, si, ti: (row(bi, si, ti), col0 // lw + si))
    vec = pl.BlockSpec((1, lw), lambda bi, si, ti: (0, si))
    vec_l = pl.BlockSpec((1, LORA_PAD), lambda bi, si, ti: (0, 0))
    return pl.pallas_call(
        _rwkv_prompt_kernel,
        grid=(b, ns, nt),
        in_specs=[col_spec(COL_R), col_spec(COL_KR), col_spec(COL_VR),
                  pl.BlockSpec((tb, LORA_PAD), lambda bi, si, ti: (row(bi, si, ti), COL_LORA // LORA_PAD)),
                  vec, vec, vec, vec_l, vec, vec, vec, vec, vec, vec, vec,
                  pl.BlockSpec((LANES, lw), lambda bi, si, ti: (0, si)),
                  pl.BlockSpec((LANES, lw), lambda bi, si, ti: (0, si)),
                  pl.BlockSpec((2 * LANES, lw), lambda bi, si, ti: (0, si))],
        out_specs=[pl.BlockSpec((tb, lw), lambda bi, si, ti: (row(bi, si, ti), si)),
                   pl.BlockSpec((1, 2 * (lw // LANES), HEAD_DIM, HEAD_DIM), lambda bi, si, ti: (bi, si, 0, 0))],
        out_shape=[jax.ShapeDtypeStruct((b * t, RWKV_WIDTH), BF16),
                   jax.ShapeDtypeStruct((b, N_RWKV_HEADS, HEAD_DIM, HEAD_DIM), F32)],
        scratch_shapes=[pltpu.VMEM((8, lw), F32), pltpu.VMEM((8, lw), F32), pltpu.VMEM((8, lw), F32),
                        pltpu.VMEM((8, LORA_PAD), F32),
                        pltpu.VMEM((lw // LANES, LANES, LANES), F32),
                        pltpu.VMEM((tb, lw), F32)],
        compiler_params=_cparams(3),
        name="rwkv_prompt",
    )(p_all, p_all, p_all, p_all,
      prm["mu_r"], prm["mu_k"], prm["mu_v"], prm["mu_l"], prm["w0"], prm["a0"], prm["k_k"], prm["k_a"],
      prm["r_k"], prm["gn_w"], prm["gn_b"], prm["wd"], prm["wi"], prm["wg"])


def _merge_kernel(x_ref, ao_ref, ro_ref, wga_ref, wgb_ref, wpa_ref, wpr_ref, wo_ref, g_ref, b_ref, h_ref,
                  xb_ref, acc_ref):
    j = pl.program_id(1)

    @pl.when(j == 0)
    def _():
        xb_ref[...] = x_ref[...].astype(BF16)
        acc_ref[...] = jnp.zeros_like(acc_ref)

    xb = xb_ref[...]
    ga = _dot_t(xb, wga_ref[...])
    gb = _dot_t(xb, wgb_ref[...])
    a = jnp.dot(ao_ref[...], wpa_ref[...], preferred_element_type=F32)
    r = jnp.dot(ro_ref[...], wpr_ref[...], preferred_element_type=F32)
    m = _sigmoid(ga) * a + _sigmoid(gb) * r
    acc_ref[...] += jnp.dot(m.astype(BF16), wo_ref[...], preferred_element_type=F32)

    @pl.when(j == pl.num_programs(1) - 1)
    def _():
        z = DEEPNORM_ALPHA * x_ref[...] + acc_ref[...]
        h_ref[...] = _layer_norm_rows(z, g_ref[...], b_ref[...])


def _merge(x, attn_o, rwkv_o, w_gates, wpa, wpr, wo, ln_g, ln_b, tm, tj):
    m = x.shape[0]
    nj = D_MODEL // tj
    return pl.pallas_call(
        _merge_kernel,
        grid=(m // tm, nj),
        in_specs=[pl.BlockSpec((tm, D_MODEL), lambda i, j: (i, 0)),
                  pl.BlockSpec((tm, ATTN_WIDTH), lambda i, j: (i, 0)),
                  pl.BlockSpec((tm, RWKV_WIDTH), lambda i, j: (i, 0)),
                  pl.BlockSpec((tj, D_MODEL), lambda i, j: (j, 0)),
                  pl.BlockSpec((tj, D_MODEL), lambda i, j: (nj + j, 0)),
                  pl.BlockSpec((ATTN_WIDTH, tj), lambda i, j: (0, j)),
                  pl.BlockSpec((RWKV_WIDTH, tj), lambda i, j: (0, j)),
                  pl.BlockSpec((tj, D_MODEL), lambda i, j: (j, 0)),
                  pl.BlockSpec((1, D_MODEL), lambda i, j: (0, 0)),
                  pl.BlockSpec((1, D_MODEL), lambda i, j: (0, 0))],
        out_specs=pl.BlockSpec((tm, D_MODEL), lambda i, j: (i, 0)),
        out_shape=jax.ShapeDtypeStruct((m, D_MODEL), F32),
        scratch_shapes=[pltpu.VMEM((tm, D_MODEL), BF16), pltpu.VMEM((tm, D_MODEL), F32)],
        compiler_params=_cparams(2),
        name="merge_ln1",
    )(x, attn_o, rwkv_o, w_gates, w_gates, wpa, wpr, wo, ln_g, ln_b)


def _ffn_kernel(h_ref, wu_ref, wd_ref, g_ref, b_ref, y_ref, hb_ref, acc_ref):
    f = pl.program_id(1)

    @pl.when(f == 0)
    def _():
        hb_ref[...] = h_ref[...].astype(BF16)
        acc_ref[...] = jnp.zeros_like(acc_ref)

    u = jnp.dot(hb_ref[...], wu_ref[...], preferred_element_type=F32)
    u = jnp.square(jnp.maximum(u, 0.0))
    acc_ref[...] += jnp.dot(u.astype(BF16), wd_ref[...], preferred_element_type=F32)

    @pl.when(f == pl.num_programs(1) - 1)
    def _():
        z = DEEPNORM_ALPHA * h_ref[...] + acc_ref[...]
        y_ref[...] = _layer_norm_rows(z, g_ref[...], b_ref[...])


def _ffn(h, wu, wd, ln_g, ln_b, tm, tf):
    m = h.shape[0]
    return pl.pallas_call(
        _ffn_kernel,
        grid=(m // tm, D_FF // tf),
        in_specs=[pl.BlockSpec((tm, D_MODEL), lambda i, f: (i, 0)),
                  pl.BlockSpec((D_MODEL, tf), lambda i, f: (0, f)),
                  pl.BlockSpec((tf, D_MODEL), lambda i, f: (f, 0)),
                  pl.BlockSpec((1, D_MODEL), lambda i, f: (0, 0)),
                  pl.BlockSpec((1, D_MODEL), lambda i, f: (0, 0))],
        out_specs=pl.BlockSpec((tm, D_MODEL), lambda i, f: (i, 0)),
        out_shape=jax.ShapeDtypeStruct((m, D_MODEL), F32),
        scratch_shapes=[pltpu.VMEM((tm, D_MODEL), BF16), pltpu.VMEM((tm, D_MODEL), F32)],
        compiler_params=_cparams(2),
        name="ffn_ln2",
    )(h, wu, wd, ln_g, ln_b)


def _attn_sample_kernel(q_ref, kvt_ref, ck_ref, cv_ref, c_ref, sp_ref, sm_ref, cc_ref, spc_ref, smc_ref, sink_ref,
                        o_ref, nk_ref, nv_ref):
    bt = q_ref.shape[0]
    win = ck_ref.shape[2]
    n = kvt_ref.shape[1]
    half = ROT_DIM // 2
    group = N_Q_HEADS // N_KV_HEADS
    c, sp, sm = c_ref[0:1, :], sp_ref[0:1, :], sm_ref[0:1, :]
    sink = sink_ref[:, 0:1]
    row16 = lax.broadcasted_iota(jnp.int32, (N_Q_HEADS, KV_WIDTH), 0)
    lane16 = lax.broadcasted_iota(jnp.int32, (N_Q_HEADS, KV_WIDTH), 1)
    own_kv = (lane16 // HEAD_DIM) == (row16 % N_KV_HEADS)
    urow = row16 // N_KV_HEADS
    pos = lax.broadcasted_iota(jnp.int32, (KV_WIDTH, win), 1)
    seq = lax.broadcasted_iota(jnp.int32, (KV_WIDTH, n), 1)

    kt = kvt_ref[0:KV_WIDTH, :]
    kt = kt * cc_ref[...] + pltpu.roll(kt, KV_WIDTH - half, 0) * spc_ref[...] + pltpu.roll(kt, half, 0) * smc_ref[...]
    vt = kvt_ref[KV_WIDTH:2 * KV_WIDTH, :]

    def rope_row(x):
        return jnp.concatenate([_rope_tile(x[:, LANES * t:LANES * (t + 1)], c, sp, sm)
                                for t in range(x.shape[1] // LANES)], axis=1)

    rng = range(bt)
    me = [seq == pl.program_id(0) * bt + b for b in rng]
    k_col = [jnp.sum(jnp.where(me[b], kt, 0.0), axis=1, keepdims=True) for b in rng]
    v_col = [jnp.sum(jnp.where(me[b], vt, 0.0), axis=1, keepdims=True) for b in rng]
    nk = [jnp.where(pos == win - 1, k_col[b], pltpu.roll(ck_ref[b], win - 1, 1)) for b in rng]
    nv = [jnp.where(pos == win - 1, v_col[b], pltpu.roll(cv_ref[b], win - 1, 1)) for b in rng]
    for b in rng:
        nk_ref[b] = nk[b]
        nv_ref[b] = nv[b]
    qmat = []
    for b in rng:
        q = rope_row(q_ref[b:b + 1, :]) * (HEAD_DIM ** -0.5)
        qb = [jnp.broadcast_to(q[:, KV_WIDTH * u:KV_WIDTH * (u + 1)], (N_Q_HEADS, KV_WIDTH)) for u in range(group)]
        qsel = jnp.where(urow == 0, qb[0], jnp.where(urow == 1, qb[1], jnp.where(urow == 2, qb[2], qb[3])))
        qmat.append(jnp.where(own_kv, qsel, 0.0).astype(BF16))
    s = [jnp.dot(qmat[b], nk[b].astype(BF16), preferred_element_type=F32) for b in rng]
    m = [jnp.maximum(jnp.max(s[b], axis=1, keepdims=True), sink) for b in rng]
    p = [jnp.exp(s[b] - m[b]) for b in rng]
    den = [jnp.sum(p[b], axis=1, keepdims=True) + jnp.exp(sink - m[b]) for b in rng]
    o = [_dot_t(p[b].astype(BF16), nv[b].astype(BF16)) / den[b] for b in rng]
    out_rows = []
    for b in rng:
        ob = jnp.where(own_kv, o[b], 0.0)
        chunks = [jnp.sum(jnp.where(urow == u, ob, 0.0), axis=0, keepdims=True) for u in range(group)]
        out_rows.append(jnp.concatenate(chunks, axis=1))
    o_ref[...] = jnp.concatenate(out_rows, axis=0).astype(o_ref.dtype)


def _attn_sample(q_perm, kv_new_t, cache_kt, cache_vt, sink_mat, bt):
    n, win = cache_kt.shape[0], cache_kt.shape[2]
    tabs = _rope_tables(jnp.full((1,), PAST_LEN, jnp.int32))
    c, sp, sm = (jnp.broadcast_to(z, (8, LANES)) for z in tabs)
    cc, spc, smc = (jnp.broadcast_to(jnp.tile(z, (1, KV_WIDTH // LANES)).T, (KV_WIDTH, n)) for z in tabs)
    small = lambda shape: pl.BlockSpec(shape, lambda i: (0, 0))
    cache_spec = pl.BlockSpec((bt, KV_WIDTH, win), lambda i: (i, 0, 0))
    return pl.pallas_call(
        _attn_sample_kernel,
        grid=(n // bt,),
        in_specs=[pl.BlockSpec((bt, ATTN_WIDTH), lambda i: (i, 0)),
                  small((2 * KV_WIDTH, n)),
                  cache_spec, cache_spec,
                  small((8, LANES)), small((8, LANES)), small((8, LANES)),
                  small((KV_WIDTH, n)), small((KV_WIDTH, n)), small((KV_WIDTH, n)),
                  small((N_Q_HEADS, LANES))],
        out_specs=[pl.BlockSpec((bt, ATTN_WIDTH), lambda i: (i, 0)), cache_spec, cache_spec],
        out_shape=[jax.ShapeDtypeStruct((n, ATTN_WIDTH), BF16),
                   jax.ShapeDtypeStruct(cache_kt.shape, F32),
                   jax.ShapeDtypeStruct(cache_vt.shape, F32)],
        compiler_params=_cparams(1),
        name="attn_sample",
    )(q_perm, kv_new_t, cache_kt, cache_vt, c, sp, sm, cc, spc, smc, sink_mat)


def _rwkv_sample_kernel(r_ref, k_ref, v_ref, l_ref, pr_ref, pk_ref, pv_ref, pl_ref, st_ref,
                        mur_ref, muk_ref, muv_ref, mul_ref, w0_ref, a0_ref, kk_ref, ka_ref,
                        rk_ref, gnw_ref, gnb_ref, wd_ref, wi_ref, wg_ref,
                        o_ref, ns_ref, vec_s, keep_s, y_s):
    h = pl.program_id(0)
    hd = HEAD_DIM
    q_a, q_w, q_b, q_k, q_r, q_v = range(6)

    @pl.when(h == 0)
    def _():
        bd2 = _head_block_ones()
        prm = tuple(ref[...] for ref in (mur_ref, muk_ref, muv_ref, mul_ref, w0_ref, a0_ref, kk_ref, ka_ref))
        r, k, v, logw, a_s, b_s, gate = _rwkv_prepare(
            r_ref[...], k_ref[...], v_ref[...], l_ref[...], pr_ref[...], pk_ref[...], pv_ref[...], pl_ref[...],
            prm, wd_ref, wi_ref, wg_ref, bd2)
        for qi, x in enumerate((a_s, jnp.exp(logw), b_s, k, r, v)):
            xt = x.T
            for hh in range(N_RWKV_HEADS):
                vec_s[qi, hh] = xt[hd * hh:hd * (hh + 1), :]
        for qi, x in enumerate((r, k, v, gate)):
            keep_s[qi] = x

    a_h, w_h, b_h, k_h, r_h = (vec_s[qi, h] for qi in (q_a, q_w, q_b, q_k, q_r))
    for i in range(hd):
        s = st_ref[0, i]
        sa = jnp.sum(s * a_h, axis=0, keepdims=True)
        s_new = s * w_h + sa * b_h + vec_s[q_v, h, i:i + 1, :] * k_h
        ns_ref[0, i] = s_new
        y_s[h, i:i + 1, :] = jnp.sum(s_new * r_h, axis=0, keepdims=True)

    @pl.when(h == pl.num_programs(0) - 1)
    def _():
        y = jnp.concatenate([y_s[hh] for hh in range(N_RWKV_HEADS)], axis=0).T
        out = _rwkv_finish(y, keep_s[0], keep_s[1], keep_s[2], keep_s[3], rk_ref[...], gnw_ref[...], gnb_ref[...],
                           _head_block_ones())
        o_ref[...] = out.astype(o_ref.dtype)


def _rwkv_sample(p_all, shift, shift_l, state_t, prm):
    n = state_t.shape[-1]
    wide = lambda col0: pl.BlockSpec((n, RWKV_WIDTH), lambda h: (0, col0 // RWKV_WIDTH))
    vec = pl.BlockSpec((1, RWKV_WIDTH), lambda h: (0, 0))
    vec_l = pl.BlockSpec((1, LORA_PAD), lambda h: (0, 0))
    st_spec = pl.BlockSpec((1, HEAD_DIM, HEAD_DIM, n), lambda h: (h, 0, 0, 0))
    return pl.pallas_call(
        _rwkv_sample_kernel,
        grid=(N_RWKV_HEADS,),
        in_specs=[wide(COL_R), wide(COL_KR), wide(COL_VR),
                  pl.BlockSpec((n, LORA_PAD), lambda h: (0, COL_LORA // LORA_PAD)),
                  wide(0), wide(RWKV_WIDTH), wide(2 * RWKV_WIDTH),
                  pl.BlockSpec((n, LORA_PAD), lambda h: (0, 0)),
                  st_spec,
                  vec, vec, vec, vec_l, vec, vec, vec, vec, vec, vec, vec,
                  pl.BlockSpec((LANES, RWKV_WIDTH), lambda h: (0, 0)),
                  pl.BlockSpec((LANES, RWKV_WIDTH), lambda h: (0, 0)),
                  pl.BlockSpec((2 * LANES, RWKV_WIDTH), lambda h: (0, 0))],
        out_specs=[pl.BlockSpec((n, RWKV_WIDTH), lambda h: (0, 0)), st_spec],
        out_shape=[jax.ShapeDtypeStruct((n, RWKV_WIDTH), BF16), jax.ShapeDtypeStruct(state_t.shape, F32)],
        scratch_shapes=[pltpu.VMEM((6, N_RWKV_HEADS, HEAD_DIM, n), F32),
                        pltpu.VMEM((4, n, RWKV_WIDTH), F32),
                        pltpu.VMEM((N_RWKV_HEADS, HEAD_DIM, n), F32)],
        compiler_params=_cparams(1),
        name="rwkv_sample",
    )(p_all, p_all, p_all, p_all, shift, shift, shift, shift_l, state_t,
      prm["mu_r"], prm["mu_k"], prm["mu_v"], prm["mu_l"], prm["w0"], prm["a0"], prm["k_k"], prm["k_a"],
      prm["r_k"], prm["gn_w"], prm["gn_b"], prm["wd"], prm["wi"], prm["wg"])


def _pack_w_in(w_in):
    wt = w_in.T
    c0 = ATTN_WIDTH
    q = wt[:c0]
    k = wt[c0:c0 + KV_WIDTH]
    v = wt[c0 + KV_WIDTH:c0 + 2 * KV_WIDTH]
    c1 = c0 + 2 * KV_WIDTH
    rkv = wt[c1:c1 + 3 * RWKV_WIDTH]
    lora = wt[c1 + 3 * RWKV_WIDTH:c1 + RWKV_PROJ_WIDTH]
    c2 = c1 + RWKV_PROJ_WIDTH
    gates = wt[c2:c2 + 2 * D_MODEL]
    pad = jnp.zeros((LORA_PAD - LORA_WIDTH, D_MODEL), w_in.dtype)
    return jnp.concatenate([q, rkv, k, v, lora, pad], axis=0).astype(BF16), gates.astype(BF16)


def _swap_head_order(z, outer, inner):
    n = z.shape[0]
    return z.reshape(n, outer, inner, HEAD_DIM).transpose(0, 2, 1, 3).reshape(n, outer * inner * HEAD_DIM)


def _shift_columns(p_rows):
    return jnp.concatenate([p_rows[:, COL_R:COL_R + 3 * RWKV_WIDTH], p_rows[:, COL_LORA:COL_LORA + LORA_WIDTH]], axis=1)


def _forward(x_prompt, x_sample, cache_k_win, cache_v_win, state_shift, state_wkv, w, cfg):
    b, t, _ = x_prompt.shape
    n_s = x_sample.shape[0]
    row = lambda z: z.reshape(1, -1).astype(F32)
    mu = w["mu_shift"]
    wd, wi, wg = _rwkv_lora_weights(w["w_decay_up"], w["w_iclr_up"], w["w_gate_up"])
    prm = dict(
        mu_r=row(mu[:RWKV_WIDTH]), mu_k=row(mu[RWKV_WIDTH:2 * RWKV_WIDTH]), mu_v=row(mu[2 * RWKV_WIDTH:3 * RWKV_WIDTH]),
        mu_l=row(jnp.pad(mu[3 * RWKV_WIDTH:], (0, LORA_PAD - LORA_WIDTH))),
        w0=row(w["w0"]), a0=row(w["a0"]), k_k=row(w["k_k"]), k_a=row(w["k_a"]), r_k=row(w["r_k"]),
        gn_w=row(w["gn_w"]), gn_b=row(w["gn_b"]), wd=wd, wi=wi, wg=wg)
    wpa = w["w_proj_attn"].astype(BF16)
    wpr = w["w_proj_rwkv"].astype(BF16)
    wo = w["w_out"].astype(BF16)
    wu = w["w_up"].astype(BF16)
    wdn = w["w_down"].astype(BF16)
    ln1g, ln1b, ln2g, ln2b = row(w["ln1_g"]), row(w["ln1_b"]), row(w["ln2_g"]), row(w["ln2_b"])
    sinks = w["attn_sinks"].astype(F32)

    xp = x_prompt.reshape(b * t, D_MODEL)
    w_packed, w_gates = _pack_w_in(w["w_in"])
    pp = _inproj(xp, w_packed, cfg["tm_in"], cfg["tn_in"])
    attn_p, kwin_p, vwin_p = _attn_prompt(pp, sinks, b, t)
    rwkv_p, wkv_p = _rwkv_prompt(pp, prm, b, t, cfg["tb_rwkv"], cfg["lw_rwkv"])
    hp = _merge(xp, attn_p, rwkv_p, w_gates, wpa, wpr, wo, ln1g, ln1b, cfg["tm_merge"], cfg["tj_merge"])
    yp = _ffn(hp, wu, wdn, ln2g, ln2b, cfg["tm_ffn"], cfg["tf_ffn"])
    shift_p = _shift_columns(pp.reshape(b, t, PACK_WIDTH)[:, t - 1])

    group = N_Q_HEADS // N_KV_HEADS
    xs = x_sample.reshape(n_s, D_MODEL)
    ps = _inproj(xs, w_packed, n_s, cfg["tn_in"])
    q_perm = _swap_head_order(ps[:, COL_Q:COL_Q + ATTN_WIDTH], N_KV_HEADS, group)
    sink_mat = jnp.broadcast_to(sinks.reshape(N_KV_HEADS, group).T.reshape(N_Q_HEADS, 1), (N_Q_HEADS, LANES))
    win = cache_k_win.shape[1]
    to_t = lambda z: jnp.transpose(z, (0, 2, 3, 1)).reshape(n_s, KV_WIDTH, win)
    from_t = lambda z: jnp.transpose(z.reshape(n_s, N_KV_HEADS, HEAD_DIM, win), (0, 3, 1, 2))[None]
    attn_s, nk_t, nv_t = _attn_sample(q_perm, ps[:, COL_K:COL_K + 2 * KV_WIDTH].T, to_t(cache_k_win), to_t(cache_v_win),
                                      sink_mat, cfg["bt_sample"])
    attn_s = _swap_head_order(attn_s, group, N_KV_HEADS)
    shift_l = jnp.pad(state_shift[:, 3 * RWKV_WIDTH:], ((0, 0), (0, LORA_PAD - LORA_WIDTH)))
    rwkv_s, wkv_t = _rwkv_sample(ps, state_shift, shift_l, jnp.transpose(state_wkv, (1, 2, 3, 0)), prm)
    wkv_s = jnp.transpose(wkv_t, (3, 0, 1, 2))
    hs = _merge(xs, attn_s, rwkv_s, w_gates, wpa, wpr, wo, ln1g, ln1b, n_s, cfg["tj_merge"])
    ys = _ffn(hs, wu, wdn, ln2g, ln2b, n_s, cfg["tf_ffn"])
    shift_s = _shift_columns(ps)

    kv5 = lambda z: z.reshape(1, z.shape[0], z.shape[1], N_KV_HEADS, HEAD_DIM)
    return (yp.reshape(b, t, D_MODEL), ys.reshape(n_s, 1, D_MODEL),
            kv5(kwin_p), kv5(vwin_p), shift_p[None], wkv_p[None],
            from_t(nk_t), from_t(nv_t), shift_s[None], wkv_s[None])


_CFG = dict(tm_in=1024, tn_in=1024, tb_rwkv=128, lw_rwkv=1024, tm_merge=512, tj_merge=512, tm_ffn=512, tf_ffn=1024, bt_sample=8)


def kernel(x_prompt, x_sample, cache_k_win, cache_v_win, state_shift, state_wkv, w_in, attn_sinks, mu_shift, w0,
           w_decay_up, a0, w_iclr_up, w_gate_up, k_k, k_a, r_k, gn_w, gn_b, w_proj_attn, w_proj_rwkv, w_out,
           ln1_g, ln1_b, w_up, w_down, ln2_g, ln2_b):
    w = dict(w_in=w_in[0], attn_sinks=attn_sinks[0], mu_shift=mu_shift[0], w0=w0[0], w_decay_up=w_decay_up[0],
             a0=a0[0], w_iclr_up=w_iclr_up[0], w_gate_up=w_gate_up[0], k_k=k_k[0], k_a=k_a[0], r_k=r_k[0],
             gn_w=gn_w[0], gn_b=gn_b[0], w_proj_attn=w_proj_attn[0], w_proj_rwkv=w_proj_rwkv[0], w_out=w_out[0],
             ln1_g=ln1_g[0], ln1_b=ln1_b[0], w_up=w_up[0], w_down=w_down[0], ln2_g=ln2_g[0], ln2_b=ln2_b[0])
    return _forward(x_prompt, x_sample, cache_k_win[0], cache_v_win[0], state_shift[0], state_wkv[0], w, _CFG)
```

```python
import functools

import jax
import jax.numpy as jnp
import numpy as np
from jax import lax
from jax.experimental import pallas as pl
from jax.experimental.pallas import tpu as pltpu

F32 = jnp.float32
BF16 = jnp.bfloat16

D_MODEL = 2048
HEAD_DIM = 64
N_Q_HEADS = 16
N_KV_HEADS = 4
ATTN_WIDTH = N_Q_HEADS * HEAD_DIM
KV_WIDTH = N_KV_HEADS * HEAD_DIM
WINDOW = 128
ROPE_THETA = 500000.0
ROT_DIM = HEAD_DIM // 4
N_RWKV_HEADS = 16
RWKV_WIDTH = N_RWKV_HEADS * HEAD_DIM
DECAY_LORA = 64
ICLR_LORA = 64
GATE_LORA = 160
LORA_WIDTH = DECAY_LORA + ICLR_LORA + GATE_LORA
RWKV_PROJ_WIDTH = 3 * RWKV_WIDTH + LORA_WIDTH
D_FF = 4 * D_MODEL
PAST_LEN = 16384
DEEPNORM_ALPHA = 2.0 ** 0.25
LN_EPS = 1e-5
GN_EPS = HEAD_DIM * 1e-5
NEG_BIG = -1e30

LANES = 128
CHUNK = 64
VMEM_LIMIT = 56 * 1024 * 1024

COL_Q = 0
COL_R = 1024
COL_KR = 2048
COL_VR = 3072
COL_K = 4096
COL_V = 4352
COL_LORA = 4608
LORA_PAD = 512
PACK_WIDTH = COL_LORA + LORA_PAD


def _cparams(n_axes):
    return pltpu.CompilerParams(dimension_semantics=("arbitrary",) * n_axes, vmem_limit_bytes=VMEM_LIMIT)


def _sigmoid(x):
    return 1.0 / (1.0 + jnp.exp(-x))


def _softplus(x):
    return jnp.maximum(x, 0.0) + jnp.log(1.0 + jnp.exp(-jnp.abs(x)))


def _layer_norm_rows(z, g, b):
    mu = jnp.mean(z, axis=-1, keepdims=True)
    d = z - mu
    var = jnp.mean(d * d, axis=-1, keepdims=True)
    return d * lax.rsqrt(var + LN_EPS) * g + b


def _dot_t(a, b):
    return lax.dot_general(a, b, (((1,), (1,)), ((), ())), preferred_element_type=F32)


def _split_bf16(x):
    hi = x.astype(BF16)
    lo = (x - hi.astype(F32)).astype(BF16)
    return hi, lo


def _head_block_ones():
    r = lax.broadcasted_iota(jnp.int32, (2 * LANES, LANES), 0)
    c = lax.broadcasted_iota(jnp.int32, (2 * LANES, LANES), 1)
    return jnp.where((r % LANES) // HEAD_DIM == c // HEAD_DIM, 1.0, 0.0).astype(BF16)


def _head_sum(x, bd2):
    outs = []
    for t in range(x.shape[1] // LANES):
        hi, lo = _split_bf16(x[:, LANES * t:LANES * (t + 1)])
        outs.append(jnp.dot(jnp.concatenate([hi, lo], axis=1), bd2, preferred_element_type=F32))
    return outs[0] if len(outs) == 1 else jnp.concatenate(outs, axis=1)


def _inproj_kernel(x_ref, w_ref, o_ref, xb_ref):
    @pl.when(pl.program_id(1) == 0)
    def _():
        xb_ref[...] = x_ref[...].astype(BF16)

    o_ref[...] = _dot_t(xb_ref[...], w_ref[...])


def _inproj(x, w_packed_t, tm, tn):
    m, k = x.shape
    n = w_packed_t.shape[0]
    return pl.pallas_call(
        _inproj_kernel,
        grid=(m // tm, n // tn),
        in_specs=[pl.BlockSpec((tm, k), lambda i, j: (i, 0)),
                  pl.BlockSpec((tn, k), lambda i, j: (j, 0))],
        out_specs=pl.BlockSpec((tm, tn), lambda i, j: (i, j)),
        out_shape=jax.ShapeDtypeStruct((m, n), F32),
        scratch_shapes=[pltpu.VMEM((tm, k), BF16)],
        compiler_params=_cparams(2),
        name="inproj",
    )(x, w_packed_t)


def _rope_tables(pos):
    inv = ROPE_THETA ** (-jnp.arange(0, ROT_DIM, 2, dtype=F32) / ROT_DIM)
    ang = pos.astype(F32)[:, None] * inv[None, :]
    cos, sin = jnp.cos(ang), jnp.sin(ang)
    t = pos.shape[0]
    half = ROT_DIM // 2
    pad = HEAD_DIM - ROT_DIM
    c_head = jnp.concatenate([cos, cos, jnp.ones((t, pad), F32)], axis=1)
    sp_head = jnp.concatenate([-sin, jnp.zeros((t, half + pad), F32)], axis=1)
    sm_head = jnp.concatenate([jnp.zeros((t, half), F32), sin, jnp.zeros((t, pad), F32)], axis=1)
    rep = LANES // HEAD_DIM
    return tuple(jnp.tile(z, (1, rep)) for z in (c_head, sp_head, sm_head))


def _rope_tile(x, c, sp, sm):
    half = ROT_DIM // 2
    return x * c + pltpu.roll(x, LANES - half, 1) * sp + pltpu.roll(x, half, 1) * sm


def _attn_prompt_kernel(q_ref, k_ref, v_ref, c_ref, sp_ref, sm_ref, sink_ref,
                        o_ref, kwin_ref, vwin_ref, kp_ref, vp_ref):
    i = pl.program_id(1)
    blk = WINDOW

    @pl.when(i == 0)
    def _():
        kp_ref[...] = jnp.zeros_like(kp_ref)
        vp_ref[...] = jnp.zeros_like(vp_ref)

    c, sp, sm = c_ref[...], sp_ref[...], sm_ref[...]
    lane = lax.broadcasted_iota(jnp.int32, (2 * blk, LANES), 1)
    lo2 = lane < HEAD_DIM
    lo1 = lax.broadcasted_iota(jnp.int32, (blk, LANES), 1) < HEAD_DIM

    qi = lax.broadcasted_iota(jnp.int32, (blk, 2 * blk), 0)
    kj = lax.broadcasted_iota(jnp.int32, (blk, 2 * blk), 1)
    kmin = jnp.where(i == 0, blk, 0)
    valid = (kj > qi) & (kj <= qi + blk) & (kj >= kmin)

    n_kv_tiles = KV_WIDTH // LANES
    kk_g, va_g, vb_g = [], [], []
    for t in range(n_kv_tiles):
        sl = slice(LANES * t, LANES * (t + 1))
        kcur = _rope_tile(k_ref[:, sl], c, sp, sm)
        vcur = v_ref[:, sl]
        kwin_ref[0, :, sl] = kcur
        vwin_ref[0, :, sl] = vcur
        kall = jnp.concatenate([kp_ref[:, sl], kcur], axis=0)
        vall = jnp.concatenate([vp_ref[:, sl], vcur], axis=0)
        kp_ref[:, sl] = kcur
        vp_ref[:, sl] = vcur
        kswap = pltpu.roll(kall, HEAD_DIM, 1)
        vswap = pltpu.roll(vall, HEAD_DIM, 1)
        kk_g += [jnp.where(lo2, kall, kswap).astype(BF16), jnp.where(lo2, kswap, kall).astype(BF16)]
        va_g += [jnp.where(lo2, vall, 0.0).astype(BF16), jnp.where(lo2, vswap, 0.0).astype(BF16)]
        vb_g += [jnp.where(lo2, 0.0, vswap).astype(BF16), jnp.where(lo2, 0.0, vall).astype(BF16)]

    group = N_Q_HEADS // N_KV_HEADS
    for j in range(ATTN_WIDTH // LANES):
        sl = slice(LANES * j, LANES * (j + 1))
        g = (2 * j) // group
        qt = _rope_tile(q_ref[:, sl], c, sp, sm) * (HEAD_DIM ** -0.5)
        o_tile = None
        for hh in range(2):
            sink = sink_ref[2 * j + hh]
            qm = (jnp.where(lo1, qt, 0.0) if hh == 0 else jnp.where(lo1, 0.0, qt)).astype(BF16)
            s = lax.dot_general(qm, kk_g[g], (((1,), (1,)), ((), ())), preferred_element_type=F32)
            s = jnp.where(valid, s, NEG_BIG)
            m = jnp.maximum(jnp.max(s, axis=1, keepdims=True), sink)
            p = jnp.exp(s - m)
            den = jnp.sum(p, axis=1, keepdims=True) + jnp.exp(sink - m)
            vmat = va_g[g] if hh == 0 else vb_g[g]
            o_h = jnp.dot(p.astype(BF16), vmat, preferred_element_type=F32) / den
            o_tile = o_h if o_tile is None else o_tile + o_h
        o_ref[:, sl] = o_tile.astype(o_ref.dtype)


def _attn_prompt(p_all, sinks, b, t):
    blk = WINDOW
    nb = t // blk
    c, sp, sm = _rope_tables(jnp.arange(t, dtype=jnp.int32))
    tab_spec = pl.BlockSpec((blk, LANES), lambda bi, i: (i, 0))
    row = lambda bi, i: bi * nb + i
    return pl.pallas_call(
        _attn_prompt_kernel,
        grid=(b, nb),
        in_specs=[pl.BlockSpec((blk, ATTN_WIDTH), lambda bi, i: (row(bi, i), COL_Q // ATTN_WIDTH)),
                  pl.BlockSpec((blk, KV_WIDTH), lambda bi, i: (row(bi, i), COL_K // KV_WIDTH)),
                  pl.BlockSpec((blk, KV_WIDTH), lambda bi, i: (row(bi, i), COL_V // KV_WIDTH)),
                  tab_spec, tab_spec, tab_spec,
                  pl.BlockSpec(memory_space=pltpu.SMEM)],
        out_specs=[pl.BlockSpec((blk, ATTN_WIDTH), lambda bi, i: (row(bi, i), 0)),
                   pl.BlockSpec((1, blk, KV_WIDTH), lambda bi, i: (bi, 0, 0)),
                   pl.BlockSpec((1, blk, KV_WIDTH), lambda bi, i: (bi, 0, 0))],
        out_shape=[jax.ShapeDtypeStruct((b * t, ATTN_WIDTH), BF16),
                   jax.ShapeDtypeStruct((b, blk, KV_WIDTH), F32),
                   jax.ShapeDtypeStruct((b, blk, KV_WIDTH), F32)],
        scratch_shapes=[pltpu.VMEM((blk, KV_WIDTH), F32), pltpu.VMEM((blk, KV_WIDTH), F32)],
        compiler_params=_cparams(2),
        name="attn_prompt",
    )(p_all, p_all, p_all, c, sp, sm, sinks)


def _rwkv_prepare(r_in, k_in, v_in, l_in, pr, pk, pv, plr, prm, wd_ref, wi_ref, wg_ref, bd2):
    mur, muk, muv, mul, w0, a0, k_k, k_a = prm
    r = r_in + (pr - r_in) * mur
    k = k_in + (pk - k_in) * muk
    v = v_in + (pv - v_in) * muv
    ls = l_in + (plr - l_in) * mul
    l0 = ls[:, :LANES]
    lane = lax.broadcasted_iota(jnp.int32, l0.shape, 1)
    z0 = jnp.where(lane < DECAY_LORA, jnp.tanh(l0), l0).astype(BF16)
    zg = _sigmoid(ls[:, LANES:3 * LANES]).astype(BF16)
    dec_up = jnp.dot(z0, wd_ref[...], preferred_element_type=F32)
    icl_up = jnp.dot(z0, wi_ref[...], preferred_element_type=F32)
    gate = jnp.dot(zg, wg_ref[...], preferred_element_type=F32)
    w_log = -_softplus(-(w0 + dec_up)) - 0.5
    logw = -jnp.exp(w_log)
    a_sig = _sigmoid(a0 + icl_up)
    kk = k * k_k
    nrm = jnp.maximum(jnp.sqrt(_head_sum(kk * kk, bd2)), 1e-12)
    kk = kk / nrm
    k = k * (1.0 + (a_sig - 1.0) * k_a)
    return r, k, v, logw, -kk, kk * a_sig, gate


def _rwkv_finish(y, r, k, v, gate, r_k, gn_w, gn_b, bd2):
    mu = _head_sum(y, bd2) * (1.0 / HEAD_DIM)
    d = y - mu
    var = _head_sum(d * d, bd2) * (1.0 / HEAD_DIM)
    yn = d * lax.rsqrt(var + GN_EPS) * gn_w + gn_b
    bonus = _head_sum(r * k * r_k, bd2) * v
    return (yn + bonus) * gate


def _rwkv_prompt_kernel(r_ref, k_ref, v_ref, l_ref,
                        mur_ref, muk_ref, muv_ref, mul_ref, w0_ref, a0_ref, kk_ref, ka_ref,
                        rk_ref, gnw_ref, gnb_ref, wd_ref, wi_ref, wg_ref,
                        o_ref, sout_ref,
                        cr_ref, ck_ref, cv_ref, cl_ref, st_ref, y_ref):
    ti = pl.program_id(2)
    tb, lw = r_ref.shape
    n_pair = lw // LANES
    n_chunk = tb // CHUNK

    @pl.when(ti == 0)
    def _():
        for ref in (cr_ref, ck_ref, cv_ref, cl_ref, st_ref):
            ref[...] = jnp.zeros_like(ref)

    def shifted(x, carry_ref):
        rolled = pltpu.roll(x, 1, 0)
        row = lax.broadcasted_iota(jnp.int32, x.shape, 0)
        prev = jnp.where(row == 0, carry_ref[0:1, :], rolled)
        carry_ref[0:1, :] = x[tb - 1:tb, :]
        return prev

    bd2 = _head_block_ones()
    r_in, k_in, v_in, l_in = r_ref[...], k_ref[...], v_ref[...], l_ref[...]
    prm = tuple(ref[...] for ref in (mur_ref, muk_ref, muv_ref, mul_ref, w0_ref, a0_ref, kk_ref, ka_ref))
    r, k, v, logw, a_s, b_s, gate = _rwkv_prepare(
        r_in, k_in, v_in, l_in,
        shifted(r_in, cr_ref), shifted(k_in, ck_ref), shifted(v_in, cv_ref), shifted(l_in, cl_ref),
        prm, wd_ref, wi_ref, wg_ref, bd2)

    c = CHUNK
    ri = lax.broadcasted_iota(jnp.int32, (c, 3 * c), 0)
    ci = lax.broadcasted_iota(jnp.int32, (c, 3 * c), 1) % c
    tri3 = jnp.where(ri >= ci, 1.0, 0.0).astype(BF16)
    r2 = lax.broadcasted_iota(jnp.int32, (2 * c, LANES), 0)
    l2 = lax.broadcasted_iota(jnp.int32, (2 * c, LANES), 1)
    tt, ss = r2 % c, l2 % c
    causal = ss < tt + r2 // c
    lo1 = lax.broadcasted_iota(jnp.int32, (c, LANES), 1) < HEAD_DIM
    diag_blocks = (r2 // HEAD_DIM) == (l2 // HEAD_DIM)
    eye_side = jnp.where(lax.broadcasted_iota(jnp.int32, (c, LANES), 0) == lax.broadcasted_iota(jnp.int32, (c, LANES), 1) % c,
                         1.0, 0.0)

    def stack_heads(x):
        return jnp.concatenate([jnp.where(lo1, x, 0.0), jnp.where(lo1, 0.0, x)], axis=0)

    def dot_t(a, b):
        return lax.dot_general(a, b, (((1,), (1,)), ((), ())), preferred_element_type=F32)

    def dot_tt(a, b):
        return lax.dot_general(a, b, (((0,), (0,)), ((), ())), preferred_element_type=F32)

    def dot(a, b):
        return jnp.dot(a.astype(BF16), b.astype(BF16), preferred_element_type=F32)

    lanes = [slice(LANES * pi, LANES * (pi + 1)) for pi in range(n_pair)]
    units = [(ch, pi) for ch in range(n_chunk) for pi in range(n_pair)]
    un = range(len(units))
    el = []
    for ch in range(n_chunk):
        rows = slice(c * ch, c * (ch + 1))
        lw_c = logw[rows]
        hi = lw_c.astype(BF16)
        rem = lw_c - hi.astype(F32)
        mid = rem.astype(BF16)
        low = (rem - mid.astype(F32)).astype(BF16)
        lcum = jnp.dot(tri3, jnp.concatenate([hi, mid, low], axis=0), preferred_element_type=F32)
        ltot = lcum[c - 1:c, :]
        p_inv = jnp.exp(-lcum)
        p_tail = jnp.exp(ltot - lcum)
        el.append(dict(aq=a_s[rows] * jnp.exp(lcum - lw_c), rq=r[rows] * jnp.exp(lcum), bk=b_s[rows] * p_inv,
                       kq=k[rows] * p_inv, bt=b_s[rows] * p_tail, kt=k[rows] * p_tail, v=v[rows],
                       p_end=jnp.exp(ltot)))
    op = lambda name, u: el[units[u][0]][name][:, lanes[units[u][1]]]

    ar = [jnp.concatenate([op("aq", u), op("rq", u)], axis=0).astype(BF16) for u in un]
    gb = [jnp.where(causal, dot_t(ar[u], stack_heads(op("bk", u)).astype(BF16)), 0.0) for u in un]
    gk = [jnp.where(causal, dot_t(ar[u], stack_heads(op("kq", u)).astype(BF16)), 0.0) for u in un]
    gv = [dot(gk[u], stack_heads(op("v", u))) for u in un]
    pw = [gb[u][:c] for u in un]
    tm = [eye_side + pw[u] for u in un]
    for level in range(1, 6):
        bd = [stack_heads(pw[u]).astype(BF16) for u in un]
        if level == 1:
            pw = [jnp.dot(pw[u].astype(BF16), bd[u], preferred_element_type=F32) for u in un]
            bd = [stack_heads(pw[u]).astype(BF16) for u in un]
        if level < 5:
            both = [jnp.dot(jnp.concatenate([tm[u], pw[u]], axis=0).astype(BF16), bd[u], preferred_element_type=F32)
                    for u in un]
            tm = [tm[u] + both[u][:c] for u in un]
            pw = [both[u][c:] for u in un]
        else:
            tm = [tm[u] + jnp.dot(tm[u].astype(BF16), bd[u], preferred_element_type=F32) for u in un]
    tax = [dot(tm[u], jnp.concatenate([stack_heads(op("aq", u)), stack_heads(gv[u][:c])], axis=1)) for u in un]
    taq = [tax[u][:, :LANES] for u in un]
    txv = [tax[u][:, LANES:] for u in un]
    arx = [dot(gb[u][c:], jnp.concatenate([stack_heads(taq[u]), stack_heads(txv[u])], axis=1)) for u in un]
    mb = [jnp.where(diag_blocks, dot_tt(op("bt", u).astype(BF16), taq[u].astype(BF16)), 0.0).astype(BF16) for u in un]
    cct = [jnp.where(diag_blocks,
                     dot_tt(jnp.concatenate([txv[u], op("v", u)], axis=0).astype(BF16),
                            jnp.concatenate([op("bt", u), op("kt", u)], axis=0).astype(BF16)), 0.0) for u in un]
    rqp = [(op("rq", u) + arx[u][:, :LANES]).astype(BF16) for u in un]
    yc = [gv[u][c:] + arx[u][:, LANES:] for u in un]
    for u, (ch, pi) in enumerate(units):
        s_old = st_ref[pi]
        sb = s_old.astype(BF16)
        y_ref[c * ch:c * (ch + 1), lanes[pi]] = dot_t(rqp[u], sb) + yc[u]
        st_ref[pi] = s_old * op("p_end", u) + dot_t(sb, mb[u]) + cct[u]

    out = _rwkv_finish(y_ref[...], r, k, v, gate, rk_ref[...], gnw_ref[...], gnb_ref[...], bd2)
    o_ref[...] = out.astype(o_ref.dtype)

    @pl.when(ti == pl.num_programs(2) - 1)
    def _():
        for pi in range(n_pair):
            s = st_ref[pi]
            sout_ref[0, 2 * pi] = s[:HEAD_DIM, :HEAD_DIM]
            sout_ref[0, 2 * pi + 1] = s[HEAD_DIM:, HEAD_DIM:]


def _rwkv_lora_weights(w_decay_up, w_iclr_up, w_gate_up):
    z64 = jnp.zeros((DECAY_LORA, RWKV_WIDTH), F32)
    wd = jnp.concatenate([w_decay_up, z64], axis=0).astype(BF16)
    wi = jnp.concatenate([z64, w_iclr_up], axis=0).astype(BF16)
    wg = jnp.concatenate([w_gate_up, jnp.zeros((2 * LANES - GATE_LORA, RWKV_WIDTH), F32)], axis=0).astype(BF16)
    return wd, wi, wg


def _rwkv_prompt(p_all, prm, b, t, tb, lw):
    nt = t // tb
    ns = RWKV_WIDTH // lw
    row = lambda bi, si, ti: bi * nt + ti
    col_spec = lambda col0: pl.BlockSpec((tb, lw), lambda bi, si, ti: (row(bi, si, ti), col0 // lw + si))
    vec = pl.BlockSpec((1, lw), lambda bi, si, ti: (0, si))
    vec_l = pl.BlockSpec((1, LORA_PAD), lambda bi, si, ti: (0, 0))
    return pl.pallas_call(
        _rwkv_prompt_kernel,
        grid=(b, ns, nt),
        in_specs=[col_spec(COL_R), col_spec(COL_KR), col_spec(COL_VR),
                  pl.BlockSpec((tb, LORA_PAD), lambda bi, si, ti: (row(bi, si, ti), COL_LORA // LORA_PAD)),
                  vec, vec, vec, vec_l, vec, vec, vec, vec, vec, vec, vec,
                  pl.BlockSpec((LANES, lw), lambda bi, si, ti: (0, si)),
                  pl.BlockSpec((LANES, lw), lambda bi, si, ti: (0, si)),
                  pl.BlockSpec((2 * LANES, lw), lambda bi, si, ti: (0, si))],
        out_specs=[pl.BlockSpec((tb, lw), lambda bi, si, ti: (row(bi, si, ti), si)),
                   pl.BlockSpec((1, 2 * (lw // LANES), HEAD_DIM, HEAD_DIM), lambda bi, si, ti: (bi, si, 0, 0))],
        out_shape=[jax.ShapeDtypeStruct((b * t, RWKV_WIDTH), BF16),
                   jax.ShapeDtypeStruct((b, N_RWKV_HEADS, HEAD_DIM, HEAD_DIM), F32)],
        scratch_shapes=[pltpu.VMEM((8, lw), F32), pltpu.VMEM((8, lw), F32), pltpu.VMEM((8, lw), F32),
                        pltpu.VMEM((8, LORA_PAD), F32),
                        pltpu.VMEM((lw // LANES, LANES, LANES), F32),
                        pltpu.VMEM((tb, lw), F32)],
        compiler_params=_cparams(3),
        name="rwkv_prompt",
    )(p_all, p_all, p_all, p_all,
      prm["mu_r"], prm["mu_k"], prm["mu_v"], prm["mu_l"], prm["w0"], prm["a0"], prm["k_k"], prm["k_a"],
      prm["r_k"], prm["gn_w"], prm["gn_b"], prm["wd"], prm["wi"], prm["wg"])


def _merge_kernel(x_ref, ao_ref, ro_ref, wga_ref, wgb_ref, wpa_ref, wpr_ref, wo_ref, g_ref, b_ref, h_ref,
                  xb_ref, acc_ref):
    j = pl.program_id(1)

    @pl.when(j == 0)
    def _():
        xb_ref[...] = x_ref[...].astype(BF16)
        acc_ref[...] = jnp.zeros_like(acc_ref)

    xb = xb_ref[...]
    ga = _dot_t(xb, wga_ref[...])
    gb = _dot_t(xb, wgb_ref[...])
    a = jnp.dot(ao_ref[...], wpa_ref[...], preferred_element_type=F32)
    r = jnp.dot(ro_ref[...], wpr_ref[...], preferred_element_type=F32)
    m = _sigmoid(ga) * a + _sigmoid(gb) * r
    acc_ref[...] += jnp.dot(m.astype(BF16), wo_ref[...], preferred_element_type=F32)

    @pl.when(j == pl.num_programs(1) - 1)
    def _():
        z = DEEPNORM_ALPHA * x_ref[...] + acc_ref[...]
        h_ref[...] = _layer_norm_rows(z, g_ref[...], b_ref[...])


def _merge(x, attn_o, rwkv_o, w_gates, wpa, wpr, wo, ln_g, ln_b, tm, tj):
    m = x.shape[0]
    nj = D_MODEL // tj
    return pl.pallas_call(
        _merge_kernel,
        grid=(m // tm, nj),
        in_specs=[pl.BlockSpec((tm, D_MODEL), lambda i, j: (i, 0)),
                  pl.BlockSpec((tm, ATTN_WIDTH), lambda i, j: (i, 0)),
                  pl.BlockSpec((tm, RWKV_WIDTH), lambda i, j: (i, 0)),
                  pl.BlockSpec((tj, D_MODEL), lambda i, j: (j, 0)),
                  pl.BlockSpec((tj, D_MODEL), lambda i, j: (nj + j, 0)),
                  pl.BlockSpec((ATTN_WIDTH, tj), lambda i, j: (0, j)),
                  pl.BlockSpec((RWKV_WIDTH, tj), lambda i, j: (0, j)),
                  pl.BlockSpec((tj, D_MODEL), lambda i, j: (j, 0)),
                  pl.BlockSpec((1, D_MODEL), lambda i, j: (0, 0)),
                  pl.BlockSpec((1, D_MODEL), lambda i, j: (0, 0))],
        out_specs=pl.BlockSpec((tm, D_MODEL), lambda i, j: (i, 0)),
        out_shape=jax.ShapeDtypeStruct((m, D_MODEL), F32),
        scratch_shapes=[pltpu.VMEM((tm, D_MODEL), BF16), pltpu.VMEM((tm, D_MODEL), F32)],
        compiler_params=_cparams(2),
        name="merge_ln1",
    )(x, attn_o, rwkv_o, w_gates, w_gates, wpa, wpr, wo, ln_g, ln_b)


def _ffn_kernel(h_ref, wu_ref, wd_ref, g_ref, b_ref, y_ref, hb_ref, acc_ref):
    f = pl.program_id(1)

    @pl.when(f == 0)
    def _():
        hb_ref[...] = h_ref[...].astype(BF16)
        acc_ref[...] = jnp.zeros_like(acc_ref)

    u = jnp.dot(hb_ref[...], wu_ref[...], preferred_element_type=F32)
    u = jnp.square(jnp.maximum(u, 0.0))
    acc_ref[...] += jnp.dot(u.astype(BF16), wd_ref[...], preferred_element_type=F32)

    @pl.when(f == pl.num_programs(1) - 1)
    def _():
        z = DEEPNORM_ALPHA * h_ref[...] + acc_ref[...]
        y_ref[...] = _layer_norm_rows(z, g_ref[...], b_ref[...])


def _ffn(h, wu, wd, ln_g, ln_b, tm, tf):
    m = h.shape[0]
    return pl.pallas_call(
        _ffn_kernel,
        grid=(m // tm, D_FF // tf),
        in_specs=[pl.BlockSpec((tm, D_MODEL), lambda i, f: (i, 0)),
                  pl.BlockSpec((D_MODEL, tf), lambda i, f: (0, f)),
                  pl.BlockSpec((tf, D_MODEL), lambda i, f: (f, 0)),
                  pl.BlockSpec((1, D_MODEL), lambda i, f: (0, 0)),
                  pl.BlockSpec((1, D_MODEL), lambda i, f: (0, 0))],
        out_specs=pl.BlockSpec((tm, D_MODEL), lambda i, f: (i, 0)),
        out_shape=jax.ShapeDtypeStruct((m, D_MODEL), F32),
        scratch_shapes=[pltpu.VMEM((tm, D_MODEL), BF16), pltpu.VMEM((tm, D_MODEL), F32)],
        compiler_params=_cparams(2),
        name="ffn_ln2",
    )(h, wu, wd, ln_g, ln_b)


def _attn_sample_kernel(q_ref, kvt_ref, ck_ref, cv_ref, c_ref, sp_ref, sm_ref, cc_ref, spc_ref, smc_ref, sink_ref,
                        o_ref, nk_ref, nv_ref):
    bt = q_ref.shape[0]
    win = ck_ref.shape[2]
    n = kvt_ref.shape[1]
    half = ROT_DIM // 2
    group = N_Q_HEADS // N_KV_HEADS
    c, sp, sm = c_ref[0:1, :], sp_ref[0:1, :], sm_ref[0:1, :]
    sink = sink_ref[:, 0:1]
    row16 = lax.broadcasted_iota(jnp.int32, (N_Q_HEADS, KV_WIDTH), 0)
    lane16 = lax.broadcasted_iota(jnp.int32, (N_Q_HEADS, KV_WIDTH), 1)
    own_kv = (lane16 // HEAD_DIM) == (row16 % N_KV_HEADS)
    urow = row16 // N_KV_HEADS
    pos = lax.broadcasted_iota(jnp.int32, (KV_WIDTH, win), 1)
    seq = lax.broadcasted_iota(jnp.int32, (KV_WIDTH, n), 1)

    kt = kvt_ref[0:KV_WIDTH, :]
    kt = kt * cc_ref[...] + pltpu.roll(kt, KV_WIDTH - half, 0) * spc_ref[...] + pltpu.roll(kt, half, 0) * smc_ref[...]
    vt = kvt_ref[KV_WIDTH:2 * KV_WIDTH, :]

    def rope_row(x):
        return jnp.concatenate([_rope_tile(x[:, LANES * t:LANES * (t + 1)], c, sp, sm)
                                for t in range(x.shape[1] // LANES)], axis=1)

    rng = range(bt)
    me = [seq == pl.program_id(0) * bt + b for b in rng]
    k_col = [jnp.sum(jnp.where(me[b], kt, 0.0), axis=1, keepdims=True) for b in rng]
    v_col = [jnp.sum(jnp.where(me[b], vt, 0.0), axis=1, keepdims=True) for b in rng]
    nk = [jnp.where(pos == win - 1, k_col[b], pltpu.roll(ck_ref[b], win - 1, 1)) for b in rng]
    nv = [jnp.where(pos == win - 1, v_col[b], pltpu.roll(cv_ref[b], win - 1, 1)) for b in rng]
    for b in rng:
        nk_ref[b] = nk[b]
        nv_ref[b] = nv[b]
    qmat = []
    for b in rng:
        q = rope_row(q_ref[b:b + 1, :]) * (HEAD_DIM ** -0.5)
        qb = [jnp.broadcast_to(q[:, KV_WIDTH * u:KV_WIDTH * (u + 1)], (N_Q_HEADS, KV_WIDTH)) for u in range(group)]
        qsel = jnp.where(urow == 0, qb[0], jnp.where(urow == 1, qb[1], jnp.where(urow == 2, qb[2], qb[3])))
        qmat.append(jnp.where(own_kv, qsel, 0.0).astype(BF16))
    s = [jnp.dot(qmat[b], nk[b].astype(BF16), preferred_element_type=F32) for b in rng]
    m = [jnp.maximum(jnp.max(s[b], axis=1, keepdims=True), sink) for b in rng]
    p = [jnp.exp(s[b] - m[b]) for b in rng]
    den = [jnp.sum(p[b], axis=1, keepdims=True) + jnp.exp(sink - m[b]) for b in rng]
    o = [_dot_t(p[b].astype(BF16), nv[b].astype(BF16)) / den[b] for b in rng]
    out_rows = []
    for b in rng:
        ob = jnp.where(own_kv, o[b], 0.0)
        chunks = [jnp.sum(jnp.where(urow == u, ob, 0.0), axis=0, keepdims=True) for u in range(group)]
        out_rows.append(jnp.concatenate(chunks, axis=1))
    o_ref[...] = jnp.concatenate(out_rows, axis=0).astype(o_ref.dtype)


def _attn_sample(q_perm, kv_new_t, cache_kt, cache_vt, sink_mat, bt):
    n, win = cache_kt.shape[0], cache_kt.shape[2]
    tabs = _rope_tables(jnp.full((1,), PAST_LEN, jnp.int32))
    c, sp, sm = (jnp.broadcast_to(z, (8, LANES)) for z in tabs)
    cc, spc, smc = (jnp.broadcast_to(jnp.tile(z, (1, KV_WIDTH // LANES)).T, (KV_WIDTH, n)) for z in tabs)
    small = lambda shape: pl.BlockSpec(shape, lambda i: (0, 0))
    cache_spec = pl.BlockSpec((bt, KV_WIDTH, win), lambda i: (i, 0, 0))
    return pl.pallas_call(
        _attn_sample_kernel,
        grid=(n // bt,),
        in_specs=[pl.BlockSpec((bt, ATTN_WIDTH), lambda i: (i, 0)),
                  small((2 * KV_WIDTH, n)),
                  cache_spec, cache_spec,
                  small((8, LANES)), small((8, LANES)), small((8, LANES)),
                  small((KV_WIDTH, n)), small((KV_WIDTH, n)), small((KV_WIDTH, n)),
                  small((N_Q_HEADS, LANES))],
        out_specs=[pl.BlockSpec((bt, ATTN_WIDTH), lambda i: (i, 0)), cache_spec, cache_spec],
        out_shape=[jax.ShapeDtypeStruct((n, ATTN_WIDTH), BF16),
                   jax.ShapeDtypeStruct(cache_kt.shape, F32),
                   jax.ShapeDtypeStruct(cache_vt.shape, F32)],
        compiler_params=_cparams(1),
        name="attn_sample",
    )(q_perm, kv_new_t, cache_kt, cache_vt, c, sp, sm, cc, spc, smc, sink_mat)


def _rwkv_sample_kernel(r_ref, k_ref, v_ref, l_ref, pr_ref, pk_ref, pv_ref, pl_ref, st_ref,
                        mur_ref, muk_ref, muv_ref, mul_ref, w0_ref, a0_ref, kk_ref, ka_ref,
                        rk_ref, gnw_ref, gnb_ref, wd_ref, wi_ref, wg_ref,
                        o_ref, ns_ref, vec_s, keep_s, y_s):
    h = pl.program_id(0)
    hd = HEAD_DIM
    q_a, q_w, q_b, q_k, q_r, q_v = range(6)

    @pl.when(h == 0)
    def _():
        bd2 = _head_block_ones()
        prm = tuple(ref[...] for ref in (mur_ref, muk_ref, muv_ref, mul_ref, w0_ref, a0_ref, kk_ref, ka_ref))
        r, k, v, logw, a_s, b_s, gate = _rwkv_prepare(
            r_ref[...], k_ref[...], v_ref[...], l_ref[...], pr_ref[...], pk_ref[...], pv_ref[...], pl_ref[...],
            prm, wd_ref, wi_ref, wg_ref, bd2)
        for qi, x in enumerate((a_s, jnp.exp(logw), b_s, k, r, v)):
            xt = x.T
            for hh in range(N_RWKV_HEADS):
                vec_s[qi, hh] = xt[hd * hh:hd * (hh + 1), :]
        for qi, x in enumerate((r, k, v, gate)):
            keep_s[qi] = x

    a_h, w_h, b_h, k_h, r_h = (vec_s[qi, h] for qi in (q_a, q_w, q_b, q_k, q_r))
    for i in range(hd):
        s = st_ref[0, i]
        sa = jnp.sum(s * a_h, axis=0, keepdims=True)
        s_new = s * w_h + sa * b_h + vec_s[q_v, h, i:i + 1, :] * k_h
        ns_ref[0, i] = s_new
        y_s[h, i:i + 1, :] = jnp.sum(s_new * r_h, axis=0, keepdims=True)

    @pl.when(h == pl.num_programs(0) - 1)
    def _():
        y = jnp.concatenate([y_s[hh] for hh in range(N_RWKV_HEADS)], axis=0).T
        out = _rwkv_finish(y, keep_s[0], keep_s[1], keep_s[2], keep_s[3], rk_ref[...], gnw_ref[...], gnb_ref[...],
                           _head_block_ones())
        o_ref[...] = out.astype(o_ref.dtype)


def _rwkv_sample(p_all, shift, shift_l, state_t, prm):
    n = state_t.shape[-1]
    wide = lambda col0: pl.BlockSpec((n, RWKV_WIDTH), lambda h: (0, col0 // RWKV_WIDTH))
    vec = pl.BlockSpec((1, RWKV_WIDTH), lambda h: (0, 0))
    vec_l = pl.BlockSpec((1, LORA_PAD), lambda h: (0, 0))
    st_spec = pl.BlockSpec((1, HEAD_DIM, HEAD_DIM, n), lambda h: (h, 0, 0, 0))
    return pl.pallas_call(
        _rwkv_sample_kernel,
        grid=(N_RWKV_HEADS,),
        in_specs=[wide(COL_R), wide(COL_KR), wide(COL_VR),
                  pl.BlockSpec((n, LORA_PAD), lambda h: (0, COL_LORA // LORA_PAD)),
                  wide(0), wide(RWKV_WIDTH), wide(2 * RWKV_WIDTH),
                  pl.BlockSpec((n, LORA_PAD), lambda h: (0, 0)),
                  st_spec,
                  vec, vec, vec, vec_l, vec, vec, vec, vec, vec, vec, vec,
                  pl.BlockSpec((LANES, RWKV_WIDTH), lambda h: (0, 0)),
                  pl.BlockSpec((LANES, RWKV_WIDTH), lambda h: (0, 0)),
                  pl.BlockSpec((2 * LANES, RWKV_WIDTH), lambda h: (0, 0))],
        out_specs=[pl.BlockSpec((n, RWKV_WIDTH), lambda h: (0, 0)), st_spec],
        out_shape=[jax.ShapeDtypeStruct((n, RWKV_WIDTH), BF16), jax.ShapeDtypeStruct(state_t.shape, F32)],
        scratch_shapes=[pltpu.VMEM((6, N_RWKV_HEADS, HEAD_DIM, n), F32),
                        pltpu.VMEM((4, n, RWKV_WIDTH), F32),
                        pltpu.VMEM((N_RWKV_HEADS, HEAD_DIM, n), F32)],
        compiler_params=_cparams(1),
        name="rwkv_sample",
    )(p_all, p_all, p_all, p_all, shift, shift, shift, shift_l, state_t,
      prm["mu_r"], prm["mu_k"], prm["mu_v"], prm["mu_l"], prm["w0"], prm["a0"], prm["k_k"], prm["k_a"],
      prm["r_k"], prm["gn_w"], prm["gn_b"], prm["wd"], prm["wi"], prm["wg"])


def _pack_w_in(w_in):
    wt = w_in.T
    c0 = ATTN_WIDTH
    q = wt[:c0]
    k = wt[c0:c0 + KV_WIDTH]
    v = wt[c0 + KV_WIDTH:c0 + 2 * KV_WIDTH]
    c1 = c0 + 2 * KV_WIDTH
    rkv = wt[c1:c1 + 3 * RWKV_WIDTH]
    lora = wt[c1 + 3 * RWKV_WIDTH:c1 + RWKV_PROJ_WIDTH]
    c2 = c1 + RWKV_PROJ_WIDTH
    gates = wt[c2:c2 + 2 * D_MODEL]
    pad = jnp.zeros((LORA_PAD - LORA_WIDTH, D_MODEL), w_in.dtype)
    return jnp.concatenate([q, rkv, k, v, lora, pad], axis=0).astype(BF16), gates.astype(BF16)


def _swap_head_order(z, outer, inner):
    n = z.shape[0]
    return z.reshape(n, outer, inner, HEAD_DIM).transpose(0, 2, 1, 3).reshape(n, outer * inner * HEAD_DIM)


def _shift_columns(p_rows):
    return jnp.concatenate([p_rows[:, COL_R:COL_R + 3 * RWKV_WIDTH], p_rows[:, COL_LORA:COL_LORA + LORA_WIDTH]], axis=1)


def _forward(x_prompt, x_sample, cache_k_win, cache_v_win, state_shift, state_wkv, w, cfg):
    b, t, _ = x_prompt.shape
    n_s = x_sample.shape[0]
    row = lambda z: z.reshape(1, -1).astype(F32)
    mu = w["mu_shift"]
    wd, wi, wg = _rwkv_lora_weights(w["w_decay_up"], w["w_iclr_up"], w["w_gate_up"])
    prm = dict(
        mu_r=row(mu[:RWKV_WIDTH]), mu_k=row(mu[RWKV_WIDTH:2 * RWKV_WIDTH]), mu_v=row(mu[2 * RWKV_WIDTH:3 * RWKV_WIDTH]),
        mu_l=row(jnp.pad(mu[3 * RWKV_WIDTH:], (0, LORA_PAD - LORA_WIDTH))),
        w0=row(w["w0"]), a0=row(w["a0"]), k_k=row(w["k_k"]), k_a=row(w["k_a"]), r_k=row(w["r_k"]),
        gn_w=row(w["gn_w"]), gn_b=row(w["gn_b"]), wd=wd, wi=wi, wg=wg)
    wpa = w["w_proj_attn"].astype(BF16)
    wpr = w["w_proj_rwkv"].astype(BF16)
    wo = w["w_out"].astype(BF16)
    wu = w["w_up"].astype(BF16)
    wdn = w["w_down"].astype(BF16)
    ln1g, ln1b, ln2g, ln2b = row(w["ln1_g"]), row(w["ln1_b"]), row(w["ln2_g"]), row(w["ln2_b"])
    sinks = w["attn_sinks"].astype(F32)

    xp = x_prompt.reshape(b * t, D_MODEL)
    w_packed, w_gates = _pack_w_in(w["w_in"])
    pp = _inproj(xp, w_packed, cfg["tm_in"], cfg["tn_in"])
    attn_p, kwin_p, vwin_p = _attn_prompt(pp, sinks, b, t)
    rwkv_p, wkv_p = _rwkv_prompt(pp, prm, b, t, cfg["tb_rwkv"], cfg["lw_rwkv"])
    hp = _merge(xp, attn_p, rwkv_p, w_gates, wpa, wpr, wo, ln1g, ln1b, cfg["tm_merge"], cfg["tj_merge"])
    yp = _ffn(hp, wu, wdn, ln2g, ln2b, cfg["tm_ffn"], cfg["tf_ffn"])
    shift_p = _shift_columns(pp.reshape(b, t, PACK_WIDTH)[:, t - 1])

    group = N_Q_HEADS // N_KV_HEADS
    xs = x_sample.reshape(n_s, D_MODEL)
    ps = _inproj(xs, w_packed, n_s, cfg["tn_in"])
    q_perm = _swap_head_order(ps[:, COL_Q:COL_Q + ATTN_WIDTH], N_KV_HEADS, group)
    sink_mat = jnp.broadcast_to(sinks.reshape(N_KV_HEADS, group).T.reshape(N_Q_HEADS, 1), (N_Q_HEADS, LANES))
    win = cache_k_win.shape[1]
    to_t = lambda z: jnp.transpose(z, (0, 2, 3, 1)).reshape(n_s, KV_WIDTH, win)
    from_t = lambda z: jnp.transpose(z.reshape(n_s, N_KV_HEADS, HEAD_DIM, win), (0, 3, 1, 2))[None]
    attn_s, nk_t, nv_t = _attn_sample(q_perm, ps[:, COL_K:COL_K + 2 * KV_WIDTH].T, to_t(cache_k_win), to_t(cache_v_win),
                                      sink_mat, cfg["bt_sample"])
    attn_s = _swap_head_order(attn_s, group, N_KV_HEADS)
    shift_l = jnp.pad(state_shift[:, 3 * RWKV_WIDTH:], ((0, 0), (0, LORA_PAD - LORA_WIDTH)))
    rwkv_s, wkv_t = _rwkv_sample(ps, state_shift, shift_l, jnp.transpose(state_wkv, (1, 2, 3, 0)), prm)
    wkv_s = jnp.transpose(wkv_t, (3, 0, 1, 2))
    hs = _merge(xs, attn_s, rwkv_s, w_gates, wpa, wpr, wo, ln1g, ln1b, n_s, cfg["tj_merge"])
    ys = _ffn(hs, wu, wdn, ln2g, ln2b, n_s, cfg["tf_ffn"])
    shift_s = _shift_columns(ps)

    kv5 = lambda z: z.reshape(1, z.shape[0], z.shape[1], N_KV_HEADS, HEAD_DIM)
    return (yp.reshape(b, t, D_MODEL), ys.reshape(n_s, 1, D_MODEL),
            kv5(kwin_p), kv5(vwin_p), shift_p[None], wkv_p[None],
            from_t(nk_t), from_t(nv_t), shift_s[None], wkv_s[None])


_CFG = dict(tm_in=1024, tn_in=1024, tb_rwkv=128, lw_rwkv=1024, tm_merge=512, tj_merge=512, tm_ffn=512, tf_ffn=1024, bt_sample=8)


def kernel(x_prompt, x_sample, cache_k_win, cache_v_win, state_shift, state_wkv, w_in, attn_sinks, mu_shift, w0,
           w_decay_up, a0, w_iclr_up, w_gate_up, k_k, k_a, r_k, gn_w, gn_b, w_proj_attn, w_proj_rwkv, w_out,
           ln1_g, ln1_b, w_up, w_down, ln2_g, ln2_b):
    w = dict(w_in=w_in[0], attn_sinks=attn_sinks[0], mu_shift=mu_shift[0], w0=w0[0], w_decay_up=w_decay_up[0],
             a0=a0[0], w_iclr_up=w_iclr_up[0], w_gate_up=w_gate_up[0], k_k=k_k[0], k_a=k_a[0], r_k=r_k[0],
             gn_w=gn_w[0], gn_b=gn_b[0], w_proj_attn=w_proj_attn[0], w_proj_rwkv=w_proj_rwkv[0], w_out=w_out[0],
             ln1_g=ln1_g[0], ln1_b=ln1_b[0], w_up=w_up[0], w_down=w_down[0], ln2_g=ln2_g[0], ln2_b=ln2_b[0])
    return _forward(x_prompt, x_sample, cache_k_win[0], cache_v_win[0], state_shift[0], state_wkv[0], w, _CFG)
```

```python
import functools

import jax
import jax.numpy as jnp
import numpy as np
from jax import lax
from jax.experimental import pallas as pl
from jax.experimental.pallas import tpu as pltpu

F32 = jnp.float32
BF16 = jnp.bfloat16

D_MODEL = 2048
HEAD_DIM = 64
N_Q_HEADS = 16
N_KV_HEADS = 4
ATTN_WIDTH = N_Q_HEADS * HEAD_DIM
KV_WIDTH = N_KV_HEADS * HEAD_DIM
WINDOW = 128
ROPE_THETA = 500000.0
ROT_DIM = HEAD_DIM // 4
N_RWKV_HEADS = 16
RWKV_WIDTH = N_RWKV_HEADS * HEAD_DIM
DECAY_LORA = 64
ICLR_LORA = 64
GATE_LORA = 160
LORA_WIDTH = DECAY_LORA + ICLR_LORA + GATE_LORA
RWKV_PROJ_WIDTH = 3 * RWKV_WIDTH + LORA_WIDTH
D_FF = 4 * D_MODEL
PAST_LEN = 16384
DEEPNORM_ALPHA = 2.0 ** 0.25
LN_EPS = 1e-5
GN_EPS = HEAD_DIM * 1e-5
NEG_BIG = -1e30

LANES = 128
HEADS_PER_GROUP = 8
CHUNK = 64
VMEM_LIMIT = 56 * 1024 * 1024

COL_Q = 0
COL_R = 1024
COL_KR = 2048
COL_VR = 3072
COL_K = 4096
COL_V = 4352
COL_LORA = 4608
LORA_PAD = 512
PACK_WIDTH = COL_LORA + LORA_PAD


def _cparams(n_axes):
    return pltpu.CompilerParams(dimension_semantics=("arbitrary",) * n_axes, vmem_limit_bytes=VMEM_LIMIT)


def _sigmoid(x):
    return 1.0 / (1.0 + jnp.exp(-x))


def _softplus(x):
    return jnp.maximum(x, 0.0) + jnp.log(1.0 + jnp.exp(-jnp.abs(x)))


def _layer_norm_rows(z, g, b):
    mu = jnp.mean(z, axis=-1, keepdims=True)
    d = z - mu
    var = jnp.mean(d * d, axis=-1, keepdims=True)
    return d * lax.rsqrt(var + LN_EPS) * g + b


def _dot_t(a, b):
    return lax.dot_general(a, b, (((1,), (1,)), ((), ())), preferred_element_type=F32)


def _split_bf16(x):
    hi = x.astype(BF16)
    lo = (x - hi.astype(F32)).astype(BF16)
    return hi, lo


def _head_block_ones():
    r = lax.broadcasted_iota(jnp.int32, (2 * LANES, LANES), 0)
    c = lax.broadcasted_iota(jnp.int32, (2 * LANES, LANES), 1)
    return jnp.where((r % LANES) // HEAD_DIM == c // HEAD_DIM, 1.0, 0.0).astype(BF16)


def _head_sum(x, bd2):
    outs = []
    for t in range(x.shape[1] // LANES):
        hi, lo = _split_bf16(x[:, LANES * t:LANES * (t + 1)])
        outs.append(jnp.dot(jnp.concatenate([hi, lo], axis=1), bd2, preferred_element_type=F32))
    return outs[0] if len(outs) == 1 else jnp.concatenate(outs, axis=1)


def _inproj_kernel(x_ref, w_ref, o_ref, xb_ref):
    @pl.when(pl.program_id(1) == 0)
    def _():
        xb_ref[...] = x_ref[...].astype(BF16)

    o_ref[...] = _dot_t(xb_ref[...], w_ref[...])


def _inproj(x, w_packed_t, tm, tn):
    m, k = x.shape
    n = w_packed_t.shape[0]
    return pl.pallas_call(
        _inproj_kernel,
        grid=(m // tm, n // tn),
        in_specs=[pl.BlockSpec((tm, k), lambda i, j: (i, 0)),
                  pl.BlockSpec((tn, k), lambda i, j: (j, 0))],
        out_specs=pl.BlockSpec((tm, tn), lambda i, j: (i, j)),
        out_shape=jax.ShapeDtypeStruct((m, n), F32),
        scratch_shapes=[pltpu.VMEM((tm, k), BF16)],
        compiler_params=_cparams(2),
        name="inproj",
    )(x, w_packed_t)


def _rope_tables(pos):
    inv = ROPE_THETA ** (-jnp.arange(0, ROT_DIM, 2, dtype=F32) / ROT_DIM)
    ang = pos.astype(F32)[:, None] * inv[None, :]
    cos, sin = jnp.cos(ang), jnp.sin(ang)
    t = pos.shape[0]
    half = ROT_DIM // 2
    pad = HEAD_DIM - ROT_DIM
    c_head = jnp.concatenate([cos, cos, jnp.ones((t, pad), F32)], axis=1)
    sp_head = jnp.concatenate([-sin, jnp.zeros((t, half + pad), F32)], axis=1)
    sm_head = jnp.concatenate([jnp.zeros((t, half), F32), sin, jnp.zeros((t, pad), F32)], axis=1)
    rep = LANES // HEAD_DIM
    return tuple(jnp.tile(z, (1, rep)) for z in (c_head, sp_head, sm_head))


def _rope_tile(x, c, sp, sm):
    half = ROT_DIM // 2
    return x * c + pltpu.roll(x, LANES - half, 1) * sp + pltpu.roll(x, half, 1) * sm


def _attn_prompt_kernel(q_ref, k_ref, v_ref, c_ref, sp_ref, sm_ref, sink_ref,
                        o_ref, kwin_ref, vwin_ref, kp_ref, vp_ref):
    i = pl.program_id(1)
    blk = WINDOW

    @pl.when(i == 0)
    def _():
        kp_ref[...] = jnp.zeros_like(kp_ref)
        vp_ref[...] = jnp.zeros_like(vp_ref)

    c, sp, sm = c_ref[...], sp_ref[...], sm_ref[...]
    lane = lax.broadcasted_iota(jnp.int32, (2 * blk, LANES), 1)
    lo2 = lane < HEAD_DIM
    lo1 = lax.broadcasted_iota(jnp.int32, (blk, LANES), 1) < HEAD_DIM

    qi = lax.broadcasted_iota(jnp.int32, (blk, 2 * blk), 0)
    kj = lax.broadcasted_iota(jnp.int32, (blk, 2 * blk), 1)
    kmin = jnp.where(i == 0, blk, 0)
    valid = (kj > qi) & (kj <= qi + blk) & (kj >= kmin)

    n_kv_tiles = KV_WIDTH // LANES
    kk_g, va_g, vb_g = [], [], []
    for t in range(n_kv_tiles):
        sl = slice(LANES * t, LANES * (t + 1))
        kcur = _rope_tile(k_ref[:, sl], c, sp, sm)
        vcur = v_ref[:, sl]
        kwin_ref[0, :, sl] = kcur
        vwin_ref[0, :, sl] = vcur
        kall = jnp.concatenate([kp_ref[:, sl], kcur], axis=0)
        vall = jnp.concatenate([vp_ref[:, sl], vcur], axis=0)
        kp_ref[:, sl] = kcur
        vp_ref[:, sl] = vcur
        kswap = pltpu.roll(kall, HEAD_DIM, 1)
        vswap = pltpu.roll(vall, HEAD_DIM, 1)
        kk_g += [jnp.where(lo2, kall, kswap).astype(BF16), jnp.where(lo2, kswap, kall).astype(BF16)]
        va_g += [jnp.where(lo2, vall, 0.0).astype(BF16), jnp.where(lo2, vswap, 0.0).astype(BF16)]
        vb_g += [jnp.where(lo2, 0.0, vswap).astype(BF16), jnp.where(lo2, 0.0, vall).astype(BF16)]

    group = N_Q_HEADS // N_KV_HEADS
    tiles = range(ATTN_WIDTH // LANES)
    qt = [_rope_tile(q_ref[:, LANES * j:LANES * (j + 1)], c, sp, sm) * (HEAD_DIM ** -0.5) for j in tiles]
    for h0 in range(0, N_Q_HEADS, HEADS_PER_GROUP):
        heads = range(h0, h0 + HEADS_PER_GROUP)
        qm = {h: (jnp.where(lo1, qt[h // 2], 0.0) if h % 2 == 0 else jnp.where(lo1, 0.0, qt[h // 2])).astype(BF16) for h in heads}
        sinks = {h: sink_ref[h] for h in heads}
        s = {h: jnp.where(valid, _dot_t(qm[h], kk_g[h // group]), NEG_BIG) for h in heads}
        m = {h: jnp.maximum(jnp.max(s[h], axis=1, keepdims=True), sinks[h]) for h in heads}
        p = {h: jnp.exp(s[h] - m[h]) for h in heads}
        den = {h: jnp.sum(p[h], axis=1, keepdims=True) + jnp.exp(sinks[h] - m[h]) for h in heads}
        o = {h: jnp.dot(p[h].astype(BF16), (va_g if h % 2 == 0 else vb_g)[h // group], preferred_element_type=F32) / den[h]
             for h in heads}
        for j in range(h0 // 2, (h0 + HEADS_PER_GROUP) // 2):
            o_ref[:, LANES * j:LANES * (j + 1)] = (o[2 * j] + o[2 * j + 1]).astype(o_ref.dtype)


def _attn_prompt(p_all, sinks, b, t):
    blk = WINDOW
    nb = t // blk
    c, sp, sm = _rope_tables(jnp.arange(t, dtype=jnp.int32))
    tab_spec = pl.BlockSpec((blk, LANES), lambda bi, i: (i, 0))
    row = lambda bi, i: bi * nb + i
    return pl.pallas_call(
        _attn_prompt_kernel,
        grid=(b, nb),
        in_specs=[pl.BlockSpec((blk, ATTN_WIDTH), lambda bi, i: (row(bi, i), COL_Q // ATTN_WIDTH)),
                  pl.BlockSpec((blk, KV_WIDTH), lambda bi, i: (row(bi, i), COL_K // KV_WIDTH)),
                  pl.BlockSpec((blk, KV_WIDTH), lambda bi, i: (row(bi, i), COL_V // KV_WIDTH)),
                  tab_spec, tab_spec, tab_spec,
                  pl.BlockSpec(memory_space=pltpu.SMEM)],
        out_specs=[pl.BlockSpec((blk, ATTN_WIDTH), lambda bi, i: (row(bi, i), 0)),
                   pl.BlockSpec((1, blk, KV_WIDTH), lambda bi, i: (bi, 0, 0)),
                   pl.BlockSpec((1, blk, KV_WIDTH), lambda bi, i: (bi, 0, 0))],
        out_shape=[jax.ShapeDtypeStruct((b * t, ATTN_WIDTH), BF16),
                   jax.ShapeDtypeStruct((b, blk, KV_WIDTH), F32),
                   jax.ShapeDtypeStruct((b, blk, KV_WIDTH), F32)],
        scratch_shapes=[pltpu.VMEM((blk, KV_WIDTH), F32), pltpu.VMEM((blk, KV_WIDTH), F32)],
        compiler_params=_cparams(2),
        name="attn_prompt",
    )(p_all, p_all, p_all, c, sp, sm, sinks)


def _rwkv_prepare(r_in, k_in, v_in, l_in, pr, pk, pv, plr, prm, wd_ref, wi_ref, wg_ref, bd2):
    mur, muk, muv, mul, w0, a0, k_k, k_a = prm
    r = r_in + (pr - r_in) * mur
    k = k_in + (pk - k_in) * muk
    v = v_in + (pv - v_in) * muv
    ls = l_in + (plr - l_in) * mul
    l0 = ls[:, :LANES]
    lane = lax.broadcasted_iota(jnp.int32, l0.shape, 1)
    z0 = jnp.where(lane < DECAY_LORA, jnp.tanh(l0), l0).astype(BF16)
    zg = _sigmoid(ls[:, LANES:3 * LANES]).astype(BF16)
    dec_up = jnp.dot(z0, wd_ref[...], preferred_element_type=F32)
    icl_up = jnp.dot(z0, wi_ref[...], preferred_element_type=F32)
    gate = jnp.dot(zg, wg_ref[...], preferred_element_type=F32)
    w_log = -_softplus(-(w0 + dec_up)) - 0.5
    logw = -jnp.exp(w_log)
    a_sig = _sigmoid(a0 + icl_up)
    kk = k * k_k
    nrm = jnp.maximum(jnp.sqrt(_head_sum(kk * kk, bd2)), 1e-12)
    kk = kk / nrm
    k = k * (1.0 + (a_sig - 1.0) * k_a)
    return r, k, v, logw, -kk, kk * a_sig, gate


def _rwkv_finish(y, r, k, v, gate, r_k, gn_w, gn_b, bd2):
    mu = _head_sum(y, bd2) * (1.0 / HEAD_DIM)
    d = y - mu
    var = _head_sum(d * d, bd2) * (1.0 / HEAD_DIM)
    yn = d * lax.rsqrt(var + GN_EPS) * gn_w + gn_b
    bonus = _head_sum(r * k * r_k, bd2) * v
    return (yn + bonus) * gate


def _rwkv_prompt_kernel(r_ref, k_ref, v_ref, l_ref,
                        mur_ref, muk_ref, muv_ref, mul_ref, w0_ref, a0_ref, kk_ref, ka_ref,
                        rk_ref, gnw_ref, gnb_ref, wd_ref, wi_ref, wg_ref,
                        o_ref, sout_ref,
                        cr_ref, ck_ref, cv_ref, cl_ref, st_ref, y_ref):
    ti = pl.program_id(2)
    tb, lw = r_ref.shape
    n_pair = lw // LANES
    n_chunk = tb // CHUNK

    @pl.when(ti == 0)
    def _():
        for ref in (cr_ref, ck_ref, cv_ref, cl_ref, st_ref):
            ref[...] = jnp.zeros_like(ref)

    def shifted(x, carry_ref):
        rolled = pltpu.roll(x, 1, 0)
        row = lax.broadcasted_iota(jnp.int32, x.shape, 0)
        prev = jnp.where(row == 0, carry_ref[0:1, :], rolled)
        carry_ref[0:1, :] = x[tb - 1:tb, :]
        return prev

    bd2 = _head_block_ones()
    r_in, k_in, v_in, l_in = r_ref[...], k_ref[...], v_ref[...], l_ref[...]
    prm = tuple(ref[...] for ref in (mur_ref, muk_ref, muv_ref, mul_ref, w0_ref, a0_ref, kk_ref, ka_ref))
    r, k, v, logw, a_s, b_s, gate = _rwkv_prepare(
        r_in, k_in, v_in, l_in,
        shifted(r_in, cr_ref), shifted(k_in, ck_ref), shifted(v_in, cv_ref), shifted(l_in, cl_ref),
        prm, wd_ref, wi_ref, wg_ref, bd2)

    c = CHUNK
    ri = lax.broadcasted_iota(jnp.int32, (c, 3 * c), 0)
    ci = lax.broadcasted_iota(jnp.int32, (c, 3 * c), 1) % c
    tri3 = jnp.where(ri >= ci, 1.0, 0.0).astype(BF16)
    r2 = lax.broadcasted_iota(jnp.int32, (2 * c, LANES), 0)
    l2 = lax.broadcasted_iota(jnp.int32, (2 * c, LANES), 1)
    tt, ss = r2 % c, l2 % c
    causal = ss < tt + r2 // c
    lo1 = lax.broadcasted_iota(jnp.int32, (c, LANES), 1) < HEAD_DIM
    diag_blocks = (r2 // HEAD_DIM) == (l2 // HEAD_DIM)
    eye_side = jnp.where(lax.broadcasted_iota(jnp.int32, (c, LANES), 0) == lax.broadcasted_iota(jnp.int32, (c, LANES), 1) % c,
                         1.0, 0.0)

    def stack_heads(x):
        return jnp.concatenate([jnp.where(lo1, x, 0.0), jnp.where(lo1, 0.0, x)], axis=0)

    def dot_t(a, b):
        return lax.dot_general(a, b, (((1,), (1,)), ((), ())), preferred_element_type=F32)

    def dot_tt(a, b):
        return lax.dot_general(a, b, (((0,), (0,)), ((), ())), preferred_element_type=F32)

    def dot(a, b):
        return jnp.dot(a.astype(BF16), b.astype(BF16), preferred_element_type=F32)

    lanes = [slice(LANES * pi, LANES * (pi + 1)) for pi in range(n_pair)]
    units = [(ch, pi) for ch in range(n_chunk) for pi in range(n_pair)]
    un = range(len(units))
    el = []
    for ch in range(n_chunk):
        rows = slice(c * ch, c * (ch + 1))
        lw_c = logw[rows]
        hi = lw_c.astype(BF16)
        rem = lw_c - hi.astype(F32)
        mid = rem.astype(BF16)
        low = (rem - mid.astype(F32)).astype(BF16)
        lcum = jnp.dot(tri3, jnp.concatenate([hi, mid, low], axis=0), preferred_element_type=F32)
        ltot = lcum[c - 1:c, :]
        p_inv = jnp.exp(-lcum)
        p_tail = jnp.exp(ltot - lcum)
        el.append(dict(aq=a_s[rows] * jnp.exp(lcum - lw_c), rq=r[rows] * jnp.exp(lcum), bk=b_s[rows] * p_inv,
                       kq=k[rows] * p_inv, bt=b_s[rows] * p_tail, kt=k[rows] * p_tail, v=v[rows],
                       p_end=jnp.exp(ltot)))
    op = lambda name, u: el[units[u][0]][name][:, lanes[units[u][1]]]

    ar = [jnp.concatenate([op("aq", u), op("rq", u)], axis=0).astype(BF16) for u in un]
    gb = [jnp.where(causal, dot_t(ar[u], stack_heads(op("bk", u)).astype(BF16)), 0.0) for u in un]
    gk = [jnp.where(causal, dot_t(ar[u], stack_heads(op("kq", u)).astype(BF16)), 0.0) for u in un]
    gv = [dot(gk[u], stack_heads(op("v", u))) for u in un]
    pw = [gb[u][:c] for u in un]
    tm = [eye_side + pw[u] for u in un]
    for level in range(1, 6):
        bd = [stack_heads(pw[u]).astype(BF16) for u in un]
        if level == 1:
            pw = [jnp.dot(pw[u].astype(BF16), bd[u], preferred_element_type=F32) for u in un]
            bd = [stack_heads(pw[u]).astype(BF16) for u in un]
        if level < 5:
            both = [jnp.dot(jnp.concatenate([tm[u], pw[u]], axis=0).astype(BF16), bd[u], preferred_element_type=F32)
                    for u in un]
            tm = [tm[u] + both[u][:c] for u in un]
            pw = [both[u][c:] for u in un]
        else:
            tm = [tm[u] + jnp.dot(tm[u].astype(BF16), bd[u], preferred_element_type=F32) for u in un]
    tax = [dot(tm[u], jnp.concatenate([stack_heads(op("aq", u)), stack_heads(gv[u][:c])], axis=1)) for u in un]
    taq = [tax[u][:, :LANES] for u in un]
    txv = [tax[u][:, LANES:] for u in un]
    arx = [dot(gb[u][c:], jnp.concatenate([stack_heads(taq[u]), stack_heads(txv[u])], axis=1)) for u in un]
    mb = [jnp.where(diag_blocks, dot_tt(op("bt", u).astype(BF16), taq[u].astype(BF16)), 0.0).astype(BF16) for u in un]
    cct = [jnp.where(diag_blocks,
                     dot_tt(jnp.concatenate([txv[u], op("v", u)], axis=0).astype(BF16),
                            jnp.concatenate([op("bt", u), op("kt", u)], axis=0).astype(BF16)), 0.0) for u in un]
    rqp = [(op("rq", u) + arx[u][:, :LANES]).astype(BF16) for u in un]
    yc = [gv[u][c:] + arx[u][:, LANES:] for u in un]
    for u, (ch, pi) in enumerate(units):
        s_old = st_ref[pi]
        sb = s_old.astype(BF16)
        y_ref[c * ch:c * (ch + 1), lanes[pi]] = dot_t(rqp[u], sb) + yc[u]
        st_ref[pi] = s_old * op("p_end", u) + dot_t(sb, mb[u]) + cct[u]

    out = _rwkv_finish(y_ref[...], r, k, v, gate, rk_ref[...], gnw_ref[...], gnb_ref[...], bd2)
    o_ref[...] = out.astype(o_ref.dtype)

    @pl.when(ti == pl.num_programs(2) - 1)
    def _():
        for pi in range(n_pair):
            s = st_ref[pi]
            sout_ref[0, 2 * pi] = s[:HEAD_DIM, :HEAD_DIM]
            sout_ref[0, 2 * pi + 1] = s[HEAD_DIM:, HEAD_DIM:]


def _rwkv_lora_weights(w_decay_up, w_iclr_up, w_gate_up):
    z64 = jnp.zeros((DECAY_LORA, RWKV_WIDTH), F32)
    wd = jnp.concatenate([w_decay_up, z64], axis=0).astype(BF16)
    wi = jnp.concatenate([z64, w_iclr_up], axis=0).astype(BF16)
    wg = jnp.concatenate([w_gate_up, jnp.zeros((2 * LANES - GATE_LORA, RWKV_WIDTH), F32)], axis=0).astype(BF16)
    return wd, wi, wg


def _rwkv_prompt(p_all, prm, b, t, tb, lw):
    nt = t // tb
    ns = RWKV_WIDTH // lw
    row = lambda bi, si, ti: bi * nt + ti
    col_spec = lambda col0: pl.BlockSpec((tb, lw), lambda bi, si, ti: (row(bi, si, ti), col0 // lw + si))
    vec = pl.BlockSpec((1, lw), lambda bi, si, ti: (0, si))
    vec_l = pl.BlockSpec((1, LORA_PAD), lambda bi, si, ti: (0, 0))
    return pl.pallas_call(
        _rwkv_prompt_kernel,
        grid=(b, ns, nt),
        in_specs=[col_spec(COL_R), col_spec(COL_KR), col_spec(COL_VR),
                  pl.BlockSpec((tb, LORA_PAD), lambda bi, si, ti: (row(bi, si, ti), COL_LORA // LORA_PAD)),
                  vec, vec, vec, vec_l, vec, vec, vec, vec, vec, vec, vec,
                  pl.BlockSpec((LANES, lw), lambda bi, si, ti: (0, si)),
                  pl.BlockSpec((LANES, lw), lambda bi, si, ti: (0, si)),
                  pl.BlockSpec((2 * LANES, lw), lambda bi, si, ti: (0, si))],
        out_specs=[pl.BlockSpec((tb, lw), lambda bi, si, ti: (row(bi, si, ti), si)),
                   pl.BlockSpec((1, 2 * (lw // LANES), HEAD_DIM, HEAD_DIM), lambda bi, si, ti: (bi, si, 0, 0))],
        out_shape=[jax.ShapeDtypeStruct((b * t, RWKV_WIDTH), BF16),
                   jax.ShapeDtypeStruct((b, N_RWKV_HEADS, HEAD_DIM, HEAD_DIM), F32)],
        scratch_shapes=[pltpu.VMEM((8, lw), F32), pltpu.VMEM((8, lw), F32), pltpu.VMEM((8, lw), F32),
                        pltpu.VMEM((8, LORA_PAD), F32),
                        pltpu.VMEM((lw // LANES, LANES, LANES), F32),
                        pltpu.VMEM((tb, lw), F32)],
        compiler_params=_cparams(3),
        name="rwkv_prompt",
    )(p_all, p_all, p_all, p_all,
      prm["mu_r"], prm["mu_k"], prm["mu_v"], prm["mu_l"], prm["w0"], prm["a0"], prm["k_k"], prm["k_a"],
      prm["r_k"], prm["gn_w"], prm["gn_b"], prm["wd"], prm["wi"], prm["wg"])


def _merge_kernel(x_ref, ao_ref, ro_ref, wga_ref, wgb_ref, wpa_ref, wpr_ref, wo_ref, g_ref, b_ref, h_ref,
                  xb_ref, acc_ref):
    j = pl.program_id(1)

    @pl.when(j == 0)
    def _():
        xb_ref[...] = x_ref[...].astype(BF16)
        acc_ref[...] = jnp.zeros_like(acc_ref)

    xb = xb_ref[...]
    ga = _dot_t(xb, wga_ref[...])
    gb = _dot_t(xb, wgb_ref[...])
    a = jnp.dot(ao_ref[...], wpa_ref[...], preferred_element_type=F32)
    r = jnp.dot(ro_ref[...], wpr_ref[...], preferred_element_type=F32)
    m = _sigmoid(ga) * a + _sigmoid(gb) * r
    acc_ref[...] += jnp.dot(m.astype(BF16), wo_ref[...], preferred_element_type=F32)

    @pl.when(j == pl.num_programs(1) - 1)
    def _():
        z = DEEPNORM_ALPHA * x_ref[...] + acc_ref[...]
        h_ref[...] = _layer_norm_rows(z, g_ref[...], b_ref[...])


def _merge(x, attn_o, rwkv_o, w_gates, wpa, wpr, wo, ln_g, ln_b, tm, tj):
    m = x.shape[0]
    nj = D_MODEL // tj
    return pl.pallas_call(
        _merge_kernel,
        grid=(m // tm, nj),
        in_specs=[pl.BlockSpec((tm, D_MODEL), lambda i, j: (i, 0)),
                  pl.BlockSpec((tm, ATTN_WIDTH), lambda i, j: (i, 0)),
                  pl.BlockSpec((tm, RWKV_WIDTH), lambda i, j: (i, 0)),
                  pl.BlockSpec((tj, D_MODEL), lambda i, j: (j, 0)),
                  pl.BlockSpec((tj, D_MODEL), lambda i, j: (nj + j, 0)),
                  pl.BlockSpec((ATTN_WIDTH, tj), lambda i, j: (0, j)),
                  pl.BlockSpec((RWKV_WIDTH, tj), lambda i, j: (0, j)),
                  pl.BlockSpec((tj, D_MODEL), lambda i, j: (j, 0)),
                  pl.BlockSpec((1, D_MODEL), lambda i, j: (0, 0)),
                  pl.BlockSpec((1, D_MODEL), lambda i, j: (0, 0))],
        out_specs=pl.BlockSpec((tm, D_MODEL), lambda i, j: (i, 0)),
        out_shape=jax.ShapeDtypeStruct((m, D_MODEL), F32),
        scratch_shapes=[pltpu.VMEM((tm, D_MODEL), BF16), pltpu.VMEM((tm, D_MODEL), F32)],
        compiler_params=_cparams(2),
        name="merge_ln1",
    )(x, attn_o, rwkv_o, w_gates, w_gates, wpa, wpr, wo, ln_g, ln_b)


def _ffn_kernel(h_ref, wu_ref, wd_ref, g_ref, b_ref, y_ref, hb_ref, acc_ref):
    f = pl.program_id(1)

    @pl.when(f == 0)
    def _():
        hb_ref[...] = h_ref[...].astype(BF16)
        acc_ref[...] = jnp.zeros_like(acc_ref)

    u = jnp.dot(hb_ref[...], wu_ref[...], preferred_element_type=F32)
    u = jnp.square(jnp.maximum(u, 0.0))
    acc_ref[...] += jnp.dot(u.astype(BF16), wd_ref[...], preferred_element_type=F32)

    @pl.when(f == pl.num_programs(1) - 1)
    def _():
        z = DEEPNORM_ALPHA * h_ref[...] + acc_ref[...]
        y_ref[...] = _layer_norm_rows(z, g_ref[...], b_ref[...])


def _ffn(h, wu, wd, ln_g, ln_b, tm, tf):
    m = h.shape[0]
    return pl.pallas_call(
        _ffn_kernel,
        grid=(m // tm, D_FF // tf),
        in_specs=[pl.BlockSpec((tm, D_MODEL), lambda i, f: (i, 0)),
                  pl.BlockSpec((D_MODEL, tf), lambda i, f: (0, f)),
                  pl.BlockSpec((tf, D_MODEL), lambda i, f: (f, 0)),
                  pl.BlockSpec((1, D_MODEL), lambda i, f: (0, 0)),
                  pl.BlockSpec((1, D_MODEL), lambda i, f: (0, 0))],
        out_specs=pl.BlockSpec((tm, D_MODEL), lambda i, f: (i, 0)),
        out_shape=jax.ShapeDtypeStruct((m, D_MODEL), F32),
        scratch_shapes=[pltpu.VMEM((tm, D_MODEL), BF16), pltpu.VMEM((tm, D_MODEL), F32)],
        compiler_params=_cparams(2),
        name="ffn_ln2",
    )(h, wu, wd, ln_g, ln_b)


def _attn_sample_kernel(q_ref, kvt_ref, ck_ref, cv_ref, c_ref, sp_ref, sm_ref, cc_ref, spc_ref, smc_ref, sink_ref,
                        o_ref, nk_ref, nv_ref):
    bt = q_ref.shape[0]
    win = ck_ref.shape[2]
    n = kvt_ref.shape[1]
    half = ROT_DIM // 2
    group = N_Q_HEADS // N_KV_HEADS
    c, sp, sm = c_ref[0:1, :], sp_ref[0:1, :], sm_ref[0:1, :]
    sink = sink_ref[:, 0:1]
    row16 = lax.broadcasted_iota(jnp.int32, (N_Q_HEADS, KV_WIDTH), 0)
    lane16 = lax.broadcasted_iota(jnp.int32, (N_Q_HEADS, KV_WIDTH), 1)
    own_kv = (lane16 // HEAD_DIM) == (row16 % N_KV_HEADS)
    urow = row16 // N_KV_HEADS
    pos = lax.broadcasted_iota(jnp.int32, (KV_WIDTH, win), 1)
    seq = lax.broadcasted_iota(jnp.int32, (KV_WIDTH, n), 1)

    kt = kvt_ref[0:KV_WIDTH, :]
    kt = kt * cc_ref[...] + pltpu.roll(kt, KV_WIDTH - half, 0) * spc_ref[...] + pltpu.roll(kt, half, 0) * smc_ref[...]
    vt = kvt_ref[KV_WIDTH:2 * KV_WIDTH, :]

    def rope_row(x):
        return jnp.concatenate([_rope_tile(x[:, LANES * t:LANES * (t + 1)], c, sp, sm)
                                for t in range(x.shape[1] // LANES)], axis=1)

    rng = range(bt)
    me = [seq == pl.program_id(0) * bt + b for b in rng]
    k_col = [jnp.sum(jnp.where(me[b], kt, 0.0), axis=1, keepdims=True) for b in rng]
    v_col = [jnp.sum(jnp.where(me[b], vt, 0.0), axis=1, keepdims=True) for b in rng]
    nk = [jnp.where(pos == win - 1, k_col[b], pltpu.roll(ck_ref[b], win - 1, 1)) for b in rng]
    nv = [jnp.where(pos == win - 1, v_col[b], pltpu.roll(cv_ref[b], win - 1, 1)) for b in rng]
    for b in rng:
        nk_ref[b] = nk[b]
        nv_ref[b] = nv[b]
    qmat = []
    for b in rng:
        q = rope_row(q_ref[b:b + 1, :]) * (HEAD_DIM ** -0.5)
        qb = [jnp.broadcast_to(q[:, KV_WIDTH * u:KV_WIDTH * (u + 1)], (N_Q_HEADS, KV_WIDTH)) for u in range(group)]
        qsel = jnp.where(urow == 0, qb[0], jnp.where(urow == 1, qb[1], jnp.where(urow == 2, qb[2], qb[3])))
        qmat.append(jnp.where(own_kv, qsel, 0.0).astype(BF16))
    s = [jnp.dot(qmat[b], nk[b].astype(BF16), preferred_element_type=F32) for b in rng]
    m = [jnp.maximum(jnp.max(s[b], axis=1, keepdims=True), sink) for b in rng]
    p = [jnp.exp(s[b] - m[b]) for b in rng]
    den = [jnp.sum(p[b], axis=1, keepdims=True) + jnp.exp(sink - m[b]) for b in rng]
    o = [_dot_t(p[b].astype(BF16), nv[b].astype(BF16)) / den[b] for b in rng]
    out_rows = []
    for b in rng:
        ob = jnp.where(own_kv, o[b], 0.0)
        chunks = [jnp.sum(jnp.where(urow == u, ob, 0.0), axis=0, keepdims=True) for u in range(group)]
        out_rows.append(jnp.concatenate(chunks, axis=1))
    o_ref[...] = jnp.concatenate(out_rows, axis=0).astype(o_ref.dtype)


def _attn_sample(q_perm, kv_new_t, cache_kt, cache_vt, sink_mat, bt):
    n, win = cache_kt.shape[0], cache_kt.shape[2]
    tabs = _rope_tables(jnp.full((1,), PAST_LEN, jnp.int32))
    c, sp, sm = (jnp.broadcast_to(z, (8, LANES)) for z in tabs)
    cc, spc, smc = (jnp.broadcast_to(jnp.tile(z, (1, KV_WIDTH // LANES)).T, (KV_WIDTH, n)) for z in tabs)
    small = lambda shape: pl.BlockSpec(shape, lambda i: (0, 0))
    cache_spec = pl.BlockSpec((bt, KV_WIDTH, win), lambda i: (i, 0, 0))
    return pl.pallas_call(
        _attn_sample_kernel,
        grid=(n // bt,),
        in_specs=[pl.BlockSpec((bt, ATTN_WIDTH), lambda i: (i, 0)),
                  small((2 * KV_WIDTH, n)),
                  cache_spec, cache_spec,
                  small((8, LANES)), small((8, LANES)), small((8, LANES)),
                  small((KV_WIDTH, n)), small((KV_WIDTH, n)), small((KV_WIDTH, n)),
                  small((N_Q_HEADS, LANES))],
        out_specs=[pl.BlockSpec((bt, ATTN_WIDTH), lambda i: (i, 0)), cache_spec, cache_spec],
        out_shape=[jax.ShapeDtypeStruct((n, ATTN_WIDTH), BF16),
                   jax.ShapeDtypeStruct(cache_kt.shape, F32),
                   jax.ShapeDtypeStruct(cache_vt.shape, F32)],
        compiler_params=_cparams(1),
        name="attn_sample",
    )(q_perm, kv_new_t, cache_kt, cache_vt, c, sp, sm, cc, spc, smc, sink_mat)


def _rwkv_sample_kernel(r_ref, k_ref, v_ref, l_ref, pr_ref, pk_ref, pv_ref, pl_ref, st_ref,
                        mur_ref, muk_ref, muv_ref, mul_ref, w0_ref, a0_ref, kk_ref, ka_ref,
                        rk_ref, gnw_ref, gnb_ref, wd_ref, wi_ref, wg_ref,
                        o_ref, ns_ref, vec_s, keep_s, y_s):
    h = pl.program_id(0)
    hd = HEAD_DIM
    q_a, q_w, q_b, q_k, q_r, q_v = range(6)

    @pl.when(h == 0)
    def _():
        bd2 = _head_block_ones()
        prm = tuple(ref[...] for ref in (mur_ref, muk_ref, muv_ref, mul_ref, w0_ref, a0_ref, kk_ref, ka_ref))
        r, k, v, logw, a_s, b_s, gate = _rwkv_prepare(
            r_ref[...], k_ref[...], v_ref[...], l_ref[...], pr_ref[...], pk_ref[...], pv_ref[...], pl_ref[...],
            prm, wd_ref, wi_ref, wg_ref, bd2)
        for qi, x in enumerate((a_s, jnp.exp(logw), b_s, k, r, v)):
            xt = x.T
            for hh in range(N_RWKV_HEADS):
                vec_s[qi, hh] = xt[hd * hh:hd * (hh + 1), :]
        for qi, x in enumerate((r, k, v, gate)):
            keep_s[qi] = x

    a_h, w_h, b_h, k_h, r_h = (vec_s[qi, h] for qi in (q_a, q_w, q_b, q_k, q_r))
    for i in range(hd):
        s = st_ref[0, i]
        sa = jnp.sum(s * a_h, axis=0, keepdims=True)
        s_new = s * w_h + sa * b_h + vec_s[q_v, h, i:i + 1, :] * k_h
        ns_ref[0, i] = s_new
        y_s[h, i:i + 1, :] = jnp.sum(s_new * r_h, axis=0, keepdims=True)

    @pl.when(h == pl.num_programs(0) - 1)
    def _():
        y = jnp.concatenate([y_s[hh] for hh in range(N_RWKV_HEADS)], axis=0).T
        out = _rwkv_finish(y, keep_s[0], keep_s[1], keep_s[2], keep_s[3], rk_ref[...], gnw_ref[...], gnb_ref[...],
                           _head_block_ones())
        o_ref[...] = out.astype(o_ref.dtype)


def _rwkv_sample(p_all, shift, shift_l, state_t, prm):
    n = state_t.shape[-1]
    wide = lambda col0: pl.BlockSpec((n, RWKV_WIDTH), lambda h: (0, col0 // RWKV_WIDTH))
    vec = pl.BlockSpec((1, RWKV_WIDTH), lambda h: (0, 0))
    vec_l = pl.BlockSpec((1, LORA_PAD), lambda h: (0, 0))
    st_spec = pl.BlockSpec((1, HEAD_DIM, HEAD_DIM, n), lambda h: (h, 0, 0, 0))
    return pl.pallas_call(
        _rwkv_sample_kernel,
        grid=(N_RWKV_HEADS,),
        in_specs=[wide(COL_R), wide(COL_KR), wide(COL_VR),
                  pl.BlockSpec((n, LORA_PAD), lambda h: (0, COL_LORA // LORA_PAD)),
                  wide(0), wide(RWKV_WIDTH), wide(2 * RWKV_WIDTH),
                  pl.BlockSpec((n, LORA_PAD), lambda h: (0, 0)),
                  st_spec,
                  vec, vec, vec, vec_l, vec, vec, vec, vec, vec, vec, vec,
                  pl.BlockSpec((LANES, RWKV_WIDTH), lambda h: (0, 0)),
                  pl.BlockSpec((LANES, RWKV_WIDTH), lambda h: (0, 0)),
                  pl.BlockSpec((2 * LANES, RWKV_WIDTH), lambda h: (0, 0))],
        out_specs=[pl.BlockSpec((n, RWKV_WIDTH), lambda h: (0, 0)), st_spec],
        out_shape=[jax.ShapeDtypeStruct((n, RWKV_WIDTH), BF16), jax.ShapeDtypeStruct(state_t.shape, F32)],
        scratch_shapes=[pltpu.VMEM((6, N_RWKV_HEADS, HEAD_DIM, n), F32),
                        pltpu.VMEM((4, n, RWKV_WIDTH), F32),
                        pltpu.VMEM((N_RWKV_HEADS, HEAD_DIM, n), F32)],
        compiler_params=_cparams(1),
        name="rwkv_sample",
    )(p_all, p_all, p_all, p_all, shift, shift, shift, shift_l, state_t,
      prm["mu_r"], prm["mu_k"], prm["mu_v"], prm["mu_l"], prm["w0"], prm["a0"], prm["k_k"], prm["k_a"],
      prm["r_k"], prm["gn_w"], prm["gn_b"], prm["wd"], prm["wi"], prm["wg"])


def _pack_w_in(w_in):
    wt = w_in.T
    c0 = ATTN_WIDTH
    q = wt[:c0]
    k = wt[c0:c0 + KV_WIDTH]
    v = wt[c0 + KV_WIDTH:c0 + 2 * KV_WIDTH]
    c1 = c0 + 2 * KV_WIDTH
    rkv = wt[c1:c1 + 3 * RWKV_WIDTH]
    lora = wt[c1 + 3 * RWKV_WIDTH:c1 + RWKV_PROJ_WIDTH]
    c2 = c1 + RWKV_PROJ_WIDTH
    gates = wt[c2:c2 + 2 * D_MODEL]
    pad = jnp.zeros((LORA_PAD - LORA_WIDTH, D_MODEL), w_in.dtype)
    return jnp.concatenate([q, rkv, k, v, lora, pad], axis=0).astype(BF16), gates.astype(BF16)


def _swap_head_order(z, outer, inner):
    n = z.shape[0]
    return z.reshape(n, outer, inner, HEAD_DIM).transpose(0, 2, 1, 3).reshape(n, outer * inner * HEAD_DIM)


def _shift_columns(p_rows):
    return jnp.concatenate([p_rows[:, COL_R:COL_R + 3 * RWKV_WIDTH], p_rows[:, COL_LORA:COL_LORA + LORA_WIDTH]], axis=1)


def _forward(x_prompt, x_sample, cache_k_win, cache_v_win, state_shift, state_wkv, w, cfg):
    b, t, _ = x_prompt.shape
    n_s = x_sample.shape[0]
    row = lambda z: z.reshape(1, -1).astype(F32)
    mu = w["mu_shift"]
    wd, wi, wg = _rwkv_lora_weights(w["w_decay_up"], w["w_iclr_up"], w["w_gate_up"])
    prm = dict(
        mu_r=row(mu[:RWKV_WIDTH]), mu_k=row(mu[RWKV_WIDTH:2 * RWKV_WIDTH]), mu_v=row(mu[2 * RWKV_WIDTH:3 * RWKV_WIDTH]),
        mu_l=row(jnp.pad(mu[3 * RWKV_WIDTH:], (0, LORA_PAD - LORA_WIDTH))),
        w0=row(w["w0"]), a0=row(w["a0"]), k_k=row(w["k_k"]), k_a=row(w["k_a"]), r_k=row(w["r_k"]),
        gn_w=row(w["gn_w"]), gn_b=row(w["gn_b"]), wd=wd, wi=wi, wg=wg)
    wpa = w["w_proj_attn"].astype(BF16)
    wpr = w["w_proj_rwkv"].astype(BF16)
    wo = w["w_out"].astype(BF16)
    wu = w["w_up"].astype(BF16)
    wdn = w["w_down"].astype(BF16)
    ln1g, ln1b, ln2g, ln2b = row(w["ln1_g"]), row(w["ln1_b"]), row(w["ln2_g"]), row(w["ln2_b"])
    sinks = w["attn_sinks"].astype(F32)

    xp = x_prompt.reshape(b * t, D_MODEL)
    w_packed, w_gates = _pack_w_in(w["w_in"])
    pp = _inproj(xp, w_packed, cfg["tm_in"], cfg["tn_in"])
    attn_p, kwin_p, vwin_p = _attn_prompt(pp, sinks, b, t)
    rwkv_p, wkv_p = _rwkv_prompt(pp, prm, b, t, cfg["tb_rwkv"], cfg["lw_rwkv"])
    hp = _merge(xp, attn_p, rwkv_p, w_gates, wpa, wpr, wo, ln1g, ln1b, cfg["tm_merge"], cfg["tj_merge"])
    yp = _ffn(hp, wu, wdn, ln2g, ln2b, cfg["tm_ffn"], cfg["tf_ffn"])
    shift_p = _shift_columns(pp.reshape(b, t, PACK_WIDTH)[:, t - 1])

    group = N_Q_HEADS // N_KV_HEADS
    xs = x_sample.reshape(n_s, D_MODEL)
    ps = _inproj(xs, w_packed, n_s, cfg["tn_in"])
    q_perm = _swap_head_order(ps[:, COL_Q:COL_Q + ATTN_WIDTH], N_KV_HEADS, group)
    sink_mat = jnp.broadcast_to(sinks.reshape(N_KV_HEADS, group).T.reshape(N_Q_HEADS, 1), (N_Q_HEADS, LANES))
    win = cache_k_win.shape[1]
    to_t = lambda z: jnp.transpose(z, (0, 2, 3, 1)).reshape(n_s, KV_WIDTH, win)
    from_t = lambda z: jnp.transpose(z.reshape(n_s, N_KV_HEADS, HEAD_DIM, win), (0, 3, 1, 2))[None]
    attn_s, nk_t, nv_t = _attn_sample(q_perm, ps[:, COL_K:COL_K + 2 * KV_WIDTH].T, to_t(cache_k_win), to_t(cache_v_win),
                                      sink_mat, cfg["bt_sample"])
    attn_s = _swap_head_order(attn_s, group, N_KV_HEADS)
    shift_l = jnp.pad(state_shift[:, 3 * RWKV_WIDTH:], ((0, 0), (0, LORA_PAD - LORA_WIDTH)))
    rwkv_s, wkv_t = _rwkv_sample(ps, state_shift, shift_l, jnp.transpose(state_wkv, (1, 2, 3, 0)), prm)
    wkv_s = jnp.transpose(wkv_t, (3, 0, 1, 2))
    hs = _merge(xs, attn_s, rwkv_s, w_gates, wpa, wpr, wo, ln1g, ln1b, n_s, cfg["tj_merge"])
    ys = _ffn(hs, wu, wdn, ln2g, ln2b, n_s, cfg["tf_ffn"])
    shift_s = _shift_columns(ps)

    kv5 = lambda z: z.reshape(1, z.shape[0], z.shape[1], N_KV_HEADS, HEAD_DIM)
    return (yp.reshape(b, t, D_MODEL), ys.reshape(n_s, 1, D_MODEL),
            kv5(kwin_p), kv5(vwin_p), shift_p[None], wkv_p[None],
            from_t(nk_t), from_t(nv_t), shift_s[None], wkv_s[None])


_CFG = dict(tm_in=1024, tn_in=1024, tb_rwkv=128, lw_rwkv=1024, tm_merge=512, tj_merge=512, tm_ffn=512, tf_ffn=1024, bt_sample=8)


def kernel(x_prompt, x_sample, cache_k_win, cache_v_win, state_shift, state_wkv, w_in, attn_sinks, mu_shift, w0,
           w_decay_up, a0, w_iclr_up, w_gate_up, k_k, k_a, r_k, gn_w, gn_b, w_proj_attn, w_proj_rwkv, w_out,
           ln1_g, ln1_b, w_up, w_down, ln2_g, ln2_b):
    w = dict(w_in=w_in[0], attn_sinks=attn_sinks[0], mu_shift=mu_shift[0], w0=w0[0], w_decay_up=w_decay_up[0],
             a0=a0[0], w_iclr_up=w_iclr_up[0], w_gate_up=w_gate_up[0], k_k=k_k[0], k_a=k_a[0], r_k=r_k[0],
             gn_w=gn_w[0], gn_b=gn_b[0], w_proj_attn=w_proj_attn[0], w_proj_rwkv=w_proj_rwkv[0], w_out=w_out[0],
             ln1_g=ln1_g[0], ln1_b=ln1_b[0], w_up=w_up[0], w_down=w_down[0], ln2_g=ln2_g[0], ln2_b=ln2_b[0])
    return _forward(x_prompt, x_sample, cache_k_win[0], cache_v_win[0], state_shift[0], state_wkv[0], w, _CFG)
```

```python
import jax
import jax.numpy as jnp
from jax import lax
from jax.experimental import pallas as pl
from jax.experimental.pallas import tpu as pltpu

F32 = jnp.float32
BF16 = jnp.bfloat16

D_MODEL = 2048
HEAD_DIM = 64
N_Q_HEADS = 16
N_KV_HEADS = 4
ATTN_WIDTH = N_Q_HEADS * HEAD_DIM
KV_WIDTH = N_KV_HEADS * HEAD_DIM
WINDOW = 128
ROPE_THETA = 500000.0
ROT_DIM = HEAD_DIM // 4
N_RWKV_HEADS = 16
RWKV_WIDTH = N_RWKV_HEADS * HEAD_DIM
DECAY_LORA = 64
ICLR_LORA = 64
GATE_LORA = 160
LORA_WIDTH = DECAY_LORA + ICLR_LORA + GATE_LORA
RWKV_PROJ_WIDTH = 3 * RWKV_WIDTH + LORA_WIDTH
D_FF = 4 * D_MODEL
PAST_LEN = 16384
DEEPNORM_ALPHA = 2.0 ** 0.25
LN_EPS = 1e-5
GN_EPS = HEAD_DIM * 1e-5
NEG_BIG = -1e30

LANES = 128
HEADS_PER_GROUP = 8
CHUNK = 64
VMEM_LIMIT = 56 * 1024 * 1024

COL_Q = 0
COL_R = 1024
COL_KR = 2048
COL_VR = 3072
COL_K = 4096
COL_V = 4352
COL_LORA = 4608
LORA_PAD = 512
PACK_WIDTH = COL_LORA + LORA_PAD


def _cparams(n_axes):
    return pltpu.CompilerParams(dimension_semantics=("arbitrary",) * n_axes, vmem_limit_bytes=VMEM_LIMIT)


def _sigmoid(x):
    return 1.0 / (1.0 + jnp.exp(-x))


def _softplus(x):
    return jnp.maximum(x, 0.0) + jnp.log(1.0 + jnp.exp(-jnp.abs(x)))


def _layer_norm_rows(z, g, b):
    mu = jnp.mean(z, axis=-1, keepdims=True)
    d = z - mu
    var = jnp.mean(d * d, axis=-1, keepdims=True)
    return d * lax.rsqrt(var + LN_EPS) * g + b


def _dot_t(a, b):
    return lax.dot_general(a, b, (((1,), (1,)), ((), ())), preferred_element_type=F32)


def _split_bf16(x):
    hi = x.astype(BF16)
    lo = (x - hi.astype(F32)).astype(BF16)
    return hi, lo


def _head_block_ones():
    r = lax.broadcasted_iota(jnp.int32, (2 * LANES, LANES), 0)
    c = lax.broadcasted_iota(jnp.int32, (2 * LANES, LANES), 1)
    return jnp.where((r % LANES) // HEAD_DIM == c // HEAD_DIM, 1.0, 0.0).astype(BF16)


def _head_sum(x, bd2):
    outs = []
    for t in range(x.shape[1] // LANES):
        hi, lo = _split_bf16(x[:, LANES * t:LANES * (t + 1)])
        outs.append(jnp.dot(jnp.concatenate([hi, lo], axis=1), bd2, preferred_element_type=F32))
    return outs[0] if len(outs) == 1 else jnp.concatenate(outs, axis=1)


def _inproj_kernel(x_ref, w_ref, o_ref, xb_ref):
    @pl.when(pl.program_id(1) == 0)
    def _():
        xb_ref[...] = x_ref[...].astype(BF16)

    o_ref[...] = _dot_t(xb_ref[...], w_ref[...])


def _inproj(x, w_packed_t, tm, tn):
    m, k = x.shape
    n = w_packed_t.shape[0]
    return pl.pallas_call(
        _inproj_kernel,
        grid=(m // tm, n // tn),
        in_specs=[pl.BlockSpec((tm, k), lambda i, j: (i, 0)),
                  pl.BlockSpec((tn, k), lambda i, j: (j, 0))],
        out_specs=pl.BlockSpec((tm, tn), lambda i, j: (i, j)),
        out_shape=jax.ShapeDtypeStruct((m, n), F32),
        scratch_shapes=[pltpu.VMEM((tm, k), BF16)],
        compiler_params=_cparams(2),
        name="inproj",
    )(x, w_packed_t)


def _rope_tables(pos):
    inv = ROPE_THETA ** (-jnp.arange(0, ROT_DIM, 2, dtype=F32) / ROT_DIM)
    ang = pos.astype(F32)[:, None] * inv[None, :]
    cos, sin = jnp.cos(ang), jnp.sin(ang)
    t = pos.shape[0]
    half = ROT_DIM // 2
    pad = HEAD_DIM - ROT_DIM
    c_head = jnp.concatenate([cos, cos, jnp.ones((t, pad), F32)], axis=1)
    sp_head = jnp.concatenate([-sin, jnp.zeros((t, half + pad), F32)], axis=1)
    sm_head = jnp.concatenate([jnp.zeros((t, half), F32), sin, jnp.zeros((t, pad), F32)], axis=1)
    rep = LANES // HEAD_DIM
    return tuple(jnp.tile(z, (1, rep)) for z in (c_head, sp_head, sm_head))


def _rope_tile(x, c, sp, sm):
    half = ROT_DIM // 2
    return x * c + pltpu.roll(x, LANES - half, 1) * sp + pltpu.roll(x, half, 1) * sm


def _attn_prompt_kernel(q_ref, k_ref, v_ref, c_ref, sp_ref, sm_ref, sink_ref,
                        o_ref, kwin_ref, vwin_ref, kp_ref, vp_ref):
    i = pl.program_id(1)
    blk = WINDOW

    @pl.when(i == 0)
    def _():
        kp_ref[...] = jnp.zeros_like(kp_ref)
        vp_ref[...] = jnp.zeros_like(vp_ref)

    c, sp, sm = c_ref[...], sp_ref[...], sm_ref[...]
    lane = lax.broadcasted_iota(jnp.int32, (2 * blk, LANES), 1)
    lo2 = lane < HEAD_DIM
    lo1 = lax.broadcasted_iota(jnp.int32, (blk, LANES), 1) < HEAD_DIM

    qi = lax.broadcasted_iota(jnp.int32, (blk, 2 * blk), 0)
    kj = lax.broadcasted_iota(jnp.int32, (blk, 2 * blk), 1)
    kmin = jnp.where(i == 0, blk, 0)
    valid = (kj > qi) & (kj <= qi + blk) & (kj >= kmin)

    n_kv_tiles = KV_WIDTH // LANES
    kk_g, va_g, vb_g = [], [], []
    for t in range(n_kv_tiles):
        sl = slice(LANES * t, LANES * (t + 1))
        kcur = _rope_tile(k_ref[:, sl], c, sp, sm)
        vcur = v_ref[:, sl]
        kwin_ref[0, :, sl] = kcur
        vwin_ref[0, :, sl] = vcur
        kall = jnp.concatenate([kp_ref[:, sl], kcur], axis=0)
        vall = jnp.concatenate([vp_ref[:, sl], vcur], axis=0)
        kp_ref[:, sl] = kcur
        vp_ref[:, sl] = vcur
        kswap = pltpu.roll(kall, HEAD_DIM, 1)
        vswap = pltpu.roll(vall, HEAD_DIM, 1)
        kk_g += [jnp.where(lo2, kall, kswap).astype(BF16), jnp.where(lo2, kswap, kall).astype(BF16)]
        va_g += [jnp.where(lo2, vall, 0.0).astype(BF16), jnp.where(lo2, vswap, 0.0).astype(BF16)]
        vb_g += [jnp.where(lo2, 0.0, vswap).astype(BF16), jnp.where(lo2, 0.0, vall).astype(BF16)]

    group = N_Q_HEADS // N_KV_HEADS
    tiles = range(ATTN_WIDTH // LANES)
    qt = [_rope_tile(q_ref[:, LANES * j:LANES * (j + 1)], c, sp, sm) * (HEAD_DIM ** -0.5) for j in tiles]
    for h0 in range(0, N_Q_HEADS, HEADS_PER_GROUP):
        heads = range(h0, h0 + HEADS_PER_GROUP)
        qm = {h: (jnp.where(lo1, qt[h // 2], 0.0) if h % 2 == 0 else jnp.where(lo1, 0.0, qt[h // 2])).astype(BF16) for h in heads}
        sinks = {h: sink_ref[h] for h in heads}
        s = {h: jnp.where(valid, _dot_t(qm[h], kk_g[h // group]), NEG_BIG) for h in heads}
        m = {h: jnp.maximum(jnp.max(s[h], axis=1, keepdims=True), sinks[h]) for h in heads}
        p = {h: jnp.exp(s[h] - m[h]) for h in heads}
        den = {h: jnp.sum(p[h], axis=1, keepdims=True) + jnp.exp(sinks[h] - m[h]) for h in heads}
        o = {h: jnp.dot(p[h].astype(BF16), (va_g if h % 2 == 0 else vb_g)[h // group], preferred_element_type=F32) / den[h]
             for h in heads}
        for j in range(h0 // 2, (h0 + HEADS_PER_GROUP) // 2):
            o_ref[:, LANES * j:LANES * (j + 1)] = (o[2 * j] + o[2 * j + 1]).astype(o_ref.dtype)


def _attn_prompt(p_all, sinks, b, t):
    blk = WINDOW
    nb = t // blk
    c, sp, sm = _rope_tables(jnp.arange(t, dtype=jnp.int32))
    tab_spec = pl.BlockSpec((blk, LANES), lambda bi, i: (i, 0))
    row = lambda bi, i: bi * nb + i
    return pl.pallas_call(
        _attn_prompt_kernel,
        grid=(b, nb),
        in_specs=[pl.BlockSpec((blk, ATTN_WIDTH), lambda bi, i: (row(bi, i), COL_Q // ATTN_WIDTH)),
                  pl.BlockSpec((blk, KV_WIDTH), lambda bi, i: (row(bi, i), COL_K // KV_WIDTH)),
                  pl.BlockSpec((blk, KV_WIDTH), lambda bi, i: (row(bi, i), COL_V // KV_WIDTH)),
                  tab_spec, tab_spec, tab_spec,
                  pl.BlockSpec(memory_space=pltpu.SMEM)],
        out_specs=[pl.BlockSpec((blk, ATTN_WIDTH), lambda bi, i: (row(bi, i), 0)),
                   pl.BlockSpec((1, blk, KV_WIDTH), lambda bi, i: (bi, 0, 0)),
                   pl.BlockSpec((1, blk, KV_WIDTH), lambda bi, i: (bi, 0, 0))],
        out_shape=[jax.ShapeDtypeStruct((b * t, ATTN_WIDTH), BF16),
                   jax.ShapeDtypeStruct((b, blk, KV_WIDTH), F32),
                   jax.ShapeDtypeStruct((b, blk, KV_WIDTH), F32)],
        scratch_shapes=[pltpu.VMEM((blk, KV_WIDTH), F32), pltpu.VMEM((blk, KV_WIDTH), F32)],
        compiler_params=_cparams(2),
        name="attn_prompt",
    )(p_all, p_all, p_all, c, sp, sm, sinks)


def _rwkv_prepare(r_in, k_in, v_in, l_in, pr, pk, pv, plr, prm, wd_ref, wi_ref, wg_ref, bd2):
    mur, muk, muv, mul, w0, a0, k_k, k_a = prm
    r = r_in + (pr - r_in) * mur
    k = k_in + (pk - k_in) * muk
    v = v_in + (pv - v_in) * muv
    ls = l_in + (plr - l_in) * mul
    l0 = ls[:, :LANES]
    lane = lax.broadcasted_iota(jnp.int32, l0.shape, 1)
    z0 = jnp.where(lane < DECAY_LORA, jnp.tanh(l0), l0).astype(BF16)
    zg = _sigmoid(ls[:, LANES:3 * LANES]).astype(BF16)
    dec_up = jnp.dot(z0, wd_ref[...], preferred_element_type=F32)
    icl_up = jnp.dot(z0, wi_ref[...], preferred_element_type=F32)
    gate = jnp.dot(zg, wg_ref[...], preferred_element_type=F32)
    w_log = -_softplus(-(w0 + dec_up)) - 0.5
    logw = -jnp.exp(w_log)
    a_sig = _sigmoid(a0 + icl_up)
    kk = k * k_k
    nrm = jnp.maximum(jnp.sqrt(_head_sum(kk * kk, bd2)), 1e-12)
    kk = kk / nrm
    k = k * (1.0 + (a_sig - 1.0) * k_a)
    return r, k, v, logw, -kk, kk * a_sig, gate


def _rwkv_finish(y, r, k, v, gate, r_k, gn_w, gn_b, bd2):
    mu = _head_sum(y, bd2) * (1.0 / HEAD_DIM)
    d = y - mu
    var = _head_sum(d * d, bd2) * (1.0 / HEAD_DIM)
    yn = d * lax.rsqrt(var + GN_EPS) * gn_w + gn_b
    bonus = _head_sum(r * k * r_k, bd2) * v
    return (yn + bonus) * gate


def _rwkv_prompt_kernel(r_ref, k_ref, v_ref, l_ref,
                        mur_ref, muk_ref, muv_ref, mul_ref, w0_ref, a0_ref, kk_ref, ka_ref,
                        rk_ref, gnw_ref, gnb_ref, wd_ref, wi_ref, wg_ref,
                        o_ref, sout_ref,
                        cr_ref, ck_ref, cv_ref, cl_ref, st_ref, y_ref):
    ti = pl.program_id(2)
    tb, lw = r_ref.shape
    n_pair = lw // LANES
    n_chunk = tb // CHUNK

    @pl.when(ti == 0)
    def _():
        for ref in (cr_ref, ck_ref, cv_ref, cl_ref, st_ref):
            ref[...] = jnp.zeros_like(ref)

    def shifted(x, carry_ref):
        rolled = pltpu.roll(x, 1, 0)
        row = lax.broadcasted_iota(jnp.int32, x.shape, 0)
        prev = jnp.where(row == 0, carry_ref[0:1, :], rolled)
        carry_ref[0:1, :] = x[tb - 1:tb, :]
        return prev

    bd2 = _head_block_ones()
    r_in, k_in, v_in, l_in = r_ref[...], k_ref[...], v_ref[...], l_ref[...]
    prm = tuple(ref[...] for ref in (mur_ref, muk_ref, muv_ref, mul_ref, w0_ref, a0_ref, kk_ref, ka_ref))
    r, k, v, logw, a_s, b_s, gate = _rwkv_prepare(
        r_in, k_in, v_in, l_in,
        shifted(r_in, cr_ref), shifted(k_in, ck_ref), shifted(v_in, cv_ref), shifted(l_in, cl_ref),
        prm, wd_ref, wi_ref, wg_ref, bd2)

    c = CHUNK
    ri = lax.broadcasted_iota(jnp.int32, (c, 3 * c), 0)
    ci = lax.broadcasted_iota(jnp.int32, (c, 3 * c), 1) % c
    tri3 = jnp.where(ri >= ci, 1.0, 0.0).astype(BF16)
    r2 = lax.broadcasted_iota(jnp.int32, (2 * c, LANES), 0)
    l2 = lax.broadcasted_iota(jnp.int32, (2 * c, LANES), 1)
    tt, ss = r2 % c, l2 % c
    causal = ss < tt + r2 // c
    lo1 = lax.broadcasted_iota(jnp.int32, (c, LANES), 1) < HEAD_DIM
    diag_blocks = (r2 // HEAD_DIM) == (l2 // HEAD_DIM)
    eye_side = jnp.where(lax.broadcasted_iota(jnp.int32, (c, LANES), 0) == lax.broadcasted_iota(jnp.int32, (c, LANES), 1) % c,
                         1.0, 0.0)

    def stack_heads(x):
        return jnp.concatenate([jnp.where(lo1, x, 0.0), jnp.where(lo1, 0.0, x)], axis=0)

    def dot_t(a, b):
        return lax.dot_general(a, b, (((1,), (1,)), ((), ())), preferred_element_type=F32)

    def dot_tt(a, b):
        return lax.dot_general(a, b, (((0,), (0,)), ((), ())), preferred_element_type=F32)

    def dot(a, b):
        return jnp.dot(a.astype(BF16), b.astype(BF16), preferred_element_type=F32)

    lanes = [slice(LANES * pi, LANES * (pi + 1)) for pi in range(n_pair)]
    units = [(ch, pi) for ch in range(n_chunk) for pi in range(n_pair)]
    un = range(len(units))
    el = []
    for ch in range(n_chunk):
        rows = slice(c * ch, c * (ch + 1))
        lw_c = logw[rows]
        hi = lw_c.astype(BF16)
        rem = lw_c - hi.astype(F32)
        mid = rem.astype(BF16)
        low = (rem - mid.astype(F32)).astype(BF16)
        lcum = jnp.dot(tri3, jnp.concatenate([hi, mid, low], axis=0), preferred_element_type=F32)
        ltot = lcum[c - 1:c, :]
        p_inv = jnp.exp(-lcum)
        p_tail = jnp.exp(ltot - lcum)
        el.append(dict(aq=a_s[rows] * jnp.exp(lcum - lw_c), rq=r[rows] * jnp.exp(lcum), bk=b_s[rows] * p_inv,
                       kq=k[rows] * p_inv, bt=b_s[rows] * p_tail, kt=k[rows] * p_tail, v=v[rows],
                       p_end=jnp.exp(ltot)))
    op = lambda name, u: el[units[u][0]][name][:, lanes[units[u][1]]]

    ar = [jnp.concatenate([op("aq", u), op("rq", u)], axis=0).astype(BF16) for u in un]
    gb = [jnp.where(causal, dot_t(ar[u], stack_heads(op("bk", u)).astype(BF16)), 0.0) for u in un]
    gk = [jnp.where(causal, dot_t(ar[u], stack_heads(op("kq", u)).astype(BF16)), 0.0) for u in un]
    gv = [dot(gk[u], stack_heads(op("v", u))) for u in un]
    pw = [gb[u][:c] for u in un]
    tm = [eye_side + pw[u] for u in un]
    for level in range(1, 6):
        bd = [stack_heads(pw[u]).astype(BF16) for u in un]
        if level == 1:
            pw = [jnp.dot(pw[u].astype(BF16), bd[u], preferred_element_type=F32) for u in un]
            bd = [stack_heads(pw[u]).astype(BF16) for u in un]
        if level < 5:
            both = [jnp.dot(jnp.concatenate([tm[u], pw[u]], axis=0).astype(BF16), bd[u], preferred_element_type=F32)
                    for u in un]
            tm = [tm[u] + both[u][:c] for u in un]
            pw = [both[u][c:] for u in un]
        else:
            tm = [tm[u] + jnp.dot(tm[u].astype(BF16), bd[u], preferred_element_type=F32) for u in un]
    tax = [dot(tm[u], jnp.concatenate([stack_heads(op("aq", u)), stack_heads(gv[u][:c])], axis=1)) for u in un]
    taq = [tax[u][:, :LANES] for u in un]
    txv = [tax[u][:, LANES:] for u in un]
    arx = [dot(gb[u][c:], jnp.concatenate([stack_heads(taq[u]), stack_heads(txv[u])], axis=1)) for u in un]
    mb = [jnp.where(diag_blocks, dot_tt(op("bt", u).astype(BF16), taq[u].astype(BF16)), 0.0).astype(BF16) for u in un]
    cct = [jnp.where(diag_blocks,
                     dot_tt(jnp.concatenate([txv[u], op("v", u)], axis=0).astype(BF16),
                            jnp.concatenate([op("bt", u), op("kt", u)], axis=0).astype(BF16)), 0.0) for u in un]
    rqp = [(op("rq", u) + arx[u][:, :LANES]).astype(BF16) for u in un]
    yc = [gv[u][c:] + arx[u][:, LANES:] for u in un]
    for u, (ch, pi) in enumerate(units):
        s_old = st_ref[pi]
        sb = s_old.astype(BF16)
        y_ref[c * ch:c * (ch + 1), lanes[pi]] = dot_t(rqp[u], sb) + yc[u]
        st_ref[pi] = s_old * op("p_end", u) + dot_t(sb, mb[u]) + cct[u]

    out = _rwkv_finish(y_ref[...], r, k, v, gate, rk_ref[...], gnw_ref[...], gnb_ref[...], bd2)
    o_ref[...] = out.astype(o_ref.dtype)

    @pl.when(ti == pl.num_programs(2) - 1)
    def _():
        for pi in range(n_pair):
            s = st_ref[pi]
            sout_ref[0, 2 * pi] = s[:HEAD_DIM, :HEAD_DIM]
            sout_ref[0, 2 * pi + 1] = s[HEAD_DIM:, HEAD_DIM:]


def _rwkv_lora_weights(w_decay_up, w_iclr_up, w_gate_up):
    z64 = jnp.zeros((DECAY_LORA, RWKV_WIDTH), F32)
    wd = jnp.concatenate([w_decay_up, z64], axis=0).astype(BF16)
    wi = jnp.concatenate([z64, w_iclr_up], axis=0).astype(BF16)
    wg = jnp.concatenate([w_gate_up, jnp.zeros((2 * LANES - GATE_LORA, RWKV_WIDTH), F32)], axis=0).astype(BF16)
    return wd, wi, wg


def _rwkv_prompt(p_all, prm, b, t, tb, lw):
    nt = t // tb
    ns = RWKV_WIDTH // lw
    row = lambda bi, si, ti: bi * nt + ti
    col_spec = lambda col0: pl.BlockSpec((tb, lw), lambda bi, si, ti: (row(bi, si, ti), col0 // lw + si))
    vec = pl.BlockSpec((1, lw), lambda bi, si, ti: (0, si))
    vec_l = pl.BlockSpec((1, LORA_PAD), lambda bi, si, ti: (0, 0))
    return pl.pallas_call(
        _rwkv_prompt_kernel,
        grid=(b, ns, nt),
        in_specs=[col_spec(COL_R), col_spec(COL_KR), col_spec(COL_VR),
                  pl.BlockSpec((tb, LORA_PAD), lambda bi, si, ti: (row(bi, si, ti), COL_LORA // LORA_PAD)),
                  vec, vec, vec, vec_l, vec, vec, vec, vec, vec, vec, vec,
                  pl.BlockSpec((LANES, lw), lambda bi, si, ti: (0, si)),
                  pl.BlockSpec((LANES, lw), lambda bi, si, ti: (0, si)),
                  pl.BlockSpec((2 * LANES, lw), lambda bi, si, ti: (0, si))],
        out_specs=[pl.BlockSpec((tb, lw), lambda bi, si, ti: (row(bi, si, ti), si)),
                   pl.BlockSpec((1, 2 * (lw // LANES), HEAD_DIM, HEAD_DIM), lambda bi, si, ti: (bi, si, 0, 0))],
        out_shape=[jax.ShapeDtypeStruct((b * t, RWKV_WIDTH), BF16),
                   jax.ShapeDtypeStruct((b, N_RWKV_HEADS, HEAD_DIM, HEAD_DIM), F32)],
        scratch_shapes=[pltpu.VMEM((8, lw), F32), pltpu.VMEM((8, lw), F32), pltpu.VMEM((8, lw), F32),
                        pltpu.VMEM((8, LORA_PAD), F32),
                        pltpu.VMEM((lw // LANES, LANES, LANES), F32),
                        pltpu.VMEM((tb, lw), F32)],
        compiler_params=_cparams(3),
        name="rwkv_prompt",
    )(p_all, p_all, p_all, p_all,
      prm["mu_r"], prm["mu_k"], prm["mu_v"], prm["mu_l"], prm["w0"], prm["a0"], prm["k_k"], prm["k_a"],
      prm["r_k"], prm["gn_w"], prm["gn_b"], prm["wd"], prm["wi"], prm["wg"])


def _merge_kernel(x_ref, ao_ref, ro_ref, wga_ref, wgb_ref, wpa_ref, wpr_ref, wo_ref, g_ref, b_ref, h_ref,
                  xb_ref, acc_ref):
    j = pl.program_id(1)

    @pl.when(j == 0)
    def _():
        xb_ref[...] = x_ref[...].astype(BF16)
        acc_ref[...] = jnp.zeros_like(acc_ref)

    xb = xb_ref[...]
    ga = _dot_t(xb, wga_ref[...])
    gb = _dot_t(xb, wgb_ref[...])
    a = jnp.dot(ao_ref[...], wpa_ref[...], preferred_element_type=F32)
    r = jnp.dot(ro_ref[...], wpr_ref[...], preferred_element_type=F32)
    m = _sigmoid(ga) * a + _sigmoid(gb) * r
    acc_ref[...] += jnp.dot(m.astype(BF16), wo_ref[...], preferred_element_type=F32)

    @pl.when(j == pl.num_programs(1) - 1)
    def _():
        z = DEEPNORM_ALPHA * x_ref[...] + acc_ref[...]
        h_ref[...] = _layer_norm_rows(z, g_ref[...], b_ref[...])


def _merge(x, attn_o, rwkv_o, w_gates, wpa, wpr, wo, ln_g, ln_b, tm, tj):
    m = x.shape[0]
    nj = D_MODEL // tj
    return pl.pallas_call(
        _merge_kernel,
        grid=(m // tm, nj),
        in_specs=[pl.BlockSpec((tm, D_MODEL), lambda i, j: (i, 0)),
                  pl.BlockSpec((tm, ATTN_WIDTH), lambda i, j: (i, 0)),
                  pl.BlockSpec((tm, RWKV_WIDTH), lambda i, j: (i, 0)),
                  pl.BlockSpec((tj, D_MODEL), lambda i, j: (j, 0)),
                  pl.BlockSpec((tj, D_MODEL), lambda i, j: (nj + j, 0)),
                  pl.BlockSpec((ATTN_WIDTH, tj), lambda i, j: (0, j)),
                  pl.BlockSpec((RWKV_WIDTH, tj), lambda i, j: (0, j)),
                  pl.BlockSpec((tj, D_MODEL), lambda i, j: (j, 0)),
                  pl.BlockSpec((1, D_MODEL), lambda i, j: (0, 0)),
                  pl.BlockSpec((1, D_MODEL), lambda i, j: (0, 0))],
        out_specs=pl.BlockSpec((tm, D_MODEL), lambda i, j: (i, 0)),
        out_shape=jax.ShapeDtypeStruct((m, D_MODEL), F32),
        scratch_shapes=[pltpu.VMEM((tm, D_MODEL), BF16), pltpu.VMEM((tm, D_MODEL), F32)],
        compiler_params=_cparams(2),
        name="merge_ln1",
    )(x, attn_o, rwkv_o, w_gates, w_gates, wpa, wpr, wo, ln_g, ln_b)


def _ffn_kernel(h_ref, wu_ref, wd_ref, g_ref, b_ref, y_ref, hb_ref, acc_ref):
    f = pl.program_id(1)

    @pl.when(f == 0)
    def _():
        hb_ref[...] = h_ref[...].astype(BF16)
        acc_ref[...] = jnp.zeros_like(acc_ref)

    u = jnp.dot(hb_ref[...], wu_ref[...], preferred_element_type=F32)
    u = jnp.square(jnp.maximum(u, 0.0))
    acc_ref[...] += jnp.dot(u.astype(BF16), wd_ref[...], preferred_element_type=F32)

    @pl.when(f == pl.num_programs(1) - 1)
    def _():
        z = DEEPNORM_ALPHA * h_ref[...] + acc_ref[...]
        y_ref[...] = _layer_norm_rows(z, g_ref[...], b_ref[...])


def _ffn(h, wu, wd, ln_g, ln_b, tm, tf):
    m = h.shape[0]
    return pl.pallas_call(
        _ffn_kernel,
        grid=(m // tm, D_FF // tf),
        in_specs=[pl.BlockSpec((tm, D_MODEL), lambda i, f: (i, 0)),
                  pl.BlockSpec((D_MODEL, tf), lambda i, f: (0, f)),
                  pl.BlockSpec((tf, D_MODEL), lambda i, f: (f, 0)),
                  pl.BlockSpec((1, D_MODEL), lambda i, f: (0, 0)),
                  pl.BlockSpec((1, D_MODEL), lambda i, f: (0, 0))],
        out_specs=pl.BlockSpec((tm, D_MODEL), lambda i, f: (i, 0)),
        out_shape=jax.ShapeDtypeStruct((m, D_MODEL), F32),
        scratch_shapes=[pltpu.VMEM((tm, D_MODEL), BF16), pltpu.VMEM((tm, D_MODEL), F32)],
        compiler_params=_cparams(2),
        name="ffn_ln2",
    )(h, wu, wd, ln_g, ln_b)


def _attn_sample_kernel(q_ref, kvt_ref, ck_ref, cv_ref, c_ref, sp_ref, sm_ref, cc_ref, spc_ref, smc_ref, sink_ref,
                        o_ref, nk_ref, nv_ref):
    bt = q_ref.shape[0]
    win = ck_ref.shape[2]
    n = kvt_ref.shape[1]
    half = ROT_DIM // 2
    group = N_Q_HEADS // N_KV_HEADS
    c, sp, sm = c_ref[0:1, :], sp_ref[0:1, :], sm_ref[0:1, :]
    sink = sink_ref[:, 0:1]
    row16 = lax.broadcasted_iota(jnp.int32, (N_Q_HEADS, KV_WIDTH), 0)
    lane16 = lax.broadcasted_iota(jnp.int32, (N_Q_HEADS, KV_WIDTH), 1)
    own_kv = (lane16 // HEAD_DIM) == (row16 % N_KV_HEADS)
    urow = row16 // N_KV_HEADS
    pos = lax.broadcasted_iota(jnp.int32, (KV_WIDTH, win), 1)
    seq = lax.broadcasted_iota(jnp.int32, (KV_WIDTH, n), 1)

    kt = kvt_ref[0:KV_WIDTH, :]
    kt = kt * cc_ref[...] + pltpu.roll(kt, KV_WIDTH - half, 0) * spc_ref[...] + pltpu.roll(kt, half, 0) * smc_ref[...]
    vt = kvt_ref[KV_WIDTH:2 * KV_WIDTH, :]

    def rope_row(x):
        return jnp.concatenate([_rope_tile(x[:, LANES * t:LANES * (t + 1)], c, sp, sm)
                                for t in range(x.shape[1] // LANES)], axis=1)

    rng = range(bt)
    me = [seq == pl.program_id(0) * bt + b for b in rng]
    k_col = [jnp.sum(jnp.where(me[b], kt, 0.0), axis=1, keepdims=True) for b in rng]
    v_col = [jnp.sum(jnp.where(me[b], vt, 0.0), axis=1, keepdims=True) for b in rng]
    nk = [jnp.where(pos == win - 1, k_col[b], pltpu.roll(ck_ref[b], win - 1, 1)) for b in rng]
    nv = [jnp.where(pos == win - 1, v_col[b], pltpu.roll(cv_ref[b], win - 1, 1)) for b in rng]
    for b in rng:
        nk_ref[b] = nk[b]
        nv_ref[b] = nv[b]
    qmat = []
    for b in rng:
        q = rope_row(q_ref[b:b + 1, :]) * (HEAD_DIM ** -0.5)
        qb = [jnp.broadcast_to(q[:, KV_WIDTH * u:KV_WIDTH * (u + 1)], (N_Q_HEADS, KV_WIDTH)) for u in range(group)]
        qsel = jnp.where(urow == 0, qb[0], jnp.where(urow == 1, qb[1], jnp.where(urow == 2, qb[2], qb[3])))
        qmat.append(jnp.where(own_kv, qsel, 0.0).astype(BF16))
    s = [jnp.dot(qmat[b], nk[b].astype(BF16), preferred_element_type=F32) for b in rng]
    m = [jnp.maximum(jnp.max(s[b], axis=1, keepdims=True), sink) for b in rng]
    p = [jnp.exp(s[b] - m[b]) for b in rng]
    den = [jnp.sum(p[b], axis=1, keepdims=True) + jnp.exp(sink - m[b]) for b in rng]
    o = [_dot_t(p[b].astype(BF16), nv[b].astype(BF16)) / den[b] for b in rng]
    out_rows = []
    for b in rng:
        ob = jnp.where(own_kv, o[b], 0.0)
        chunks = [jnp.sum(jnp.where(urow == u, ob, 0.0), axis=0, keepdims=True) for u in range(group)]
        out_rows.append(jnp.concatenate(chunks, axis=1))
    o_ref[...] = jnp.concatenate(out_rows, axis=0).astype(o_ref.dtype)


def _attn_sample(q_perm, kv_new_t, cache_kt, cache_vt, sink_mat, bt):
    n, win = cache_kt.shape[0], cache_kt.shape[2]
    tabs = _rope_tables(jnp.full((1,), PAST_LEN, jnp.int32))
    c, sp, sm = (jnp.broadcast_to(z, (8, LANES)) for z in tabs)
    cc, spc, smc = (jnp.broadcast_to(jnp.tile(z, (1, KV_WIDTH // LANES)).T, (KV_WIDTH, n)) for z in tabs)
    small = lambda shape: pl.BlockSpec(shape, lambda i: (0, 0))
    cache_spec = pl.BlockSpec((bt, KV_WIDTH, win), lambda i: (i, 0, 0))
    return pl.pallas_call(
        _attn_sample_kernel,
        grid=(n // bt,),
        in_specs=[pl.BlockSpec((bt, ATTN_WIDTH), lambda i: (i, 0)),
                  small((2 * KV_WIDTH, n)),
                  cache_spec, cache_spec,
                  small((8, LANES)), small((8, LANES)), small((8, LANES)),
                  small((KV_WIDTH, n)), small((KV_WIDTH, n)), small((KV_WIDTH, n)),
                  small((N_Q_HEADS, LANES))],
        out_specs=[pl.BlockSpec((bt, ATTN_WIDTH), lambda i: (i, 0)), cache_spec, cache_spec],
        out_shape=[jax.ShapeDtypeStruct((n, ATTN_WIDTH), BF16),
                   jax.ShapeDtypeStruct(cache_kt.shape, F32),
                   jax.ShapeDtypeStruct(cache_vt.shape, F32)],
        compiler_params=_cparams(1),
        name="attn_sample",
    )(q_perm, kv_new_t, cache_kt, cache_vt, c, sp, sm, cc, spc, smc, sink_mat)


def _rwkv_sample_kernel(r_ref, k_ref, v_ref, l_ref, pr_ref, pk_ref, pv_ref, pl_ref, st_ref,
                        mur_ref, muk_ref, muv_ref, mul_ref, w0_ref, a0_ref, kk_ref, ka_ref,
                        rk_ref, gnw_ref, gnb_ref, wd_ref, wi_ref, wg_ref,
                        o_ref, ns_ref, vec_s, keep_s, y_s):
    h = pl.program_id(0)
    hd = HEAD_DIM
    q_a, q_w, q_b, q_k, q_r, q_v = range(6)

    @pl.when(h == 0)
    def _():
        bd2 = _head_block_ones()
        prm = tuple(ref[...] for ref in (mur_ref, muk_ref, muv_ref, mul_ref, w0_ref, a0_ref, kk_ref, ka_ref))
        r, k, v, logw, a_s, b_s, gate = _rwkv_prepare(
            r_ref[...], k_ref[...], v_ref[...], l_ref[...], pr_ref[...], pk_ref[...], pv_ref[...], pl_ref[...],
            prm, wd_ref, wi_ref, wg_ref, bd2)
        for qi, x in enumerate((a_s, jnp.exp(logw), b_s, k, r, v)):
            xt = x.T
            for hh in range(N_RWKV_HEADS):
                vec_s[qi, hh] = xt[hd * hh:hd * (hh + 1), :]
        for qi, x in enumerate((r, k, v, gate)):
            keep_s[qi] = x

    a_h, w_h, b_h, k_h, r_h = (vec_s[qi, h] for qi in (q_a, q_w, q_b, q_k, q_r))
    for i in range(hd):
        s = st_ref[0, i]
        sa = jnp.sum(s * a_h, axis=0, keepdims=True)
        s_new = s * w_h + sa * b_h + vec_s[q_v, h, i:i + 1, :] * k_h
        ns_ref[0, i] = s_new
        y_s[h, i:i + 1, :] = jnp.sum(s_new * r_h, axis=0, keepdims=True)

    @pl.when(h == pl.num_programs(0) - 1)
    def _():
        y = jnp.concatenate([y_s[hh] for hh in range(N_RWKV_HEADS)], axis=0).T
        out = _rwkv_finish(y, keep_s[0], keep_s[1], keep_s[2], keep_s[3], rk_ref[...], gnw_ref[...], gnb_ref[...],
                           _head_block_ones())
        o_ref[...] = out.astype(o_ref.dtype)


def _rwkv_sample(p_all, shift, shift_l, state_t, prm):
    n = state_t.shape[-1]
    wide = lambda col0: pl.BlockSpec((n, RWKV_WIDTH), lambda h: (0, col0 // RWKV_WIDTH))
    vec = pl.BlockSpec((1, RWKV_WIDTH), lambda h: (0, 0))
    vec_l = pl.BlockSpec((1, LORA_PAD), lambda h: (0, 0))
    st_spec = pl.BlockSpec((1, HEAD_DIM, HEAD_DIM, n), lambda h: (h, 0, 0, 0))
    return pl.pallas_call(
        _rwkv_sample_kernel,
        grid=(N_RWKV_HEADS,),
        in_specs=[wide(COL_R), wide(COL_KR), wide(COL_VR),
                  pl.BlockSpec((n, LORA_PAD), lambda h: (0, COL_LORA // LORA_PAD)),
                  wide(0), wide(RWKV_WIDTH), wide(2 * RWKV_WIDTH),
                  pl.BlockSpec((n, LORA_PAD), lambda h: (0, 0)),
                  st_spec,
                  vec, vec, vec, vec_l, vec, vec, vec, vec, vec, vec, vec,
                  pl.BlockSpec((LANES, RWKV_WIDTH), lambda h: (0, 0)),
                  pl.BlockSpec((LANES, RWKV_WIDTH), lambda h: (0, 0)),
                  pl.BlockSpec((2 * LANES, RWKV_WIDTH), lambda h: (0, 0))],
        out_specs=[pl.BlockSpec((n, RWKV_WIDTH), lambda h: (0, 0)), st_spec],
        out_shape=[jax.ShapeDtypeStruct((n, RWKV_WIDTH), BF16), jax.ShapeDtypeStruct(state_t.shape, F32)],
        scratch_shapes=[pltpu.VMEM((6, N_RWKV_HEADS, HEAD_DIM, n), F32),
                        pltpu.VMEM((4, n, RWKV_WIDTH), F32),
                        pltpu.VMEM((N_RWKV_HEADS, HEAD_DIM, n), F32)],
        compiler_params=_cparams(1),
        name="rwkv_sample",
    )(p_all, p_all, p_all, p_all, shift, shift, shift, shift_l, state_t,
      prm["mu_r"], prm["mu_k"], prm["mu_v"], prm["mu_l"], prm["w0"], prm["a0"], prm["k_k"], prm["k_a"],
      prm["r_k"], prm["gn_w"], prm["gn_b"], prm["wd"], prm["wi"], prm["wg"])


def _pack_w_in(w_in):
    wt = w_in.T
    c0 = ATTN_WIDTH
    q = wt[:c0]
    k = wt[c0:c0 + KV_WIDTH]
    v = wt[c0 + KV_WIDTH:c0 + 2 * KV_WIDTH]
    c1 = c0 + 2 * KV_WIDTH
    rkv = wt[c1:c1 + 3 * RWKV_WIDTH]
    lora = wt[c1 + 3 * RWKV_WIDTH:c1 + RWKV_PROJ_WIDTH]
    c2 = c1 + RWKV_PROJ_WIDTH
    gates = wt[c2:c2 + 2 * D_MODEL]
    pad = jnp.zeros((LORA_PAD - LORA_WIDTH, D_MODEL), w_in.dtype)
    return jnp.concatenate([q, rkv, k, v, lora, pad], axis=0).astype(BF16), gates.astype(BF16)


def _swap_head_order(z, outer, inner):
    n = z.shape[0]
    return z.reshape(n, outer, inner, HEAD_DIM).transpose(0, 2, 1, 3).reshape(n, outer * inner * HEAD_DIM)


def _shift_columns(p_rows):
    return jnp.concatenate([p_rows[:, COL_R:COL_R + 3 * RWKV_WIDTH], p_rows[:, COL_LORA:COL_LORA + LORA_WIDTH]], axis=1)


def _forward(x_prompt, x_sample, cache_k_win, cache_v_win, state_shift, state_wkv, w, cfg):
    b, t, _ = x_prompt.shape
    n_s = x_sample.shape[0]
    row = lambda z: z.reshape(1, -1).astype(F32)
    mu = w["mu_shift"]
    wd, wi, wg = _rwkv_lora_weights(w["w_decay_up"], w["w_iclr_up"], w["w_gate_up"])
    prm = dict(
        mu_r=row(mu[:RWKV_WIDTH]), mu_k=row(mu[RWKV_WIDTH:2 * RWKV_WIDTH]), mu_v=row(mu[2 * RWKV_WIDTH:3 * RWKV_WIDTH]),
        mu_l=row(jnp.pad(mu[3 * RWKV_WIDTH:], (0, LORA_PAD - LORA_WIDTH))),
        w0=row(w["w0"]), a0=row(w["a0"]), k_k=row(w["k_k"]), k_a=row(w["k_a"]), r_k=row(w["r_k"]),
        gn_w=row(w["gn_w"]), gn_b=row(w["gn_b"]), wd=wd, wi=wi, wg=wg)
    wpa = w["w_proj_attn"].astype(BF16)
    wpr = w["w_proj_rwkv"].astype(BF16)
    wo = w["w_out"].astype(BF16)
    wu = w["w_up"].astype(BF16)
    wdn = w["w_down"].astype(BF16)
    ln1g, ln1b, ln2g, ln2b = row(w["ln1_g"]), row(w["ln1_b"]), row(w["ln2_g"]), row(w["ln2_b"])
    sinks = w["attn_sinks"].astype(F32)

    xp = x_prompt.reshape(b * t, D_MODEL)
    w_packed, w_gates = _pack_w_in(w["w_in"])
    pp = _inproj(xp, w_packed, cfg["tm_in"], cfg["tn_in"])
    attn_p, kwin_p, vwin_p = _attn_prompt(pp, sinks, b, t)
    rwkv_p, wkv_p = _rwkv_prompt(pp, prm, b, t, cfg["tb_rwkv"], cfg["lw_rwkv"])
    hp = _merge(xp, attn_p, rwkv_p, w_gates, wpa, wpr, wo, ln1g, ln1b, cfg["tm_merge"], cfg["tj_merge"])
    yp = _ffn(hp, wu, wdn, ln2g, ln2b, cfg["tm_ffn"], cfg["tf_ffn"])
    shift_p = _shift_columns(pp.reshape(b, t, PACK_WIDTH)[:, t - 1])

    group = N_Q_HEADS // N_KV_HEADS
    xs = x_sample.reshape(n_s, D_MODEL)
    ps = _inproj(xs, w_packed, n_s, cfg["tn_in"])
    q_perm = _swap_head_order(ps[:, COL_Q:COL_Q + ATTN_WIDTH], N_KV_HEADS, group)
    sink_mat = jnp.broadcast_to(sinks.reshape(N_KV_HEADS, group).T.reshape(N_Q_HEADS, 1), (N_Q_HEADS, LANES))
    win = cache_k_win.shape[1]
    to_t = lambda z: jnp.transpose(z, (0, 2, 3, 1)).reshape(n_s, KV_WIDTH, win)
    from_t = lambda z: jnp.transpose(z.reshape(n_s, N_KV_HEADS, HEAD_DIM, win), (0, 3, 1, 2))[None]
    attn_s, nk_t, nv_t = _attn_sample(q_perm, ps[:, COL_K:COL_K + 2 * KV_WIDTH].T, to_t(cache_k_win), to_t(cache_v_win),
                                      sink_mat, cfg["bt_sample"])
    attn_s = _swap_head_order(attn_s, group, N_KV_HEADS)
    shift_l = jnp.pad(state_shift[:, 3 * RWKV_WIDTH:], ((0, 0), (0, LORA_PAD - LORA_WIDTH)))
    rwkv_s, wkv_t = _rwkv_sample(ps, state_shift, shift_l, jnp.transpose(state_wkv, (1, 2, 3, 0)), prm)
    wkv_s = jnp.transpose(wkv_t, (3, 0, 1, 2))
    hs = _merge(xs, attn_s, rwkv_s, w_gates, wpa, wpr, wo, ln1g, ln1b, n_s, cfg["tj_merge"])
    ys = _ffn(hs, wu, wdn, ln2g, ln2b, n_s, cfg["tf_ffn"])
    shift_s = _shift_columns(ps)

    kv5 = lambda z: z.reshape(1, z.shape[0], z.shape[1], N_KV_HEADS, HEAD_DIM)
    return (yp.reshape(b, t, D_MODEL), ys.reshape(n_s, 1, D_MODEL),
            kv5(kwin_p), kv5(vwin_p), shift_p[None], wkv_p[None],
            from_t(nk_t), from_t(nv_t), shift_s[None], wkv_s[None])


_CFG = dict(tm_in=1024, tn_in=1024, tb_rwkv=256, lw_rwkv=1024, tm_merge=512, tj_merge=512, tm_ffn=512, tf_ffn=1024, bt_sample=16)


def kernel(x_prompt, x_sample, cache_k_win, cache_v_win, state_shift, state_wkv, w_in, attn_sinks, mu_shift, w0,
           w_decay_up, a0, w_iclr_up, w_gate_up, k_k, k_a, r_k, gn_w, gn_b, w_proj_attn, w_proj_rwkv, w_out,
           ln1_g, ln1_b, w_up, w_down, ln2_g, ln2_b):
    w = dict(w_in=w_in[0], attn_sinks=attn_sinks[0], mu_shift=mu_shift[0], w0=w0[0], w_decay_up=w_decay_up[0],
             a0=a0[0], w_iclr_up=w_iclr_up[0], w_gate_up=w_gate_up[0], k_k=k_k[0], k_a=k_a[0], r_k=r_k[0],
             gn_w=gn_w[0], gn_b=gn_b[0], w_proj_attn=w_proj_attn[0], w_proj_rwkv=w_proj_rwkv[0], w_out=w_out[0],
             ln1_g=ln1_g[0], ln1_b=ln1_b[0], w_up=w_up[0], w_down=w_down[0], ln2_g=ln2_g[0], ln2_b=ln2_b[0])
    return _forward(x_prompt, x_sample, cache_k_win[0], cache_v_win[0], state_shift[0], state_wkv[0], w, _CFG)
```

```python
import jax
import jax.numpy as jnp
from jax import lax
from jax.experimental import pallas as pl
from jax.experimental.pallas import tpu as pltpu

F32 = jnp.float32
BF16 = jnp.bfloat16

D_MODEL = 2048
HEAD_DIM = 64
N_Q_HEADS = 16
N_KV_HEADS = 4
ATTN_WIDTH = N_Q_HEADS * HEAD_DIM
KV_WIDTH = N_KV_HEADS * HEAD_DIM
WINDOW = 128
ROPE_THETA = 500000.0
ROT_DIM = HEAD_DIM // 4
N_RWKV_HEADS = 16
RWKV_WIDTH = N_RWKV_HEADS * HEAD_DIM
DECAY_LORA = 64
ICLR_LORA = 64
GATE_LORA = 160
LORA_WIDTH = DECAY_LORA + ICLR_LORA + GATE_LORA
RWKV_PROJ_WIDTH = 3 * RWKV_WIDTH + LORA_WIDTH
D_FF = 4 * D_MODEL
PAST_LEN = 16384
DEEPNORM_ALPHA = 2.0 ** 0.25
LN_EPS = 1e-5
GN_EPS = HEAD_DIM * 1e-5
NEG_BIG = -1e30

LANES = 128
BF16_ROWS_PER_TILE = 16
HEADS_PER_GROUP = 8
CHUNK = 64
VMEM_LIMIT = 56 * 1024 * 1024

COL_Q = 0
COL_K = 1024
COL_V = 1280
COL_R = 1536
COL_KR = 2560
COL_VR = 3584
COL_LORA = 4608
LORA_PAD = 512
PACK_WIDTH = COL_LORA + LORA_PAD
GATE_ROW0 = COL_LORA + LORA_WIDTH


def _cparams(n_axes):
    return pltpu.CompilerParams(dimension_semantics=("arbitrary",) * n_axes, vmem_limit_bytes=VMEM_LIMIT)


def _sigmoid(x):
    return 1.0 / (1.0 + jnp.exp(-x))


def _softplus(x):
    return jnp.maximum(x, 0.0) + jnp.log(1.0 + jnp.exp(-jnp.abs(x)))


def _layer_norm_rows(z, g, b):
    mu = jnp.mean(z, axis=-1, keepdims=True)
    d = z - mu
    var = jnp.mean(d * d, axis=-1, keepdims=True)
    return d * lax.rsqrt(var + LN_EPS) * g + b


def _dot_t(a, b):
    return lax.dot_general(a, b, (((1,), (1,)), ((), ())), preferred_element_type=F32)


def _split_bf16(x):
    hi = x.astype(BF16)
    lo = (x - hi.astype(F32)).astype(BF16)
    return hi, lo


def _head_block_ones():
    r = lax.broadcasted_iota(jnp.int32, (2 * LANES, LANES), 0)
    c = lax.broadcasted_iota(jnp.int32, (2 * LANES, LANES), 1)
    return jnp.where((r % LANES) // HEAD_DIM == c // HEAD_DIM, 1.0, 0.0).astype(BF16)


def _head_sum(x, bd2):
    outs = []
    for t in range(x.shape[1] // LANES):
        hi, lo = _split_bf16(x[:, LANES * t:LANES * (t + 1)])
        outs.append(jnp.dot(jnp.concatenate([hi, lo], axis=1), bd2, preferred_element_type=F32))
    return outs[0] if len(outs) == 1 else jnp.concatenate(outs, axis=1)


def _inproj_kernel(x_ref, w_ref, o_ref, xb_ref):
    @pl.when(pl.program_id(1) == 0)
    def _():
        xb_ref[...] = x_ref[...].astype(BF16)

    o_ref[...] = _dot_t(xb_ref[...], w_ref[...])


def _inproj(x, w_t, tm, tn):
    m, k = x.shape
    n = PACK_WIDTH
    return pl.pallas_call(
        _inproj_kernel,
        grid=(m // tm, n // tn),
        in_specs=[pl.BlockSpec((tm, k), lambda i, j: (i, 0)),
                  pl.BlockSpec((tn, k), lambda i, j: (j, 0))],
        out_specs=pl.BlockSpec((tm, tn), lambda i, j: (i, j)),
        out_shape=jax.ShapeDtypeStruct((m, n), F32),
        scratch_shapes=[pltpu.VMEM((tm, k), BF16)],
        compiler_params=_cparams(2),
        name="inproj",
    )(x, w_t)


def _rope_tables(pos):
    inv = ROPE_THETA ** (-jnp.arange(0, ROT_DIM, 2, dtype=F32) / ROT_DIM)
    ang = pos.astype(F32)[:, None] * inv[None, :]
    cos, sin = jnp.cos(ang), jnp.sin(ang)
    t = pos.shape[0]
    half = ROT_DIM // 2
    pad = HEAD_DIM - ROT_DIM
    c_head = jnp.concatenate([cos, cos, jnp.ones((t, pad), F32)], axis=1)
    sp_head = jnp.concatenate([-sin, jnp.zeros((t, half + pad), F32)], axis=1)
    sm_head = jnp.concatenate([jnp.zeros((t, half), F32), sin, jnp.zeros((t, pad), F32)], axis=1)
    rep = LANES // HEAD_DIM
    return tuple(jnp.tile(z, (1, rep)) for z in (c_head, sp_head, sm_head))


def _rope_tile(x, c, sp, sm):
    half = ROT_DIM // 2
    return x * c + pltpu.roll(x, LANES - half, 1) * sp + pltpu.roll(x, half, 1) * sm


def _attn_prompt_kernel(q_ref, k_ref, v_ref, c_ref, sp_ref, sm_ref, sink_ref,
                        o_ref, kwin_ref, vwin_ref, kp_ref, vp_ref):
    i = pl.program_id(1)
    blk = WINDOW

    @pl.when(i == 0)
    def _():
        kp_ref[...] = jnp.zeros_like(kp_ref)
        vp_ref[...] = jnp.zeros_like(vp_ref)

    c, sp, sm = c_ref[...], sp_ref[...], sm_ref[...]
    lane = lax.broadcasted_iota(jnp.int32, (2 * blk, LANES), 1)
    lo2 = lane < HEAD_DIM
    lo1 = lax.broadcasted_iota(jnp.int32, (blk, LANES), 1) < HEAD_DIM

    qi = lax.broadcasted_iota(jnp.int32, (blk, 2 * blk), 0)
    kj = lax.broadcasted_iota(jnp.int32, (blk, 2 * blk), 1)
    kmin = jnp.where(i == 0, blk, 0)
    valid = (kj > qi) & (kj <= qi + blk) & (kj >= kmin)

    n_kv_tiles = KV_WIDTH // LANES
    kk_g, va_g, vb_g = [], [], []
    for t in range(n_kv_tiles):
        sl = slice(LANES * t, LANES * (t + 1))
        kcur = _rope_tile(k_ref[:, sl], c, sp, sm)
        vcur = v_ref[:, sl]
        kwin_ref[0, :, sl] = kcur
        vwin_ref[0, :, sl] = vcur
        kall = jnp.concatenate([kp_ref[:, sl], kcur], axis=0)
        vall = jnp.concatenate([vp_ref[:, sl], vcur], axis=0)
        kp_ref[:, sl] = kcur
        vp_ref[:, sl] = vcur
        kswap = pltpu.roll(kall, HEAD_DIM, 1)
        vswap = pltpu.roll(vall, HEAD_DIM, 1)
        kk_g += [jnp.where(lo2, kall, kswap).astype(BF16), jnp.where(lo2, kswap, kall).astype(BF16)]
        va_g += [jnp.where(lo2, vall, 0.0).astype(BF16), jnp.where(lo2, vswap, 0.0).astype(BF16)]
        vb_g += [jnp.where(lo2, 0.0, vswap).astype(BF16), jnp.where(lo2, 0.0, vall).astype(BF16)]

    group = N_Q_HEADS // N_KV_HEADS
    tiles = range(ATTN_WIDTH // LANES)
    qt = [_rope_tile(q_ref[:, LANES * j:LANES * (j + 1)], c, sp, sm) * (HEAD_DIM ** -0.5) for j in tiles]
    for h0 in range(0, N_Q_HEADS, HEADS_PER_GROUP):
        heads = range(h0, h0 + HEADS_PER_GROUP)
        qm = {h: (jnp.where(lo1, qt[h // 2], 0.0) if h % 2 == 0 else jnp.where(lo1, 0.0, qt[h // 2])).astype(BF16) for h in heads}
        sinks = {h: sink_ref[h] for h in heads}
        s = {h: jnp.where(valid, _dot_t(qm[h], kk_g[h // group]), NEG_BIG) for h in heads}
        m = {h: jnp.maximum(jnp.max(s[h], axis=1, keepdims=True), sinks[h]) for h in heads}
        p = {h: jnp.exp(s[h] - m[h]) for h in heads}
        den = {h: jnp.sum(p[h], axis=1, keepdims=True) + jnp.exp(sinks[h] - m[h]) for h in heads}
        o = {h: jnp.dot(p[h].astype(BF16), (va_g if h % 2 == 0 else vb_g)[h // group], preferred_element_type=F32) / den[h]
             for h in heads}
        for j in range(h0 // 2, (h0 + HEADS_PER_GROUP) // 2):
            o_ref[:, LANES * j:LANES * (j + 1)] = (o[2 * j] + o[2 * j + 1]).astype(o_ref.dtype)


def _attn_prompt(p_all, sinks, b, t):
    blk = WINDOW
    nb = t // blk
    c, sp, sm = _rope_tables(jnp.arange(t, dtype=jnp.int32))
    tab_spec = pl.BlockSpec((blk, LANES), lambda bi, i: (i, 0))
    row = lambda bi, i: bi * nb + i
    return pl.pallas_call(
        _attn_prompt_kernel,
        grid=(b, nb),
        in_specs=[pl.BlockSpec((blk, ATTN_WIDTH), lambda bi, i: (row(bi, i), COL_Q // ATTN_WIDTH)),
                  pl.BlockSpec((blk, KV_WIDTH), lambda bi, i: (row(bi, i), COL_K // KV_WIDTH)),
                  pl.BlockSpec((blk, KV_WIDTH), lambda bi, i: (row(bi, i), COL_V // KV_WIDTH)),
                  tab_spec, tab_spec, tab_spec,
                  pl.BlockSpec(memory_space=pltpu.SMEM)],
        out_specs=[pl.BlockSpec((blk, ATTN_WIDTH), lambda bi, i: (row(bi, i), 0)),
                   pl.BlockSpec((1, blk, KV_WIDTH), lambda bi, i: (bi, 0, 0)),
                   pl.BlockSpec((1, blk, KV_WIDTH), lambda bi, i: (bi, 0, 0))],
        out_shape=[jax.ShapeDtypeStruct((b * t, ATTN_WIDTH), BF16),
                   jax.ShapeDtypeStruct((b, blk, KV_WIDTH), F32),
                   jax.ShapeDtypeStruct((b, blk, KV_WIDTH), F32)],
        scratch_shapes=[pltpu.VMEM((blk, KV_WIDTH), F32), pltpu.VMEM((blk, KV_WIDTH), F32)],
        compiler_params=_cparams(2),
        name="attn_prompt",
    )(p_all, p_all, p_all, c, sp, sm, sinks)


def _rwkv_prepare(r_in, k_in, v_in, l_in, pr, pk, pv, plr, prm, wd_ref, wi_ref, wg_ref, bd2):
    mur, muk, muv, mul, w0, a0, k_k, k_a = prm
    r = r_in + (pr - r_in) * mur
    k = k_in + (pk - k_in) * muk
    v = v_in + (pv - v_in) * muv
    ls = l_in + (plr - l_in) * mul
    l0 = ls[:, :LANES]
    lane = lax.broadcasted_iota(jnp.int32, l0.shape, 1)
    z0 = jnp.where(lane < DECAY_LORA, jnp.tanh(l0), l0).astype(BF16)
    zg = _sigmoid(ls[:, LANES:3 * LANES]).astype(BF16)
    dec_up = jnp.dot(z0, wd_ref[...], preferred_element_type=F32)
    icl_up = jnp.dot(z0, wi_ref[...], preferred_element_type=F32)
    gate = jnp.dot(zg, wg_ref[...], preferred_element_type=F32)
    w_log = -_softplus(-(w0 + dec_up)) - 0.5
    logw = -jnp.exp(w_log)
    a_sig = _sigmoid(a0 + icl_up)
    kk = k * k_k
    nrm = jnp.maximum(jnp.sqrt(_head_sum(kk * kk, bd2)), 1e-12)
    kk = kk / nrm
    k = k * (1.0 + (a_sig - 1.0) * k_a)
    return r, k, v, logw, -kk, kk * a_sig, gate


def _rwkv_finish(y, r, k, v, gate, r_k, gn_w, gn_b, bd2):
    mu = _head_sum(y, bd2) * (1.0 / HEAD_DIM)
    d = y - mu
    var = _head_sum(d * d, bd2) * (1.0 / HEAD_DIM)
    yn = d * lax.rsqrt(var + GN_EPS) * gn_w + gn_b
    bonus = _head_sum(r * k * r_k, bd2) * v
    return (yn + bonus) * gate


def _rwkv_prompt_kernel(r_ref, k_ref, v_ref, l_ref,
                        mur_ref, muk_ref, muv_ref, mul_ref, w0_ref, a0_ref, kk_ref, ka_ref,
                        rk_ref, gnw_ref, gnb_ref, wd_ref, wi_ref, wg_ref,
                        o_ref, sout_ref,
                        cr_ref, ck_ref, cv_ref, cl_ref, st_ref, y_ref):
    ti = pl.program_id(2)
    tb, lw = r_ref.shape
    n_pair = lw // LANES
    n_chunk = tb // CHUNK

    @pl.when(ti == 0)
    def _():
        for ref in (cr_ref, ck_ref, cv_ref, cl_ref, st_ref):
            ref[...] = jnp.zeros_like(ref)

    def shifted(x, carry_ref):
        rolled = pltpu.roll(x, 1, 0)
        row = lax.broadcasted_iota(jnp.int32, x.shape, 0)
        prev = jnp.where(row == 0, carry_ref[0:1, :], rolled)
        carry_ref[0:1, :] = x[tb - 1:tb, :]
        return prev

    bd2 = _head_block_ones()
    r_in, k_in, v_in, l_in = r_ref[...], k_ref[...], v_ref[...], l_ref[...]
    prm = tuple(ref[...] for ref in (mur_ref, muk_ref, muv_ref, mul_ref, w0_ref, a0_ref, kk_ref, ka_ref))
    r, k, v, logw, a_s, b_s, gate = _rwkv_prepare(
        r_in, k_in, v_in, l_in,
        shifted(r_in, cr_ref), shifted(k_in, ck_ref), shifted(v_in, cv_ref), shifted(l_in, cl_ref),
        prm, wd_ref, wi_ref, wg_ref, bd2)

    c = CHUNK
    ri = lax.broadcasted_iota(jnp.int32, (c, 3 * c), 0)
    ci = lax.broadcasted_iota(jnp.int32, (c, 3 * c), 1) % c
    tri3 = jnp.where(ri >= ci, 1.0, 0.0).astype(BF16)
    r2 = lax.broadcasted_iota(jnp.int32, (2 * c, LANES), 0)
    l2 = lax.broadcasted_iota(jnp.int32, (2 * c, LANES), 1)
    tt, ss = r2 % c, l2 % c
    causal = ss < tt + r2 // c
    lo1 = lax.broadcasted_iota(jnp.int32, (c, LANES), 1) < HEAD_DIM
    diag_blocks = (r2 // HEAD_DIM) == (l2 // HEAD_DIM)
    eye_side = jnp.where(lax.broadcasted_iota(jnp.int32, (c, LANES), 0) == lax.broadcasted_iota(jnp.int32, (c, LANES), 1) % c,
                         1.0, 0.0)

    def stack_heads(x):
        return jnp.concatenate([jnp.where(lo1, x, 0.0), jnp.where(lo1, 0.0, x)], axis=0)

    def dot_t(a, b):
        return lax.dot_general(a, b, (((1,), (1,)), ((), ())), preferred_element_type=F32)

    def dot_tt(a, b):
        return lax.dot_general(a, b, (((0,), (0,)), ((), ())), preferred_element_type=F32)

    def dot(a, b):
        return jnp.dot(a.astype(BF16), b.astype(BF16), preferred_element_type=F32)

    lanes = [slice(LANES * pi, LANES * (pi + 1)) for pi in range(n_pair)]
    units = [(ch, pi) for ch in range(n_chunk) for pi in range(n_pair)]
    un = range(len(units))
    el = []
    for ch in range(n_chunk):
        rows = slice(c * ch, c * (ch + 1))
        lw_c = logw[rows]
        hi = lw_c.astype(BF16)
        rem = lw_c - hi.astype(F32)
        mid = rem.astype(BF16)
        low = (rem - mid.astype(F32)).astype(BF16)
        lcum = jnp.dot(tri3, jnp.concatenate([hi, mid, low], axis=0), preferred_element_type=F32)
        ltot = lcum[c - 1:c, :]
        p_inv = jnp.exp(-lcum)
        p_tail = jnp.exp(ltot - lcum)
        el.append(dict(aq=a_s[rows] * jnp.exp(lcum - lw_c), rq=r[rows] * jnp.exp(lcum), bk=b_s[rows] * p_inv,
                       kq=k[rows] * p_inv, bt=b_s[rows] * p_tail, kt=k[rows] * p_tail, v=v[rows],
                       p_end=jnp.exp(ltot)))
    op = lambda name, u: el[units[u][0]][name][:, lanes[units[u][1]]]

    ar = [jnp.concatenate([op("aq", u), op("rq", u)], axis=0).astype(BF16) for u in un]
    gb = [jnp.where(causal, dot_t(ar[u], stack_heads(op("bk", u)).astype(BF16)), 0.0) for u in un]
    gk = [jnp.where(causal, dot_t(ar[u], stack_heads(op("kq", u)).astype(BF16)), 0.0) for u in un]
    gv = [dot(gk[u], stack_heads(op("v", u))) for u in un]
    pw = [gb[u][:c] for u in un]
    tm = [eye_side + pw[u] for u in un]
    for level in range(1, 6):
        bd = [stack_heads(pw[u]).astype(BF16) for u in un]
        if level == 1:
            pw = [jnp.dot(pw[u].astype(BF16), bd[u], preferred_element_type=F32) for u in un]
            bd = [stack_heads(pw[u]).astype(BF16) for u in un]
        if level < 5:
            both = [jnp.dot(jnp.concatenate([tm[u], pw[u]], axis=0).astype(BF16), bd[u], preferred_element_type=F32)
                    for u in un]
            tm = [tm[u] + both[u][:c] for u in un]
            pw = [both[u][c:] for u in un]
        else:
            tm = [tm[u] + jnp.dot(tm[u].astype(BF16), bd[u], preferred_element_type=F32) for u in un]
    tax = [dot(tm[u], jnp.concatenate([stack_heads(op("aq", u)), stack_heads(gv[u][:c])], axis=1)) for u in un]
    taq = [tax[u][:, :LANES] for u in un]
    txv = [tax[u][:, LANES:] for u in un]
    arx = [dot(gb[u][c:], jnp.concatenate([stack_heads(taq[u]), stack_heads(txv[u])], axis=1)) for u in un]
    mb = [jnp.where(diag_blocks, dot_tt(op("bt", u).astype(BF16), taq[u].astype(BF16)), 0.0).astype(BF16) for u in un]
    cct = [jnp.where(diag_blocks,
                     dot_tt(jnp.concatenate([txv[u], op("v", u)], axis=0).astype(BF16),
                            jnp.concatenate([op("bt", u), op("kt", u)], axis=0).astype(BF16)), 0.0) for u in un]
    rqp = [(op("rq", u) + arx[u][:, :LANES]).astype(BF16) for u in un]
    yc = [gv[u][c:] + arx[u][:, LANES:] for u in un]
    for u, (ch, pi) in enumerate(units):
        s_old = st_ref[pi]
        sb = s_old.astype(BF16)
        y_ref[c * ch:c * (ch + 1), lanes[pi]] = dot_t(rqp[u], sb) + yc[u]
        st_ref[pi] = s_old * op("p_end", u) + dot_t(sb, mb[u]) + cct[u]

    out = _rwkv_finish(y_ref[...], r, k, v, gate, rk_ref[...], gnw_ref[...], gnb_ref[...], bd2)
    o_ref[...] = out.astype(o_ref.dtype)

    @pl.when(ti == pl.num_programs(2) - 1)
    def _():
        for pi in range(n_pair):
            s = st_ref[pi]
            sout_ref[0, 2 * pi] = s[:HEAD_DIM, :HEAD_DIM]
            sout_ref[0, 2 * pi + 1] = s[HEAD_DIM:, HEAD_DIM:]


def _rwkv_lora_weights(w_decay_up, w_iclr_up, w_gate_up):
    z64 = jnp.zeros((DECAY_LORA, RWKV_WIDTH), F32)
    wd = jnp.concatenate([w_decay_up, z64], axis=0).astype(BF16)
    wi = jnp.concatenate([z64, w_iclr_up], axis=0).astype(BF16)
    wg = jnp.concatenate([w_gate_up, jnp.zeros((2 * LANES - GATE_LORA, RWKV_WIDTH), F32)], axis=0).astype(BF16)
    return wd, wi, wg


def _rwkv_prompt(p_all, prm, b, t, tb, lw):
    nt = t // tb
    ns = RWKV_WIDTH // lw
    row = lambda bi, si, ti: bi * nt + ti
    col_spec = lambda col0: pl.BlockSpec(
        (pl.Element(tb), pl.Element(lw)),
        lambda bi, si, ti: (pl.multiple_of(row(bi, si, ti) * tb, tb), pl.multiple_of(col0 + si * lw, LANES)))
    vec = pl.BlockSpec((1, lw), lambda bi, si, ti: (0, si))
    vec_l = pl.BlockSpec((1, LORA_PAD), lambda bi, si, ti: (0, 0))
    return pl.pallas_call(
        _rwkv_prompt_kernel,
        grid=(b, ns, nt),
        in_specs=[col_spec(COL_R), col_spec(COL_KR), col_spec(COL_VR),
                  pl.BlockSpec((tb, LORA_PAD), lambda bi, si, ti: (row(bi, si, ti), COL_LORA // LORA_PAD)),
                  vec, vec, vec, vec_l, vec, vec, vec, vec, vec, vec, vec,
                  pl.BlockSpec((LANES, lw), lambda bi, si, ti: (0, si)),
                  pl.BlockSpec((LANES, lw), lambda bi, si, ti: (0, si)),
                  pl.BlockSpec((2 * LANES, lw), lambda bi, si, ti: (0, si))],
        out_specs=[pl.BlockSpec((tb, lw), lambda bi, si, ti: (row(bi, si, ti), si)),
                   pl.BlockSpec((1, 2 * (lw // LANES), HEAD_DIM, HEAD_DIM), lambda bi, si, ti: (bi, si, 0, 0))],
        out_shape=[jax.ShapeDtypeStruct((b * t, RWKV_WIDTH), BF16),
                   jax.ShapeDtypeStruct((b, N_RWKV_HEADS, HEAD_DIM, HEAD_DIM), F32)],
        scratch_shapes=[pltpu.VMEM((8, lw), F32), pltpu.VMEM((8, lw), F32), pltpu.VMEM((8, lw), F32),
                        pltpu.VMEM((8, LORA_PAD), F32),
                        pltpu.VMEM((lw // LANES, LANES, LANES), F32),
                        pltpu.VMEM((tb, lw), F32)],
        compiler_params=_cparams(3),
        name="rwkv_prompt",
    )(p_all, p_all, p_all, p_all,
      prm["mu_r"], prm["mu_k"], prm["mu_v"], prm["mu_l"], prm["w0"], prm["a0"], prm["k_k"], prm["k_a"],
      prm["r_k"], prm["gn_w"], prm["gn_b"], prm["wd"], prm["wi"], prm["wg"])


def _merge_kernel(x_ref, ao_ref, ro_ref, wga_ref, wgb_ref, wpa_ref, wpr_ref, wo_ref, g_ref, b_ref, h_ref,
                  xb_ref, acc_ref):
    j = pl.program_id(1)

    @pl.when(j == 0)
    def _():
        xb_ref[...] = x_ref[...].astype(BF16)
        acc_ref[...] = jnp.zeros_like(acc_ref)

    xb = xb_ref[...]
    ga = _dot_t(xb, wga_ref[...])
    gb = _dot_t(xb, wgb_ref[...])
    a = jnp.dot(ao_ref[...], wpa_ref[...], preferred_element_type=F32)
    r = jnp.dot(ro_ref[...], wpr_ref[...], preferred_element_type=F32)
    m = _sigmoid(ga) * a + _sigmoid(gb) * r
    acc_ref[...] += jnp.dot(m.astype(BF16), wo_ref[...], preferred_element_type=F32)

    @pl.when(j == pl.num_programs(1) - 1)
    def _():
        z = DEEPNORM_ALPHA * x_ref[...] + acc_ref[...]
        h_ref[...] = _layer_norm_rows(z, g_ref[...], b_ref[...])


def _merge(x, attn_o, rwkv_o, w_in_t, wpa, wpr, wo, ln_g, ln_b, tm, tj):
    m = x.shape[0]
    nj = D_MODEL // tj
    gate_spec = lambda row0: pl.BlockSpec(
        (pl.Element(tj), pl.Element(D_MODEL)),
        lambda i, j: (pl.multiple_of(row0 + j * tj, BF16_ROWS_PER_TILE), 0))
    return pl.pallas_call(
        _merge_kernel,
        grid=(m // tm, nj),
        in_specs=[pl.BlockSpec((tm, D_MODEL), lambda i, j: (i, 0)),
                  pl.BlockSpec((tm, ATTN_WIDTH), lambda i, j: (i, 0)),
                  pl.BlockSpec((tm, RWKV_WIDTH), lambda i, j: (i, 0)),
                  gate_spec(GATE_ROW0), gate_spec(GATE_ROW0 + D_MODEL),
                  pl.BlockSpec((ATTN_WIDTH, tj), lambda i, j: (0, j)),
                  pl.BlockSpec((RWKV_WIDTH, tj), lambda i, j: (0, j)),
                  pl.BlockSpec((tj, D_MODEL), lambda i, j: (j, 0)),
                  pl.BlockSpec((1, D_MODEL), lambda i, j: (0, 0)),
                  pl.BlockSpec((1, D_MODEL), lambda i, j: (0, 0))],
        out_specs=pl.BlockSpec((tm, D_MODEL), lambda i, j: (i, 0)),
        out_shape=jax.ShapeDtypeStruct((m, D_MODEL), F32),
        scratch_shapes=[pltpu.VMEM((tm, D_MODEL), BF16), pltpu.VMEM((tm, D_MODEL), F32)],
        compiler_params=_cparams(2),
        name="merge_ln1",
    )(x, attn_o, rwkv_o, w_in_t, w_in_t, wpa, wpr, wo, ln_g, ln_b)


def _ffn_kernel(h_ref, wu_ref, wd_ref, g_ref, b_ref, y_ref, hb_ref, acc_ref):
    f = pl.program_id(1)

    @pl.when(f == 0)
    def _():
        hb_ref[...] = h_ref[...].astype(BF16)
        acc_ref[...] = jnp.zeros_like(acc_ref)

    u = jnp.dot(hb_ref[...], wu_ref[...], preferred_element_type=F32)
    u = jnp.square(jnp.maximum(u, 0.0))
    acc_ref[...] += jnp.dot(u.astype(BF16), wd_ref[...], preferred_element_type=F32)

    @pl.when(f == pl.num_programs(1) - 1)
    def _():
        z = DEEPNORM_ALPHA * h_ref[...] + acc_ref[...]
        y_ref[...] = _layer_norm_rows(z, g_ref[...], b_ref[...])


def _ffn(h, wu, wd, ln_g, ln_b, tm, tf):
    m = h.shape[0]
    return pl.pallas_call(
        _ffn_kernel,
        grid=(m // tm, D_FF // tf),
        in_specs=[pl.BlockSpec((tm, D_MODEL), lambda i, f: (i, 0)),
                  pl.BlockSpec((D_MODEL, tf), lambda i, f: (0, f)),
                  pl.BlockSpec((tf, D_MODEL), lambda i, f: (f, 0)),
                  pl.BlockSpec((1, D_MODEL), lambda i, f: (0, 0)),
                  pl.BlockSpec((1, D_MODEL), lambda i, f: (0, 0))],
        out_specs=pl.BlockSpec((tm, D_MODEL), lambda i, f: (i, 0)),
        out_shape=jax.ShapeDtypeStruct((m, D_MODEL), F32),
        scratch_shapes=[pltpu.VMEM((tm, D_MODEL), BF16), pltpu.VMEM((tm, D_MODEL), F32)],
        compiler_params=_cparams(2),
        name="ffn_ln2",
    )(h, wu, wd, ln_g, ln_b)


def _attn_sample_kernel(q_ref, kvt_ref, ck_ref, cv_ref, c_ref, sp_ref, sm_ref, cc_ref, spc_ref, smc_ref, sink_ref,
                        o_ref, nk_ref, nv_ref):
    bt = q_ref.shape[0]
    win = ck_ref.shape[2]
    n = kvt_ref.shape[1]
    half = ROT_DIM // 2
    group = N_Q_HEADS // N_KV_HEADS
    c, sp, sm = c_ref[0:1, :], sp_ref[0:1, :], sm_ref[0:1, :]
    sink = sink_ref[:, 0:1]
    row16 = lax.broadcasted_iota(jnp.int32, (N_Q_HEADS, KV_WIDTH), 0)
    lane16 = lax.broadcasted_iota(jnp.int32, (N_Q_HEADS, KV_WIDTH), 1)
    own_kv = (lane16 // HEAD_DIM) == (row16 % N_KV_HEADS)
    urow = row16 // N_KV_HEADS
    pos = lax.broadcasted_iota(jnp.int32, (KV_WIDTH, win), 1)
    seq = lax.broadcasted_iota(jnp.int32, (KV_WIDTH, n), 1)

    kt = kvt_ref[0:KV_WIDTH, :]
    kt = kt * cc_ref[...] + pltpu.roll(kt, KV_WIDTH - half, 0) * spc_ref[...] + pltpu.roll(kt, half, 0) * smc_ref[...]
    vt = kvt_ref[KV_WIDTH:2 * KV_WIDTH, :]

    def rope_row(x):
        return jnp.concatenate([_rope_tile(x[:, LANES * t:LANES * (t + 1)], c, sp, sm)
                                for t in range(x.shape[1] // LANES)], axis=1)

    rng = range(bt)
    me = [seq == pl.program_id(0) * bt + b for b in rng]
    k_col = [jnp.sum(jnp.where(me[b], kt, 0.0), axis=1, keepdims=True) for b in rng]
    v_col = [jnp.sum(jnp.where(me[b], vt, 0.0), axis=1, keepdims=True) for b in rng]
    nk = [jnp.where(pos == win - 1, k_col[b], pltpu.roll(ck_ref[b], win - 1, 1)) for b in rng]
    nv = [jnp.where(pos == win - 1, v_col[b], pltpu.roll(cv_ref[b], win - 1, 1)) for b in rng]
    for b in rng:
        nk_ref[b] = nk[b]
        nv_ref[b] = nv[b]
    qmat = []
    for b in rng:
        q = rope_row(q_ref[b:b + 1, :]) * (HEAD_DIM ** -0.5)
        qb = [jnp.broadcast_to(q[:, KV_WIDTH * u:KV_WIDTH * (u + 1)], (N_Q_HEADS, KV_WIDTH)) for u in range(group)]
        qsel = jnp.where(urow == 0, qb[0], jnp.where(urow == 1, qb[1], jnp.where(urow == 2, qb[2], qb[3])))
        qmat.append(jnp.where(own_kv, qsel, 0.0).astype(BF16))
    s = [jnp.dot(qmat[b], nk[b].astype(BF16), preferred_element_type=F32) for b in rng]
    m = [jnp.maximum(jnp.max(s[b], axis=1, keepdims=True), sink) for b in rng]
    p = [jnp.exp(s[b] - m[b]) for b in rng]
    den = [jnp.sum(p[b], axis=1, keepdims=True) + jnp.exp(sink - m[b]) for b in rng]
    o = [_dot_t(p[b].astype(BF16), nv[b].astype(BF16)) / den[b] for b in rng]
    out_rows = []
    for b in rng:
        ob = jnp.where(own_kv, o[b], 0.0)
        chunks = [jnp.sum(jnp.where(urow == u, ob, 0.0), axis=0, keepdims=True) for u in range(group)]
        out_rows.append(jnp.concatenate(chunks, axis=1))
    o_ref[...] = jnp.concatenate(out_rows, axis=0).astype(o_ref.dtype)


def _attn_sample(q_perm, kv_new_t, cache_kt, cache_vt, sink_mat, bt):
    n, win = cache_kt.shape[0], cache_kt.shape[2]
    tabs = _rope_tables(jnp.full((1,), PAST_LEN, jnp.int32))
    c, sp, sm = (jnp.broadcast_to(z, (8, LANES)) for z in tabs)
    cc, spc, smc = (jnp.broadcast_to(jnp.tile(z, (1, KV_WIDTH // LANES)).T, (KV_WIDTH, n)) for z in tabs)
    small = lambda shape: pl.BlockSpec(shape, lambda i: (0, 0))
    cache_spec = pl.BlockSpec((bt, KV_WIDTH, win), lambda i: (i, 0, 0))
    return pl.pallas_call(
        _attn_sample_kernel,
        grid=(n // bt,),
        in_specs=[pl.BlockSpec((bt, ATTN_WIDTH), lambda i: (i, 0)),
                  small((2 * KV_WIDTH, n)),
                  cache_spec, cache_spec,
                  small((8, LANES)), small((8, LANES)), small((8, LANES)),
                  small((KV_WIDTH, n)), small((KV_WIDTH, n)), small((KV_WIDTH, n)),
                  small((N_Q_HEADS, LANES))],
        out_specs=[pl.BlockSpec((bt, ATTN_WIDTH), lambda i: (i, 0)), cache_spec, cache_spec],
        out_shape=[jax.ShapeDtypeStruct((n, ATTN_WIDTH), BF16),
                   jax.ShapeDtypeStruct(cache_kt.shape, F32),
                   jax.ShapeDtypeStruct(cache_vt.shape, F32)],
        compiler_params=_cparams(1),
        name="attn_sample",
    )(q_perm, kv_new_t, cache_kt, cache_vt, c, sp, sm, cc, spc, smc, sink_mat)


def _rwkv_sample_kernel(r_ref, k_ref, v_ref, l_ref, pr_ref, pk_ref, pv_ref, pl_ref, st_ref,
                        mur_ref, muk_ref, muv_ref, mul_ref, w0_ref, a0_ref, kk_ref, ka_ref,
                        rk_ref, gnw_ref, gnb_ref, wd_ref, wi_ref, wg_ref,
                        o_ref, ns_ref, vec_s, keep_s, y_s):
    h = pl.program_id(0)
    hd = HEAD_DIM
    q_a, q_w, q_b, q_k, q_r, q_v = range(6)

    @pl.when(h == 0)
    def _():
        bd2 = _head_block_ones()
        prm = tuple(ref[...] for ref in (mur_ref, muk_ref, muv_ref, mul_ref, w0_ref, a0_ref, kk_ref, ka_ref))
        r, k, v, logw, a_s, b_s, gate = _rwkv_prepare(
            r_ref[...], k_ref[...], v_ref[...], l_ref[...], pr_ref[...], pk_ref[...], pv_ref[...], pl_ref[...],
            prm, wd_ref, wi_ref, wg_ref, bd2)
        for qi, x in enumerate((a_s, jnp.exp(logw), b_s, k, r, v)):
            xt = x.T
            for hh in range(N_RWKV_HEADS):
                vec_s[qi, hh] = xt[hd * hh:hd * (hh + 1), :]
        for qi, x in enumerate((r, k, v, gate)):
            keep_s[qi] = x

    a_h, w_h, b_h, k_h, r_h = (vec_s[qi, h] for qi in (q_a, q_w, q_b, q_k, q_r))
    for i in range(hd):
        s = st_ref[0, i]
        sa = jnp.sum(s * a_h, axis=0, keepdims=True)
        s_new = s * w_h + sa * b_h + vec_s[q_v, h, i:i + 1, :] * k_h
        ns_ref[0, i] = s_new
        y_s[h, i:i + 1, :] = jnp.sum(s_new * r_h, axis=0, keepdims=True)

    @pl.when(h == pl.num_programs(0) - 1)
    def _():
        y = jnp.concatenate([y_s[hh] for hh in range(N_RWKV_HEADS)], axis=0).T
        out = _rwkv_finish(y, keep_s[0], keep_s[1], keep_s[2], keep_s[3], rk_ref[...], gnw_ref[...], gnb_ref[...],
                           _head_block_ones())
        o_ref[...] = out.astype(o_ref.dtype)


def _rwkv_sample(p_all, shift, shift_l, state_t, prm):
    n = state_t.shape[-1]
    wide = lambda col0: pl.BlockSpec((n, RWKV_WIDTH), lambda h: (0, col0 // RWKV_WIDTH))
    proj = lambda col0: pl.BlockSpec((pl.Element(n), pl.Element(RWKV_WIDTH)), lambda h: (0, pl.multiple_of(col0, LANES)))
    vec = pl.BlockSpec((1, RWKV_WIDTH), lambda h: (0, 0))
    vec_l = pl.BlockSpec((1, LORA_PAD), lambda h: (0, 0))
    st_spec = pl.BlockSpec((1, HEAD_DIM, HEAD_DIM, n), lambda h: (h, 0, 0, 0))
    return pl.pallas_call(
        _rwkv_sample_kernel,
        grid=(N_RWKV_HEADS,),
        in_specs=[proj(COL_R), proj(COL_KR), proj(COL_VR),
                  pl.BlockSpec((n, LORA_PAD), lambda h: (0, COL_LORA // LORA_PAD)),
                  wide(0), wide(RWKV_WIDTH), wide(2 * RWKV_WIDTH),
                  pl.BlockSpec((n, LORA_PAD), lambda h: (0, 0)),
                  st_spec,
                  vec, vec, vec, vec_l, vec, vec, vec, vec, vec, vec, vec,
                  pl.BlockSpec((LANES, RWKV_WIDTH), lambda h: (0, 0)),
                  pl.BlockSpec((LANES, RWKV_WIDTH), lambda h: (0, 0)),
                  pl.BlockSpec((2 * LANES, RWKV_WIDTH), lambda h: (0, 0))],
        out_specs=[pl.BlockSpec((n, RWKV_WIDTH), lambda h: (0, 0)), st_spec],
        out_shape=[jax.ShapeDtypeStruct((n, RWKV_WIDTH), BF16), jax.ShapeDtypeStruct(state_t.shape, F32)],
        scratch_shapes=[pltpu.VMEM((6, N_RWKV_HEADS, HEAD_DIM, n), F32),
                        pltpu.VMEM((4, n, RWKV_WIDTH), F32),
                        pltpu.VMEM((N_RWKV_HEADS, HEAD_DIM, n), F32)],
        compiler_params=_cparams(1),
        name="rwkv_sample",
    )(p_all, p_all, p_all, p_all, shift, shift, shift, shift_l, state_t,
      prm["mu_r"], prm["mu_k"], prm["mu_v"], prm["mu_l"], prm["w0"], prm["a0"], prm["k_k"], prm["k_a"],
      prm["r_k"], prm["gn_w"], prm["gn_b"], prm["wd"], prm["wi"], prm["wg"])


def _swap_head_order(z, outer, inner):
    n = z.shape[0]
    return z.reshape(n, outer, inner, HEAD_DIM).transpose(0, 2, 1, 3).reshape(n, outer * inner * HEAD_DIM)


def _shift_columns(p_rows):
    return p_rows[:, COL_R:COL_R + RWKV_PROJ_WIDTH]


def _forward(x_prompt, x_sample, cache_k_win, cache_v_win, state_shift, state_wkv, w, cfg):
    b, t, _ = x_prompt.shape
    n_s = x_sample.shape[0]
    row = lambda z: z.reshape(1, -1).astype(F32)
    mu = w["mu_shift"]
    wd, wi, wg = _rwkv_lora_weights(w["w_decay_up"], w["w_iclr_up"], w["w_gate_up"])
    prm = dict(
        mu_r=row(mu[:RWKV_WIDTH]), mu_k=row(mu[RWKV_WIDTH:2 * RWKV_WIDTH]), mu_v=row(mu[2 * RWKV_WIDTH:3 * RWKV_WIDTH]),
        mu_l=row(jnp.pad(mu[3 * RWKV_WIDTH:], (0, LORA_PAD - LORA_WIDTH))),
        w0=row(w["w0"]), a0=row(w["a0"]), k_k=row(w["k_k"]), k_a=row(w["k_a"]), r_k=row(w["r_k"]),
        gn_w=row(w["gn_w"]), gn_b=row(w["gn_b"]), wd=wd, wi=wi, wg=wg)
    wpa = w["w_proj_attn"].astype(BF16)
    wpr = w["w_proj_rwkv"].astype(BF16)
    wo = w["w_out"].astype(BF16)
    wu = w["w_up"].astype(BF16)
    wdn = w["w_down"].astype(BF16)
    ln1g, ln1b, ln2g, ln2b = row(w["ln1_g"]), row(w["ln1_b"]), row(w["ln2_g"]), row(w["ln2_b"])
    sinks = w["attn_sinks"].astype(F32)

    xp = x_prompt.reshape(b * t, D_MODEL)
    w_in_t = w["w_in"].T.astype(BF16)
    pp = _inproj(xp, w_in_t, cfg["tm_in"], cfg["tn_in"])
    attn_p, kwin_p, vwin_p = _attn_prompt(pp, sinks, b, t)
    rwkv_p, wkv_p = _rwkv_prompt(pp, prm, b, t, cfg["tb_rwkv"], cfg["lw_rwkv"])
    hp = _merge(xp, attn_p, rwkv_p, w_in_t, wpa, wpr, wo, ln1g, ln1b, cfg["tm_merge"], cfg["tj_merge"])
    yp = _ffn(hp, wu, wdn, ln2g, ln2b, cfg["tm_ffn"], cfg["tf_ffn"])
    shift_p = _shift_columns(pp.reshape(b, t, PACK_WIDTH)[:, t - 1])

    group = N_Q_HEADS // N_KV_HEADS
    xs = x_sample.reshape(n_s, D_MODEL)
    ps = _inproj(xs, w_in_t, n_s, cfg["tn_in"])
    q_perm = _swap_head_order(ps[:, COL_Q:COL_Q + ATTN_WIDTH], N_KV_HEADS, group)
    sink_mat = jnp.broadcast_to(sinks.reshape(N_KV_HEADS, group).T.reshape(N_Q_HEADS, 1), (N_Q_HEADS, LANES))
    win = cache_k_win.shape[1]
    to_t = lambda z: jnp.transpose(z, (0, 2, 3, 1)).reshape(n_s, KV_WIDTH, win)
    from_t = lambda z: jnp.transpose(z.reshape(n_s, N_KV_HEADS, HEAD_DIM, win), (0, 3, 1, 2))[None]
    attn_s, nk_t, nv_t = _attn_sample(q_perm, ps[:, COL_K:COL_K + 2 * KV_WIDTH].T, to_t(cache_k_win), to_t(cache_v_win),
                                      sink_mat, cfg["bt_sample"])
    attn_s = _swap_head_order(attn_s, group, N_KV_HEADS)
    shift_l = jnp.pad(state_shift[:, 3 * RWKV_WIDTH:], ((0, 0), (0, LORA_PAD - LORA_WIDTH)))
    rwkv_s, wkv_t = _rwkv_sample(ps, state_shift, shift_l, jnp.transpose(state_wkv, (1, 2, 3, 0)), prm)
    wkv_s = jnp.transpose(wkv_t, (3, 0, 1, 2))
    hs = _merge(xs, attn_s, rwkv_s, w_in_t, wpa, wpr, wo, ln1g, ln1b, n_s, cfg["tj_merge"])
    ys = _ffn(hs, wu, wdn, ln2g, ln2b, n_s, cfg["tf_ffn"])
    shift_s = _shift_columns(ps)

    kv5 = lambda z: z.reshape(1, z.shape[0], z.shape[1], N_KV_HEADS, HEAD_DIM)
    return (yp.reshape(b, t, D_MODEL), ys.reshape(n_s, 1, D_MODEL),
            kv5(kwin_p), kv5(vwin_p), shift_p[None], wkv_p[None],
            from_t(nk_t), from_t(nv_t), shift_s[None], wkv_s[None])


_CFG = dict(tm_in=1024, tn_in=1024, tb_rwkv=256, lw_rwkv=1024, tm_merge=512, tj_merge=512, tm_ffn=512, tf_ffn=1024, bt_sample=16)


def kernel(x_prompt, x_sample, cache_k_win, cache_v_win, state_shift, state_wkv, w_in, attn_sinks, mu_shift, w0,
           w_decay_up, a0, w_iclr_up, w_gate_up, k_k, k_a, r_k, gn_w, gn_b, w_proj_attn, w_proj_rwkv, w_out,
           ln1_g, ln1_b, w_up, w_down, ln2_g, ln2_b):
    w = dict(w_in=w_in[0], attn_sinks=attn_sinks[0], mu_shift=mu_shift[0], w0=w0[0], w_decay_up=w_decay_up[0],
             a0=a0[0], w_iclr_up=w_iclr_up[0], w_gate_up=w_gate_up[0], k_k=k_k[0], k_a=k_a[0], r_k=r_k[0],
             gn_w=gn_w[0], gn_b=gn_b[0], w_proj_attn=w_proj_attn[0], w_proj_rwkv=w_proj_rwkv[0], w_out=w_out[0],
             ln1_g=ln1_g[0], ln1_b=ln1_b[0], w_up=w_up[0], w_down=w_down[0], ln2_g=ln2_g[0], ln2_b=ln2_b[0])
    return _forward(x_prompt, x_sample, cache_k_win[0], cache_v_win[0], state_shift[0], state_wkv[0], w, _CFG)
```

```python
import jax
import jax.numpy as jnp
from jax import lax
from jax.experimental import pallas as pl
from jax.experimental.pallas import tpu as pltpu

F32 = jnp.float32
BF16 = jnp.bfloat16

D_MODEL = 2048
HEAD_DIM = 64
N_Q_HEADS = 16
N_KV_HEADS = 4
ATTN_WIDTH = N_Q_HEADS * HEAD_DIM
KV_WIDTH = N_KV_HEADS * HEAD_DIM
WINDOW = 128
ROPE_THETA = 500000.0
ROT_DIM = HEAD_DIM // 4
N_RWKV_HEADS = 16
RWKV_WIDTH = N_RWKV_HEADS * HEAD_DIM
DECAY_LORA = 64
ICLR_LORA = 64
GATE_LORA = 160
LORA_WIDTH = DECAY_LORA + ICLR_LORA + GATE_LORA
RWKV_PROJ_WIDTH = 3 * RWKV_WIDTH + LORA_WIDTH
D_FF = 4 * D_MODEL
PAST_LEN = 16384
DEEPNORM_ALPHA = 2.0 ** 0.25
LN_EPS = 1e-5
GN_EPS = HEAD_DIM * 1e-5
NEG_BIG = -1e30

LANES = 128
BF16_ROWS_PER_TILE = 16
HEADS_PER_GROUP = 8
CHUNK = 64
VMEM_LIMIT = 56 * 1024 * 1024

COL_Q = 0
COL_K = 1024
COL_V = 1280
COL_R = 1536
COL_KR = 2560
COL_VR = 3584
COL_LORA = 4608
LORA_PAD = 512
PACK_WIDTH = COL_LORA + LORA_PAD
GATE_ROW0 = COL_LORA + LORA_WIDTH


def _cparams(n_axes):
    return pltpu.CompilerParams(dimension_semantics=("arbitrary",) * n_axes, vmem_limit_bytes=VMEM_LIMIT)


def _sigmoid(x):
    return 1.0 / (1.0 + jnp.exp(-x))


def _softplus(x):
    return jnp.maximum(x, 0.0) + jnp.log(1.0 + jnp.exp(-jnp.abs(x)))


def _layer_norm_rows(z, g, b):
    mu = jnp.mean(z, axis=-1, keepdims=True)
    d = z - mu
    var = jnp.mean(d * d, axis=-1, keepdims=True)
    return d * lax.rsqrt(var + LN_EPS) * g + b


def _dot_t(a, b):
    return lax.dot_general(a, b, (((1,), (1,)), ((), ())), preferred_element_type=F32)


def _split_bf16(x):
    hi = x.astype(BF16)
    lo = (x - hi.astype(F32)).astype(BF16)
    return hi, lo


def _head_block_ones():
    r = lax.broadcasted_iota(jnp.int32, (2 * LANES, LANES), 0)
    c = lax.broadcasted_iota(jnp.int32, (2 * LANES, LANES), 1)
    return jnp.where((r % LANES) // HEAD_DIM == c // HEAD_DIM, 1.0, 0.0).astype(BF16)


def _head_sum(x, bd2):
    outs = []
    for t in range(x.shape[1] // LANES):
        hi, lo = _split_bf16(x[:, LANES * t:LANES * (t + 1)])
        outs.append(jnp.dot(jnp.concatenate([hi, lo], axis=1), bd2, preferred_element_type=F32))
    return outs[0] if len(outs) == 1 else jnp.concatenate(outs, axis=1)


def _inproj_kernel(x_ref, w_ref, o_ref, xb_ref):
    @pl.when(pl.program_id(1) == 0)
    def _():
        xb_ref[...] = x_ref[...].astype(BF16)

    o_ref[...] = _dot_t(xb_ref[...], w_ref[...])


def _inproj(x, w_t, tm, tn):
    m, k = x.shape
    n = PACK_WIDTH
    return pl.pallas_call(
        _inproj_kernel,
        grid=(m // tm, n // tn),
        in_specs=[pl.BlockSpec((tm, k), lambda i, j: (i, 0)),
                  pl.BlockSpec((tn, k), lambda i, j: (j, 0))],
        out_specs=pl.BlockSpec((tm, tn), lambda i, j: (i, j)),
        out_shape=jax.ShapeDtypeStruct((m, n), F32),
        scratch_shapes=[pltpu.VMEM((tm, k), BF16)],
        compiler_params=_cparams(2),
        name="inproj",
    )(x, w_t)


def _rope_tables(pos):
    inv = ROPE_THETA ** (-jnp.arange(0, ROT_DIM, 2, dtype=F32) / ROT_DIM)
    ang = pos.astype(F32)[:, None] * inv[None, :]
    cos, sin = jnp.cos(ang), jnp.sin(ang)
    t = pos.shape[0]
    half = ROT_DIM // 2
    pad = HEAD_DIM - ROT_DIM
    c_head = jnp.concatenate([cos, cos, jnp.ones((t, pad), F32)], axis=1)
    sp_head = jnp.concatenate([-sin, jnp.zeros((t, half + pad), F32)], axis=1)
    sm_head = jnp.concatenate([jnp.zeros((t, half), F32), sin, jnp.zeros((t, pad), F32)], axis=1)
    rep = LANES // HEAD_DIM
    return tuple(jnp.tile(z, (1, rep)) for z in (c_head, sp_head, sm_head))


def _rope_tile(x, c, sp, sm):
    half = ROT_DIM // 2
    return x * c + pltpu.roll(x, LANES - half, 1) * sp + pltpu.roll(x, half, 1) * sm


def _attn_prompt_kernel(q_ref, k_ref, v_ref, c_ref, sp_ref, sm_ref, sink_ref,
                        o_ref, kwin_ref, vwin_ref, kp_ref, vp_ref):
    i = pl.program_id(1)
    blk = WINDOW

    @pl.when(i == 0)
    def _():
        kp_ref[...] = jnp.zeros_like(kp_ref)
        vp_ref[...] = jnp.zeros_like(vp_ref)

    c, sp, sm = c_ref[...], sp_ref[...], sm_ref[...]
    lane = lax.broadcasted_iota(jnp.int32, (2 * blk, LANES), 1)
    lo2 = lane < HEAD_DIM
    lo1 = lax.broadcasted_iota(jnp.int32, (blk, LANES), 1) < HEAD_DIM

    own = lax.broadcasted_iota(jnp.int32, (blk, blk), 1) <= lax.broadcasted_iota(jnp.int32, (blk, blk), 0)
    no_prev = jnp.where(i == 0, NEG_BIG, 0.0)
    zero_p = jnp.zeros((blk, blk), BF16)

    n_kv_tiles = KV_WIDTH // LANES
    kk_g, va_g, vb_g = [], [], []
    for t in range(n_kv_tiles):
        sl = slice(LANES * t, LANES * (t + 1))
        kcur = _rope_tile(k_ref[:, sl], c, sp, sm)
        vcur = v_ref[:, sl]
        kwin_ref[0, :, sl] = kcur
        vwin_ref[0, :, sl] = vcur
        kall = jnp.concatenate([kp_ref[:, sl], kcur], axis=0)
        vall = jnp.concatenate([vp_ref[:, sl], vcur], axis=0)
        kp_ref[:, sl] = kcur
        vp_ref[:, sl] = vcur
        kswap = pltpu.roll(kall, HEAD_DIM, 1)
        vswap = pltpu.roll(vall, HEAD_DIM, 1)
        kk_g += [jnp.where(lo2, kall, kswap).astype(BF16), jnp.where(lo2, kswap, kall).astype(BF16)]
        va_g += [jnp.where(lo2, vall, 0.0).astype(BF16), jnp.where(lo2, vswap, 0.0).astype(BF16)]
        vb_g += [jnp.where(lo2, 0.0, vswap).astype(BF16), jnp.where(lo2, 0.0, vall).astype(BF16)]

    group = N_Q_HEADS // N_KV_HEADS
    tiles = range(ATTN_WIDTH // LANES)
    qt = [_rope_tile(q_ref[:, LANES * j:LANES * (j + 1)], c, sp, sm) * (HEAD_DIM ** -0.5) for j in tiles]
    for h0 in range(0, N_Q_HEADS, HEADS_PER_GROUP):
        heads = range(h0, h0 + HEADS_PER_GROUP)
        qm = {h: (jnp.where(lo1, qt[h // 2], 0.0) if h % 2 == 0 else jnp.where(lo1, 0.0, qt[h // 2])).astype(BF16) for h in heads}
        sinks = {h: sink_ref[h] for h in heads}
        s2 = {h: _dot_t(qm[h], kk_g[h // group]) for h in heads}
        s = {h: jnp.where(own, s2[h][:, blk:], s2[h][:, :blk] + no_prev) for h in heads}
        m = {h: jnp.maximum(jnp.max(s[h], axis=1, keepdims=True), sinks[h]) for h in heads}
        p = {h: jnp.exp(s[h] - m[h]) for h in heads}
        den = {h: jnp.sum(p[h], axis=1, keepdims=True) + jnp.exp(sinks[h] - m[h]) for h in heads}
        pb = {h: p[h].astype(BF16) for h in heads}
        p2 = {h: jnp.concatenate([jnp.where(own, zero_p, pb[h]), jnp.where(own, pb[h], zero_p)], axis=1) for h in heads}
        o = {h: jnp.dot(p2[h], (va_g if h % 2 == 0 else vb_g)[h // group], preferred_element_type=F32) / den[h]
             for h in heads}
        for j in range(h0 // 2, (h0 + HEADS_PER_GROUP) // 2):
            o_ref[:, LANES * j:LANES * (j + 1)] = (o[2 * j] + o[2 * j + 1]).astype(o_ref.dtype)


def _attn_prompt(p_all, sinks, b, t):
    blk = WINDOW
    nb = t // blk
    c, sp, sm = _rope_tables(jnp.arange(t, dtype=jnp.int32))
    tab_spec = pl.BlockSpec((blk, LANES), lambda bi, i: (i, 0))
    row = lambda bi, i: bi * nb + i
    return pl.pallas_call(
        _attn_prompt_kernel,
        grid=(b, nb),
        in_specs=[pl.BlockSpec((blk, ATTN_WIDTH), lambda bi, i: (row(bi, i), COL_Q // ATTN_WIDTH)),
                  pl.BlockSpec((blk, KV_WIDTH), lambda bi, i: (row(bi, i), COL_K // KV_WIDTH)),
                  pl.BlockSpec((blk, KV_WIDTH), lambda bi, i: (row(bi, i), COL_V // KV_WIDTH)),
                  tab_spec, tab_spec, tab_spec,
                  pl.BlockSpec(memory_space=pltpu.SMEM)],
        out_specs=[pl.BlockSpec((blk, ATTN_WIDTH), lambda bi, i: (row(bi, i), 0)),
                   pl.BlockSpec((1, blk, KV_WIDTH), lambda bi, i: (bi, 0, 0)),
                   pl.BlockSpec((1, blk, KV_WIDTH), lambda bi, i: (bi, 0, 0))],
        out_shape=[jax.ShapeDtypeStruct((b * t, ATTN_WIDTH), BF16),
                   jax.ShapeDtypeStruct((b, blk, KV_WIDTH), F32),
                   jax.ShapeDtypeStruct((b, blk, KV_WIDTH), F32)],
        scratch_shapes=[pltpu.VMEM((blk, KV_WIDTH), F32), pltpu.VMEM((blk, KV_WIDTH), F32)],
        compiler_params=_cparams(2),
        name="attn_prompt",
    )(p_all, p_all, p_all, c, sp, sm, sinks)


def _rwkv_prepare(r_in, k_in, v_in, l_in, pr, pk, pv, plr, prm, wd_ref, wi_ref, wg_ref, bd2):
    mur, muk, muv, mul, w0, a0, k_k, k_a = prm
    r = r_in + (pr - r_in) * mur
    k = k_in + (pk - k_in) * muk
    v = v_in + (pv - v_in) * muv
    ls = l_in + (plr - l_in) * mul
    l0 = ls[:, :LANES]
    lane = lax.broadcasted_iota(jnp.int32, l0.shape, 1)
    z0 = jnp.where(lane < DECAY_LORA, jnp.tanh(l0), l0).astype(BF16)
    zg = _sigmoid(ls[:, LANES:3 * LANES]).astype(BF16)
    dec_up = jnp.dot(z0, wd_ref[...], preferred_element_type=F32)
    icl_up = jnp.dot(z0, wi_ref[...], preferred_element_type=F32)
    gate = jnp.dot(zg, wg_ref[...], preferred_element_type=F32)
    w_log = -_softplus(-(w0 + dec_up)) - 0.5
    logw = -jnp.exp(w_log)
    a_sig = _sigmoid(a0 + icl_up)
    kk = k * k_k
    nrm = jnp.maximum(jnp.sqrt(_head_sum(kk * kk, bd2)), 1e-12)
    kk = kk / nrm
    k = k * (1.0 + (a_sig - 1.0) * k_a)
    return r, k, v, logw, -kk, kk * a_sig, gate


def _rwkv_finish(y, r, k, v, gate, r_k, gn_w, gn_b, bd2):
    mu = _head_sum(y, bd2) * (1.0 / HEAD_DIM)
    d = y - mu
    var = _head_sum(d * d, bd2) * (1.0 / HEAD_DIM)
    yn = d * lax.rsqrt(var + GN_EPS) * gn_w + gn_b
    bonus = _head_sum(r * k * r_k, bd2) * v
    return (yn + bonus) * gate


def _rwkv_prompt_kernel(r_ref, k_ref, v_ref, l_ref,
                        mur_ref, muk_ref, muv_ref, mul_ref, w0_ref, a0_ref, kk_ref, ka_ref,
                        rk_ref, gnw_ref, gnb_ref, wd_ref, wi_ref, wg_ref,
                        o_ref, sout_ref,
                        cr_ref, ck_ref, cv_ref, cl_ref, st_ref, y_ref):
    ti = pl.program_id(2)
    tb, lw = r_ref.shape
    n_pair = lw // LANES
    n_chunk = tb // CHUNK

    @pl.when(ti == 0)
    def _():
        for ref in (cr_ref, ck_ref, cv_ref, cl_ref, st_ref):
            ref[...] = jnp.zeros_like(ref)

    def shifted(x, carry_ref):
        rolled = pltpu.roll(x, 1, 0)
        row = lax.broadcasted_iota(jnp.int32, x.shape, 0)
        prev = jnp.where(row == 0, carry_ref[0:1, :], rolled)
        carry_ref[0:1, :] = x[tb - 1:tb, :]
        return prev

    bd2 = _head_block_ones()
    r_in, k_in, v_in, l_in = r_ref[...], k_ref[...], v_ref[...], l_ref[...]
    prm = tuple(ref[...] for ref in (mur_ref, muk_ref, muv_ref, mul_ref, w0_ref, a0_ref, kk_ref, ka_ref))
    r, k, v, logw, a_s, b_s, gate = _rwkv_prepare(
        r_in, k_in, v_in, l_in,
        shifted(r_in, cr_ref), shifted(k_in, ck_ref), shifted(v_in, cv_ref), shifted(l_in, cl_ref),
        prm, wd_ref, wi_ref, wg_ref, bd2)

    c = CHUNK
    ri = lax.broadcasted_iota(jnp.int32, (c, 3 * c), 0)
    ci = lax.broadcasted_iota(jnp.int32, (c, 3 * c), 1) % c
    tri3 = jnp.where(ri >= ci, 1.0, 0.0).astype(BF16)
    r2 = lax.broadcasted_iota(jnp.int32, (2 * c, LANES), 0)
    l2 = lax.broadcasted_iota(jnp.int32, (2 * c, LANES), 1)
    tt, ss = r2 % c, l2 % c
    causal = ss < tt + r2 // c
    lo1 = lax.broadcasted_iota(jnp.int32, (c, LANES), 1) < HEAD_DIM
    diag_blocks = (r2 // HEAD_DIM) == (l2 // HEAD_DIM)
    eye_side = jnp.where(lax.broadcasted_iota(jnp.int32, (c, LANES), 0) == lax.broadcasted_iota(jnp.int32, (c, LANES), 1) % c,
                         1.0, 0.0)

    def stack_heads(x):
        return jnp.concatenate([jnp.where(lo1, x, 0.0), jnp.where(lo1, 0.0, x)], axis=0)

    def dot_t(a, b):
        return lax.dot_general(a, b, (((1,), (1,)), ((), ())), preferred_element_type=F32)

    def dot_tt(a, b):
        return lax.dot_general(a, b, (((0,), (0,)), ((), ())), preferred_element_type=F32)

    def dot(a, b):
        return jnp.dot(a.astype(BF16), b.astype(BF16), preferred_element_type=F32)

    lanes = [slice(LANES * pi, LANES * (pi + 1)) for pi in range(n_pair)]
    units = [(ch, pi) for ch in range(n_chunk) for pi in range(n_pair)]
    un = range(len(units))
    el = []
    for ch in range(n_chunk):
        rows = slice(c * ch, c * (ch + 1))
        lw_c = logw[rows]
        hi = lw_c.astype(BF16)
        rem = lw_c - hi.astype(F32)
        mid = rem.astype(BF16)
        low = (rem - mid.astype(F32)).astype(BF16)
        lcum = jnp.dot(tri3, jnp.concatenate([hi, mid, low], axis=0), preferred_element_type=F32)
        ltot = lcum[c - 1:c, :]
        p_inv = jnp.exp(-lcum)
        p_tail = jnp.exp(ltot - lcum)
        el.append(dict(aq=a_s[rows] * jnp.exp(lcum - lw_c), rq=r[rows] * jnp.exp(lcum), bk=b_s[rows] * p_inv,
                       kq=k[rows] * p_inv, bt=b_s[rows] * p_tail, kt=k[rows] * p_tail, v=v[rows],
                       p_end=jnp.exp(ltot)))
    op = lambda name, u: el[units[u][0]][name][:, lanes[units[u][1]]]

    ar = [jnp.concatenate([op("aq", u), op("rq", u)], axis=0).astype(BF16) for u in un]
    gb = [jnp.where(causal, dot_t(ar[u], stack_heads(op("bk", u)).astype(BF16)), 0.0) for u in un]
    gk = [jnp.where(causal, dot_t(ar[u], stack_heads(op("kq", u)).astype(BF16)), 0.0) for u in un]
    gv = [dot(gk[u], stack_heads(op("v", u))) for u in un]
    pw = [gb[u][:c] for u in un]
    tm = [eye_side + pw[u] for u in un]
    for level in range(1, 6):
        bd = [stack_heads(pw[u]).astype(BF16) for u in un]
        if level == 1:
            pw = [jnp.dot(pw[u].astype(BF16), bd[u], preferred_element_type=F32) for u in un]
            bd = [stack_heads(pw[u]).astype(BF16) for u in un]
        if level < 5:
            both = [jnp.dot(jnp.concatenate([tm[u], pw[u]], axis=0).astype(BF16), bd[u], preferred_element_type=F32)
                    for u in un]
            tm = [tm[u] + both[u][:c] for u in un]
            pw = [both[u][c:] for u in un]
        else:
            tm = [tm[u] + jnp.dot(tm[u].astype(BF16), bd[u], preferred_element_type=F32) for u in un]
    tax = [dot(tm[u], jnp.concatenate([stack_heads(op("aq", u)), stack_heads(gv[u][:c])], axis=1)) for u in un]
    taq = [tax[u][:, :LANES] for u in un]
    txv = [tax[u][:, LANES:] for u in un]
    arx = [dot(gb[u][c:], jnp.concatenate([stack_heads(taq[u]), stack_heads(txv[u])], axis=1)) for u in un]
    mb = [jnp.where(diag_blocks, dot_tt(op("bt", u).astype(BF16), taq[u].astype(BF16)), 0.0).astype(BF16) for u in un]
    cct = [jnp.where(diag_blocks,
                     dot_tt(jnp.concatenate([txv[u], op("v", u)], axis=0).astype(BF16),
                            jnp.concatenate([op("bt", u), op("kt", u)], axis=0).astype(BF16)), 0.0) for u in un]
    rqp = [(op("rq", u) + arx[u][:, :LANES]).astype(BF16) for u in un]
    yc = [gv[u][c:] + arx[u][:, LANES:] for u in un]
    for u, (ch, pi) in enumerate(units):
        s_old = st_ref[pi]
        sb = s_old.astype(BF16)
        y_ref[c * ch:c * (ch + 1), lanes[pi]] = dot_t(rqp[u], sb) + yc[u]
        st_ref[pi] = s_old * op("p_end", u) + dot_t(sb, mb[u]) + cct[u]

    out = _rwkv_finish(y_ref[...], r, k, v, gate, rk_ref[...], gnw_ref[...], gnb_ref[...], bd2)
    o_ref[...] = out.astype(o_ref.dtype)

    @pl.when(ti == pl.num_programs(2) - 1)
    def _():
        for pi in range(n_pair):
            s = st_ref[pi]
            sout_ref[0, 2 * pi] = s[:HEAD_DIM, :HEAD_DIM]
            sout_ref[0, 2 * pi + 1] = s[HEAD_DIM:, HEAD_DIM:]


def _rwkv_lora_weights(w_decay_up, w_iclr_up, w_gate_up):
    z64 = jnp.zeros((DECAY_LORA, RWKV_WIDTH), F32)
    wd = jnp.concatenate([w_decay_up, z64], axis=0).astype(BF16)
    wi = jnp.concatenate([z64, w_iclr_up], axis=0).astype(BF16)
    wg = jnp.concatenate([w_gate_up, jnp.zeros((2 * LANES - GATE_LORA, RWKV_WIDTH), F32)], axis=0).astype(BF16)
    return wd, wi, wg


def _rwkv_prompt(p_all, prm, b, t, tb, lw):
    nt = t // tb
    ns = RWKV_WIDTH // lw
    row = lambda bi, si, ti: bi * nt + ti
    col_spec = lambda col0: pl.BlockSpec(
        (pl.Element(tb), pl.Element(lw)),
        lambda bi, si, ti: (pl.multiple_of(row(bi, si, ti) * tb, tb), pl.multiple_of(col0 + si * lw, LANES)))
    vec = pl.BlockSpec((1, lw), lambda bi, si, ti: (0, si))
    vec_l = pl.BlockSpec((1, LORA_PAD), lambda bi, si, ti: (0, 0))
    return pl.pallas_call(
        _rwkv_prompt_kernel,
        grid=(b, ns, nt),
        in_specs=[col_spec(COL_R), col_spec(COL_KR), col_spec(COL_VR),
                  pl.BlockSpec((tb, LORA_PAD), lambda bi, si, ti: (row(bi, si, ti), COL_LORA // LORA_PAD)),
                  vec, vec, vec, vec_l, vec, vec, vec, vec, vec, vec, vec,
                  pl.BlockSpec((LANES, lw), lambda bi, si, ti: (0, si)),
                  pl.BlockSpec((LANES, lw), lambda bi, si, ti: (0, si)),
                  pl.BlockSpec((2 * LANES, lw), lambda bi, si, ti: (0, si))],
        out_specs=[pl.BlockSpec((tb, lw), lambda bi, si, ti: (row(bi, si, ti), si)),
                   pl.BlockSpec((1, 2 * (lw // LANES), HEAD_DIM, HEAD_DIM), lambda bi, si, ti: (bi, si, 0, 0))],
        out_shape=[jax.ShapeDtypeStruct((b * t, RWKV_WIDTH), BF16),
                   jax.ShapeDtypeStruct((b, N_RWKV_HEADS, HEAD_DIM, HEAD_DIM), F32)],
        scratch_shapes=[pltpu.VMEM((8, lw), F32), pltpu.VMEM((8, lw), F32), pltpu.VMEM((8, lw), F32),
                        pltpu.VMEM((8, LORA_PAD), F32),
                        pltpu.VMEM((lw // LANES, LANES, LANES), F32),
                        pltpu.VMEM((tb, lw), F32)],
        compiler_params=_cparams(3),
        name="rwkv_prompt",
    )(p_all, p_all, p_all, p_all,
      prm["mu_r"], prm["mu_k"], prm["mu_v"], prm["mu_l"], prm["w0"], prm["a0"], prm["k_k"], prm["k_a"],
      prm["r_k"], prm["gn_w"], prm["gn_b"], prm["wd"], prm["wi"], prm["wg"])


def _merge_kernel(x_ref, ao_ref, ro_ref, wga_ref, wgb_ref, wpa_ref, wpr_ref, wo_ref, g_ref, b_ref, h_ref,
                  xb_ref, acc_ref):
    j = pl.program_id(1)

    @pl.when(j == 0)
    def _():
        xb_ref[...] = x_ref[...].astype(BF16)
        acc_ref[...] = jnp.zeros_like(acc_ref)

    xb = xb_ref[...]
    ga = _dot_t(xb, wga_ref[...])
    gb = _dot_t(xb, wgb_ref[...])
    a = jnp.dot(ao_ref[...], wpa_ref[...], preferred_element_type=F32)
    r = jnp.dot(ro_ref[...], wpr_ref[...], preferred_element_type=F32)
    m = _sigmoid(ga) * a + _sigmoid(gb) * r
    acc_ref[...] += jnp.dot(m.astype(BF16), wo_ref[...], preferred_element_type=F32)

    @pl.when(j == pl.num_programs(1) - 1)
    def _():
        z = DEEPNORM_ALPHA * x_ref[...] + acc_ref[...]
        h_ref[...] = _layer_norm_rows(z, g_ref[...], b_ref[...])


def _merge(x, attn_o, rwkv_o, w_in_t, wpa, wpr, wo, ln_g, ln_b, tm, tj):
    m = x.shape[0]
    nj = D_MODEL // tj
    gate_spec = lambda row0: pl.BlockSpec(
        (pl.Element(tj), pl.Element(D_MODEL)),
        lambda i, j: (pl.multiple_of(row0 + j * tj, BF16_ROWS_PER_TILE), 0))
    return pl.pallas_call(
        _merge_kernel,
        grid=(m // tm, nj),
        in_specs=[pl.BlockSpec((tm, D_MODEL), lambda i, j: (i, 0)),
                  pl.BlockSpec((tm, ATTN_WIDTH), lambda i, j: (i, 0)),
                  pl.BlockSpec((tm, RWKV_WIDTH), lambda i, j: (i, 0)),
                  gate_spec(GATE_ROW0), gate_spec(GATE_ROW0 + D_MODEL),
                  pl.BlockSpec((ATTN_WIDTH, tj), lambda i, j: (0, j)),
                  pl.BlockSpec((RWKV_WIDTH, tj), lambda i, j: (0, j)),
                  pl.BlockSpec((tj, D_MODEL), lambda i, j: (j, 0)),
                  pl.BlockSpec((1, D_MODEL), lambda i, j: (0, 0)),
                  pl.BlockSpec((1, D_MODEL), lambda i, j: (0, 0))],
        out_specs=pl.BlockSpec((tm, D_MODEL), lambda i, j: (i, 0)),
        out_shape=jax.ShapeDtypeStruct((m, D_MODEL), F32),
        scratch_shapes=[pltpu.VMEM((tm, D_MODEL), BF16), pltpu.VMEM((tm, D_MODEL), F32)],
        compiler_params=_cparams(2),
        name="merge_ln1",
    )(x, attn_o, rwkv_o, w_in_t, w_in_t, wpa, wpr, wo, ln_g, ln_b)


def _ffn_kernel(h_ref, wu_ref, wd_ref, g_ref, b_ref, y_ref, hb_ref, acc_ref):
    f = pl.program_id(1)

    @pl.when(f == 0)
    def _():
        hb_ref[...] = h_ref[...].astype(BF16)
        acc_ref[...] = jnp.zeros_like(acc_ref)

    u = jnp.dot(hb_ref[...], wu_ref[...], preferred_element_type=F32)
    u = jnp.square(jnp.maximum(u, 0.0))
    acc_ref[...] += jnp.dot(u.astype(BF16), wd_ref[...], preferred_element_type=F32)

    @pl.when(f == pl.num_programs(1) - 1)
    def _():
        z = DEEPNORM_ALPHA * h_ref[...] + acc_ref[...]
        y_ref[...] = _layer_norm_rows(z, g_ref[...], b_ref[...])


def _ffn(h, wu, wd, ln_g, ln_b, tm, tf):
    m = h.shape[0]
    return pl.pallas_call(
        _ffn_kernel,
        grid=(m // tm, D_FF // tf),
        in_specs=[pl.BlockSpec((tm, D_MODEL), lambda i, f: (i, 0)),
                  pl.BlockSpec((D_MODEL, tf), lambda i, f: (0, f)),
                  pl.BlockSpec((tf, D_MODEL), lambda i, f: (f, 0)),
                  pl.BlockSpec((1, D_MODEL), lambda i, f: (0, 0)),
                  pl.BlockSpec((1, D_MODEL), lambda i, f: (0, 0))],
        out_specs=pl.BlockSpec((tm, D_MODEL), lambda i, f: (i, 0)),
        out_shape=jax.ShapeDtypeStruct((m, D_MODEL), F32),
        scratch_shapes=[pltpu.VMEM((tm, D_MODEL), BF16), pltpu.VMEM((tm, D_MODEL), F32)],
        compiler_params=_cparams(2),
        name="ffn_ln2",
    )(h, wu, wd, ln_g, ln_b)


def _attn_sample_kernel(q_ref, kvt_ref, ck_ref, cv_ref, c_ref, sp_ref, sm_ref, cc_ref, spc_ref, smc_ref, sink_ref,
                        o_ref, nk_ref, nv_ref):
    bt = q_ref.shape[0]
    win = ck_ref.shape[2]
    n = kvt_ref.shape[1]
    half = ROT_DIM // 2
    group = N_Q_HEADS // N_KV_HEADS
    c, sp, sm = c_ref[0:1, :], sp_ref[0:1, :], sm_ref[0:1, :]
    sink = sink_ref[:, 0:1]
    row16 = lax.broadcasted_iota(jnp.int32, (N_Q_HEADS, KV_WIDTH), 0)
    lane16 = lax.broadcasted_iota(jnp.int32, (N_Q_HEADS, KV_WIDTH), 1)
    own_kv = (lane16 // HEAD_DIM) == (row16 % N_KV_HEADS)
    urow = row16 // N_KV_HEADS
    pos = lax.broadcasted_iota(jnp.int32, (KV_WIDTH, win), 1)
    seq = lax.broadcasted_iota(jnp.int32, (KV_WIDTH, n), 1)

    kt = kvt_ref[0:KV_WIDTH, :]
    kt = kt * cc_ref[...] + pltpu.roll(kt, KV_WIDTH - half, 0) * spc_ref[...] + pltpu.roll(kt, half, 0) * smc_ref[...]
    vt = kvt_ref[KV_WIDTH:2 * KV_WIDTH, :]

    def rope_row(x):
        return jnp.concatenate([_rope_tile(x[:, LANES * t:LANES * (t + 1)], c, sp, sm)
                                for t in range(x.shape[1] // LANES)], axis=1)

    rng = range(bt)
    me = [seq == pl.program_id(0) * bt + b for b in rng]
    k_col = [jnp.sum(jnp.where(me[b], kt, 0.0), axis=1, keepdims=True) for b in rng]
    v_col = [jnp.sum(jnp.where(me[b], vt, 0.0), axis=1, keepdims=True) for b in rng]
    nk = [jnp.where(pos == win - 1, k_col[b], pltpu.roll(ck_ref[b], win - 1, 1)) for b in rng]
    nv = [jnp.where(pos == win - 1, v_col[b], pltpu.roll(cv_ref[b], win - 1, 1)) for b in rng]
    for b in rng:
        nk_ref[b] = nk[b]
        nv_ref[b] = nv[b]
    qmat = []
    for b in rng:
        q = rope_row(q_ref[b:b + 1, :]) * (HEAD_DIM ** -0.5)
        qb = [jnp.broadcast_to(q[:, KV_WIDTH * u:KV_WIDTH * (u + 1)], (N_Q_HEADS, KV_WIDTH)) for u in range(group)]
        qsel = jnp.where(urow == 0, qb[0], jnp.where(urow == 1, qb[1], jnp.where(urow == 2, qb[2], qb[3])))
        qmat.append(jnp.where(own_kv, qsel, 0.0).astype(BF16))
    s = [jnp.dot(qmat[b], nk[b].astype(BF16), preferred_element_type=F32) for b in rng]
    m = [jnp.maximum(jnp.max(s[b], axis=1, keepdims=True), sink) for b in rng]
    p = [jnp.exp(s[b] - m[b]) for b in rng]
    den = [jnp.sum(p[b], axis=1, keepdims=True) + jnp.exp(sink - m[b]) for b in rng]
    o = [_dot_t(p[b].astype(BF16), nv[b].astype(BF16)) / den[b] for b in rng]
    out_rows = []
    for b in rng:
        ob = jnp.where(own_kv, o[b], 0.0)
        chunks = [jnp.sum(jnp.where(urow == u, ob, 0.0), axis=0, keepdims=True) for u in range(group)]
        out_rows.append(jnp.concatenate(chunks, axis=1))
    o_ref[...] = jnp.concatenate(out_rows, axis=0).astype(o_ref.dtype)


def _attn_sample(q_perm, kv_new_t, cache_kt, cache_vt, sink_mat, bt):
    n, win = cache_kt.shape[0], cache_kt.shape[2]
    tabs = _rope_tables(jnp.full((1,), PAST_LEN, jnp.int32))
    c, sp, sm = (jnp.broadcast_to(z, (8, LANES)) for z in tabs)
    cc, spc, smc = (jnp.broadcast_to(jnp.tile(z, (1, KV_WIDTH // LANES)).T, (KV_WIDTH, n)) for z in tabs)
    small = lambda shape: pl.BlockSpec(shape, lambda i: (0, 0))
    cache_spec = pl.BlockSpec((bt, KV_WIDTH, win), lambda i: (i, 0, 0))
    return pl.pallas_call(
        _attn_sample_kernel,
        grid=(n // bt,),
        in_specs=[pl.BlockSpec((bt, ATTN_WIDTH), lambda i: (i, 0)),
                  small((2 * KV_WIDTH, n)),
                  cache_spec, cache_spec,
                  small((8, LANES)), small((8, LANES)), small((8, LANES)),
                  small((KV_WIDTH, n)), small((KV_WIDTH, n)), small((KV_WIDTH, n)),
                  small((N_Q_HEADS, LANES))],
        out_specs=[pl.BlockSpec((bt, ATTN_WIDTH), lambda i: (i, 0)), cache_spec, cache_spec],
        out_shape=[jax.ShapeDtypeStruct((n, ATTN_WIDTH), BF16),
                   jax.ShapeDtypeStruct(cache_kt.shape, F32),
                   jax.ShapeDtypeStruct(cache_vt.shape, F32)],
        compiler_params=_cparams(1),
        name="attn_sample",
    )(q_perm, kv_new_t, cache_kt, cache_vt, c, sp, sm, cc, spc, smc, sink_mat)


def _rwkv_sample_kernel(r_ref, k_ref, v_ref, l_ref, pr_ref, pk_ref, pv_ref, pl_ref, st_ref,
                        mur_ref, muk_ref, muv_ref, mul_ref, w0_ref, a0_ref, kk_ref, ka_ref,
                        rk_ref, gnw_ref, gnb_ref, wd_ref, wi_ref, wg_ref,
                        o_ref, ns_ref, vec_s, keep_s, y_s):
    h = pl.program_id(0)
    hd = HEAD_DIM
    q_a, q_w, q_b, q_k, q_r, q_v = range(6)

    @pl.when(h == 0)
    def _():
        bd2 = _head_block_ones()
        prm = tuple(ref[...] for ref in (mur_ref, muk_ref, muv_ref, mul_ref, w0_ref, a0_ref, kk_ref, ka_ref))
        r, k, v, logw, a_s, b_s, gate = _rwkv_prepare(
            r_ref[...], k_ref[...], v_ref[...], l_ref[...], pr_ref[...], pk_ref[...], pv_ref[...], pl_ref[...],
            prm, wd_ref, wi_ref, wg_ref, bd2)
        for qi, x in enumerate((a_s, jnp.exp(logw), b_s, k, r, v)):
            xt = x.T
            for hh in range(N_RWKV_HEADS):
                vec_s[qi, hh] = xt[hd * hh:hd * (hh + 1), :]
        for qi, x in enumerate((r, k, v, gate)):
            keep_s[qi] = x

    a_h, w_h, b_h, k_h, r_h = (vec_s[qi, h] for qi in (q_a, q_w, q_b, q_k, q_r))
    for i in range(hd):
        s = st_ref[0, i]
        sa = jnp.sum(s * a_h, axis=0, keepdims=True)
        s_new = s * w_h + sa * b_h + vec_s[q_v, h, i:i + 1, :] * k_h
        ns_ref[0, i] = s_new
        y_s[h, i:i + 1, :] = jnp.sum(s_new * r_h, axis=0, keepdims=True)

    @pl.when(h == pl.num_programs(0) - 1)
    def _():
        y = jnp.concatenate([y_s[hh] for hh in range(N_RWKV_HEADS)], axis=0).T
        out = _rwkv_finish(y, keep_s[0], keep_s[1], keep_s[2], keep_s[3], rk_ref[...], gnw_ref[...], gnb_ref[...],
                           _head_block_ones())
        o_ref[...] = out.astype(o_ref.dtype)


def _rwkv_sample(p_all, shift, shift_l, state_t, prm):
    n = state_t.shape[-1]
    wide = lambda col0: pl.BlockSpec((n, RWKV_WIDTH), lambda h: (0, col0 // RWKV_WIDTH))
    proj = lambda col0: pl.BlockSpec((pl.Element(n), pl.Element(RWKV_WIDTH)), lambda h: (0, pl.multiple_of(col0, LANES)))
    vec = pl.BlockSpec((1, RWKV_WIDTH), lambda h: (0, 0))
    vec_l = pl.BlockSpec((1, LORA_PAD), lambda h: (0, 0))
    st_spec = pl.BlockSpec((1, HEAD_DIM, HEAD_DIM, n), lambda h: (h, 0, 0, 0))
    return pl.pallas_call(
        _rwkv_sample_kernel,
        grid=(N_RWKV_HEADS,),
        in_specs=[proj(COL_R), proj(COL_KR), proj(COL_VR),
                  pl.BlockSpec((n, LORA_PAD), lambda h: (0, COL_LORA // LORA_PAD)),
                  wide(0), wide(RWKV_WIDTH), wide(2 * RWKV_WIDTH),
                  pl.BlockSpec((n, LORA_PAD), lambda h: (0, 0)),
                  st_spec,
                  vec, vec, vec, vec_l, vec, vec, vec, vec, vec, vec, vec,
                  pl.BlockSpec((LANES, RWKV_WIDTH), lambda h: (0, 0)),
                  pl.BlockSpec((LANES, RWKV_WIDTH), lambda h: (0, 0)),
                  pl.BlockSpec((2 * LANES, RWKV_WIDTH), lambda h: (0, 0))],
        out_specs=[pl.BlockSpec((n, RWKV_WIDTH), lambda h: (0, 0)), st_spec],
        out_shape=[jax.ShapeDtypeStruct((n, RWKV_WIDTH), BF16), jax.ShapeDtypeStruct(state_t.shape, F32)],
        scratch_shapes=[pltpu.VMEM((6, N_RWKV_HEADS, HEAD_DIM, n), F32),
                        pltpu.VMEM((4, n, RWKV_WIDTH), F32),
                        pltpu.VMEM((N_RWKV_HEADS, HEAD_DIM, n), F32)],
        compiler_params=_cparams(1),
        name="rwkv_sample",
    )(p_all, p_all, p_all, p_all, shift, shift, shift, shift_l, state_t,
      prm["mu_r"], prm["mu_k"], prm["mu_v"], prm["mu_l"], prm["w0"], prm["a0"], prm["k_k"], prm["k_a"],
      prm["r_k"], prm["gn_w"], prm["gn_b"], prm["wd"], prm["wi"], prm["wg"])


def _swap_head_order(z, outer, inner):
    n = z.shape[0]
    return z.reshape(n, outer, inner, HEAD_DIM).transpose(0, 2, 1, 3).reshape(n, outer * inner * HEAD_DIM)


def _shift_columns(p_rows):
    return p_rows[:, COL_R:COL_R + RWKV_PROJ_WIDTH]


def _forward(x_prompt, x_sample, cache_k_win, cache_v_win, state_shift, state_wkv, w, cfg):
    b, t, _ = x_prompt.shape
    n_s = x_sample.shape[0]
    row = lambda z: z.reshape(1, -1).astype(F32)
    mu = w["mu_shift"]
    wd, wi, wg = _rwkv_lora_weights(w["w_decay_up"], w["w_iclr_up"], w["w_gate_up"])
    prm = dict(
        mu_r=row(mu[:RWKV_WIDTH]), mu_k=row(mu[RWKV_WIDTH:2 * RWKV_WIDTH]), mu_v=row(mu[2 * RWKV_WIDTH:3 * RWKV_WIDTH]),
        mu_l=row(jnp.pad(mu[3 * RWKV_WIDTH:], (0, LORA_PAD - LORA_WIDTH))),
        w0=row(w["w0"]), a0=row(w["a0"]), k_k=row(w["k_k"]), k_a=row(w["k_a"]), r_k=row(w["r_k"]),
        gn_w=row(w["gn_w"]), gn_b=row(w["gn_b"]), wd=wd, wi=wi, wg=wg)
    wpa = w["w_proj_attn"].astype(BF16)
    wpr = w["w_proj_rwkv"].astype(BF16)
    wo = w["w_out"].astype(BF16)
    wu = w["w_up"].astype(BF16)
    wdn = w["w_down"].astype(BF16)
    ln1g, ln1b, ln2g, ln2b = row(w["ln1_g"]), row(w["ln1_b"]), row(w["ln2_g"]), row(w["ln2_b"])
    sinks = w["attn_sinks"].astype(F32)

    xp = x_prompt.reshape(b * t, D_MODEL)
    w_in_t = w["w_in"].T.astype(BF16)
    pp = _inproj(xp, w_in_t, cfg["tm_in"], cfg["tn_in"])
    attn_p, kwin_p, vwin_p = _attn_prompt(pp, sinks, b, t)
    rwkv_p, wkv_p = _rwkv_prompt(pp, prm, b, t, cfg["tb_rwkv"], cfg["lw_rwkv"])
    hp = _merge(xp, attn_p, rwkv_p, w_in_t, wpa, wpr, wo, ln1g, ln1b, cfg["tm_merge"], cfg["tj_merge"])
    yp = _ffn(hp, wu, wdn, ln2g, ln2b, cfg["tm_ffn"], cfg["tf_ffn"])
    shift_p = _shift_columns(pp.reshape(b, t, PACK_WIDTH)[:, t - 1])

    group = N_Q_HEADS // N_KV_HEADS
    xs = x_sample.reshape(n_s, D_MODEL)
    ps = _inproj(xs, w_in_t, n_s, cfg["tn_in"])
    q_perm = _swap_head_order(ps[:, COL_Q:COL_Q + ATTN_WIDTH], N_KV_HEADS, group)
    sink_mat = jnp.broadcast_to(sinks.reshape(N_KV_HEADS, group).T.reshape(N_Q_HEADS, 1), (N_Q_HEADS, LANES))
    win = cache_k_win.shape[1]
    to_t = lambda z: jnp.transpose(z, (0, 2, 3, 1)).reshape(n_s, KV_WIDTH, win)
    from_t = lambda z: jnp.transpose(z.reshape(n_s, N_KV_HEADS, HEAD_DIM, win), (0, 3, 1, 2))[None]
    attn_s, nk_t, nv_t = _attn_sample(q_perm, ps[:, COL_K:COL_K + 2 * KV_WIDTH].T, to_t(cache_k_win), to_t(cache_v_win),
                                      sink_mat, cfg["bt_sample"])
    attn_s = _swap_head_order(attn_s, group, N_KV_HEADS)
    shift_l = jnp.pad(state_shift[:, 3 * RWKV_WIDTH:], ((0, 0), (0, LORA_PAD - LORA_WIDTH)))
    rwkv_s, wkv_t = _rwkv_sample(ps, state_shift, shift_l, jnp.transpose(state_wkv, (1, 2, 3, 0)), prm)
    wkv_s = jnp.transpose(wkv_t, (3, 0, 1, 2))
    hs = _merge(xs, attn_s, rwkv_s, w_in_t, wpa, wpr, wo, ln1g, ln1b, n_s, cfg["tj_merge"])
    ys = _ffn(hs, wu, wdn, ln2g, ln2b, n_s, cfg["tf_ffn"])
    shift_s = _shift_columns(ps)

    kv5 = lambda z: z.reshape(1, z.shape[0], z.shape[1], N_KV_HEADS, HEAD_DIM)
    return (yp.reshape(b, t, D_MODEL), ys.reshape(n_s, 1, D_MODEL),
            kv5(kwin_p), kv5(vwin_p), shift_p[None], wkv_p[None],
            from_t(nk_t), from_t(nv_t), shift_s[None], wkv_s[None])


_CFG = dict(tm_in=1024, tn_in=1024, tb_rwkv=256, lw_rwkv=1024, tm_merge=512, tj_merge=512, tm_ffn=512, tf_ffn=1024, bt_sample=16)


def kernel(x_prompt, x_sample, cache_k_win, cache_v_win, state_shift, state_wkv, w_in, attn_sinks, mu_shift, w0,
           w_decay_up, a0, w_iclr_up, w_gate_up, k_k, k_a, r_k, gn_w, gn_b, w_proj_attn, w_proj_rwkv, w_out,
           ln1_g, ln1_b, w_up, w_down, ln2_g, ln2_b):
    w = dict(w_in=w_in[0], attn_sinks=attn_sinks[0], mu_shift=mu_shift[0], w0=w0[0], w_decay_up=w_decay_up[0],
             a0=a0[0], w_iclr_up=w_iclr_up[0], w_gate_up=w_gate_up[0], k_k=k_k[0], k_a=k_a[0], r_k=r_k[0],
             gn_w=gn_w[0], gn_b=gn_b[0], w_proj_attn=w_proj_attn[0], w_proj_rwkv=w_proj_rwkv[0], w_out=w_out[0],
             ln1_g=ln1_g[0], ln1_b=ln1_b[0], w_up=w_up[0], w_down=w_down[0], ln2_g=ln2_g[0], ln2_b=ln2_b[0])
    return _forward(x_prompt, x_sample, cache_k_win[0], cache_v_win[0], state_shift[0], state_wkv[0], w, _CFG)
```

```python
import jax
import jax.numpy as jnp
from jax import lax
from jax.experimental import pallas as pl
from jax.experimental.pallas import tpu as pltpu

F32 = jnp.float32
BF16 = jnp.bfloat16

D_MODEL = 2048
HEAD_DIM = 64
N_Q_HEADS = 16
N_KV_HEADS = 4
ATTN_WIDTH = N_Q_HEADS * HEAD_DIM
KV_WIDTH = N_KV_HEADS * HEAD_DIM
WINDOW = 128
ROPE_THETA = 500000.0
ROT_DIM = HEAD_DIM // 4
N_RWKV_HEADS = 16
RWKV_WIDTH = N_RWKV_HEADS * HEAD_DIM
DECAY_LORA = 64
ICLR_LORA = 64
GATE_LORA = 160
LORA_WIDTH = DECAY_LORA + ICLR_LORA + GATE_LORA
RWKV_PROJ_WIDTH = 3 * RWKV_WIDTH + LORA_WIDTH
D_FF = 4 * D_MODEL
PAST_LEN = 16384
DEEPNORM_ALPHA = 2.0 ** 0.25
LN_EPS = 1e-5
GN_EPS = HEAD_DIM * 1e-5
NEG_BIG = -1e30
DECAY_SCALE = 0.6065306597126334
NORM_FLOOR_SQ = 1e-24

LANES = 128
BF16_ROWS_PER_TILE = 16
HEADS_PER_GROUP = 8
CHUNK = 64
VMEM_LIMIT = 56 * 1024 * 1024

COL_Q = 0
COL_K = 1024
COL_V = 1280
COL_R = 1536
COL_KR = 2560
COL_VR = 3584
COL_LORA = 4608
LORA_PAD = 512
PACK_WIDTH = COL_LORA + LORA_PAD
GATE_ROW0 = COL_LORA + LORA_WIDTH


def _cparams(n_axes):
    return pltpu.CompilerParams(dimension_semantics=("arbitrary",) * n_axes, vmem_limit_bytes=VMEM_LIMIT)


def _sigmoid(x):
    return 1.0 / (1.0 + jnp.exp(-x))


def _layer_norm_rows(z, g, b):
    mu = jnp.mean(z, axis=-1, keepdims=True)
    d = z - mu
    var = jnp.mean(d * d, axis=-1, keepdims=True)
    return d * lax.rsqrt(var + LN_EPS) * g + b


def _dot_t(a, b):
    return lax.dot_general(a, b, (((1,), (1,)), ((), ())), preferred_element_type=F32)


def _head_block_ones():
    r = lax.broadcasted_iota(jnp.int32, (LANES, LANES), 0)
    c = lax.broadcasted_iota(jnp.int32, (LANES, LANES), 1)
    return jnp.where(r // HEAD_DIM == c // HEAD_DIM, 1.0, 0.0).astype(BF16)


def _head_sum(x, bd):
    outs = [jnp.dot(x[:, LANES * t:LANES * (t + 1)].astype(BF16), bd, preferred_element_type=F32)
            for t in range(x.shape[1] // LANES)]
    return outs[0] if len(outs) == 1 else jnp.concatenate(outs, axis=1)


def _inproj_kernel(x_ref, w_ref, o_ref, xb_ref):
    @pl.when(pl.program_id(1) == 0)
    def _():
        xb_ref[...] = x_ref[...].astype(BF16)

    o_ref[...] = _dot_t(xb_ref[...], w_ref[...])


def _inproj(x, w_t, tm, tn):
    m, k = x.shape
    n = PACK_WIDTH
    return pl.pallas_call(
        _inproj_kernel,
        grid=(m // tm, n // tn),
        in_specs=[pl.BlockSpec((tm, k), lambda i, j: (i, 0)),
                  pl.BlockSpec((tn, k), lambda i, j: (j, 0))],
        out_specs=pl.BlockSpec((tm, tn), lambda i, j: (i, j)),
        out_shape=jax.ShapeDtypeStruct((m, n), F32),
        scratch_shapes=[pltpu.VMEM((tm, k), BF16)],
        compiler_params=_cparams(2),
        name="inproj",
    )(x, w_t)


def _rope_tables(pos):
    inv = ROPE_THETA ** (-jnp.arange(0, ROT_DIM, 2, dtype=F32) / ROT_DIM)
    ang = pos.astype(F32)[:, None] * inv[None, :]
    cos, sin = jnp.cos(ang), jnp.sin(ang)
    t = pos.shape[0]
    half = ROT_DIM // 2
    pad = HEAD_DIM - ROT_DIM
    c_head = jnp.concatenate([cos, cos, jnp.ones((t, pad), F32)], axis=1)
    sp_head = jnp.concatenate([-sin, jnp.zeros((t, half + pad), F32)], axis=1)
    sm_head = jnp.concatenate([jnp.zeros((t, half), F32), sin, jnp.zeros((t, pad), F32)], axis=1)
    rep = LANES // HEAD_DIM
    return tuple(jnp.tile(z, (1, rep)) for z in (c_head, sp_head, sm_head))


def _rope_tile(x, c, sp, sm):
    half = ROT_DIM // 2
    return x * c + pltpu.roll(x, LANES - half, 1) * sp + pltpu.roll(x, half, 1) * sm


def _attn_prompt_kernel(q_ref, k_ref, v_ref, c_ref, sp_ref, sm_ref, sink_ref,
                        o_ref, kwin_ref, vwin_ref, kp_ref, vp_ref):
    i = pl.program_id(1)
    blk = WINDOW

    @pl.when(i == 0)
    def _():
        kp_ref[...] = jnp.zeros_like(kp_ref)
        vp_ref[...] = jnp.zeros_like(vp_ref)

    c, sp, sm = c_ref[...], sp_ref[...], sm_ref[...]
    lane = lax.broadcasted_iota(jnp.int32, (2 * blk, LANES), 1)
    lo2 = lane < HEAD_DIM
    lo1 = lax.broadcasted_iota(jnp.int32, (blk, LANES), 1) < HEAD_DIM

    own = lax.broadcasted_iota(jnp.int32, (blk, blk), 1) <= lax.broadcasted_iota(jnp.int32, (blk, blk), 0)
    no_prev = jnp.where(i == 0, NEG_BIG, 0.0)
    zero_p = jnp.zeros((blk, blk), BF16)

    n_kv_tiles = KV_WIDTH // LANES
    kk_g, va_g, vb_g = [], [], []
    for t in range(n_kv_tiles):
        sl = slice(LANES * t, LANES * (t + 1))
        kcur = _rope_tile(k_ref[:, sl], c, sp, sm)
        vcur = v_ref[:, sl]
        kwin_ref[0, :, sl] = kcur
        vwin_ref[0, :, sl] = vcur
        kall = jnp.concatenate([kp_ref[:, sl], kcur], axis=0)
        vall = jnp.concatenate([vp_ref[:, sl], vcur], axis=0)
        kp_ref[:, sl] = kcur
        vp_ref[:, sl] = vcur
        kswap = pltpu.roll(kall, HEAD_DIM, 1)
        vswap = pltpu.roll(vall, HEAD_DIM, 1)
        kk_g += [jnp.where(lo2, kall, kswap).astype(BF16), jnp.where(lo2, kswap, kall).astype(BF16)]
        va_g += [jnp.where(lo2, vall, 0.0).astype(BF16), jnp.where(lo2, vswap, 0.0).astype(BF16)]
        vb_g += [jnp.where(lo2, 0.0, vswap).astype(BF16), jnp.where(lo2, 0.0, vall).astype(BF16)]

    group = N_Q_HEADS // N_KV_HEADS
    tiles = range(ATTN_WIDTH // LANES)
    qt = [_rope_tile(q_ref[:, LANES * j:LANES * (j + 1)], c, sp, sm) * (HEAD_DIM ** -0.5) for j in tiles]
    for h0 in range(0, N_Q_HEADS, HEADS_PER_GROUP):
        heads = range(h0, h0 + HEADS_PER_GROUP)
        qm = {h: (jnp.where(lo1, qt[h // 2], 0.0) if h % 2 == 0 else jnp.where(lo1, 0.0, qt[h // 2])).astype(BF16) for h in heads}
        sinks = {h: sink_ref[h] for h in heads}
        s2 = {h: _dot_t(qm[h], kk_g[h // group]) for h in heads}
        s = {h: jnp.where(own, s2[h][:, blk:], s2[h][:, :blk] + no_prev) for h in heads}
        m = {h: jnp.maximum(jnp.max(s[h], axis=1, keepdims=True), sinks[h]) for h in heads}
        p = {h: jnp.exp(s[h] - m[h]) for h in heads}
        den = {h: jnp.sum(p[h], axis=1, keepdims=True) + jnp.exp(sinks[h] - m[h]) for h in heads}
        pb = {h: p[h].astype(BF16) for h in heads}
        p2 = {h: jnp.concatenate([jnp.where(own, zero_p, pb[h]), jnp.where(own, pb[h], zero_p)], axis=1) for h in heads}
        o = {h: jnp.dot(p2[h], (va_g if h % 2 == 0 else vb_g)[h // group], preferred_element_type=F32) / den[h]
             for h in heads}
        for j in range(h0 // 2, (h0 + HEADS_PER_GROUP) // 2):
            o_ref[:, LANES * j:LANES * (j + 1)] = (o[2 * j] + o[2 * j + 1]).astype(o_ref.dtype)


def _attn_prompt(p_all, sinks, b, t):
    blk = WINDOW
    nb = t // blk
    c, sp, sm = _rope_tables(jnp.arange(t, dtype=jnp.int32))
    tab_spec = pl.BlockSpec((blk, LANES), lambda bi, i: (i, 0))
    row = lambda bi, i: bi * nb + i
    return pl.pallas_call(
        _attn_prompt_kernel,
        grid=(b, nb),
        in_specs=[pl.BlockSpec((blk, ATTN_WIDTH), lambda bi, i: (row(bi, i), COL_Q // ATTN_WIDTH)),
                  pl.BlockSpec((blk, KV_WIDTH), lambda bi, i: (row(bi, i), COL_K // KV_WIDTH)),
                  pl.BlockSpec((blk, KV_WIDTH), lambda bi, i: (row(bi, i), COL_V // KV_WIDTH)),
                  tab_spec, tab_spec, tab_spec,
                  pl.BlockSpec(memory_space=pltpu.SMEM)],
        out_specs=[pl.BlockSpec((blk, ATTN_WIDTH), lambda bi, i: (row(bi, i), 0)),
                   pl.BlockSpec((1, blk, KV_WIDTH), lambda bi, i: (bi, 0, 0)),
                   pl.BlockSpec((1, blk, KV_WIDTH), lambda bi, i: (bi, 0, 0))],
        out_shape=[jax.ShapeDtypeStruct((b * t, ATTN_WIDTH), BF16),
                   jax.ShapeDtypeStruct((b, blk, KV_WIDTH), F32),
                   jax.ShapeDtypeStruct((b, blk, KV_WIDTH), F32)],
        scratch_shapes=[pltpu.VMEM((blk, KV_WIDTH), F32), pltpu.VMEM((blk, KV_WIDTH), F32)],
        compiler_params=_cparams(2),
        name="attn_prompt",
    )(p_all, p_all, p_all, c, sp, sm, sinks)


def _rwkv_prepare(r_in, k_in, v_in, l_in, pr, pk, pv, plr, prm, wd_ref, wi_ref, wg_ref, bd2):
    mur, muk, muv, mul, w0, a0, k_k, k_a = prm
    r = r_in + (pr - r_in) * mur
    k = k_in + (pk - k_in) * muk
    v = v_in + (pv - v_in) * muv
    ls = l_in + (plr - l_in) * mul
    l0 = ls[:, :LANES]
    lane = lax.broadcasted_iota(jnp.int32, l0.shape, 1)
    z0 = jnp.where(lane < DECAY_LORA, jnp.tanh(l0), l0).astype(BF16)
    zg = _sigmoid(ls[:, LANES:3 * LANES]).astype(BF16)
    dec_up = jnp.dot(z0, wd_ref[...], preferred_element_type=F32)
    icl_up = jnp.dot(z0, wi_ref[...], preferred_element_type=F32)
    gate = jnp.dot(zg, wg_ref[...], preferred_element_type=F32)
    logw = -DECAY_SCALE * _sigmoid(w0 + dec_up)
    a_sig = _sigmoid(a0 + icl_up)
    kk = k * k_k
    kk = kk * lax.rsqrt(jnp.maximum(_head_sum(kk * kk, bd2), NORM_FLOOR_SQ))
    k = k * (1.0 + (a_sig - 1.0) * k_a)
    return r, k, v, logw, -kk, kk * a_sig, gate


def _rwkv_finish(y, r, k, v, gate, r_k, gn_w, gn_b, bd2):
    mu = _head_sum(y, bd2) * (1.0 / HEAD_DIM)
    d = y - mu
    var = _head_sum(d * d, bd2) * (1.0 / HEAD_DIM)
    yn = d * lax.rsqrt(var + GN_EPS) * gn_w + gn_b
    bonus = _head_sum(r * k * r_k, bd2) * v
    return (yn + bonus) * gate


def _rwkv_prompt_kernel(r_ref, k_ref, v_ref, l_ref,
                        mur_ref, muk_ref, muv_ref, mul_ref, w0_ref, a0_ref, kk_ref, ka_ref,
                        rk_ref, gnw_ref, gnb_ref, wd_ref, wi_ref, wg_ref,
                        o_ref, sout_ref,
                        cr_ref, ck_ref, cv_ref, cl_ref, st_ref, y_ref):
    ti = pl.program_id(2)
    tb, lw = r_ref.shape
    n_pair = lw // LANES
    n_chunk = tb // CHUNK

    @pl.when(ti == 0)
    def _():
        for ref in (cr_ref, ck_ref, cv_ref, cl_ref, st_ref):
            ref[...] = jnp.zeros_like(ref)

    def shifted(x, carry_ref):
        rolled = pltpu.roll(x, 1, 0)
        row = lax.broadcasted_iota(jnp.int32, x.shape, 0)
        prev = jnp.where(row == 0, carry_ref[0:1, :], rolled)
        carry_ref[0:1, :] = x[tb - 1:tb, :]
        return prev

    bd2 = _head_block_ones()
    r_in, k_in, v_in, l_in = r_ref[...], k_ref[...], v_ref[...], l_ref[...]
    prm = tuple(ref[...] for ref in (mur_ref, muk_ref, muv_ref, mul_ref, w0_ref, a0_ref, kk_ref, ka_ref))
    r, k, v, logw, a_s, b_s, gate = _rwkv_prepare(
        r_in, k_in, v_in, l_in,
        shifted(r_in, cr_ref), shifted(k_in, ck_ref), shifted(v_in, cv_ref), shifted(l_in, cl_ref),
        prm, wd_ref, wi_ref, wg_ref, bd2)

    c = CHUNK
    ri = lax.broadcasted_iota(jnp.int32, (c, 3 * c), 0)
    ci = lax.broadcasted_iota(jnp.int32, (c, 3 * c), 1) % c
    tri3 = jnp.where(ri >= ci, 1.0, 0.0).astype(BF16)
    r2 = lax.broadcasted_iota(jnp.int32, (2 * c, LANES), 0)
    l2 = lax.broadcasted_iota(jnp.int32, (2 * c, LANES), 1)
    tt, ss = r2 % c, l2 % c
    causal = ss < tt + r2 // c
    lo1 = lax.broadcasted_iota(jnp.int32, (c, LANES), 1) < HEAD_DIM
    diag_blocks = (r2 // HEAD_DIM) == (l2 // HEAD_DIM)
    eye_side = jnp.where(lax.broadcasted_iota(jnp.int32, (c, LANES), 0) == lax.broadcasted_iota(jnp.int32, (c, LANES), 1) % c,
                         1.0, 0.0)

    def stack_heads(x):
        return jnp.concatenate([jnp.where(lo1, x, 0.0), jnp.where(lo1, 0.0, x)], axis=0)

    def dot_t(a, b):
        return lax.dot_general(a, b, (((1,), (1,)), ((), ())), preferred_element_type=F32)

    def dot_tt(a, b):
        return lax.dot_general(a, b, (((0,), (0,)), ((), ())), preferred_element_type=F32)

    def dot(a, b):
        return jnp.dot(a.astype(BF16), b.astype(BF16), preferred_element_type=F32)

    lanes = [slice(LANES * pi, LANES * (pi + 1)) for pi in range(n_pair)]
    units = [(ch, pi) for ch in range(n_chunk) for pi in range(n_pair)]
    un = range(len(units))
    el = []
    for ch in range(n_chunk):
        rows = slice(c * ch, c * (ch + 1))
        lw_c = logw[rows]
        hi = lw_c.astype(BF16)
        rem = lw_c - hi.astype(F32)
        mid = rem.astype(BF16)
        low = (rem - mid.astype(F32)).astype(BF16)
        lcum = jnp.dot(tri3, jnp.concatenate([hi, mid, low], axis=0), preferred_element_type=F32)
        ltot = lcum[c - 1:c, :]
        p_inv = jnp.exp(-lcum)
        p_tail = jnp.exp(ltot - lcum)
        el.append(dict(aq=a_s[rows] * jnp.exp(lcum - lw_c), rq=r[rows] * jnp.exp(lcum), bk=b_s[rows] * p_inv,
                       kq=k[rows] * p_inv, bt=b_s[rows] * p_tail, kt=k[rows] * p_tail, v=v[rows],
                       p_end=jnp.exp(ltot)))
    op = lambda name, u: el[units[u][0]][name][:, lanes[units[u][1]]]

    ar = [jnp.concatenate([op("aq", u), op("rq", u)], axis=0).astype(BF16) for u in un]
    gb = [jnp.where(causal, dot_t(ar[u], stack_heads(op("bk", u)).astype(BF16)), 0.0) for u in un]
    gk = [jnp.where(causal, dot_t(ar[u], stack_heads(op("kq", u)).astype(BF16)), 0.0) for u in un]
    gv = [dot(gk[u], stack_heads(op("v", u))) for u in un]
    pw = [gb[u][:c] for u in un]
    tm = [eye_side + pw[u] for u in un]
    for level in range(1, 6):
        bd = [stack_heads(pw[u]).astype(BF16) for u in un]
        if level == 1:
            pw = [jnp.dot(pw[u].astype(BF16), bd[u], preferred_element_type=F32) for u in un]
            bd = [stack_heads(pw[u]).astype(BF16) for u in un]
        if level < 5:
            both = [jnp.dot(jnp.concatenate([tm[u], pw[u]], axis=0).astype(BF16), bd[u], preferred_element_type=F32)
                    for u in un]
            tm = [tm[u] + both[u][:c] for u in un]
            pw = [both[u][c:] for u in un]
        else:
            tm = [tm[u] + jnp.dot(tm[u].astype(BF16), bd[u], preferred_element_type=F32) for u in un]
    tax = [dot(tm[u], jnp.concatenate([stack_heads(op("aq", u)), stack_heads(gv[u][:c])], axis=1)) for u in un]
    taq = [tax[u][:, :LANES] for u in un]
    txv = [tax[u][:, LANES:] for u in un]
    arx = [dot(gb[u][c:], jnp.concatenate([stack_heads(taq[u]), stack_heads(txv[u])], axis=1)) for u in un]
    mb = [jnp.where(diag_blocks, dot_tt(op("bt", u).astype(BF16), taq[u].astype(BF16)), 0.0).astype(BF16) for u in un]
    cct = [jnp.where(diag_blocks,
                     dot_tt(jnp.concatenate([txv[u], op("v", u)], axis=0).astype(BF16),
                            jnp.concatenate([op("bt", u), op("kt", u)], axis=0).astype(BF16)), 0.0) for u in un]
    rqp = [(op("rq", u) + arx[u][:, :LANES]).astype(BF16) for u in un]
    yc = [gv[u][c:] + arx[u][:, LANES:] for u in un]
    for u, (ch, pi) in enumerate(units):
        s_old = st_ref[pi]
        sb = s_old.astype(BF16)
        y_ref[c * ch:c * (ch + 1), lanes[pi]] = dot_t(rqp[u], sb) + yc[u]
        st_ref[pi] = s_old * op("p_end", u) + dot_t(sb, mb[u]) + cct[u]

    out = _rwkv_finish(y_ref[...], r, k, v, gate, rk_ref[...], gnw_ref[...], gnb_ref[...], bd2)
    o_ref[...] = out.astype(o_ref.dtype)

    @pl.when(ti == pl.num_programs(2) - 1)
    def _():
        for pi in range(n_pair):
            s = st_ref[pi]
            sout_ref[0, 2 * pi] = s[:HEAD_DIM, :HEAD_DIM]
            sout_ref[0, 2 * pi + 1] = s[HEAD_DIM:, HEAD_DIM:]


def _rwkv_lora_weights(w_decay_up, w_iclr_up, w_gate_up):
    z64 = jnp.zeros((DECAY_LORA, RWKV_WIDTH), F32)
    wd = jnp.concatenate([w_decay_up, z64], axis=0).astype(BF16)
    wi = jnp.concatenate([z64, w_iclr_up], axis=0).astype(BF16)
    wg = jnp.concatenate([w_gate_up, jnp.zeros((2 * LANES - GATE_LORA, RWKV_WIDTH), F32)], axis=0).astype(BF16)
    return wd, wi, wg


def _rwkv_prompt(p_all, prm, b, t, tb, lw):
    nt = t // tb
    ns = RWKV_WIDTH // lw
    row = lambda bi, si, ti: bi * nt + ti
    col_spec = lambda col0: pl.BlockSpec(
        (pl.Element(tb), pl.Element(lw)),
        lambda bi, si, ti: (pl.multiple_of(row(bi, si, ti) * tb, tb), pl.multiple_of(col0 + si * lw, LANES)))
    vec = pl.BlockSpec((1, lw), lambda bi, si, ti: (0, si))
    vec_l = pl.BlockSpec((1, LORA_PAD), lambda bi, si, ti: (0, 0))
    return pl.pallas_call(
        _rwkv_prompt_kernel,
        grid=(b, ns, nt),
        in_specs=[col_spec(COL_R), col_spec(COL_KR), col_spec(COL_VR),
                  pl.BlockSpec((tb, LORA_PAD), lambda bi, si, ti: (row(bi, si, ti), COL_LORA // LORA_PAD)),
                  vec, vec, vec, vec_l, vec, vec, vec, vec, vec, vec, vec,
                  pl.BlockSpec((LANES, lw), lambda bi, si, ti: (0, si)),
                  pl.BlockSpec((LANES, lw), lambda bi, si, ti: (0, si)),
                  pl.BlockSpec((2 * LANES, lw), lambda bi, si, ti: (0, si))],
        out_specs=[pl.BlockSpec((tb, lw), lambda bi, si, ti: (row(bi, si, ti), si)),
                   pl.BlockSpec((1, 2 * (lw // LANES), HEAD_DIM, HEAD_DIM), lambda bi, si, ti: (bi, si, 0, 0))],
        out_shape=[jax.ShapeDtypeStruct((b * t, RWKV_WIDTH), BF16),
                   jax.ShapeDtypeStruct((b, N_RWKV_HEADS, HEAD_DIM, HEAD_DIM), F32)],
        scratch_shapes=[pltpu.VMEM((8, lw), F32), pltpu.VMEM((8, lw), F32), pltpu.VMEM((8, lw), F32),
                        pltpu.VMEM((8, LORA_PAD), F32),
                        pltpu.VMEM((lw // LANES, LANES, LANES), F32),
                        pltpu.VMEM((tb, lw), F32)],
        compiler_params=_cparams(3),
        name="rwkv_prompt",
    )(p_all, p_all, p_all, p_all,
      prm["mu_r"], prm["mu_k"], prm["mu_v"], prm["mu_l"], prm["w0"], prm["a0"], prm["k_k"], prm["k_a"],
      prm["r_k"], prm["gn_w"], prm["gn_b"], prm["wd"], prm["wi"], prm["wg"])


def _merge_kernel(x_ref, ao_ref, ro_ref, wga_ref, wgb_ref, wpa_ref, wpr_ref, wo_ref, g_ref, b_ref, h_ref,
                  xb_ref, acc_ref):
    j = pl.program_id(1)

    @pl.when(j == 0)
    def _():
        xb_ref[...] = x_ref[...].astype(BF16)
        acc_ref[...] = jnp.zeros_like(acc_ref)

    xb = xb_ref[...]
    ga = _dot_t(xb, wga_ref[...])
    gb = _dot_t(xb, wgb_ref[...])
    a = jnp.dot(ao_ref[...], wpa_ref[...], preferred_element_type=F32)
    r = jnp.dot(ro_ref[...], wpr_ref[...], preferred_element_type=F32)
    m = _sigmoid(ga) * a + _sigmoid(gb) * r
    acc_ref[...] += jnp.dot(m.astype(BF16), wo_ref[...], preferred_element_type=F32)

    @pl.when(j == pl.num_programs(1) - 1)
    def _():
        z = DEEPNORM_ALPHA * x_ref[...] + acc_ref[...]
        h_ref[...] = _layer_norm_rows(z, g_ref[...], b_ref[...])


def _merge(x, attn_o, rwkv_o, w_in_t, wpa, wpr, wo, ln_g, ln_b, tm, tj):
    m = x.shape[0]
    nj = D_MODEL // tj
    gate_spec = lambda row0: pl.BlockSpec(
        (pl.Element(tj), pl.Element(D_MODEL)),
        lambda i, j: (pl.multiple_of(row0 + j * tj, BF16_ROWS_PER_TILE), 0))
    return pl.pallas_call(
        _merge_kernel,
        grid=(m // tm, nj),
        in_specs=[pl.BlockSpec((tm, D_MODEL), lambda i, j: (i, 0)),
                  pl.BlockSpec((tm, ATTN_WIDTH), lambda i, j: (i, 0)),
                  pl.BlockSpec((tm, RWKV_WIDTH), lambda i, j: (i, 0)),
                  gate_spec(GATE_ROW0), gate_spec(GATE_ROW0 + D_MODEL),
                  pl.BlockSpec((ATTN_WIDTH, tj), lambda i, j: (0, j)),
                  pl.BlockSpec((RWKV_WIDTH, tj), lambda i, j: (0, j)),
                  pl.BlockSpec((tj, D_MODEL), lambda i, j: (j, 0)),
                  pl.BlockSpec((1, D_MODEL), lambda i, j: (0, 0)),
                  pl.BlockSpec((1, D_MODEL), lambda i, j: (0, 0))],
        out_specs=pl.BlockSpec((tm, D_MODEL), lambda i, j: (i, 0)),
        out_shape=jax.ShapeDtypeStruct((m, D_MODEL), F32),
        scratch_shapes=[pltpu.VMEM((tm, D_MODEL), BF16), pltpu.VMEM((tm, D_MODEL), F32)],
        compiler_params=_cparams(2),
        name="merge_ln1",
    )(x, attn_o, rwkv_o, w_in_t, w_in_t, wpa, wpr, wo, ln_g, ln_b)


def _ffn_kernel(h_ref, wu_ref, wd_ref, g_ref, b_ref, y_ref, hb_ref, acc_ref):
    f = pl.program_id(1)

    @pl.when(f == 0)
    def _():
        hb_ref[...] = h_ref[...].astype(BF16)
        acc_ref[...] = jnp.zeros_like(acc_ref)

    u = jnp.dot(hb_ref[...], wu_ref[...], preferred_element_type=F32)
    u = jnp.square(jnp.maximum(u, 0.0))
    acc_ref[...] += jnp.dot(u.astype(BF16), wd_ref[...], preferred_element_type=F32)

    @pl.when(f == pl.num_programs(1) - 1)
    def _():
        z = DEEPNORM_ALPHA * h_ref[...] + acc_ref[...]
        y_ref[...] = _layer_norm_rows(z, g_ref[...], b_ref[...])


def _ffn(h, wu, wd, ln_g, ln_b, tm, tf):
    m = h.shape[0]
    return pl.pallas_call(
        _ffn_kernel,
        grid=(m // tm, D_FF // tf),
        in_specs=[pl.BlockSpec((tm, D_MODEL), lambda i, f: (i, 0)),
                  pl.BlockSpec((D_MODEL, tf), lambda i, f: (0, f)),
                  pl.BlockSpec((tf, D_MODEL), lambda i, f: (f, 0)),
                  pl.BlockSpec((1, D_MODEL), lambda i, f: (0, 0)),
                  pl.BlockSpec((1, D_MODEL), lambda i, f: (0, 0))],
        out_specs=pl.BlockSpec((tm, D_MODEL), lambda i, f: (i, 0)),
        out_shape=jax.ShapeDtypeStruct((m, D_MODEL), F32),
        scratch_shapes=[pltpu.VMEM((tm, D_MODEL), BF16), pltpu.VMEM((tm, D_MODEL), F32)],
        compiler_params=_cparams(2),
        name="ffn_ln2",
    )(h, wu, wd, ln_g, ln_b)


def _attn_sample_kernel(q_ref, kvt_ref, ck_ref, cv_ref, c_ref, sp_ref, sm_ref, cc_ref, spc_ref, smc_ref, sink_ref,
                        o_ref, nk_ref, nv_ref):
    bt = q_ref.shape[0]
    win = ck_ref.shape[2]
    n = kvt_ref.shape[1]
    half = ROT_DIM // 2
    group = N_Q_HEADS // N_KV_HEADS
    c, sp, sm = c_ref[0:1, :], sp_ref[0:1, :], sm_ref[0:1, :]
    sink = sink_ref[:, 0:1]
    row16 = lax.broadcasted_iota(jnp.int32, (N_Q_HEADS, KV_WIDTH), 0)
    lane16 = lax.broadcasted_iota(jnp.int32, (N_Q_HEADS, KV_WIDTH), 1)
    own_kv = (lane16 // HEAD_DIM) == (row16 % N_KV_HEADS)
    urow = row16 // N_KV_HEADS
    pos = lax.broadcasted_iota(jnp.int32, (KV_WIDTH, win), 1)
    seq = lax.broadcasted_iota(jnp.int32, (KV_WIDTH, n), 1)

    kt = kvt_ref[0:KV_WIDTH, :]
    kt = kt * cc_ref[...] + pltpu.roll(kt, KV_WIDTH - half, 0) * spc_ref[...] + pltpu.roll(kt, half, 0) * smc_ref[...]
    vt = kvt_ref[KV_WIDTH:2 * KV_WIDTH, :]

    def rope_row(x):
        return jnp.concatenate([_rope_tile(x[:, LANES * t:LANES * (t + 1)], c, sp, sm)
                                for t in range(x.shape[1] // LANES)], axis=1)

    rng = range(bt)
    me = [seq == pl.program_id(0) * bt + b for b in rng]
    k_col = [jnp.sum(jnp.where(me[b], kt, 0.0), axis=1, keepdims=True) for b in rng]
    v_col = [jnp.sum(jnp.where(me[b], vt, 0.0), axis=1, keepdims=True) for b in rng]
    nk = [jnp.where(pos == win - 1, k_col[b], pltpu.roll(ck_ref[b], win - 1, 1)) for b in rng]
    nv = [jnp.where(pos == win - 1, v_col[b], pltpu.roll(cv_ref[b], win - 1, 1)) for b in rng]
    for b in rng:
        nk_ref[b] = nk[b]
        nv_ref[b] = nv[b]
    qmat = []
    for b in rng:
        q = rope_row(q_ref[b:b + 1, :]) * (HEAD_DIM ** -0.5)
        qb = [jnp.broadcast_to(q[:, KV_WIDTH * u:KV_WIDTH * (u + 1)], (N_Q_HEADS, KV_WIDTH)) for u in range(group)]
        qsel = jnp.where(urow == 0, qb[0], jnp.where(urow == 1, qb[1], jnp.where(urow == 2, qb[2], qb[3])))
        qmat.append(jnp.where(own_kv, qsel, 0.0).astype(BF16))
    s = [jnp.dot(qmat[b], nk[b].astype(BF16), preferred_element_type=F32) for b in rng]
    m = [jnp.maximum(jnp.max(s[b], axis=1, keepdims=True), sink) for b in rng]
    p = [jnp.exp(s[b] - m[b]) for b in rng]
    den = [jnp.sum(p[b], axis=1, keepdims=True) + jnp.exp(sink - m[b]) for b in rng]
    o = [_dot_t(p[b].astype(BF16), nv[b].astype(BF16)) / den[b] for b in rng]
    out_rows = []
    for b in rng:
        ob = jnp.where(own_kv, o[b], 0.0)
        chunks = [jnp.sum(jnp.where(urow == u, ob, 0.0), axis=0, keepdims=True) for u in range(group)]
        out_rows.append(jnp.concatenate(chunks, axis=1))
    o_ref[...] = jnp.concatenate(out_rows, axis=0).astype(o_ref.dtype)


def _attn_sample(q_perm, kv_new_t, cache_kt, cache_vt, sink_mat, bt):
    n, win = cache_kt.shape[0], cache_kt.shape[2]
    tabs = _rope_tables(jnp.full((1,), PAST_LEN, jnp.int32))
    c, sp, sm = (jnp.broadcast_to(z, (8, LANES)) for z in tabs)
    cc, spc, smc = (jnp.broadcast_to(jnp.tile(z, (1, KV_WIDTH // LANES)).T, (KV_WIDTH, n)) for z in tabs)
    small = lambda shape: pl.BlockSpec(shape, lambda i: (0, 0))
    cache_spec = pl.BlockSpec((bt, KV_WIDTH, win), lambda i: (i, 0, 0))
    return pl.pallas_call(
        _attn_sample_kernel,
        grid=(n // bt,),
        in_specs=[pl.BlockSpec((bt, ATTN_WIDTH), lambda i: (i, 0)),
                  small((2 * KV_WIDTH, n)),
                  cache_spec, cache_spec,
                  small((8, LANES)), small((8, LANES)), small((8, LANES)),
                  small((KV_WIDTH, n)), small((KV_WIDTH, n)), small((KV_WIDTH, n)),
                  small((N_Q_HEADS, LANES))],
        out_specs=[pl.BlockSpec((bt, ATTN_WIDTH), lambda i: (i, 0)), cache_spec, cache_spec],
        out_shape=[jax.ShapeDtypeStruct((n, ATTN_WIDTH), BF16),
                   jax.ShapeDtypeStruct(cache_kt.shape, F32),
                   jax.ShapeDtypeStruct(cache_vt.shape, F32)],
        compiler_params=_cparams(1),
        name="attn_sample",
    )(q_perm, kv_new_t, cache_kt, cache_vt, c, sp, sm, cc, spc, smc, sink_mat)


def _rwkv_sample_kernel(r_ref, k_ref, v_ref, l_ref, pr_ref, pk_ref, pv_ref, pl_ref, st_ref,
                        mur_ref, muk_ref, muv_ref, mul_ref, w0_ref, a0_ref, kk_ref, ka_ref,
                        rk_ref, gnw_ref, gnb_ref, wd_ref, wi_ref, wg_ref,
                        o_ref, ns_ref, vec_s, keep_s, y_s):
    h = pl.program_id(0)
    hd = HEAD_DIM
    q_a, q_w, q_b, q_k, q_r, q_v = range(6)

    @pl.when(h == 0)
    def _():
        bd2 = _head_block_ones()
        prm = tuple(ref[...] for ref in (mur_ref, muk_ref, muv_ref, mul_ref, w0_ref, a0_ref, kk_ref, ka_ref))
        r, k, v, logw, a_s, b_s, gate = _rwkv_prepare(
            r_ref[...], k_ref[...], v_ref[...], l_ref[...], pr_ref[...], pk_ref[...], pv_ref[...], pl_ref[...],
            prm, wd_ref, wi_ref, wg_ref, bd2)
        for qi, x in enumerate((a_s, jnp.exp(logw), b_s, k, r, v)):
            xt = x.T
            for hh in range(N_RWKV_HEADS):
                vec_s[qi, hh] = xt[hd * hh:hd * (hh + 1), :]
        for qi, x in enumerate((r, k, v, gate)):
            keep_s[qi] = x

    a_h, w_h, b_h, k_h, r_h = (vec_s[qi, h] for qi in (q_a, q_w, q_b, q_k, q_r))
    for i in range(hd):
        s = st_ref[0, i]
        sa = jnp.sum(s * a_h, axis=0, keepdims=True)
        s_new = s * w_h + sa * b_h + vec_s[q_v, h, i:i + 1, :] * k_h
        ns_ref[0, i] = s_new
        y_s[h, i:i + 1, :] = jnp.sum(s_new * r_h, axis=0, keepdims=True)

    @pl.when(h == pl.num_programs(0) - 1)
    def _():
        y = jnp.concatenate([y_s[hh] for hh in range(N_RWKV_HEADS)], axis=0).T
        out = _rwkv_finish(y, keep_s[0], keep_s[1], keep_s[2], keep_s[3], rk_ref[...], gnw_ref[...], gnb_ref[...],
                           _head_block_ones())
        o_ref[...] = out.astype(o_ref.dtype)


def _rwkv_sample(p_all, shift, shift_l, state_t, prm):
    n = state_t.shape[-1]
    wide = lambda col0: pl.BlockSpec((n, RWKV_WIDTH), lambda h: (0, col0 // RWKV_WIDTH))
    proj = lambda col0: pl.BlockSpec((pl.Element(n), pl.Element(RWKV_WIDTH)), lambda h: (0, pl.multiple_of(col0, LANES)))
    vec = pl.BlockSpec((1, RWKV_WIDTH), lambda h: (0, 0))
    vec_l = pl.BlockSpec((1, LORA_PAD), lambda h: (0, 0))
    st_spec = pl.BlockSpec((1, HEAD_DIM, HEAD_DIM, n), lambda h: (h, 0, 0, 0))
    return pl.pallas_call(
        _rwkv_sample_kernel,
        grid=(N_RWKV_HEADS,),
        in_specs=[proj(COL_R), proj(COL_KR), proj(COL_VR),
                  pl.BlockSpec((n, LORA_PAD), lambda h: (0, COL_LORA // LORA_PAD)),
                  wide(0), wide(RWKV_WIDTH), wide(2 * RWKV_WIDTH),
                  pl.BlockSpec((n, LORA_PAD), lambda h: (0, 0)),
                  st_spec,
                  vec, vec, vec, vec_l, vec, vec, vec, vec, vec, vec, vec,
                  pl.BlockSpec((LANES, RWKV_WIDTH), lambda h: (0, 0)),
                  pl.BlockSpec((LANES, RWKV_WIDTH), lambda h: (0, 0)),
                  pl.BlockSpec((2 * LANES, RWKV_WIDTH), lambda h: (0, 0))],
        out_specs=[pl.BlockSpec((n, RWKV_WIDTH), lambda h: (0, 0)), st_spec],
        out_shape=[jax.ShapeDtypeStruct((n, RWKV_WIDTH), BF16), jax.ShapeDtypeStruct(state_t.shape, F32)],
        scratch_shapes=[pltpu.VMEM((6, N_RWKV_HEADS, HEAD_DIM, n), F32),
                        pltpu.VMEM((4, n, RWKV_WIDTH), F32),
                        pltpu.VMEM((N_RWKV_HEADS, HEAD_DIM, n), F32)],
        compiler_params=_cparams(1),
        name="rwkv_sample",
    )(p_all, p_all, p_all, p_all, shift, shift, shift, shift_l, state_t,
      prm["mu_r"], prm["mu_k"], prm["mu_v"], prm["mu_l"], prm["w0"], prm["a0"], prm["k_k"], prm["k_a"],
      prm["r_k"], prm["gn_w"], prm["gn_b"], prm["wd"], prm["wi"], prm["wg"])


def _swap_head_order(z, outer, inner):
    n = z.shape[0]
    return z.reshape(n, outer, inner, HEAD_DIM).transpose(0, 2, 1, 3).reshape(n, outer * inner * HEAD_DIM)


def _shift_columns(p_rows):
    return p_rows[:, COL_R:COL_R + RWKV_PROJ_WIDTH]


def _forward(x_prompt, x_sample, cache_k_win, cache_v_win, state_shift, state_wkv, w, cfg):
    b, t, _ = x_prompt.shape
    n_s = x_sample.shape[0]
    row = lambda z: z.reshape(1, -1).astype(F32)
    mu = w["mu_shift"]
    wd, wi, wg = _rwkv_lora_weights(w["w_decay_up"], w["w_iclr_up"], w["w_gate_up"])
    prm = dict(
        mu_r=row(mu[:RWKV_WIDTH]), mu_k=row(mu[RWKV_WIDTH:2 * RWKV_WIDTH]), mu_v=row(mu[2 * RWKV_WIDTH:3 * RWKV_WIDTH]),
        mu_l=row(jnp.pad(mu[3 * RWKV_WIDTH:], (0, LORA_PAD - LORA_WIDTH))),
        w0=row(w["w0"]), a0=row(w["a0"]), k_k=row(w["k_k"]), k_a=row(w["k_a"]), r_k=row(w["r_k"]),
        gn_w=row(w["gn_w"]), gn_b=row(w["gn_b"]), wd=wd, wi=wi, wg=wg)
    wpa = w["w_proj_attn"].astype(BF16)
    wpr = w["w_proj_rwkv"].astype(BF16)
    wo = w["w_out"].astype(BF16)
    wu = w["w_up"].astype(BF16)
    wdn = w["w_down"].astype(BF16)
    ln1g, ln1b, ln2g, ln2b = row(w["ln1_g"]), row(w["ln1_b"]), row(w["ln2_g"]), row(w["ln2_b"])
    sinks = w["attn_sinks"].astype(F32)

    xp = x_prompt.reshape(b * t, D_MODEL)
    w_in_t = w["w_in"].T.astype(BF16)
    pp = _inproj(xp, w_in_t, cfg["tm_in"], cfg["tn_in"])
    attn_p, kwin_p, vwin_p = _attn_prompt(pp, sinks, b, t)
    rwkv_p, wkv_p = _rwkv_prompt(pp, prm, b, t, cfg["tb_rwkv"], cfg["lw_rwkv"])
    hp = _merge(xp, attn_p, rwkv_p, w_in_t, wpa, wpr, wo, ln1g, ln1b, cfg["tm_merge"], cfg["tj_merge"])
    yp = _ffn(hp, wu, wdn, ln2g, ln2b, cfg["tm_ffn"], cfg["tf_ffn"])
    shift_p = _shift_columns(pp.reshape(b, t, PACK_WIDTH)[:, t - 1])

    group = N_Q_HEADS // N_KV_HEADS
    xs = x_sample.reshape(n_s, D_MODEL)
    ps = _inproj(xs, w_in_t, n_s, cfg["tn_in"])
    q_perm = _swap_head_order(ps[:, COL_Q:COL_Q + ATTN_WIDTH], N_KV_HEADS, group)
    sink_mat = jnp.broadcast_to(sinks.reshape(N_KV_HEADS, group).T.reshape(N_Q_HEADS, 1), (N_Q_HEADS, LANES))
    win = cache_k_win.shape[1]
    to_t = lambda z: jnp.transpose(z, (0, 2, 3, 1)).reshape(n_s, KV_WIDTH, win)
    from_t = lambda z: jnp.transpose(z.reshape(n_s, N_KV_HEADS, HEAD_DIM, win), (0, 3, 1, 2))[None]
    attn_s, nk_t, nv_t = _attn_sample(q_perm, ps[:, COL_K:COL_K + 2 * KV_WIDTH].T, to_t(cache_k_win), to_t(cache_v_win),
                                      sink_mat, cfg["bt_sample"])
    attn_s = _swap_head_order(attn_s, group, N_KV_HEADS)
    shift_l = jnp.pad(state_shift[:, 3 * RWKV_WIDTH:], ((0, 0), (0, LORA_PAD - LORA_WIDTH)))
    rwkv_s, wkv_t = _rwkv_sample(ps, state_shift, shift_l, jnp.transpose(state_wkv, (1, 2, 3, 0)), prm)
    wkv_s = jnp.transpose(wkv_t, (3, 0, 1, 2))
    hs = _merge(xs, attn_s, rwkv_s, w_in_t, wpa, wpr, wo, ln1g, ln1b, n_s, cfg["tj_merge"])
    ys = _ffn(hs, wu, wdn, ln2g, ln2b, n_s, cfg["tf_ffn"])
    shift_s = _shift_columns(ps)

    kv5 = lambda z: z.reshape(1, z.shape[0], z.shape[1], N_KV_HEADS, HEAD_DIM)
    return (yp.reshape(b, t, D_MODEL), ys.reshape(n_s, 1, D_MODEL),
            kv5(kwin_p), kv5(vwin_p), shift_p[None], wkv_p[None],
            from_t(nk_t), from_t(nv_t), shift_s[None], wkv_s[None])


_CFG = dict(tm_in=1024, tn_in=1024, tb_rwkv=256, lw_rwkv=1024, tm_merge=512, tj_merge=512, tm_ffn=512, tf_ffn=1024, bt_sample=16)


def kernel(x_prompt, x_sample, cache_k_win, cache_v_win, state_shift, state_wkv, w_in, attn_sinks, mu_shift, w0,
           w_decay_up, a0, w_iclr_up, w_gate_up, k_k, k_a, r_k, gn_w, gn_b, w_proj_attn, w_proj_rwkv, w_out,
           ln1_g, ln1_b, w_up, w_down, ln2_g, ln2_b):
    w = dict(w_in=w_in[0], attn_sinks=attn_sinks[0], mu_shift=mu_shift[0], w0=w0[0], w_decay_up=w_decay_up[0],
             a0=a0[0], w_iclr_up=w_iclr_up[0], w_gate_up=w_gate_up[0], k_k=k_k[0], k_a=k_a[0], r_k=r_k[0],
             gn_w=gn_w[0], gn_b=gn_b[0], w_proj_attn=w_proj_attn[0], w_proj_rwkv=w_proj_rwkv[0], w_out=w_out[0],
             ln1_g=ln1_g[0], ln1_b=ln1_b[0], w_up=w_up[0], w_down=w_down[0], ln2_g=ln2_g[0], ln2_b=ln2_b[0])
    return _forward(x_prompt, x_sample, cache_k_win[0], cache_v_win[0], state_shift[0], state_wkv[0], w, _CFG)
```

```python
import jax
import jax.numpy as jnp
from jax import lax
from jax.experimental import pallas as pl
from jax.experimental.pallas import tpu as pltpu

F32 = jnp.float32
BF16 = jnp.bfloat16

D_MODEL = 2048
HEAD_DIM = 64
N_Q_HEADS = 16
N_KV_HEADS = 4
ATTN_WIDTH = N_Q_HEADS * HEAD_DIM
KV_WIDTH = N_KV_HEADS * HEAD_DIM
WINDOW = 128
ROPE_THETA = 500000.0
ROT_DIM = HEAD_DIM // 4
N_RWKV_HEADS = 16
RWKV_WIDTH = N_RWKV_HEADS * HEAD_DIM
DECAY_LORA = 64
ICLR_LORA = 64
GATE_LORA = 160
LORA_WIDTH = DECAY_LORA + ICLR_LORA + GATE_LORA
RWKV_PROJ_WIDTH = 3 * RWKV_WIDTH + LORA_WIDTH
D_FF = 4 * D_MODEL
PAST_LEN = 16384
DEEPNORM_ALPHA = 2.0 ** 0.25
LN_EPS = 1e-5
GN_EPS = HEAD_DIM * 1e-5
NEG_BIG = -1e30
DECAY_SCALE = 0.6065306597126334
NORM_FLOOR_SQ = 1e-24

LANES = 128
BF16_ROWS_PER_TILE = 16
HEADS_PER_GROUP = 8
CHUNK = 64
VMEM_LIMIT = 56 * 1024 * 1024

COL_Q = 0
COL_K = 1024
COL_V = 1280
COL_R = 1536
COL_KR = 2560
COL_VR = 3584
COL_LORA = 4608
LORA_PAD = 512
PACK_WIDTH = COL_LORA + LORA_PAD
GATE_ROW0 = COL_LORA + LORA_WIDTH


def _cparams(n_axes):
    return pltpu.CompilerParams(dimension_semantics=("arbitrary",) * n_axes, vmem_limit_bytes=VMEM_LIMIT)


def _sigmoid(x):
    return 1.0 / (1.0 + jnp.exp(-x))


def _layer_norm_rows(z, g, b):
    mu = jnp.mean(z, axis=-1, keepdims=True)
    d = z - mu
    var = jnp.mean(d * d, axis=-1, keepdims=True)
    return d * lax.rsqrt(var + LN_EPS) * g + b


def _dot_t(a, b):
    return lax.dot_general(a, b, (((1,), (1,)), ((), ())), preferred_element_type=F32)


def _head_block_ones():
    r = lax.broadcasted_iota(jnp.int32, (LANES, LANES), 0)
    c = lax.broadcasted_iota(jnp.int32, (LANES, LANES), 1)
    return jnp.where(r // HEAD_DIM == c // HEAD_DIM, 1.0, 0.0).astype(BF16)


def _head_sum(x, bd):
    outs = [jnp.dot(x[:, LANES * t:LANES * (t + 1)].astype(BF16), bd, preferred_element_type=F32)
            for t in range(x.shape[1] // LANES)]
    return outs[0] if len(outs) == 1 else jnp.concatenate(outs, axis=1)


def _inproj_kernel(x_ref, w_ref, o_ref, xb_ref):
    @pl.when(pl.program_id(1) == 0)
    def _():
        xb_ref[...] = x_ref[...].astype(BF16)

    o_ref[...] = _dot_t(xb_ref[...], w_ref[...])


def _inproj(x, w_t, tm, tn):
    m, k = x.shape
    n = PACK_WIDTH
    return pl.pallas_call(
        _inproj_kernel,
        grid=(m // tm, n // tn),
        in_specs=[pl.BlockSpec((tm, k), lambda i, j: (i, 0)),
                  pl.BlockSpec((tn, k), lambda i, j: (j, 0))],
        out_specs=pl.BlockSpec((tm, tn), lambda i, j: (i, j)),
        out_shape=jax.ShapeDtypeStruct((m, n), F32),
        scratch_shapes=[pltpu.VMEM((tm, k), BF16)],
        compiler_params=_cparams(2),
        name="inproj",
    )(x, w_t)


def _rope_tables(pos):
    inv = ROPE_THETA ** (-jnp.arange(0, ROT_DIM, 2, dtype=F32) / ROT_DIM)
    ang = pos.astype(F32)[:, None] * inv[None, :]
    cos, sin = jnp.cos(ang), jnp.sin(ang)
    t = pos.shape[0]
    half = ROT_DIM // 2
    pad = HEAD_DIM - ROT_DIM
    c_head = jnp.concatenate([cos, cos, jnp.ones((t, pad), F32)], axis=1)
    sp_head = jnp.concatenate([-sin, jnp.zeros((t, half + pad), F32)], axis=1)
    sm_head = jnp.concatenate([jnp.zeros((t, half), F32), sin, jnp.zeros((t, pad), F32)], axis=1)
    rep = LANES // HEAD_DIM
    return tuple(jnp.tile(z, (1, rep)) for z in (c_head, sp_head, sm_head))


def _rope_tile(x, c, sp, sm):
    half = ROT_DIM // 2
    return x * c + pltpu.roll(x, LANES - half, 1) * sp + pltpu.roll(x, half, 1) * sm


def _attn_prompt_kernel(q_ref, k_ref, v_ref, c_ref, sp_ref, sm_ref, sink_ref,
                        o_ref, kwin_ref, vwin_ref, kp_ref, vp_ref):
    i = pl.program_id(1)
    blk = WINDOW

    @pl.when(i == 0)
    def _():
        kp_ref[...] = jnp.zeros_like(kp_ref)
        vp_ref[...] = jnp.zeros_like(vp_ref)

    c, sp, sm = c_ref[...], sp_ref[...], sm_ref[...]
    lane = lax.broadcasted_iota(jnp.int32, (2 * blk, LANES), 1)
    lo2 = lane < HEAD_DIM
    lo1 = lax.broadcasted_iota(jnp.int32, (blk, LANES), 1) < HEAD_DIM

    own = lax.broadcasted_iota(jnp.int32, (blk, blk), 1) <= lax.broadcasted_iota(jnp.int32, (blk, blk), 0)
    no_prev = jnp.where(i == 0, NEG_BIG, 0.0)
    zero_p = jnp.zeros((blk, blk), BF16)

    n_kv_tiles = KV_WIDTH // LANES
    kk_g, va_g, vb_g = [], [], []
    for t in range(n_kv_tiles):
        sl = slice(LANES * t, LANES * (t + 1))
        kcur = _rope_tile(k_ref[:, sl], c, sp, sm)
        vcur = v_ref[:, sl]
        kwin_ref[0, :, sl] = kcur
        vwin_ref[0, :, sl] = vcur
        kall = jnp.concatenate([kp_ref[:, sl], kcur], axis=0)
        vall = jnp.concatenate([vp_ref[:, sl], vcur], axis=0)
        kp_ref[:, sl] = kcur
        vp_ref[:, sl] = vcur
        kswap = pltpu.roll(kall, HEAD_DIM, 1)
        vswap = pltpu.roll(vall, HEAD_DIM, 1)
        kk_g += [jnp.where(lo2, kall, kswap).astype(BF16), jnp.where(lo2, kswap, kall).astype(BF16)]
        va_g += [jnp.where(lo2, vall, 0.0).astype(BF16), jnp.where(lo2, vswap, 0.0).astype(BF16)]
        vb_g += [jnp.where(lo2, 0.0, vswap).astype(BF16), jnp.where(lo2, 0.0, vall).astype(BF16)]

    group = N_Q_HEADS // N_KV_HEADS
    tiles = range(ATTN_WIDTH // LANES)
    half = ROT_DIM // 2
    pr = lax.broadcasted_iota(jnp.int32, (LANES, LANES), 0)
    pc = lax.broadcasted_iota(jnp.int32, (LANES, LANES), 1)
    dcol = pc % HEAD_DIM
    partner = jnp.where(dcol < half, pc + half, jnp.where(dcol < ROT_DIM, pc - half, -1))
    perm = jnp.where(pr == partner, 1.0, 0.0).astype(BF16)
    ssum = sp + sm
    qt = []
    for j in tiles:
        xq = q_ref[:, LANES * j:LANES * (j + 1)]
        xp = jnp.dot(xq.astype(BF16), perm, preferred_element_type=F32)
        qt.append((xq * c + xp * ssum) * (HEAD_DIM ** -0.5))
    for h0 in range(0, N_Q_HEADS, HEADS_PER_GROUP):
        heads = range(h0, h0 + HEADS_PER_GROUP)
        qm = {h: (jnp.where(lo1, qt[h // 2], 0.0) if h % 2 == 0 else jnp.where(lo1, 0.0, qt[h // 2])).astype(BF16) for h in heads}
        sinks = {h: sink_ref[h] for h in heads}
        s2 = {h: _dot_t(qm[h], kk_g[h // group]) for h in heads}
        s = {h: jnp.where(own, s2[h][:, blk:], s2[h][:, :blk] + no_prev) for h in heads}
        m = {h: jnp.maximum(jnp.max(s[h], axis=1, keepdims=True), sinks[h]) for h in heads}
        p = {h: jnp.exp(s[h] - m[h]) for h in heads}
        den = {h: jnp.sum(p[h], axis=1, keepdims=True) + jnp.exp(sinks[h] - m[h]) for h in heads}
        pb = {h: p[h].astype(BF16) for h in heads}
        p2 = {h: jnp.concatenate([jnp.where(own, zero_p, pb[h]), jnp.where(own, pb[h], zero_p)], axis=1) for h in heads}
        o = {h: jnp.dot(p2[h], (va_g if h % 2 == 0 else vb_g)[h // group], preferred_element_type=F32) / den[h]
             for h in heads}
        for j in range(h0 // 2, (h0 + HEADS_PER_GROUP) // 2):
            o_ref[:, LANES * j:LANES * (j + 1)] = (o[2 * j] + o[2 * j + 1]).astype(o_ref.dtype)


def _attn_prompt(p_all, sinks, b, t):
    blk = WINDOW
    nb = t // blk
    c, sp, sm = _rope_tables(jnp.arange(t, dtype=jnp.int32))
    tab_spec = pl.BlockSpec((blk, LANES), lambda bi, i: (i, 0))
    row = lambda bi, i: bi * nb + i
    return pl.pallas_call(
        _attn_prompt_kernel,
        grid=(b, nb),
        in_specs=[pl.BlockSpec((blk, ATTN_WIDTH), lambda bi, i: (row(bi, i), COL_Q // ATTN_WIDTH)),
                  pl.BlockSpec((blk, KV_WIDTH), lambda bi, i: (row(bi, i), COL_K // KV_WIDTH)),
                  pl.BlockSpec((blk, KV_WIDTH), lambda bi, i: (row(bi, i), COL_V // KV_WIDTH)),
                  tab_spec, tab_spec, tab_spec,
                  pl.BlockSpec(memory_space=pltpu.SMEM)],
        out_specs=[pl.BlockSpec((blk, ATTN_WIDTH), lambda bi, i: (row(bi, i), 0)),
                   pl.BlockSpec((1, blk, KV_WIDTH), lambda bi, i: (bi, 0, 0)),
                   pl.BlockSpec((1, blk, KV_WIDTH), lambda bi, i: (bi, 0, 0))],
        out_shape=[jax.ShapeDtypeStruct((b * t, ATTN_WIDTH), BF16),
                   jax.ShapeDtypeStruct((b, blk, KV_WIDTH), F32),
                   jax.ShapeDtypeStruct((b, blk, KV_WIDTH), F32)],
        scratch_shapes=[pltpu.VMEM((blk, KV_WIDTH), F32), pltpu.VMEM((blk, KV_WIDTH), F32)],
        compiler_params=_cparams(2),
        name="attn_prompt",
    )(p_all, p_all, p_all, c, sp, sm, sinks)


def _rwkv_prepare(r_in, k_in, v_in, l_in, pr, pk, pv, plr, prm, wd_ref, wi_ref, wg_ref, bd2):
    mur, muk, muv, mul, w0, a0, k_k, k_a = prm
    r = r_in + (pr - r_in) * mur
    k = k_in + (pk - k_in) * muk
    v = v_in + (pv - v_in) * muv
    ls = l_in + (plr - l_in) * mul
    l0 = ls[:, :LANES]
    lane = lax.broadcasted_iota(jnp.int32, l0.shape, 1)
    z0 = jnp.where(lane < DECAY_LORA, jnp.tanh(l0), l0).astype(BF16)
    zg = _sigmoid(ls[:, LANES:3 * LANES]).astype(BF16)
    dec_up = jnp.dot(z0, wd_ref[...], preferred_element_type=F32)
    icl_up = jnp.dot(z0, wi_ref[...], preferred_element_type=F32)
    gate = jnp.dot(zg, wg_ref[...], preferred_element_type=F32)
    logw = -DECAY_SCALE * _sigmoid(w0 + dec_up)
    a_sig = _sigmoid(a0 + icl_up)
    kk = k * k_k
    kk = kk * lax.rsqrt(jnp.maximum(_head_sum(kk * kk, bd2), NORM_FLOOR_SQ))
    k = k * (1.0 + (a_sig - 1.0) * k_a)
    return r, k, v, logw, -kk, kk * a_sig, gate


def _rwkv_finish(y, r, k, v, gate, r_k, gn_w, gn_b, bd2):
    mu = _head_sum(y, bd2) * (1.0 / HEAD_DIM)
    d = y - mu
    var = _head_sum(d * d, bd2) * (1.0 / HEAD_DIM)
    yn = d * lax.rsqrt(var + GN_EPS) * gn_w + gn_b
    bonus = _head_sum(r * k * r_k, bd2) * v
    return (yn + bonus) * gate


def _rwkv_prompt_kernel(r_ref, k_ref, v_ref, l_ref,
                        mur_ref, muk_ref, muv_ref, mul_ref, w0_ref, a0_ref, kk_ref, ka_ref,
                        rk_ref, gnw_ref, gnb_ref, wd_ref, wi_ref, wg_ref,
                        o_ref, sout_ref,
                        cr_ref, ck_ref, cv_ref, cl_ref, st_ref, y_ref):
    ti = pl.program_id(2)
    tb, lw = r_ref.shape
    n_pair = lw // LANES
    n_chunk = tb // CHUNK

    @pl.when(ti == 0)
    def _():
        for ref in (cr_ref, ck_ref, cv_ref, cl_ref, st_ref):
            ref[...] = jnp.zeros_like(ref)

    def shifted(x, carry_ref):
        rolled = pltpu.roll(x, 1, 0)
        row = lax.broadcasted_iota(jnp.int32, x.shape, 0)
        prev = jnp.where(row == 0, carry_ref[0:1, :], rolled)
        carry_ref[0:1, :] = x[tb - 1:tb, :]
        return prev

    bd2 = _head_block_ones()
    r_in, k_in, v_in, l_in = r_ref[...], k_ref[...], v_ref[...], l_ref[...]
    prm = tuple(ref[...] for ref in (mur_ref, muk_ref, muv_ref, mul_ref, w0_ref, a0_ref, kk_ref, ka_ref))
    r, k, v, logw, a_s, b_s, gate = _rwkv_prepare(
        r_in, k_in, v_in, l_in,
        shifted(r_in, cr_ref), shifted(k_in, ck_ref), shifted(v_in, cv_ref), shifted(l_in, cl_ref),
        prm, wd_ref, wi_ref, wg_ref, bd2)

    c = CHUNK
    ri = lax.broadcasted_iota(jnp.int32, (c, 3 * c), 0)
    ci = lax.broadcasted_iota(jnp.int32, (c, 3 * c), 1) % c
    tri3 = jnp.where(ri >= ci, 1.0, 0.0).astype(BF16)
    r2 = lax.broadcasted_iota(jnp.int32, (2 * c, LANES), 0)
    l2 = lax.broadcasted_iota(jnp.int32, (2 * c, LANES), 1)
    tt, ss = r2 % c, l2 % c
    causal = ss < tt + r2 // c
    lo1 = lax.broadcasted_iota(jnp.int32, (c, LANES), 1) < HEAD_DIM
    diag_blocks = (r2 // HEAD_DIM) == (l2 // HEAD_DIM)
    eye_side = jnp.where(lax.broadcasted_iota(jnp.int32, (c, LANES), 0) == lax.broadcasted_iota(jnp.int32, (c, LANES), 1) % c,
                         1.0, 0.0)

    def stack_heads(x):
        return jnp.concatenate([jnp.where(lo1, x, 0.0), jnp.where(lo1, 0.0, x)], axis=0)

    def dot_t(a, b):
        return lax.dot_general(a, b, (((1,), (1,)), ((), ())), preferred_element_type=F32)

    def dot_tt(a, b):
        return lax.dot_general(a, b, (((0,), (0,)), ((), ())), preferred_element_type=F32)

    def dot(a, b):
        return jnp.dot(a.astype(BF16), b.astype(BF16), preferred_element_type=F32)

    lanes = [slice(LANES * pi, LANES * (pi + 1)) for pi in range(n_pair)]
    units = [(ch, pi) for ch in range(n_chunk) for pi in range(n_pair)]
    un = range(len(units))
    el = []
    for ch in range(n_chunk):
        rows = slice(c * ch, c * (ch + 1))
        lw_c = logw[rows]
        hi = lw_c.astype(BF16)
        rem = lw_c - hi.astype(F32)
        mid = rem.astype(BF16)
        low = (rem - mid.astype(F32)).astype(BF16)
        lcum = jnp.dot(tri3, jnp.concatenate([hi, mid, low], axis=0), preferred_element_type=F32)
        ltot = lcum[c - 1:c, :]
        p_inv = jnp.exp(-lcum)
        p_tail = jnp.exp(ltot - lcum)
        el.append(dict(aq=a_s[rows] * jnp.exp(lcum - lw_c), rq=r[rows] * jnp.exp(lcum), bk=b_s[rows] * p_inv,
                       kq=k[rows] * p_inv, bt=b_s[rows] * p_tail, kt=k[rows] * p_tail, v=v[rows],
                       p_end=jnp.exp(ltot)))
    op = lambda name, u: el[units[u][0]][name][:, lanes[units[u][1]]]

    ar = [jnp.concatenate([op("aq", u), op("rq", u)], axis=0).astype(BF16) for u in un]
    gb = [jnp.where(causal, dot_t(ar[u], stack_heads(op("bk", u)).astype(BF16)), 0.0) for u in un]
    gk = [jnp.where(causal, dot_t(ar[u], stack_heads(op("kq", u)).astype(BF16)), 0.0) for u in un]
    gv = [dot(gk[u], stack_heads(op("v", u))) for u in un]
    pw = [gb[u][:c] for u in un]
    tm = [eye_side + pw[u] for u in un]
    for level in range(1, 6):
        bd = [stack_heads(pw[u]).astype(BF16) for u in un]
        if level == 1:
            pw = [jnp.dot(pw[u].astype(BF16), bd[u], preferred_element_type=F32) for u in un]
            bd = [stack_heads(pw[u]).astype(BF16) for u in un]
        if level < 5:
            both = [jnp.dot(jnp.concatenate([tm[u], pw[u]], axis=0).astype(BF16), bd[u], preferred_element_type=F32)
                    for u in un]
            tm = [tm[u] + both[u][:c] for u in un]
            pw = [both[u][c:] for u in un]
        else:
            tm = [tm[u] + jnp.dot(tm[u].astype(BF16), bd[u], preferred_element_type=F32) for u in un]
    tax = [dot(tm[u], jnp.concatenate([stack_heads(op("aq", u)), stack_heads(gv[u][:c])], axis=1)) for u in un]
    taq = [tax[u][:, :LANES] for u in un]
    txv = [tax[u][:, LANES:] for u in un]
    arx = [dot(gb[u][c:], jnp.concatenate([stack_heads(taq[u]), stack_heads(txv[u])], axis=1)) for u in un]
    mb = [jnp.where(diag_blocks, dot_tt(op("bt", u).astype(BF16), taq[u].astype(BF16)), 0.0).astype(BF16) for u in un]
    cct = [jnp.where(diag_blocks,
                     dot_tt(jnp.concatenate([txv[u], op("v", u)], axis=0).astype(BF16),
                            jnp.concatenate([op("bt", u), op("kt", u)], axis=0).astype(BF16)), 0.0) for u in un]
    rqp = [(op("rq", u) + arx[u][:, :LANES]).astype(BF16) for u in un]
    yc = [gv[u][c:] + arx[u][:, LANES:] for u in un]
    for u, (ch, pi) in enumerate(units):
        s_old = st_ref[pi]
        sb = s_old.astype(BF16)
        y_ref[c * ch:c * (ch + 1), lanes[pi]] = dot_t(rqp[u], sb) + yc[u]
        st_ref[pi] = s_old * op("p_end", u) + dot_t(sb, mb[u]) + cct[u]

    out = _rwkv_finish(y_ref[...], r, k, v, gate, rk_ref[...], gnw_ref[...], gnb_ref[...], bd2)
    o_ref[...] = out.astype(o_ref.dtype)

    @pl.when(ti == pl.num_programs(2) - 1)
    def _():
        for pi in range(n_pair):
            s = st_ref[pi]
            sout_ref[0, 2 * pi] = s[:HEAD_DIM, :HEAD_DIM]
            sout_ref[0, 2 * pi + 1] = s[HEAD_DIM:, HEAD_DIM:]


def _rwkv_lora_weights(w_decay_up, w_iclr_up, w_gate_up):
    z64 = jnp.zeros((DECAY_LORA, RWKV_WIDTH), F32)
    wd = jnp.concatenate([w_decay_up, z64], axis=0).astype(BF16)
    wi = jnp.concatenate([z64, w_iclr_up], axis=0).astype(BF16)
    wg = jnp.concatenate([w_gate_up, jnp.zeros((2 * LANES - GATE_LORA, RWKV_WIDTH), F32)], axis=0).astype(BF16)
    return wd, wi, wg


def _rwkv_prompt(p_all, prm, b, t, tb, lw):
    nt = t // tb
    ns = RWKV_WIDTH // lw
    row = lambda bi, si, ti: bi * nt + ti
    col_spec = lambda col0: pl.BlockSpec(
        (pl.Element(tb), pl.Element(lw)),
        lambda bi, si, ti: (pl.multiple_of(row(bi, si, ti) * tb, tb), pl.multiple_of(col0 + si * lw, LANES)))
    vec = pl.BlockSpec((1, lw), lambda bi, si, ti: (0, si))
    vec_l = pl.BlockSpec((1, LORA_PAD), lambda bi, si, ti: (0, 0))
    return pl.pallas_call(
        _rwkv_prompt_kernel,
        grid=(b, ns, nt),
        in_specs=[col_spec(COL_R), col_spec(COL_KR), col_spec(COL_VR),
                  pl.BlockSpec((tb, LORA_PAD), lambda bi, si, ti: (row(bi, si, ti), COL_LORA // LORA_PAD)),
                  vec, vec, vec, vec_l, vec, vec, vec, vec, vec, vec, vec,
                  pl.BlockSpec((LANES, lw), lambda bi, si, ti: (0, si)),
                  pl.BlockSpec((LANES, lw), lambda bi, si, ti: (0, si)),
                  pl.BlockSpec((2 * LANES, lw), lambda bi, si, ti: (0, si))],
        out_specs=[pl.BlockSpec((tb, lw), lambda bi, si, ti: (row(bi, si, ti), si)),
                   pl.BlockSpec((1, 2 * (lw // LANES), HEAD_DIM, HEAD_DIM), lambda bi, si, ti: (bi, si, 0, 0))],
        out_shape=[jax.ShapeDtypeStruct((b * t, RWKV_WIDTH), BF16),
                   jax.ShapeDtypeStruct((b, N_RWKV_HEADS, HEAD_DIM, HEAD_DIM), F32)],
        scratch_shapes=[pltpu.VMEM((8, lw), F32), pltpu.VMEM((8, lw), F32), pltpu.VMEM((8, lw), F32),
                        pltpu.VMEM((8, LORA_PAD), F32),
                        pltpu.VMEM((lw // LANES, LANES, LANES), F32),
                        pltpu.VMEM((tb, lw), F32)],
        compiler_params=_cparams(3),
        name="rwkv_prompt",
    )(p_all, p_all, p_all, p_all,
      prm["mu_r"], prm["mu_k"], prm["mu_v"], prm["mu_l"], prm["w0"], prm["a0"], prm["k_k"], prm["k_a"],
      prm["r_k"], prm["gn_w"], prm["gn_b"], prm["wd"], prm["wi"], prm["wg"])


def _merge_kernel(x_ref, ao_ref, ro_ref, wga_ref, wgb_ref, wpa_ref, wpr_ref, wo_ref, g_ref, b_ref, h_ref,
                  xb_ref, acc_ref):
    j = pl.program_id(1)

    @pl.when(j == 0)
    def _():
        xb_ref[...] = x_ref[...].astype(BF16)
        acc_ref[...] = jnp.zeros_like(acc_ref)

    xb = xb_ref[...]
    ga = _dot_t(xb, wga_ref[...])
    gb = _dot_t(xb, wgb_ref[...])
    a = jnp.dot(ao_ref[...], wpa_ref[...], preferred_element_type=F32)
    r = jnp.dot(ro_ref[...], wpr_ref[...], preferred_element_type=F32)
    m = _sigmoid(ga) * a + _sigmoid(gb) * r
    acc_ref[...] += jnp.dot(m.astype(BF16), wo_ref[...], preferred_element_type=F32)

    @pl.when(j == pl.num_programs(1) - 1)
    def _():
        z = DEEPNORM_ALPHA * x_ref[...] + acc_ref[...]
        h_ref[...] = _layer_norm_rows(z, g_ref[...], b_ref[...])


def _merge(x, attn_o, rwkv_o, w_in_t, wpa, wpr, wo, ln_g, ln_b, tm, tj):
    m = x.shape[0]
    nj = D_MODEL // tj
    gate_spec = lambda row0: pl.BlockSpec(
        (pl.Element(tj), pl.Element(D_MODEL)),
        lambda i, j: (pl.multiple_of(row0 + j * tj, BF16_ROWS_PER_TILE), 0))
    return pl.pallas_call(
        _merge_kernel,
        grid=(m // tm, nj),
        in_specs=[pl.BlockSpec((tm, D_MODEL), lambda i, j: (i, 0)),
                  pl.BlockSpec((tm, ATTN_WIDTH), lambda i, j: (i, 0)),
                  pl.BlockSpec((tm, RWKV_WIDTH), lambda i, j: (i, 0)),
                  gate_spec(GATE_ROW0), gate_spec(GATE_ROW0 + D_MODEL),
                  pl.BlockSpec((ATTN_WIDTH, tj), lambda i, j: (0, j)),
                  pl.BlockSpec((RWKV_WIDTH, tj), lambda i, j: (0, j)),
                  pl.BlockSpec((tj, D_MODEL), lambda i, j: (j, 0)),
                  pl.BlockSpec((1, D_MODEL), lambda i, j: (0, 0)),
                  pl.BlockSpec((1, D_MODEL), lambda i, j: (0, 0))],
        out_specs=pl.BlockSpec((tm, D_MODEL), lambda i, j: (i, 0)),
        out_shape=jax.ShapeDtypeStruct((m, D_MODEL), F32),
        scratch_shapes=[pltpu.VMEM((tm, D_MODEL), BF16), pltpu.VMEM((tm, D_MODEL), F32)],
        compiler_params=_cparams(2),
        name="merge_ln1",
    )(x, attn_o, rwkv_o, w_in_t, w_in_t, wpa, wpr, wo, ln_g, ln_b)


def _ffn_kernel(h_ref, wu_ref, wd_ref, g_ref, b_ref, y_ref, hb_ref, acc_ref):
    f = pl.program_id(1)

    @pl.when(f == 0)
    def _():
        hb_ref[...] = h_ref[...].astype(BF16)
        acc_ref[...] = jnp.zeros_like(acc_ref)

    u = jnp.dot(hb_ref[...], wu_ref[...], preferred_element_type=F32)
    u = jnp.square(jnp.maximum(u, 0.0))
    acc_ref[...] += jnp.dot(u.astype(BF16), wd_ref[...], preferred_element_type=F32)

    @pl.when(f == pl.num_programs(1) - 1)
    def _():
        z = DEEPNORM_ALPHA * h_ref[...] + acc_ref[...]
        y_ref[...] = _layer_norm_rows(z, g_ref[...], b_ref[...])


def _ffn(h, wu, wd, ln_g, ln_b, tm, tf):
    m = h.shape[0]
    return pl.pallas_call(
        _ffn_kernel,
        grid=(m // tm, D_FF // tf),
        in_specs=[pl.BlockSpec((tm, D_MODEL), lambda i, f: (i, 0)),
                  pl.BlockSpec((D_MODEL, tf), lambda i, f: (0, f)),
                  pl.BlockSpec((tf, D_MODEL), lambda i, f: (f, 0)),
                  pl.BlockSpec((1, D_MODEL), lambda i, f: (0, 0)),
                  pl.BlockSpec((1, D_MODEL), lambda i, f: (0, 0))],
        out_specs=pl.BlockSpec((tm, D_MODEL), lambda i, f: (i, 0)),
        out_shape=jax.ShapeDtypeStruct((m, D_MODEL), F32),
        scratch_shapes=[pltpu.VMEM((tm, D_MODEL), BF16), pltpu.VMEM((tm, D_MODEL), F32)],
        compiler_params=_cparams(2),
        name="ffn_ln2",
    )(h, wu, wd, ln_g, ln_b)


def _attn_sample_kernel(q_ref, kvt_ref, ck_ref, cv_ref, c_ref, sp_ref, sm_ref, cc_ref, spc_ref, smc_ref, sink_ref,
                        o_ref, nk_ref, nv_ref):
    bt = q_ref.shape[0]
    win = ck_ref.shape[2]
    n = kvt_ref.shape[1]
    half = ROT_DIM // 2
    group = N_Q_HEADS // N_KV_HEADS
    c, sp, sm = c_ref[0:1, :], sp_ref[0:1, :], sm_ref[0:1, :]
    sink = sink_ref[:, 0:1]
    row16 = lax.broadcasted_iota(jnp.int32, (N_Q_HEADS, KV_WIDTH), 0)
    lane16 = lax.broadcasted_iota(jnp.int32, (N_Q_HEADS, KV_WIDTH), 1)
    own_kv = (lane16 // HEAD_DIM) == (row16 % N_KV_HEADS)
    urow = row16 // N_KV_HEADS
    pos = lax.broadcasted_iota(jnp.int32, (KV_WIDTH, win), 1)
    seq = lax.broadcasted_iota(jnp.int32, (KV_WIDTH, n), 1)

    kt = kvt_ref[0:KV_WIDTH, :]
    kt = kt * cc_ref[...] + pltpu.roll(kt, KV_WIDTH - half, 0) * spc_ref[...] + pltpu.roll(kt, half, 0) * smc_ref[...]
    vt = kvt_ref[KV_WIDTH:2 * KV_WIDTH, :]

    def rope_row(x):
        return jnp.concatenate([_rope_tile(x[:, LANES * t:LANES * (t + 1)], c, sp, sm)
                                for t in range(x.shape[1] // LANES)], axis=1)

    rng = range(bt)
    me = [seq == pl.program_id(0) * bt + b for b in rng]
    k_col = [jnp.sum(jnp.where(me[b], kt, 0.0), axis=1, keepdims=True) for b in rng]
    v_col = [jnp.sum(jnp.where(me[b], vt, 0.0), axis=1, keepdims=True) for b in rng]
    nk = [jnp.where(pos == win - 1, k_col[b], pltpu.roll(ck_ref[b], win - 1, 1)) for b in rng]
    nv = [jnp.where(pos == win - 1, v_col[b], pltpu.roll(cv_ref[b], win - 1, 1)) for b in rng]
    for b in rng:
        nk_ref[b] = nk[b]
        nv_ref[b] = nv[b]
    qmat = []
    for b in rng:
        q = rope_row(q_ref[b:b + 1, :]) * (HEAD_DIM ** -0.5)
        qb = [jnp.broadcast_to(q[:, KV_WIDTH * u:KV_WIDTH * (u + 1)], (N_Q_HEADS, KV_WIDTH)) for u in range(group)]
        qsel = jnp.where(urow == 0, qb[0], jnp.where(urow == 1, qb[1], jnp.where(urow == 2, qb[2], qb[3])))
        qmat.append(jnp.where(own_kv, qsel, 0.0).astype(BF16))
    s = [jnp.dot(qmat[b], nk[b].astype(BF16), preferred_element_type=F32) for b in rng]
    m = [jnp.maximum(jnp.max(s[b], axis=1, keepdims=True), sink) for b in rng]
    p = [jnp.exp(s[b] - m[b]) for b in rng]
    den = [jnp.sum(p[b], axis=1, keepdims=True) + jnp.exp(sink - m[b]) for b in rng]
    o = [_dot_t(p[b].astype(BF16), nv[b].astype(BF16)) / den[b] for b in rng]
    out_rows = []
    for b in rng:
        ob = jnp.where(own_kv, o[b], 0.0)
        chunks = [jnp.sum(jnp.where(urow == u, ob, 0.0), axis=0, keepdims=True) for u in range(group)]
        out_rows.append(jnp.concatenate(chunks, axis=1))
    o_ref[...] = jnp.concatenate(out_rows, axis=0).astype(o_ref.dtype)


def _attn_sample(q_perm, kv_new_t, cache_kt, cache_vt, sink_mat, bt):
    n, win = cache_kt.shape[0], cache_kt.shape[2]
    tabs = _rope_tables(jnp.full((1,), PAST_LEN, jnp.int32))
    c, sp, sm = (jnp.broadcast_to(z, (8, LANES)) for z in tabs)
    cc, spc, smc = (jnp.broadcast_to(jnp.tile(z, (1, KV_WIDTH // LANES)).T, (KV_WIDTH, n)) for z in tabs)
    small = lambda shape: pl.BlockSpec(shape, lambda i: (0, 0))
    cache_spec = pl.BlockSpec((bt, KV_WIDTH, win), lambda i: (i, 0, 0))
    return pl.pallas_call(
        _attn_sample_kernel,
        grid=(n // bt,),
        in_specs=[pl.BlockSpec((bt, ATTN_WIDTH), lambda i: (i, 0)),
                  small((2 * KV_WIDTH, n)),
                  cache_spec, cache_spec,
                  small((8, LANES)), small((8, LANES)), small((8, LANES)),
                  small((KV_WIDTH, n)), small((KV_WIDTH, n)), small((KV_WIDTH, n)),
                  small((N_Q_HEADS, LANES))],
        out_specs=[pl.BlockSpec((bt, ATTN_WIDTH), lambda i: (i, 0)), cache_spec, cache_spec],
        out_shape=[jax.ShapeDtypeStruct((n, ATTN_WIDTH), BF16),
                   jax.ShapeDtypeStruct(cache_kt.shape, F32),
                   jax.ShapeDtypeStruct(cache_vt.shape, F32)],
        compiler_params=_cparams(1),
        name="attn_sample",
    )(q_perm, kv_new_t, cache_kt, cache_vt, c, sp, sm, cc, spc, smc, sink_mat)


def _rwkv_sample_kernel(r_ref, k_ref, v_ref, l_ref, pr_ref, pk_ref, pv_ref, pl_ref, st_ref,
                        mur_ref, muk_ref, muv_ref, mul_ref, w0_ref, a0_ref, kk_ref, ka_ref,
                        rk_ref, gnw_ref, gnb_ref, wd_ref, wi_ref, wg_ref,
                        o_ref, ns_ref, vec_s, keep_s, y_s):
    h = pl.program_id(0)
    hd = HEAD_DIM
    q_a, q_w, q_b, q_k, q_r, q_v = range(6)

    @pl.when(h == 0)
    def _():
        bd2 = _head_block_ones()
        prm = tuple(ref[...] for ref in (mur_ref, muk_ref, muv_ref, mul_ref, w0_ref, a0_ref, kk_ref, ka_ref))
        r, k, v, logw, a_s, b_s, gate = _rwkv_prepare(
            r_ref[...], k_ref[...], v_ref[...], l_ref[...], pr_ref[...], pk_ref[...], pv_ref[...], pl_ref[...],
            prm, wd_ref, wi_ref, wg_ref, bd2)
        for qi, x in enumerate((a_s, jnp.exp(logw), b_s, k, r, v)):
            xt = x.T
            for hh in range(N_RWKV_HEADS):
                vec_s[qi, hh] = xt[hd * hh:hd * (hh + 1), :]
        for qi, x in enumerate((r, k, v, gate)):
            keep_s[qi] = x

    a_h, w_h, b_h, k_h, r_h = (vec_s[qi, h] for qi in (q_a, q_w, q_b, q_k, q_r))
    for i in range(hd):
        s = st_ref[0, i]
        sa = jnp.sum(s * a_h, axis=0, keepdims=True)
        s_new = s * w_h + sa * b_h + vec_s[q_v, h, i:i + 1, :] * k_h
        ns_ref[0, i] = s_new
        y_s[h, i:i + 1, :] = jnp.sum(s_new * r_h, axis=0, keepdims=True)

    @pl.when(h == pl.num_programs(0) - 1)
    def _():
        y = jnp.concatenate([y_s[hh] for hh in range(N_RWKV_HEADS)], axis=0).T
        out = _rwkv_finish(y, keep_s[0], keep_s[1], keep_s[2], keep_s[3], rk_ref[...], gnw_ref[...], gnb_ref[...],
                           _head_block_ones())
        o_ref[...] = out.astype(o_ref.dtype)


def _rwkv_sample(p_all, shift, shift_l, state_t, prm):
    n = state_t.shape[-1]
    wide = lambda col0: pl.BlockSpec((n, RWKV_WIDTH), lambda h: (0, col0 // RWKV_WIDTH))
    proj = lambda col0: pl.BlockSpec((pl.Element(n), pl.Element(RWKV_WIDTH)), lambda h: (0, pl.multiple_of(col0, LANES)))
    vec = pl.BlockSpec((1, RWKV_WIDTH), lambda h: (0, 0))
    vec_l = pl.BlockSpec((1, LORA_PAD), lambda h: (0, 0))
    st_spec = pl.BlockSpec((1, HEAD_DIM, HEAD_DIM, n), lambda h: (h, 0, 0, 0))
    return pl.pallas_call(
        _rwkv_sample_kernel,
        grid=(N_RWKV_HEADS,),
        in_specs=[proj(COL_R), proj(COL_KR), proj(COL_VR),
                  pl.BlockSpec((n, LORA_PAD), lambda h: (0, COL_LORA // LORA_PAD)),
                  wide(0), wide(RWKV_WIDTH), wide(2 * RWKV_WIDTH),
                  pl.BlockSpec((n, LORA_PAD), lambda h: (0, 0)),
                  st_spec,
                  vec, vec, vec, vec_l, vec, vec, vec, vec, vec, vec, vec,
                  pl.BlockSpec((LANES, RWKV_WIDTH), lambda h: (0, 0)),
                  pl.BlockSpec((LANES, RWKV_WIDTH), lambda h: (0, 0)),
                  pl.BlockSpec((2 * LANES, RWKV_WIDTH), lambda h: (0, 0))],
        out_specs=[pl.BlockSpec((n, RWKV_WIDTH), lambda h: (0, 0)), st_spec],
        out_shape=[jax.ShapeDtypeStruct((n, RWKV_WIDTH), BF16), jax.ShapeDtypeStruct(state_t.shape, F32)],
        scratch_shapes=[pltpu.VMEM((6, N_RWKV_HEADS, HEAD_DIM, n), F32),
                        pltpu.VMEM((4, n, RWKV_WIDTH), F32),
                        pltpu.VMEM((N_RWKV_HEADS, HEAD_DIM, n), F32)],
        compiler_params=_cparams(1),
        name="rwkv_sample",
    )(p_all, p_all, p_all, p_all, shift, shift, shift, shift_l, state_t,
      prm["mu_r"], prm["mu_k"], prm["mu_v"], prm["mu_l"], prm["w0"], prm["a0"], prm["k_k"], prm["k_a"],
      prm["r_k"], prm["gn_w"], prm["gn_b"], prm["wd"], prm["wi"], prm["wg"])


def _swap_head_order(z, outer, inner):
    n = z.shape[0]
    return z.reshape(n, outer, inner, HEAD_DIM).transpose(0, 2, 1, 3).reshape(n, outer * inner * HEAD_DIM)


def _shift_columns(p_rows):
    return p_rows[:, COL_R:COL_R + RWKV_PROJ_WIDTH]


def _forward(x_prompt, x_sample, cache_k_win, cache_v_win, state_shift, state_wkv, w, cfg):
    b, t, _ = x_prompt.shape
    n_s = x_sample.shape[0]
    row = lambda z: z.reshape(1, -1).astype(F32)
    mu = w["mu_shift"]
    wd, wi, wg = _rwkv_lora_weights(w["w_decay_up"], w["w_iclr_up"], w["w_gate_up"])
    prm = dict(
        mu_r=row(mu[:RWKV_WIDTH]), mu_k=row(mu[RWKV_WIDTH:2 * RWKV_WIDTH]), mu_v=row(mu[2 * RWKV_WIDTH:3 * RWKV_WIDTH]),
        mu_l=row(jnp.pad(mu[3 * RWKV_WIDTH:], (0, LORA_PAD - LORA_WIDTH))),
        w0=row(w["w0"]), a0=row(w["a0"]), k_k=row(w["k_k"]), k_a=row(w["k_a"]), r_k=row(w["r_k"]),
        gn_w=row(w["gn_w"]), gn_b=row(w["gn_b"]), wd=wd, wi=wi, wg=wg)
    wpa = w["w_proj_attn"].astype(BF16)
    wpr = w["w_proj_rwkv"].astype(BF16)
    wo = w["w_out"].astype(BF16)
    wu = w["w_up"].astype(BF16)
    wdn = w["w_down"].astype(BF16)
    ln1g, ln1b, ln2g, ln2b = row(w["ln1_g"]), row(w["ln1_b"]), row(w["ln2_g"]), row(w["ln2_b"])
    sinks = w["attn_sinks"].astype(F32)

    xp = x_prompt.reshape(b * t, D_MODEL)
    w_in_t = w["w_in"].T.astype(BF16)
    pp = _inproj(xp, w_in_t, cfg["tm_in"], cfg["tn_in"])
    attn_p, kwin_p, vwin_p = _attn_prompt(pp, sinks, b, t)
    rwkv_p, wkv_p = _rwkv_prompt(pp, prm, b, t, cfg["tb_rwkv"], cfg["lw_rwkv"])
    hp = _merge(xp, attn_p, rwkv_p, w_in_t, wpa, wpr, wo, ln1g, ln1b, cfg["tm_merge"], cfg["tj_merge"])
    yp = _ffn(hp, wu, wdn, ln2g, ln2b, cfg["tm_ffn"], cfg["tf_ffn"])
    shift_p = _shift_columns(pp.reshape(b, t, PACK_WIDTH)[:, t - 1])

    group = N_Q_HEADS // N_KV_HEADS
    xs = x_sample.reshape(n_s, D_MODEL)
    ps = _inproj(xs, w_in_t, n_s, cfg["tn_in"])
    q_perm = _swap_head_order(ps[:, COL_Q:COL_Q + ATTN_WIDTH], N_KV_HEADS, group)
    sink_mat = jnp.broadcast_to(sinks.reshape(N_KV_HEADS, group).T.reshape(N_Q_HEADS, 1), (N_Q_HEADS, LANES))
    win = cache_k_win.shape[1]
    to_t = lambda z: jnp.transpose(z, (0, 2, 3, 1)).reshape(n_s, KV_WIDTH, win)
    from_t = lambda z: jnp.transpose(z.reshape(n_s, N_KV_HEADS, HEAD_DIM, win), (0, 3, 1, 2))[None]
    attn_s, nk_t, nv_t = _attn_sample(q_perm, ps[:, COL_K:COL_K + 2 * KV_WIDTH].T, to_t(cache_k_win), to_t(cache_v_win),
                                      sink_mat, cfg["bt_sample"])
    attn_s = _swap_head_order(attn_s, group, N_KV_HEADS)
    shift_l = jnp.pad(state_shift[:, 3 * RWKV_WIDTH:], ((0, 0), (0, LORA_PAD - LORA_WIDTH)))
    rwkv_s, wkv_t = _rwkv_sample(ps, state_shift, shift_l, jnp.transpose(state_wkv, (1, 2, 3, 0)), prm)
    wkv_s = jnp.transpose(wkv_t, (3, 0, 1, 2))
    hs = _merge(xs, attn_s, rwkv_s, w_in_t, wpa, wpr, wo, ln1g, ln1b, n_s, cfg["tj_merge"])
    ys = _ffn(hs, wu, wdn, ln2g, ln2b, n_s, cfg["tf_ffn"])
    shift_s = _shift_columns(ps)

    kv5 = lambda z: z.reshape(1, z.shape[0], z.shape[1], N_KV_HEADS, HEAD_DIM)
    return (yp.reshape(b, t, D_MODEL), ys.reshape(n_s, 1, D_MODEL),
            kv5(kwin_p), kv5(vwin_p), shift_p[None], wkv_p[None],
            from_t(nk_t), from_t(nv_t), shift_s[None], wkv_s[None])


_CFG = dict(tm_in=1024, tn_in=1024, tb_rwkv=256, lw_rwkv=1024, tm_merge=512, tj_merge=512, tm_ffn=512, tf_ffn=1024, bt_sample=16)


def kernel(x_prompt, x_sample, cache_k_win, cache_v_win, state_shift, state_wkv, w_in, attn_sinks, mu_shift, w0,
           w_decay_up, a0, w_iclr_up, w_gate_up, k_k, k_a, r_k, gn_w, gn_b, w_proj_attn, w_proj_rwkv, w_out,
           ln1_g, ln1_b, w_up, w_down, ln2_g, ln2_b):
    w = dict(w_in=w_in[0], attn_sinks=attn_sinks[0], mu_shift=mu_shift[0], w0=w0[0], w_decay_up=w_decay_up[0],
             a0=a0[0], w_iclr_up=w_iclr_up[0], w_gate_up=w_gate_up[0], k_k=k_k[0], k_a=k_a[0], r_k=r_k[0],
             gn_w=gn_w[0], gn_b=gn_b[0], w_proj_attn=w_proj_attn[0], w_proj_rwkv=w_proj_rwkv[0], w_out=w_out[0],
             ln1_g=ln1_g[0], ln1_b=ln1_b[0], w_up=w_up[0], w_down=w_down[0], ln2_g=ln2_g[0], ln2_b=ln2_b[0])
    return _forward(x_prompt, x_sample, cache_k_win[0], cache_v_win[0], state_shift[0], state_wkv[0], w, _CFG)
```

```python
import jax
import jax.numpy as jnp
from jax import lax
from jax.experimental import pallas as pl
from jax.experimental.pallas import tpu as pltpu

F32 = jnp.float32
BF16 = jnp.bfloat16

D_MODEL = 2048
HEAD_DIM = 64
N_Q_HEADS = 16
N_KV_HEADS = 4
ATTN_WIDTH = N_Q_HEADS * HEAD_DIM
KV_WIDTH = N_KV_HEADS * HEAD_DIM
WINDOW = 128
ROPE_THETA = 500000.0
ROT_DIM = HEAD_DIM // 4
N_RWKV_HEADS = 16
RWKV_WIDTH = N_RWKV_HEADS * HEAD_DIM
DECAY_LORA = 64
ICLR_LORA = 64
GATE_LORA = 160
LORA_WIDTH = DECAY_LORA + ICLR_LORA + GATE_LORA
RWKV_PROJ_WIDTH = 3 * RWKV_WIDTH + LORA_WIDTH
D_FF = 4 * D_MODEL
PAST_LEN = 16384
DEEPNORM_ALPHA = 2.0 ** 0.25
LN_EPS = 1e-5
GN_EPS = HEAD_DIM * 1e-5
NEG_BIG = -1e30
DECAY_SCALE = 0.6065306597126334
NORM_FLOOR_SQ = 1e-24

LANES = 128
BF16_ROWS_PER_TILE = 16
HEADS_PER_GROUP = 8
CHUNK = 64
VMEM_LIMIT = 56 * 1024 * 1024

COL_Q = 0
COL_K = 1024
COL_V = 1280
COL_R = 1536
COL_KR = 2560
COL_VR = 3584
COL_LORA = 4608
LORA_PAD = 512
PACK_WIDTH = COL_LORA + LORA_PAD
GATE_ROW0 = COL_LORA + LORA_WIDTH


def _cparams(n_axes):
    return pltpu.CompilerParams(dimension_semantics=("arbitrary",) * n_axes, vmem_limit_bytes=VMEM_LIMIT)


def _sigmoid(x):
    return 1.0 / (1.0 + jnp.exp(-x))


def _layer_norm_rows(z, g, b):
    mu = jnp.mean(z, axis=-1, keepdims=True)
    d = z - mu
    var = jnp.mean(d * d, axis=-1, keepdims=True)
    return d * lax.rsqrt(var + LN_EPS) * g + b


def _dot_t(a, b):
    return lax.dot_general(a, b, (((1,), (1,)), ((), ())), preferred_element_type=F32)


def _head_block_ones():
    r = lax.broadcasted_iota(jnp.int32, (LANES, LANES), 0)
    c = lax.broadcasted_iota(jnp.int32, (LANES, LANES), 1)
    return jnp.where(r // HEAD_DIM == c // HEAD_DIM, 1.0, 0.0).astype(BF16)


def _head_sum(x, bd):
    outs = [jnp.dot(x[:, LANES * t:LANES * (t + 1)].astype(BF16), bd, preferred_element_type=F32)
            for t in range(x.shape[1] // LANES)]
    return outs[0] if len(outs) == 1 else jnp.concatenate(outs, axis=1)


def _inproj_kernel(x_ref, w_ref, o_ref, xb_ref):
    @pl.when(pl.program_id(1) == 0)
    def _():
        xb_ref[...] = x_ref[...].astype(BF16)

    o_ref[...] = _dot_t(xb_ref[...], w_ref[...])


def _inproj(x, w_t, tm, tn):
    m, k = x.shape
    n = PACK_WIDTH
    return pl.pallas_call(
        _inproj_kernel,
        grid=(m // tm, n // tn),
        in_specs=[pl.BlockSpec((tm, k), lambda i, j: (i, 0)),
                  pl.BlockSpec((tn, k), lambda i, j: (j, 0))],
        out_specs=pl.BlockSpec((tm, tn), lambda i, j: (i, j)),
        out_shape=jax.ShapeDtypeStruct((m, n), F32),
        scratch_shapes=[pltpu.VMEM((tm, k), BF16)],
        compiler_params=_cparams(2),
        name="inproj",
    )(x, w_t)


def _rope_tables(pos):
    inv = ROPE_THETA ** (-jnp.arange(0, ROT_DIM, 2, dtype=F32) / ROT_DIM)
    ang = pos.astype(F32)[:, None] * inv[None, :]
    cos, sin = jnp.cos(ang), jnp.sin(ang)
    t = pos.shape[0]
    half = ROT_DIM // 2
    pad = HEAD_DIM - ROT_DIM
    c_head = jnp.concatenate([cos, cos, jnp.ones((t, pad), F32)], axis=1)
    sp_head = jnp.concatenate([-sin, jnp.zeros((t, half + pad), F32)], axis=1)
    sm_head = jnp.concatenate([jnp.zeros((t, half), F32), sin, jnp.zeros((t, pad), F32)], axis=1)
    rep = LANES // HEAD_DIM
    return tuple(jnp.tile(z, (1, rep)) for z in (c_head, sp_head, sm_head))


def _rope_tile(x, c, sp, sm):
    half = ROT_DIM // 2
    return x * c + pltpu.roll(x, LANES - half, 1) * sp + pltpu.roll(x, half, 1) * sm


def _attn_prompt_kernel(q_ref, k_ref, v_ref, c_ref, sp_ref, sm_ref, sink_ref,
                        o_ref, kwin_ref, vwin_ref, kp_ref, vp_ref):
    i = pl.program_id(1)
    blk = WINDOW

    @pl.when(i == 0)
    def _():
        kp_ref[...] = jnp.zeros_like(kp_ref)
        vp_ref[...] = jnp.zeros_like(vp_ref)

    c, sp, sm = c_ref[...], sp_ref[...], sm_ref[...]
    lane = lax.broadcasted_iota(jnp.int32, (2 * blk, LANES), 1)
    lo2 = lane < HEAD_DIM
    lo1 = lax.broadcasted_iota(jnp.int32, (blk, LANES), 1) < HEAD_DIM

    own = lax.broadcasted_iota(jnp.int32, (blk, blk), 1) <= lax.broadcasted_iota(jnp.int32, (blk, blk), 0)
    no_prev = jnp.where(i == 0, NEG_BIG, 0.0)
    zero_p = jnp.zeros((blk, blk), BF16)

    n_kv_tiles = KV_WIDTH // LANES
    kk_g, va_g, vb_g = [], [], []
    for t in range(n_kv_tiles):
        sl = slice(LANES * t, LANES * (t + 1))
        kcur = _rope_tile(k_ref[:, sl], c, sp, sm)
        vcur = v_ref[:, sl]
        kwin_ref[0, :, sl] = kcur
        vwin_ref[0, :, sl] = vcur
        kall = jnp.concatenate([kp_ref[:, sl], kcur], axis=0)
        vall = jnp.concatenate([vp_ref[:, sl], vcur], axis=0)
        kp_ref[:, sl] = kcur
        vp_ref[:, sl] = vcur
        kswap = pltpu.roll(kall, HEAD_DIM, 1)
        vswap = pltpu.roll(vall, HEAD_DIM, 1)
        kk_g += [jnp.where(lo2, kall, kswap).astype(BF16), jnp.where(lo2, kswap, kall).astype(BF16)]
        va_g += [jnp.where(lo2, vall, 0.0).astype(BF16), jnp.where(lo2, vswap, 0.0).astype(BF16)]
        vb_g += [jnp.where(lo2, 0.0, vswap).astype(BF16), jnp.where(lo2, 0.0, vall).astype(BF16)]

    group = N_Q_HEADS // N_KV_HEADS
    tiles = range(ATTN_WIDTH // LANES)
    half = ROT_DIM // 2
    pr = lax.broadcasted_iota(jnp.int32, (LANES, LANES), 0)
    pc = lax.broadcasted_iota(jnp.int32, (LANES, LANES), 1)
    dcol = pc % HEAD_DIM
    partner = jnp.where(dcol < half, pc + half, jnp.where(dcol < ROT_DIM, pc - half, -1))
    perm = jnp.where(pr == partner, 1.0, 0.0).astype(BF16)
    ssum = sp + sm
    qt = []
    for j in tiles:
        xq = q_ref[:, LANES * j:LANES * (j + 1)]
        xp = jnp.dot(xq.astype(BF16), perm, preferred_element_type=F32)
        qt.append((xq * c + xp * ssum) * (HEAD_DIM ** -0.5))
    for h0 in range(0, N_Q_HEADS, HEADS_PER_GROUP):
        heads = range(h0, h0 + HEADS_PER_GROUP)
        qm = {h: (jnp.where(lo1, qt[h // 2], 0.0) if h % 2 == 0 else jnp.where(lo1, 0.0, qt[h // 2])).astype(BF16) for h in heads}
        sinks = {h: sink_ref[h] for h in heads}
        s2 = {h: _dot_t(qm[h], kk_g[h // group]) for h in heads}
        s = {h: jnp.where(own, s2[h][:, blk:], s2[h][:, :blk] + no_prev) for h in heads}
        m = {h: jnp.maximum(jnp.max(s[h], axis=1, keepdims=True), sinks[h]) for h in heads}
        p = {h: jnp.exp(s[h] - m[h]) for h in heads}
        den = {h: jnp.sum(p[h], axis=1, keepdims=True) + jnp.exp(sinks[h] - m[h]) for h in heads}
        pb = {h: p[h].astype(BF16) for h in heads}
        p2 = {h: jnp.concatenate([jnp.where(own, zero_p, pb[h]), jnp.where(own, pb[h], zero_p)], axis=1) for h in heads}
        o = {h: jnp.dot(p2[h], (va_g if h % 2 == 0 else vb_g)[h // group], preferred_element_type=F32) / den[h]
             for h in heads}
        for j in range(h0 // 2, (h0 + HEADS_PER_GROUP) // 2):
            o_ref[:, LANES * j:LANES * (j + 1)] = (o[2 * j] + o[2 * j + 1]).astype(o_ref.dtype)


def _attn_prompt(p_all, sinks, b, t):
    blk = WINDOW
    nb = t // blk
    c, sp, sm = _rope_tables(jnp.arange(t, dtype=jnp.int32))
    tab_spec = pl.BlockSpec((blk, LANES), lambda bi, i: (i, 0))
    row = lambda bi, i: bi * nb + i
    return pl.pallas_call(
        _attn_prompt_kernel,
        grid=(b, nb),
        in_specs=[pl.BlockSpec((blk, ATTN_WIDTH), lambda bi, i: (row(bi, i), COL_Q // ATTN_WIDTH)),
                  pl.BlockSpec((blk, KV_WIDTH), lambda bi, i: (row(bi, i), COL_K // KV_WIDTH)),
                  pl.BlockSpec((blk, KV_WIDTH), lambda bi, i: (row(bi, i), COL_V // KV_WIDTH)),
                  tab_spec, tab_spec, tab_spec,
                  pl.BlockSpec(memory_space=pltpu.SMEM)],
        out_specs=[pl.BlockSpec((blk, ATTN_WIDTH), lambda bi, i: (row(bi, i), 0)),
                   pl.BlockSpec((1, blk, KV_WIDTH), lambda bi, i: (bi, 0, 0)),
                   pl.BlockSpec((1, blk, KV_WIDTH), lambda bi, i: (bi, 0, 0))],
        out_shape=[jax.ShapeDtypeStruct((b * t, ATTN_WIDTH), BF16),
                   jax.ShapeDtypeStruct((b, blk, KV_WIDTH), F32),
                   jax.ShapeDtypeStruct((b, blk, KV_WIDTH), F32)],
        scratch_shapes=[pltpu.VMEM((blk, KV_WIDTH), F32), pltpu.VMEM((blk, KV_WIDTH), F32)],
        compiler_params=_cparams(2),
        name="attn_prompt",
    )(p_all, p_all, p_all, c, sp, sm, sinks)


def _rwkv_prepare(r_in, k_in, v_in, l_in, pr, pk, pv, plr, prm, wd_ref, wi_ref, wg_ref, bd2):
    mur, muk, muv, mul, w0, a0, k_k, k_a = prm
    r = r_in + (pr - r_in) * mur
    k = k_in + (pk - k_in) * muk
    v = v_in + (pv - v_in) * muv
    ls = l_in + (plr - l_in) * mul
    l0 = ls[:, :LANES]
    lane = lax.broadcasted_iota(jnp.int32, l0.shape, 1)
    z0 = jnp.where(lane < DECAY_LORA, jnp.tanh(l0), l0).astype(BF16)
    zg = _sigmoid(ls[:, LANES:3 * LANES]).astype(BF16)
    dec_up = jnp.dot(z0, wd_ref[...], preferred_element_type=F32)
    icl_up = jnp.dot(z0, wi_ref[...], preferred_element_type=F32)
    gate = jnp.dot(zg, wg_ref[...], preferred_element_type=F32)
    logw = -DECAY_SCALE * _sigmoid(w0 + dec_up)
    a_sig = _sigmoid(a0 + icl_up)
    kk = k * k_k
    kk = kk * lax.rsqrt(jnp.maximum(_head_sum(kk * kk, bd2), NORM_FLOOR_SQ))
    k = k * (1.0 + (a_sig - 1.0) * k_a)
    return r, k, v, logw, -kk, kk * a_sig, gate


def _rwkv_finish(y, r, k, v, gate, r_k, gn_w, gn_b, bd2):
    mu = _head_sum(y, bd2) * (1.0 / HEAD_DIM)
    d = y - mu
    var = _head_sum(d * d, bd2) * (1.0 / HEAD_DIM)
    yn = d * lax.rsqrt(var + GN_EPS) * gn_w + gn_b
    bonus = _head_sum(r * k * r_k, bd2) * v
    return (yn + bonus) * gate


def _rwkv_prompt_kernel(r_ref, k_ref, v_ref, l_ref,
                        mur_ref, muk_ref, muv_ref, mul_ref, w0_ref, a0_ref, kk_ref, ka_ref,
                        rk_ref, gnw_ref, gnb_ref, wd_ref, wi_ref, wg_ref,
                        o_ref, sout_ref,
                        cr_ref, ck_ref, cv_ref, cl_ref, st_ref, y_ref):
    ti = pl.program_id(2)
    tb, lw = r_ref.shape
    n_pair = lw // LANES
    n_chunk = tb // CHUNK

    @pl.when(ti == 0)
    def _():
        for ref in (cr_ref, ck_ref, cv_ref, cl_ref, st_ref):
            ref[...] = jnp.zeros_like(ref)

    def shifted(x, carry_ref):
        rolled = pltpu.roll(x, 1, 0)
        row = lax.broadcasted_iota(jnp.int32, x.shape, 0)
        prev = jnp.where(row == 0, carry_ref[0:1, :], rolled)
        carry_ref[0:1, :] = x[tb - 1:tb, :]
        return prev

    bd2 = _head_block_ones()
    r_in, k_in, v_in, l_in = r_ref[...], k_ref[...], v_ref[...], l_ref[...]
    prm = tuple(ref[...] for ref in (mur_ref, muk_ref, muv_ref, mul_ref, w0_ref, a0_ref, kk_ref, ka_ref))
    r, k, v, logw, a_s, b_s, gate = _rwkv_prepare(
        r_in, k_in, v_in, l_in,
        shifted(r_in, cr_ref), shifted(k_in, ck_ref), shifted(v_in, cv_ref), shifted(l_in, cl_ref),
        prm, wd_ref, wi_ref, wg_ref, bd2)

    c = CHUNK
    ri = lax.broadcasted_iota(jnp.int32, (c, 3 * c), 0)
    ci = lax.broadcasted_iota(jnp.int32, (c, 3 * c), 1) % c
    tri3 = jnp.where(ri >= ci, 1.0, 0.0).astype(BF16)
    r2 = lax.broadcasted_iota(jnp.int32, (2 * c, LANES), 0)
    l2 = lax.broadcasted_iota(jnp.int32, (2 * c, LANES), 1)
    tt, ss = r2 % c, l2 % c
    causal = ss < tt + r2 // c
    lo1 = lax.broadcasted_iota(jnp.int32, (c, LANES), 1) < HEAD_DIM
    diag_blocks = (r2 // HEAD_DIM) == (l2 // HEAD_DIM)
    eye_side = jnp.where(lax.broadcasted_iota(jnp.int32, (c, LANES), 0) == lax.broadcasted_iota(jnp.int32, (c, LANES), 1) % c,
                         1.0, 0.0)

    def stack_heads(x):
        return jnp.concatenate([jnp.where(lo1, x, 0.0), jnp.where(lo1, 0.0, x)], axis=0)

    def dot_t(a, b):
        return lax.dot_general(a, b, (((1,), (1,)), ((), ())), preferred_element_type=F32)

    def dot_tt(a, b):
        return lax.dot_general(a, b, (((0,), (0,)), ((), ())), preferred_element_type=F32)

    def dot(a, b):
        return jnp.dot(a.astype(BF16), b.astype(BF16), preferred_element_type=F32)

    lanes = [slice(LANES * pi, LANES * (pi + 1)) for pi in range(n_pair)]
    units = [(ch, pi) for ch in range(n_chunk) for pi in range(n_pair)]
    un = range(len(units))
    el = []
    for ch in range(n_chunk):
        rows = slice(c * ch, c * (ch + 1))
        lw_c = logw[rows]
        hi = lw_c.astype(BF16)
        rem = lw_c - hi.astype(F32)
        mid = rem.astype(BF16)
        low = (rem - mid.astype(F32)).astype(BF16)
        lcum = jnp.dot(tri3, jnp.concatenate([hi, mid, low], axis=0), preferred_element_type=F32)
        ltot = lcum[c - 1:c, :]
        p_inv = jnp.exp(-lcum)
        p_tail = jnp.exp(ltot - lcum)
        el.append(dict(aq=a_s[rows] * jnp.exp(lcum - lw_c), rq=r[rows] * jnp.exp(lcum), bk=b_s[rows] * p_inv,
                       kq=k[rows] * p_inv, bt=b_s[rows] * p_tail, kt=k[rows] * p_tail, v=v[rows],
                       p_end=jnp.exp(ltot)))
    op = lambda name, u: el[units[u][0]][name][:, lanes[units[u][1]]]

    ar = [jnp.concatenate([op("aq", u), op("rq", u)], axis=0).astype(BF16) for u in un]
    gb = [jnp.where(causal, dot_t(ar[u], stack_heads(op("bk", u)).astype(BF16)), 0.0) for u in un]
    gk = [jnp.where(causal, dot_t(ar[u], stack_heads(op("kq", u)).astype(BF16)), 0.0) for u in un]
    gv = [dot(gk[u], stack_heads(op("v", u))) for u in un]
    pw = [gb[u][:c] for u in un]
    tm = [eye_side + pw[u] for u in un]
    for level in range(1, 6):
        bd = [stack_heads(pw[u]).astype(BF16) for u in un]
        if level == 1:
            pw = [jnp.dot(pw[u].astype(BF16), bd[u], preferred_element_type=F32) for u in un]
            bd = [stack_heads(pw[u]).astype(BF16) for u in un]
        if level < 5:
            both = [jnp.dot(jnp.concatenate([tm[u], pw[u]], axis=0).astype(BF16), bd[u], preferred_element_type=F32)
                    for u in un]
            tm = [tm[u] + both[u][:c] for u in un]
            pw = [both[u][c:] for u in un]
        else:
            tm = [tm[u] + jnp.dot(tm[u].astype(BF16), bd[u], preferred_element_type=F32) for u in un]
    tax = [dot(tm[u], jnp.concatenate([stack_heads(op("aq", u)), stack_heads(gv[u][:c])], axis=1)) for u in un]
    taq = [tax[u][:, :LANES] for u in un]
    txv = [tax[u][:, LANES:] for u in un]
    arx = [dot(gb[u][c:], jnp.concatenate([stack_heads(taq[u]), stack_heads(txv[u])], axis=1)) for u in un]
    mb = [jnp.where(diag_blocks, dot_tt(op("bt", u).astype(BF16), taq[u].astype(BF16)), 0.0).astype(BF16) for u in un]
    cct = [jnp.where(diag_blocks,
                     dot_tt(jnp.concatenate([txv[u], op("v", u)], axis=0).astype(BF16),
                            jnp.concatenate([op("bt", u), op("kt", u)], axis=0).astype(BF16)), 0.0) for u in un]
    rqp = [(op("rq", u) + arx[u][:, :LANES]).astype(BF16) for u in un]
    yc = [gv[u][c:] + arx[u][:, LANES:] for u in un]
    for u, (ch, pi) in enumerate(units):
        s_old = st_ref[pi]
        sb = s_old.astype(BF16)
        y_ref[c * ch:c * (ch + 1), lanes[pi]] = dot_t(rqp[u], sb) + yc[u]
        st_ref[pi] = s_old * op("p_end", u) + dot_t(sb, mb[u]) + cct[u]

    out = _rwkv_finish(y_ref[...], r, k, v, gate, rk_ref[...], gnw_ref[...], gnb_ref[...], bd2)
    o_ref[...] = out.astype(o_ref.dtype)

    @pl.when(ti == pl.num_programs(2) - 1)
    def _():
        for pi in range(n_pair):
            s = st_ref[pi]
            sout_ref[0, 2 * pi] = s[:HEAD_DIM, :HEAD_DIM]
            sout_ref[0, 2 * pi + 1] = s[HEAD_DIM:, HEAD_DIM:]


def _rwkv_lora_weights(w_decay_up, w_iclr_up, w_gate_up):
    z64 = jnp.zeros((DECAY_LORA, RWKV_WIDTH), F32)
    wd = jnp.concatenate([w_decay_up, z64], axis=0).astype(BF16)
    wi = jnp.concatenate([z64, w_iclr_up], axis=0).astype(BF16)
    wg = jnp.concatenate([w_gate_up, jnp.zeros((2 * LANES - GATE_LORA, RWKV_WIDTH), F32)], axis=0).astype(BF16)
    return wd, wi, wg


def _rwkv_prompt(p_all, prm, b, t, tb, lw):
    nt = t // tb
    ns = RWKV_WIDTH // lw
    row = lambda bi, si, ti: bi * nt + ti
    col_spec = lambda col0: pl.BlockSpec(
        (pl.Element(tb), pl.Element(lw)),
        lambda bi, si, ti: (pl.multiple_of(row(bi, si, ti) * tb, tb), pl.multiple_of(col0 + si * lw, LANES)))
    vec = pl.BlockSpec((1, lw), lambda bi, si, ti: (0, si))
    vec_l = pl.BlockSpec((1, LORA_PAD), lambda bi, si, ti: (0, 0))
    return pl.pallas_call(
        _rwkv_prompt_kernel,
        grid=(b, ns, nt),
        in_specs=[col_spec(COL_R), col_spec(COL_KR), col_spec(COL_VR),
                  pl.BlockSpec((tb, LORA_PAD), lambda bi, si, ti: (row(bi, si, ti), COL_LORA // LORA_PAD)),
                  vec, vec, vec, vec_l, vec, vec, vec, vec, vec, vec, vec,
                  pl.BlockSpec((LANES, lw), lambda bi, si, ti: (0, si)),
                  pl.BlockSpec((LANES, lw), lambda bi, si, ti: (0, si)),
                  pl.BlockSpec((2 * LANES, lw), lambda bi, si, ti: (0, si))],
        out_specs=[pl.BlockSpec((tb, lw), lambda bi, si, ti: (row(bi, si, ti), si)),
                   pl.BlockSpec((1, 2 * (lw // LANES), HEAD_DIM, HEAD_DIM), lambda bi, si, ti: (bi, si, 0, 0))],
        out_shape=[jax.ShapeDtypeStruct((b * t, RWKV_WIDTH), BF16),
                   jax.ShapeDtypeStruct((b, N_RWKV_HEADS, HEAD_DIM, HEAD_DIM), F32)],
        scratch_shapes=[pltpu.VMEM((8, lw), F32), pltpu.VMEM((8, lw), F32), pltpu.VMEM((8, lw), F32),
                        pltpu.VMEM((8, LORA_PAD), F32),
                        pltpu.VMEM((lw // LANES, LANES, LANES), F32),
                        pltpu.VMEM((tb, lw), F32)],
        compiler_params=_cparams(3),
        name="rwkv_prompt",
    )(p_all, p_all, p_all, p_all,
      prm["mu_r"], prm["mu_k"], prm["mu_v"], prm["mu_l"], prm["w0"], prm["a0"], prm["k_k"], prm["k_a"],
      prm["r_k"], prm["gn_w"], prm["gn_b"], prm["wd"], prm["wi"], prm["wg"])


def _merge_kernel(x_ref, ao_ref, ro_ref, wga_ref, wgb_ref, wpa_ref, wpr_ref, wo_ref, g_ref, b_ref, h_ref,
                  xb_ref, acc_ref):
    j = pl.program_id(1)

    @pl.when(j == 0)
    def _():
        xb_ref[...] = x_ref[...].astype(BF16)
        acc_ref[...] = jnp.zeros_like(acc_ref)

    xb = xb_ref[...]
    ga = _dot_t(xb, wga_ref[...])
    gb = _dot_t(xb, wgb_ref[...])
    a = jnp.dot(ao_ref[...], wpa_ref[...], preferred_element_type=F32)
    r = jnp.dot(ro_ref[...], wpr_ref[...], preferred_element_type=F32)
    m = _sigmoid(ga) * a + _sigmoid(gb) * r
    acc_ref[...] += jnp.dot(m.astype(BF16), wo_ref[...], preferred_element_type=F32)

    @pl.when(j == pl.num_programs(1) - 1)
    def _():
        z = DEEPNORM_ALPHA * x_ref[...] + acc_ref[...]
        h_ref[...] = _layer_norm_rows(z, g_ref[...], b_ref[...])


def _merge(x, attn_o, rwkv_o, w_in_t, wpa, wpr, wo, ln_g, ln_b, tm, tj):
    m = x.shape[0]
    nj = D_MODEL // tj
    gate_spec = lambda row0: pl.BlockSpec(
        (pl.Element(tj), pl.Element(D_MODEL)),
        lambda i, j: (pl.multiple_of(row0 + j * tj, BF16_ROWS_PER_TILE), 0))
    return pl.pallas_call(
        _merge_kernel,
        grid=(m // tm, nj),
        in_specs=[pl.BlockSpec((tm, D_MODEL), lambda i, j: (i, 0)),
                  pl.BlockSpec((tm, ATTN_WIDTH), lambda i, j: (i, 0)),
                  pl.BlockSpec((tm, RWKV_WIDTH), lambda i, j: (i, 0)),
                  gate_spec(GATE_ROW0), gate_spec(GATE_ROW0 + D_MODEL),
                  pl.BlockSpec((ATTN_WIDTH, tj), lambda i, j: (0, j)),
                  pl.BlockSpec((RWKV_WIDTH, tj), lambda i, j: (0, j)),
                  pl.BlockSpec((tj, D_MODEL), lambda i, j: (j, 0)),
                  pl.BlockSpec((1, D_MODEL), lambda i, j: (0, 0)),
                  pl.BlockSpec((1, D_MODEL), lambda i, j: (0, 0))],
        out_specs=pl.BlockSpec((tm, D_MODEL), lambda i, j: (i, 0)),
        out_shape=jax.ShapeDtypeStruct((m, D_MODEL), F32),
        scratch_shapes=[pltpu.VMEM((tm, D_MODEL), BF16), pltpu.VMEM((tm, D_MODEL), F32)],
        compiler_params=_cparams(2),
        name="merge_ln1",
    )(x, attn_o, rwkv_o, w_in_t, w_in_t, wpa, wpr, wo, ln_g, ln_b)


def _ffn_kernel(h_ref, wu_ref, wd_ref, g_ref, b_ref, y_ref, hb_ref, acc_ref):
    f = pl.program_id(1)

    @pl.when(f == 0)
    def _():
        hb_ref[...] = h_ref[...].astype(BF16)
        acc_ref[...] = jnp.zeros_like(acc_ref)

    u = jnp.dot(hb_ref[...], wu_ref[...], preferred_element_type=F32)
    u = jnp.square(jnp.maximum(u, 0.0))
    acc_ref[...] += jnp.dot(u.astype(BF16), wd_ref[...], preferred_element_type=F32)

    @pl.when(f == pl.num_programs(1) - 1)
    def _():
        z = DEEPNORM_ALPHA * h_ref[...] + acc_ref[...]
        y_ref[...] = _layer_norm_rows(z, g_ref[...], b_ref[...])


def _ffn(h, wu, wd, ln_g, ln_b, tm, tf):
    m = h.shape[0]
    return pl.pallas_call(
        _ffn_kernel,
        grid=(m // tm, D_FF // tf),
        in_specs=[pl.BlockSpec((tm, D_MODEL), lambda i, f: (i, 0)),
                  pl.BlockSpec((D_MODEL, tf), lambda i, f: (0, f)),
                  pl.BlockSpec((tf, D_MODEL), lambda i, f: (f, 0)),
                  pl.BlockSpec((1, D_MODEL), lambda i, f: (0, 0)),
                  pl.BlockSpec((1, D_MODEL), lambda i, f: (0, 0))],
        out_specs=pl.BlockSpec((tm, D_MODEL), lambda i, f: (i, 0)),
        out_shape=jax.ShapeDtypeStruct((m, D_MODEL), F32),
        scratch_shapes=[pltpu.VMEM((tm, D_MODEL), BF16), pltpu.VMEM((tm, D_MODEL), F32)],
        compiler_params=_cparams(2),
        name="ffn_ln2",
    )(h, wu, wd, ln_g, ln_b)


def _attn_sample_kernel(q_ref, kvt_ref, ck_ref, cv_ref, c_ref, sp_ref, sm_ref, cc_ref, spc_ref, smc_ref, sink_ref,
                        o_ref, nk_ref, nv_ref):
    bt = q_ref.shape[0]
    win = ck_ref.shape[2]
    n = kvt_ref.shape[1]
    half = ROT_DIM // 2
    group = N_Q_HEADS // N_KV_HEADS
    c, sp, sm = c_ref[0:1, :], sp_ref[0:1, :], sm_ref[0:1, :]
    sink = sink_ref[:, 0:1]
    row16 = lax.broadcasted_iota(jnp.int32, (N_Q_HEADS, KV_WIDTH), 0)
    lane16 = lax.broadcasted_iota(jnp.int32, (N_Q_HEADS, KV_WIDTH), 1)
    own_kv = (lane16 // HEAD_DIM) == (row16 % N_KV_HEADS)
    urow = row16 // N_KV_HEADS
    pos = lax.broadcasted_iota(jnp.int32, (KV_WIDTH, win), 1)
    seq = lax.broadcasted_iota(jnp.int32, (KV_WIDTH, n), 1)

    kt = kvt_ref[0:KV_WIDTH, :]
    kt = kt * cc_ref[...] + pltpu.roll(kt, KV_WIDTH - half, 0) * spc_ref[...] + pltpu.roll(kt, half, 0) * smc_ref[...]
    vt = kvt_ref[KV_WIDTH:2 * KV_WIDTH, :]

    def rope_row(x):
        return jnp.concatenate([_rope_tile(x[:, LANES * t:LANES * (t + 1)], c, sp, sm)
                                for t in range(x.shape[1] // LANES)], axis=1)

    rng = range(bt)
    me = [seq == pl.program_id(0) * bt + b for b in rng]
    k_col = [jnp.sum(jnp.where(me[b], kt, 0.0), axis=1, keepdims=True) for b in rng]
    v_col = [jnp.sum(jnp.where(me[b], vt, 0.0), axis=1, keepdims=True) for b in rng]
    nk = [jnp.where(pos == win - 1, k_col[b], pltpu.roll(ck_ref[b], win - 1, 1)) for b in rng]
    nv = [jnp.where(pos == win - 1, v_col[b], pltpu.roll(cv_ref[b], win - 1, 1)) for b in rng]
    for b in rng:
        nk_ref[b] = nk[b]
        nv_ref[b] = nv[b]
    qmat = []
    for b in rng:
        q = rope_row(q_ref[b:b + 1, :]) * (HEAD_DIM ** -0.5)
        qb = [jnp.broadcast_to(q[:, KV_WIDTH * u:KV_WIDTH * (u + 1)], (N_Q_HEADS, KV_WIDTH)) for u in range(group)]
        qsel = jnp.where(urow == 0, qb[0], jnp.where(urow == 1, qb[1], jnp.where(urow == 2, qb[2], qb[3])))
        qmat.append(jnp.where(own_kv, qsel, 0.0).astype(BF16))
    s = [jnp.dot(qmat[b], nk[b].astype(BF16), preferred_element_type=F32) for b in rng]
    m = [jnp.maximum(jnp.max(s[b], axis=1, keepdims=True), sink) for b in rng]
    p = [jnp.exp(s[b] - m[b]) for b in rng]
    den = [jnp.sum(p[b], axis=1, keepdims=True) + jnp.exp(sink - m[b]) for b in rng]
    o = [_dot_t(p[b].astype(BF16), nv[b].astype(BF16)) / den[b] for b in rng]
    out_rows = []
    for b in rng:
        ob = jnp.where(own_kv, o[b], 0.0)
        chunks = [jnp.sum(jnp.where(urow == u, ob, 0.0), axis=0, keepdims=True) for u in range(group)]
        out_rows.append(jnp.concatenate(chunks, axis=1))
    o_ref[...] = jnp.concatenate(out_rows, axis=0).astype(o_ref.dtype)


def _attn_sample(q_perm, kv_new_t, cache_kt, cache_vt, sink_mat, bt):
    n, win = cache_kt.shape[0], cache_kt.shape[2]
    tabs = _rope_tables(jnp.full((1,), PAST_LEN, jnp.int32))
    c, sp, sm = (jnp.broadcast_to(z, (8, LANES)) for z in tabs)
    cc, spc, smc = (jnp.broadcast_to(jnp.tile(z, (1, KV_WIDTH // LANES)).T, (KV_WIDTH, n)) for z in tabs)
    small = lambda shape: pl.BlockSpec(shape, lambda i: (0, 0))
    cache_spec = pl.BlockSpec((bt, KV_WIDTH, win), lambda i: (i, 0, 0))
    return pl.pallas_call(
        _attn_sample_kernel,
        grid=(n // bt,),
        in_specs=[pl.BlockSpec((bt, ATTN_WIDTH), lambda i: (i, 0)),
                  small((2 * KV_WIDTH, n)),
                  cache_spec, cache_spec,
                  small((8, LANES)), small((8, LANES)), small((8, LANES)),
                  small((KV_WIDTH, n)), small((KV_WIDTH, n)), small((KV_WIDTH, n)),
                  small((N_Q_HEADS, LANES))],
        out_specs=[pl.BlockSpec((bt, ATTN_WIDTH), lambda i: (i, 0)), cache_spec, cache_spec],
        out_shape=[jax.ShapeDtypeStruct((n, ATTN_WIDTH), BF16),
                   jax.ShapeDtypeStruct(cache_kt.shape, F32),
                   jax.ShapeDtypeStruct(cache_vt.shape, F32)],
        compiler_params=_cparams(1),
        name="attn_sample",
    )(q_perm, kv_new_t, cache_kt, cache_vt, c, sp, sm, cc, spc, smc, sink_mat)


def _rwkv_sample_kernel(r_ref, k_ref, v_ref, l_ref, pr_ref, pk_ref, pv_ref, pl_ref, st_ref,
                        mur_ref, muk_ref, muv_ref, mul_ref, w0_ref, a0_ref, kk_ref, ka_ref,
                        rk_ref, gnw_ref, gnb_ref, wd_ref, wi_ref, wg_ref,
                        o_ref, ns_ref, vec_s, keep_s, y_s):
    h = pl.program_id(0)
    hd = HEAD_DIM
    q_a, q_w, q_b, q_k, q_r, q_v = range(6)

    @pl.when(h == 0)
    def _():
        bd2 = _head_block_ones()
        prm = tuple(ref[...] for ref in (mur_ref, muk_ref, muv_ref, mul_ref, w0_ref, a0_ref, kk_ref, ka_ref))
        r, k, v, logw, a_s, b_s, gate = _rwkv_prepare(
            r_ref[...], k_ref[...], v_ref[...], l_ref[...], pr_ref[...], pk_ref[...], pv_ref[...], pl_ref[...],
            prm, wd_ref, wi_ref, wg_ref, bd2)
        for qi, x in enumerate((a_s, jnp.exp(logw), b_s, k, r, v)):
            xt = x.T
            for hh in range(N_RWKV_HEADS):
                vec_s[qi, hh] = xt[hd * hh:hd * (hh + 1), :]
        for qi, x in enumerate((r, k, v, gate)):
            keep_s[qi] = x

    a_h, w_h, b_h, k_h, r_h = (vec_s[qi, h] for qi in (q_a, q_w, q_b, q_k, q_r))
    for i in range(hd):
        s = st_ref[0, i]
        sa = jnp.sum(s * a_h, axis=0, keepdims=True)
        s_new = s * w_h + sa * b_h + vec_s[q_v, h, i:i + 1, :] * k_h
        ns_ref[0, i] = s_new
        y_s[h, i:i + 1, :] = jnp.sum(s_new * r_h, axis=0, keepdims=True)

    @pl.when(h == pl.num_programs(0) - 1)
    def _():
        y = jnp.concatenate([y_s[hh] for hh in range(N_RWKV_HEADS)], axis=0).T
        out = _rwkv_finish(y, keep_s[0], keep_s[1], keep_s[2], keep_s[3], rk_ref[...], gnw_ref[...], gnb_ref[...],
                           _head_block_ones())
        o_ref[...] = out.astype(o_ref.dtype)


def _rwkv_sample(p_all, shift, shift_l, state_t, prm):
    n = state_t.shape[-1]
    wide = lambda col0: pl.BlockSpec((n, RWKV_WIDTH), lambda h: (0, col0 // RWKV_WIDTH))
    proj = lambda col0: pl.BlockSpec((pl.Element(n), pl.Element(RWKV_WIDTH)), lambda h: (0, pl.multiple_of(col0, LANES)))
    vec = pl.BlockSpec((1, RWKV_WIDTH), lambda h: (0, 0))
    vec_l = pl.BlockSpec((1, LORA_PAD), lambda h: (0, 0))
    st_spec = pl.BlockSpec((1, HEAD_DIM, HEAD_DIM, n), lambda h: (h, 0, 0, 0))
    return pl.pallas_call(
        _rwkv_sample_kernel,
        grid=(N_RWKV_HEADS,),
        in_specs=[proj(COL_R), proj(COL_KR), proj(COL_VR),
                  pl.BlockSpec((n, LORA_PAD), lambda h: (0, COL_LORA // LORA_PAD)),
                  wide(0), wide(RWKV_WIDTH), wide(2 * RWKV_WIDTH),
                  pl.BlockSpec((n, LORA_PAD), lambda h: (0, 0)),
                  st_spec,
                  vec, vec, vec, vec_l, vec, vec, vec, vec, vec, vec, vec,
                  pl.BlockSpec((LANES, RWKV_WIDTH), lambda h: (0, 0)),
                  pl.BlockSpec((LANES, RWKV_WIDTH), lambda h: (0, 0)),
                  pl.BlockSpec((2 * LANES, RWKV_WIDTH), lambda h: (0, 0))],
        out_specs=[pl.BlockSpec((n, RWKV_WIDTH), lambda h: (0, 0)), st_spec],
        out_shape=[jax.ShapeDtypeStruct((n, RWKV_WIDTH), BF16), jax.ShapeDtypeStruct(state_t.shape, F32)],
        scratch_shapes=[pltpu.VMEM((6, N_RWKV_HEADS, HEAD_DIM, n), F32),
                        pltpu.VMEM((4, n, RWKV_WIDTH), F32),
                        pltpu.VMEM((N_RWKV_HEADS, HEAD_DIM, n), F32)],
        compiler_params=_cparams(1),
        name="rwkv_sample",
    )(p_all, p_all, p_all, p_all, shift, shift, shift, shift_l, state_t,
      prm["mu_r"], prm["mu_k"], prm["mu_v"], prm["mu_l"], prm["w0"], prm["a0"], prm["k_k"], prm["k_a"],
      prm["r_k"], prm["gn_w"], prm["gn_b"], prm["wd"], prm["wi"], prm["wg"])


def _swap_head_order(z, outer, inner):
    n = z.shape[0]
    return z.reshape(n, outer, inner, HEAD_DIM).transpose(0, 2, 1, 3).reshape(n, outer * inner * HEAD_DIM)


def _shift_columns(p_rows):
    return p_rows[:, COL_R:COL_R + RWKV_PROJ_WIDTH]


def _forward(x_prompt, x_sample, cache_k_win, cache_v_win, state_shift, state_wkv, w, cfg):
    b, t, _ = x_prompt.shape
    n_s = x_sample.shape[0]
    row = lambda z: z.reshape(1, -1).astype(F32)
    mu = w["mu_shift"]
    wd, wi, wg = _rwkv_lora_weights(w["w_decay_up"], w["w_iclr_up"], w["w_gate_up"])
    prm = dict(
        mu_r=row(mu[:RWKV_WIDTH]), mu_k=row(mu[RWKV_WIDTH:2 * RWKV_WIDTH]), mu_v=row(mu[2 * RWKV_WIDTH:3 * RWKV_WIDTH]),
        mu_l=row(jnp.pad(mu[3 * RWKV_WIDTH:], (0, LORA_PAD - LORA_WIDTH))),
        w0=row(w["w0"]), a0=row(w["a0"]), k_k=row(w["k_k"]), k_a=row(w["k_a"]), r_k=row(w["r_k"]),
        gn_w=row(w["gn_w"]), gn_b=row(w["gn_b"]), wd=wd, wi=wi, wg=wg)
    wpa = w["w_proj_attn"].astype(BF16)
    wpr = w["w_proj_rwkv"].astype(BF16)
    wo = w["w_out"].astype(BF16)
    wu = w["w_up"].astype(BF16)
    wdn = w["w_down"].astype(BF16)
    ln1g, ln1b, ln2g, ln2b = row(w["ln1_g"]), row(w["ln1_b"]), row(w["ln2_g"]), row(w["ln2_b"])
    sinks = w["attn_sinks"].astype(F32)

    xp = x_prompt.reshape(b * t, D_MODEL)
    w_in_t = w["w_in"].T.astype(BF16)
    pp = _inproj(xp, w_in_t, cfg["tm_in"], cfg["tn_in"])
    attn_p, kwin_p, vwin_p = _attn_prompt(pp, sinks, b, t)
    rwkv_p, wkv_p = _rwkv_prompt(pp, prm, b, t, cfg["tb_rwkv"], cfg["lw_rwkv"])
    hp = _merge(xp, attn_p, rwkv_p, w_in_t, wpa, wpr, wo, ln1g, ln1b, cfg["tm_merge"], cfg["tj_merge"])
    yp = _ffn(hp, wu, wdn, ln2g, ln2b, cfg["tm_ffn"], cfg["tf_ffn"])
    shift_p = _shift_columns(pp.reshape(b, t, PACK_WIDTH)[:, t - 1])

    group = N_Q_HEADS // N_KV_HEADS
    xs = x_sample.reshape(n_s, D_MODEL)
    ps = _inproj(xs, w_in_t, n_s, cfg["tn_in"])
    q_perm = _swap_head_order(ps[:, COL_Q:COL_Q + ATTN_WIDTH], N_KV_HEADS, group)
    sink_mat = jnp.broadcast_to(sinks.reshape(N_KV_HEADS, group).T.reshape(N_Q_HEADS, 1), (N_Q_HEADS, LANES))
    win = cache_k_win.shape[1]
    to_t = lambda z: jnp.transpose(z, (0, 2, 3, 1)).reshape(n_s, KV_WIDTH, win)
    from_t = lambda z: jnp.transpose(z.reshape(n_s, N_KV_HEADS, HEAD_DIM, win), (0, 3, 1, 2))[None]
    attn_s, nk_t, nv_t = _attn_sample(q_perm, ps[:, COL_K:COL_K + 2 * KV_WIDTH].T, to_t(cache_k_win), to_t(cache_v_win),
                                      sink_mat, cfg["bt_sample"])
    attn_s = _swap_head_order(attn_s, group, N_KV_HEADS)
    shift_l = jnp.pad(state_shift[:, 3 * RWKV_WIDTH:], ((0, 0), (0, LORA_PAD - LORA_WIDTH)))
    rwkv_s, wkv_t = _rwkv_sample(ps, state_shift, shift_l, jnp.transpose(state_wkv, (1, 2, 3, 0)), prm)
    wkv_s = jnp.transpose(wkv_t, (3, 0, 1, 2))
    hs = _merge(xs, attn_s, rwkv_s, w_in_t, wpa, wpr, wo, ln1g, ln1b, n_s, cfg["tj_merge"])
    ys = _ffn(hs, wu, wdn, ln2g, ln2b, n_s, cfg["tf_ffn"])
    shift_s = _shift_columns(ps)

    kv5 = lambda z: z.reshape(1, z.shape[0], z.shape[1], N_KV_HEADS, HEAD_DIM)
    return (yp.reshape(b, t, D_MODEL), ys.reshape(n_s, 1, D_MODEL),
            kv5(kwin_p), kv5(vwin_p), shift_p[None], wkv_p[None],
            from_t(nk_t), from_t(nv_t), shift_s[None], wkv_s[None])


_CFG = dict(tm_in=1024, tn_in=1024, tb_rwkv=512, lw_rwkv=1024, tm_merge=512, tj_merge=512, tm_ffn=512, tf_ffn=1024, bt_sample=16)


def kernel(x_prompt, x_sample, cache_k_win, cache_v_win, state_shift, state_wkv, w_in, attn_sinks, mu_shift, w0,
           w_decay_up, a0, w_iclr_up, w_gate_up, k_k, k_a, r_k, gn_w, gn_b, w_proj_attn, w_proj_rwkv, w_out,
           ln1_g, ln1_b, w_up, w_down, ln2_g, ln2_b):
    w = dict(w_in=w_in[0], attn_sinks=attn_sinks[0], mu_shift=mu_shift[0], w0=w0[0], w_decay_up=w_decay_up[0],
             a0=a0[0], w_iclr_up=w_iclr_up[0], w_gate_up=w_gate_up[0], k_k=k_k[0], k_a=k_a[0], r_k=r_k[0],
             gn_w=gn_w[0], gn_b=gn_b[0], w_proj_attn=w_proj_attn[0], w_proj_rwkv=w_proj_rwkv[0], w_out=w_out[0],
             ln1_g=ln1_g[0], ln1_b=ln1_b[0], w_up=w_up[0], w_down=w_down[0], ln2_g=ln2_g[0], ln2_b=ln2_b[0])
    return _forward(x_prompt, x_sample, cache_k_win[0], cache_v_win[0], state_shift[0], state_wkv[0], w, _CFG)
```

```python
import jax
import jax.numpy as jnp
from jax import lax
from jax.experimental import pallas as pl
from jax.experimental.pallas import tpu as pltpu

F32 = jnp.float32
BF16 = jnp.bfloat16

D_MODEL = 2048
HEAD_DIM = 64
N_Q_HEADS = 16
N_KV_HEADS = 4
ATTN_WIDTH = N_Q_HEADS * HEAD_DIM
KV_WIDTH = N_KV_HEADS * HEAD_DIM
WINDOW = 128
ROPE_THETA = 500000.0
ROT_DIM = HEAD_DIM // 4
N_RWKV_HEADS = 16
RWKV_WIDTH = N_RWKV_HEADS * HEAD_DIM
DECAY_LORA = 64
ICLR_LORA = 64
GATE_LORA = 160
LORA_WIDTH = DECAY_LORA + ICLR_LORA + GATE_LORA
RWKV_PROJ_WIDTH = 3 * RWKV_WIDTH + LORA_WIDTH
D_FF = 4 * D_MODEL
PAST_LEN = 16384
DEEPNORM_ALPHA = 2.0 ** 0.25
LN_EPS = 1e-5
GN_EPS = HEAD_DIM * 1e-5
NEG_BIG = -1e30
DECAY_SCALE = 0.6065306597126334
NORM_FLOOR_SQ = 1e-24

LANES = 128
BF16_ROWS_PER_TILE = 16
HEADS_PER_GROUP = 8
CHUNK = 64
VMEM_LIMIT = 56 * 1024 * 1024

COL_Q = 0
COL_K = 1024
COL_V = 1280
COL_R = 1536
COL_KR = 2560
COL_VR = 3584
COL_LORA = 4608
LORA_PAD = 512
PACK_WIDTH = COL_LORA + LORA_PAD
GATE_ROW0 = COL_LORA + LORA_WIDTH


def _cparams(n_axes):
    return pltpu.CompilerParams(dimension_semantics=("arbitrary",) * n_axes, vmem_limit_bytes=VMEM_LIMIT)


def _sigmoid(x):
    return 1.0 / (1.0 + jnp.exp(-x))


def _layer_norm_rows(z, g, b):
    mu = jnp.mean(z, axis=-1, keepdims=True)
    d = z - mu
    var = jnp.mean(d * d, axis=-1, keepdims=True)
    return d * lax.rsqrt(var + LN_EPS) * g + b


def _dot_t(a, b):
    return lax.dot_general(a, b, (((1,), (1,)), ((), ())), preferred_element_type=F32)


def _head_block_ones():
    r = lax.broadcasted_iota(jnp.int32, (LANES, LANES), 0)
    c = lax.broadcasted_iota(jnp.int32, (LANES, LANES), 1)
    return jnp.where(r // HEAD_DIM == c // HEAD_DIM, 1.0, 0.0).astype(BF16)


def _head_sum(x, bd):
    outs = [jnp.dot(x[:, LANES * t:LANES * (t + 1)].astype(BF16), bd, preferred_element_type=F32)
            for t in range(x.shape[1] // LANES)]
    return outs[0] if len(outs) == 1 else jnp.concatenate(outs, axis=1)


def _inproj_kernel(x_ref, w_ref, o_ref, xb_ref):
    @pl.when(pl.program_id(1) == 0)
    def _():
        xb_ref[...] = x_ref[...].astype(BF16)

    o_ref[...] = _dot_t(xb_ref[...], w_ref[...]).astype(o_ref.dtype)


def _inproj(x, w_t, tm, tn):
    m, k = x.shape
    n = PACK_WIDTH
    return pl.pallas_call(
        _inproj_kernel,
        grid=(m // tm, n // tn),
        in_specs=[pl.BlockSpec((tm, k), lambda i, j: (i, 0)),
                  pl.BlockSpec((tn, k), lambda i, j: (j, 0))],
        out_specs=pl.BlockSpec((tm, tn), lambda i, j: (i, j)),
        out_shape=jax.ShapeDtypeStruct((m, n), BF16),
        scratch_shapes=[pltpu.VMEM((tm, k), BF16)],
        compiler_params=_cparams(2),
        name="inproj",
    )(x, w_t)


def _rope_tables(pos):
    inv = ROPE_THETA ** (-jnp.arange(0, ROT_DIM, 2, dtype=F32) / ROT_DIM)
    ang = pos.astype(F32)[:, None] * inv[None, :]
    cos, sin = jnp.cos(ang), jnp.sin(ang)
    t = pos.shape[0]
    half = ROT_DIM // 2
    pad = HEAD_DIM - ROT_DIM
    c_head = jnp.concatenate([cos, cos, jnp.ones((t, pad), F32)], axis=1)
    sp_head = jnp.concatenate([-sin, jnp.zeros((t, half + pad), F32)], axis=1)
    sm_head = jnp.concatenate([jnp.zeros((t, half), F32), sin, jnp.zeros((t, pad), F32)], axis=1)
    rep = LANES // HEAD_DIM
    return tuple(jnp.tile(z, (1, rep)) for z in (c_head, sp_head, sm_head))


def _rope_tile(x, c, sp, sm):
    half = ROT_DIM // 2
    return x * c + pltpu.roll(x, LANES - half, 1) * sp + pltpu.roll(x, half, 1) * sm


def _attn_prompt_kernel(q_ref, k_ref, v_ref, c_ref, sp_ref, sm_ref, sink_ref,
                        o_ref, kwin_ref, vwin_ref, kp_ref, vp_ref):
    i = pl.program_id(1)
    blk = WINDOW

    @pl.when(i == 0)
    def _():
        kp_ref[...] = jnp.zeros_like(kp_ref)
        vp_ref[...] = jnp.zeros_like(vp_ref)

    c, sp, sm = c_ref[...], sp_ref[...], sm_ref[...]
    lane = lax.broadcasted_iota(jnp.int32, (2 * blk, LANES), 1)
    lo2 = lane < HEAD_DIM
    lo1 = lax.broadcasted_iota(jnp.int32, (blk, LANES), 1) < HEAD_DIM

    own = lax.broadcasted_iota(jnp.int32, (blk, blk), 1) <= lax.broadcasted_iota(jnp.int32, (blk, blk), 0)
    no_prev = jnp.where(i == 0, NEG_BIG, 0.0)
    zero_p = jnp.zeros((blk, blk), BF16)

    n_kv_tiles = KV_WIDTH // LANES
    kk_g, va_g, vb_g = [], [], []
    for t in range(n_kv_tiles):
        sl = slice(LANES * t, LANES * (t + 1))
        kcur = _rope_tile(k_ref[:, sl].astype(F32), c, sp, sm)
        vcur = v_ref[:, sl].astype(F32)
        kwin_ref[0, :, sl] = kcur
        vwin_ref[0, :, sl] = vcur
        kall = jnp.concatenate([kp_ref[:, sl], kcur], axis=0)
        vall = jnp.concatenate([vp_ref[:, sl], vcur], axis=0)
        kp_ref[:, sl] = kcur
        vp_ref[:, sl] = vcur
        kswap = pltpu.roll(kall, HEAD_DIM, 1)
        vswap = pltpu.roll(vall, HEAD_DIM, 1)
        kk_g += [jnp.where(lo2, kall, kswap).astype(BF16), jnp.where(lo2, kswap, kall).astype(BF16)]
        va_g += [jnp.where(lo2, vall, 0.0).astype(BF16), jnp.where(lo2, vswap, 0.0).astype(BF16)]
        vb_g += [jnp.where(lo2, 0.0, vswap).astype(BF16), jnp.where(lo2, 0.0, vall).astype(BF16)]

    group = N_Q_HEADS // N_KV_HEADS
    tiles = range(ATTN_WIDTH // LANES)
    half = ROT_DIM // 2
    pr = lax.broadcasted_iota(jnp.int32, (LANES, LANES), 0)
    pc = lax.broadcasted_iota(jnp.int32, (LANES, LANES), 1)
    dcol = pc % HEAD_DIM
    partner = jnp.where(dcol < half, pc + half, jnp.where(dcol < ROT_DIM, pc - half, -1))
    perm = jnp.where(pr == partner, 1.0, 0.0).astype(BF16)
    ssum = sp + sm
    qt = []
    for j in tiles:
        xq = q_ref[:, LANES * j:LANES * (j + 1)]
        xp = jnp.dot(xq.astype(BF16), perm, preferred_element_type=F32)
        qt.append((xq.astype(F32) * c + xp * ssum) * (HEAD_DIM ** -0.5))
    for h0 in range(0, N_Q_HEADS, HEADS_PER_GROUP):
        heads = range(h0, h0 + HEADS_PER_GROUP)
        qm = {h: (jnp.where(lo1, qt[h // 2], 0.0) if h % 2 == 0 else jnp.where(lo1, 0.0, qt[h // 2])).astype(BF16) for h in heads}
        sinks = {h: sink_ref[h] for h in heads}
        s2 = {h: _dot_t(qm[h], kk_g[h // group]) for h in heads}
        s = {h: jnp.where(own, s2[h][:, blk:], s2[h][:, :blk] + no_prev) for h in heads}
        m = {h: jnp.maximum(jnp.max(s[h], axis=1, keepdims=True), sinks[h]) for h in heads}
        p = {h: jnp.exp(s[h] - m[h]) for h in heads}
        den = {h: jnp.sum(p[h], axis=1, keepdims=True) + jnp.exp(sinks[h] - m[h]) for h in heads}
        pb = {h: p[h].astype(BF16) for h in heads}
        p2 = {h: jnp.concatenate([jnp.where(own, zero_p, pb[h]), jnp.where(own, pb[h], zero_p)], axis=1) for h in heads}
        o = {h: jnp.dot(p2[h], (va_g if h % 2 == 0 else vb_g)[h // group], preferred_element_type=F32) / den[h]
             for h in heads}
        for j in range(h0 // 2, (h0 + HEADS_PER_GROUP) // 2):
            o_ref[:, LANES * j:LANES * (j + 1)] = (o[2 * j] + o[2 * j + 1]).astype(o_ref.dtype)


def _attn_prompt(p_all, sinks, b, t):
    blk = WINDOW
    nb = t // blk
    c, sp, sm = _rope_tables(jnp.arange(t, dtype=jnp.int32))
    tab_spec = pl.BlockSpec((blk, LANES), lambda bi, i: (i, 0))
    row = lambda bi, i: bi * nb + i
    return pl.pallas_call(
        _attn_prompt_kernel,
        grid=(b, nb),
        in_specs=[pl.BlockSpec((blk, ATTN_WIDTH), lambda bi, i: (row(bi, i), COL_Q // ATTN_WIDTH)),
                  pl.BlockSpec((blk, KV_WIDTH), lambda bi, i: (row(bi, i), COL_K // KV_WIDTH)),
                  pl.BlockSpec((blk, KV_WIDTH), lambda bi, i: (row(bi, i), COL_V // KV_WIDTH)),
                  tab_spec, tab_spec, tab_spec,
                  pl.BlockSpec(memory_space=pltpu.SMEM)],
        out_specs=[pl.BlockSpec((blk, ATTN_WIDTH), lambda bi, i: (row(bi, i), 0)),
                   pl.BlockSpec((1, blk, KV_WIDTH), lambda bi, i: (bi, 0, 0)),
                   pl.BlockSpec((1, blk, KV_WIDTH), lambda bi, i: (bi, 0, 0))],
        out_shape=[jax.ShapeDtypeStruct((b * t, ATTN_WIDTH), BF16),
                   jax.ShapeDtypeStruct((b, blk, KV_WIDTH), F32),
                   jax.ShapeDtypeStruct((b, blk, KV_WIDTH), F32)],
        scratch_shapes=[pltpu.VMEM((blk, KV_WIDTH), F32), pltpu.VMEM((blk, KV_WIDTH), F32)],
        compiler_params=_cparams(2),
        name="attn_prompt",
    )(p_all, p_all, p_all, c, sp, sm, sinks)


def _rwkv_prepare(r_in, k_in, v_in, l_in, pr, pk, pv, plr, prm, wd_ref, wi_ref, wg_ref, bd2):
    mur, muk, muv, mul, w0, a0, k_k, k_a = prm
    r = r_in + (pr - r_in) * mur
    k = k_in + (pk - k_in) * muk
    v = v_in + (pv - v_in) * muv
    ls = l_in + (plr - l_in) * mul
    l0 = ls[:, :LANES]
    lane = lax.broadcasted_iota(jnp.int32, l0.shape, 1)
    z0 = jnp.where(lane < DECAY_LORA, jnp.tanh(l0), l0).astype(BF16)
    zg = _sigmoid(ls[:, LANES:3 * LANES]).astype(BF16)
    dec_up = jnp.dot(z0, wd_ref[...], preferred_element_type=F32)
    icl_up = jnp.dot(z0, wi_ref[...], preferred_element_type=F32)
    gate = jnp.dot(zg, wg_ref[...], preferred_element_type=F32)
    logw = -DECAY_SCALE * _sigmoid(w0 + dec_up)
    a_sig = _sigmoid(a0 + icl_up)
    kk = k * k_k
    kk = kk * lax.rsqrt(jnp.maximum(_head_sum(kk * kk, bd2), NORM_FLOOR_SQ))
    k = k * (1.0 + (a_sig - 1.0) * k_a)
    return r, k, v, logw, -kk, kk * a_sig, gate


def _rwkv_finish(y, r, k, v, gate, r_k, gn_w, gn_b, bd2):
    mu = _head_sum(y, bd2) * (1.0 / HEAD_DIM)
    d = y - mu
    var = _head_sum(d * d, bd2) * (1.0 / HEAD_DIM)
    yn = d * lax.rsqrt(var + GN_EPS) * gn_w + gn_b
    bonus = _head_sum(r * k * r_k, bd2) * v
    return (yn + bonus) * gate


def _rwkv_prompt_kernel(r_ref, k_ref, v_ref, l_ref,
                        mur_ref, muk_ref, muv_ref, mul_ref, w0_ref, a0_ref, kk_ref, ka_ref,
                        rk_ref, gnw_ref, gnb_ref, wd_ref, wi_ref, wg_ref,
                        o_ref, sout_ref,
                        cr_ref, ck_ref, cv_ref, cl_ref, st_ref, y_ref):
    ti = pl.program_id(2)
    tb, lw = r_ref.shape
    n_pair = lw // LANES
    n_chunk = tb // CHUNK

    @pl.when(ti == 0)
    def _():
        for ref in (cr_ref, ck_ref, cv_ref, cl_ref, st_ref):
            ref[...] = jnp.zeros_like(ref)

    def shifted(x, carry_ref):
        rolled = pltpu.roll(x, 1, 0)
        row = lax.broadcasted_iota(jnp.int32, x.shape, 0)
        prev = jnp.where(row == 0, carry_ref[0:1, :], rolled)
        carry_ref[0:1, :] = x[tb - 1:tb, :]
        return prev

    bd2 = _head_block_ones()
    r_in, k_in, v_in, l_in = (ref[...].astype(F32) for ref in (r_ref, k_ref, v_ref, l_ref))
    prm = tuple(ref[...] for ref in (mur_ref, muk_ref, muv_ref, mul_ref, w0_ref, a0_ref, kk_ref, ka_ref))
    r, k, v, logw, a_s, b_s, gate = _rwkv_prepare(
        r_in, k_in, v_in, l_in,
        shifted(r_in, cr_ref), shifted(k_in, ck_ref), shifted(v_in, cv_ref), shifted(l_in, cl_ref),
        prm, wd_ref, wi_ref, wg_ref, bd2)

    c = CHUNK
    ri = lax.broadcasted_iota(jnp.int32, (c, 3 * c), 0)
    ci = lax.broadcasted_iota(jnp.int32, (c, 3 * c), 1) % c
    tri3 = jnp.where(ri >= ci, 1.0, 0.0).astype(BF16)
    r2 = lax.broadcasted_iota(jnp.int32, (2 * c, LANES), 0)
    l2 = lax.broadcasted_iota(jnp.int32, (2 * c, LANES), 1)
    tt, ss = r2 % c, l2 % c
    causal = ss < tt + r2 // c
    lo1 = lax.broadcasted_iota(jnp.int32, (c, LANES), 1) < HEAD_DIM
    diag_blocks = (r2 // HEAD_DIM) == (l2 // HEAD_DIM)
    eye_side = jnp.where(lax.broadcasted_iota(jnp.int32, (c, LANES), 0) == lax.broadcasted_iota(jnp.int32, (c, LANES), 1) % c,
                         1.0, 0.0)

    def stack_heads(x):
        return jnp.concatenate([jnp.where(lo1, x, 0.0), jnp.where(lo1, 0.0, x)], axis=0)

    def dot_t(a, b):
        return lax.dot_general(a, b, (((1,), (1,)), ((), ())), preferred_element_type=F32)

    def dot_tt(a, b):
        return lax.dot_general(a, b, (((0,), (0,)), ((), ())), preferred_element_type=F32)

    def dot(a, b):
        return jnp.dot(a.astype(BF16), b.astype(BF16), preferred_element_type=F32)

    lanes = [slice(LANES * pi, LANES * (pi + 1)) for pi in range(n_pair)]
    units = [(ch, pi) for ch in range(n_chunk) for pi in range(n_pair)]
    un = range(len(units))
    el = []
    for ch in range(n_chunk):
        rows = slice(c * ch, c * (ch + 1))
        lw_c = logw[rows]
        hi = lw_c.astype(BF16)
        rem = lw_c - hi.astype(F32)
        mid = rem.astype(BF16)
        low = (rem - mid.astype(F32)).astype(BF16)
        lcum = jnp.dot(tri3, jnp.concatenate([hi, mid, low], axis=0), preferred_element_type=F32)
        ltot = lcum[c - 1:c, :]
        p_inv = jnp.exp(-lcum)
        p_tail = jnp.exp(ltot - lcum)
        el.append(dict(aq=a_s[rows] * jnp.exp(lcum - lw_c), rq=r[rows] * jnp.exp(lcum), bk=b_s[rows] * p_inv,
                       kq=k[rows] * p_inv, bt=b_s[rows] * p_tail, kt=k[rows] * p_tail, v=v[rows],
                       p_end=jnp.exp(ltot)))
    op = lambda name, u: el[units[u][0]][name][:, lanes[units[u][1]]]

    ar = [jnp.concatenate([op("aq", u), op("rq", u)], axis=0).astype(BF16) for u in un]
    gb = [jnp.where(causal, dot_t(ar[u], stack_heads(op("bk", u)).astype(BF16)), 0.0) for u in un]
    gk = [jnp.where(causal, dot_t(ar[u], stack_heads(op("kq", u)).astype(BF16)), 0.0) for u in un]
    gv = [dot(gk[u], stack_heads(op("v", u))) for u in un]
    pw = [gb[u][:c] for u in un]
    tm = [eye_side + pw[u] for u in un]
    for level in range(1, 6):
        bd = [stack_heads(pw[u]).astype(BF16) for u in un]
        if level == 1:
            pw = [jnp.dot(pw[u].astype(BF16), bd[u], preferred_element_type=F32) for u in un]
            bd = [stack_heads(pw[u]).astype(BF16) for u in un]
        if level < 5:
            both = [jnp.dot(jnp.concatenate([tm[u], pw[u]], axis=0).astype(BF16), bd[u], preferred_element_type=F32)
                    for u in un]
            tm = [tm[u] + both[u][:c] for u in un]
            pw = [both[u][c:] for u in un]
        else:
            tm = [tm[u] + jnp.dot(tm[u].astype(BF16), bd[u], preferred_element_type=F32) for u in un]
    tax = [dot(tm[u], jnp.concatenate([stack_heads(op("aq", u)), stack_heads(gv[u][:c])], axis=1)) for u in un]
    taq = [tax[u][:, :LANES] for u in un]
    txv = [tax[u][:, LANES:] for u in un]
    arx = [dot(gb[u][c:], jnp.concatenate([stack_heads(taq[u]), stack_heads(txv[u])], axis=1)) for u in un]
    mb = [jnp.where(diag_blocks, dot_tt(op("bt", u).astype(BF16), taq[u].astype(BF16)), 0.0).astype(BF16) for u in un]
    cct = [jnp.where(diag_blocks,
                     dot_tt(jnp.concatenate([txv[u], op("v", u)], axis=0).astype(BF16),
                            jnp.concatenate([op("bt", u), op("kt", u)], axis=0).astype(BF16)), 0.0) for u in un]
    rqp = [(op("rq", u) + arx[u][:, :LANES]).astype(BF16) for u in un]
    yc = [gv[u][c:] + arx[u][:, LANES:] for u in un]
    for u, (ch, pi) in enumerate(units):
        s_old = st_ref[pi]
        sb = s_old.astype(BF16)
        y_ref[c * ch:c * (ch + 1), lanes[pi]] = dot_t(rqp[u], sb) + yc[u]
        st_ref[pi] = s_old * op("p_end", u) + dot_t(sb, mb[u]) + cct[u]

    out = _rwkv_finish(y_ref[...], r, k, v, gate, rk_ref[...], gnw_ref[...], gnb_ref[...], bd2)
    o_ref[...] = out.astype(o_ref.dtype)

    @pl.when(ti == pl.num_programs(2) - 1)
    def _():
        for pi in range(n_pair):
            s = st_ref[pi]
            sout_ref[0, 2 * pi] = s[:HEAD_DIM, :HEAD_DIM]
            sout_ref[0, 2 * pi + 1] = s[HEAD_DIM:, HEAD_DIM:]


def _rwkv_lora_weights(w_decay_up, w_iclr_up, w_gate_up):
    z64 = jnp.zeros((DECAY_LORA, RWKV_WIDTH), F32)
    wd = jnp.concatenate([w_decay_up, z64], axis=0).astype(BF16)
    wi = jnp.concatenate([z64, w_iclr_up], axis=0).astype(BF16)
    wg = jnp.concatenate([w_gate_up, jnp.zeros((2 * LANES - GATE_LORA, RWKV_WIDTH), F32)], axis=0).astype(BF16)
    return wd, wi, wg


def _rwkv_prompt(p_all, prm, b, t, tb, lw):
    nt = t // tb
    ns = RWKV_WIDTH // lw
    row = lambda bi, si, ti: bi * nt + ti
    col_spec = lambda col0: pl.BlockSpec(
        (pl.Element(tb), pl.Element(lw)),
        lambda bi, si, ti: (pl.multiple_of(row(bi, si, ti) * tb, tb), pl.multiple_of(col0 + si * lw, LANES)))
    vec = pl.BlockSpec((1, lw), lambda bi, si, ti: (0, si))
    vec_l = pl.BlockSpec((1, LORA_PAD), lambda bi, si, ti: (0, 0))
    return pl.pallas_call(
        _rwkv_prompt_kernel,
        grid=(b, ns, nt),
        in_specs=[col_spec(COL_R), col_spec(COL_KR), col_spec(COL_VR),
                  pl.BlockSpec((tb, LORA_PAD), lambda bi, si, ti: (row(bi, si, ti), COL_LORA // LORA_PAD)),
                  vec, vec, vec, vec_l, vec, vec, vec, vec, vec, vec, vec,
                  pl.BlockSpec((LANES, lw), lambda bi, si, ti: (0, si)),
                  pl.BlockSpec((LANES, lw), lambda bi, si, ti: (0, si)),
                  pl.BlockSpec((2 * LANES, lw), lambda bi, si, ti: (0, si))],
        out_specs=[pl.BlockSpec((tb, lw), lambda bi, si, ti: (row(bi, si, ti), si)),
                   pl.BlockSpec((1, 2 * (lw // LANES), HEAD_DIM, HEAD_DIM), lambda bi, si, ti: (bi, si, 0, 0))],
        out_shape=[jax.ShapeDtypeStruct((b * t, RWKV_WIDTH), BF16),
                   jax.ShapeDtypeStruct((b, N_RWKV_HEADS, HEAD_DIM, HEAD_DIM), F32)],
        scratch_shapes=[pltpu.VMEM((8, lw), F32), pltpu.VMEM((8, lw), F32), pltpu.VMEM((8, lw), F32),
                        pltpu.VMEM((8, LORA_PAD), F32),
                        pltpu.VMEM((lw // LANES, LANES, LANES), F32),
                        pltpu.VMEM((tb, lw), F32)],
        compiler_params=_cparams(3),
        name="rwkv_prompt",
    )(p_all, p_all, p_all, p_all,
      prm["mu_r"], prm["mu_k"], prm["mu_v"], prm["mu_l"], prm["w0"], prm["a0"], prm["k_k"], prm["k_a"],
      prm["r_k"], prm["gn_w"], prm["gn_b"], prm["wd"], prm["wi"], prm["wg"])


def _merge_kernel(x_ref, ao_ref, ro_ref, wga_ref, wgb_ref, wpa_ref, wpr_ref, wo_ref, g_ref, b_ref, h_ref,
                  xb_ref, acc_ref):
    j = pl.program_id(1)

    @pl.when(j == 0)
    def _():
        xb_ref[...] = x_ref[...].astype(BF16)
        acc_ref[...] = jnp.zeros_like(acc_ref)

    xb = xb_ref[...]
    ga = _dot_t(xb, wga_ref[...])
    gb = _dot_t(xb, wgb_ref[...])
    a = jnp.dot(ao_ref[...], wpa_ref[...], preferred_element_type=F32)
    r = jnp.dot(ro_ref[...], wpr_ref[...], preferred_element_type=F32)
    m = _sigmoid(ga) * a + _sigmoid(gb) * r
    acc_ref[...] += jnp.dot(m.astype(BF16), wo_ref[...], preferred_element_type=F32)

    @pl.when(j == pl.num_programs(1) - 1)
    def _():
        z = DEEPNORM_ALPHA * x_ref[...] + acc_ref[...]
        h_ref[...] = _layer_norm_rows(z, g_ref[...], b_ref[...])


def _merge(x, attn_o, rwkv_o, w_in_t, wpa, wpr, wo, ln_g, ln_b, tm, tj):
    m = x.shape[0]
    nj = D_MODEL // tj
    gate_spec = lambda row0: pl.BlockSpec(
        (pl.Element(tj), pl.Element(D_MODEL)),
        lambda i, j: (pl.multiple_of(row0 + j * tj, BF16_ROWS_PER_TILE), 0))
    return pl.pallas_call(
        _merge_kernel,
        grid=(m // tm, nj),
        in_specs=[pl.BlockSpec((tm, D_MODEL), lambda i, j: (i, 0)),
                  pl.BlockSpec((tm, ATTN_WIDTH), lambda i, j: (i, 0)),
                  pl.BlockSpec((tm, RWKV_WIDTH), lambda i, j: (i, 0)),
                  gate_spec(GATE_ROW0), gate_spec(GATE_ROW0 + D_MODEL),
                  pl.BlockSpec((ATTN_WIDTH, tj), lambda i, j: (0, j)),
                  pl.BlockSpec((RWKV_WIDTH, tj), lambda i, j: (0, j)),
                  pl.BlockSpec((tj, D_MODEL), lambda i, j: (j, 0)),
                  pl.BlockSpec((1, D_MODEL), lambda i, j: (0, 0)),
                  pl.BlockSpec((1, D_MODEL), lambda i, j: (0, 0))],
        out_specs=pl.BlockSpec((tm, D_MODEL), lambda i, j: (i, 0)),
        out_shape=jax.ShapeDtypeStruct((m, D_MODEL), F32),
        scratch_shapes=[pltpu.VMEM((tm, D_MODEL), BF16), pltpu.VMEM((tm, D_MODEL), F32)],
        compiler_params=_cparams(2),
        name="merge_ln1",
    )(x, attn_o, rwkv_o, w_in_t, w_in_t, wpa, wpr, wo, ln_g, ln_b)


def _ffn_kernel(h_ref, wu_ref, wd_ref, g_ref, b_ref, y_ref, hb_ref, acc_ref):
    f = pl.program_id(1)

    @pl.when(f == 0)
    def _():
        hb_ref[...] = h_ref[...].astype(BF16)
        acc_ref[...] = jnp.zeros_like(acc_ref)

    u = jnp.dot(hb_ref[...], wu_ref[...], preferred_element_type=F32)
    u = jnp.square(jnp.maximum(u, 0.0))
    acc_ref[...] += jnp.dot(u.astype(BF16), wd_ref[...], preferred_element_type=F32)

    @pl.when(f == pl.num_programs(1) - 1)
    def _():
        z = DEEPNORM_ALPHA * h_ref[...] + acc_ref[...]
        y_ref[...] = _layer_norm_rows(z, g_ref[...], b_ref[...])


def _ffn(h, wu, wd, ln_g, ln_b, tm, tf):
    m = h.shape[0]
    return pl.pallas_call(
        _ffn_kernel,
        grid=(m // tm, D_FF // tf),
        in_specs=[pl.BlockSpec((tm, D_MODEL), lambda i, f: (i, 0)),
                  pl.BlockSpec((D_MODEL, tf), lambda i, f: (0, f)),
                  pl.BlockSpec((tf, D_MODEL), lambda i, f: (f, 0)),
                  pl.BlockSpec((1, D_MODEL), lambda i, f: (0, 0)),
                  pl.BlockSpec((1, D_MODEL), lambda i, f: (0, 0))],
        out_specs=pl.BlockSpec((tm, D_MODEL), lambda i, f: (i, 0)),
        out_shape=jax.ShapeDtypeStruct((m, D_MODEL), F32),
        scratch_shapes=[pltpu.VMEM((tm, D_MODEL), BF16), pltpu.VMEM((tm, D_MODEL), F32)],
        compiler_params=_cparams(2),
        name="ffn_ln2",
    )(h, wu, wd, ln_g, ln_b)


def _attn_sample_kernel(q_ref, kvt_ref, ck_ref, cv_ref, c_ref, sp_ref, sm_ref, cc_ref, spc_ref, smc_ref, sink_ref,
                        o_ref, nk_ref, nv_ref):
    bt = q_ref.shape[0]
    win = ck_ref.shape[2]
    n = kvt_ref.shape[1]
    half = ROT_DIM // 2
    group = N_Q_HEADS // N_KV_HEADS
    c, sp, sm = c_ref[0:1, :], sp_ref[0:1, :], sm_ref[0:1, :]
    sink = sink_ref[:, 0:1]
    row16 = lax.broadcasted_iota(jnp.int32, (N_Q_HEADS, KV_WIDTH), 0)
    lane16 = lax.broadcasted_iota(jnp.int32, (N_Q_HEADS, KV_WIDTH), 1)
    own_kv = (lane16 // HEAD_DIM) == (row16 % N_KV_HEADS)
    urow = row16 // N_KV_HEADS
    pos = lax.broadcasted_iota(jnp.int32, (KV_WIDTH, win), 1)
    seq = lax.broadcasted_iota(jnp.int32, (KV_WIDTH, n), 1)

    kt = kvt_ref[0:KV_WIDTH, :]
    kt = kt * cc_ref[...] + pltpu.roll(kt, KV_WIDTH - half, 0) * spc_ref[...] + pltpu.roll(kt, half, 0) * smc_ref[...]
    vt = kvt_ref[KV_WIDTH:2 * KV_WIDTH, :]

    def rope_row(x):
        return jnp.concatenate([_rope_tile(x[:, LANES * t:LANES * (t + 1)], c, sp, sm)
                                for t in range(x.shape[1] // LANES)], axis=1)

    rng = range(bt)
    me = [seq == pl.program_id(0) * bt + b for b in rng]
    k_col = [jnp.sum(jnp.where(me[b], kt, 0.0), axis=1, keepdims=True) for b in rng]
    v_col = [jnp.sum(jnp.where(me[b], vt, 0.0), axis=1, keepdims=True) for b in rng]
    nk = [jnp.where(pos == win - 1, k_col[b], pltpu.roll(ck_ref[b], win - 1, 1)) for b in rng]
    nv = [jnp.where(pos == win - 1, v_col[b], pltpu.roll(cv_ref[b], win - 1, 1)) for b in rng]
    for b in rng:
        nk_ref[b] = nk[b]
        nv_ref[b] = nv[b]
    qmat = []
    for b in rng:
        q = rope_row(q_ref[b:b + 1, :]) * (HEAD_DIM ** -0.5)
        qb = [jnp.broadcast_to(q[:, KV_WIDTH * u:KV_WIDTH * (u + 1)], (N_Q_HEADS, KV_WIDTH)) for u in range(group)]
        qsel = jnp.where(urow == 0, qb[0], jnp.where(urow == 1, qb[1], jnp.where(urow == 2, qb[2], qb[3])))
        qmat.append(jnp.where(own_kv, qsel, 0.0).astype(BF16))
    s = [jnp.dot(qmat[b], nk[b].astype(BF16), preferred_element_type=F32) for b in rng]
    m = [jnp.maximum(jnp.max(s[b], axis=1, keepdims=True), sink) for b in rng]
    p = [jnp.exp(s[b] - m[b]) for b in rng]
    den = [jnp.sum(p[b], axis=1, keepdims=True) + jnp.exp(sink - m[b]) for b in rng]
    o = [_dot_t(p[b].astype(BF16), nv[b].astype(BF16)) / den[b] for b in rng]
    out_rows = []
    for b in rng:
        ob = jnp.where(own_kv, o[b], 0.0)
        chunks = [jnp.sum(jnp.where(urow == u, ob, 0.0), axis=0, keepdims=True) for u in range(group)]
        out_rows.append(jnp.concatenate(chunks, axis=1))
    o_ref[...] = jnp.concatenate(out_rows, axis=0).astype(o_ref.dtype)


def _attn_sample(q_perm, kv_new_t, cache_kt, cache_vt, sink_mat, bt):
    n, win = cache_kt.shape[0], cache_kt.shape[2]
    tabs = _rope_tables(jnp.full((1,), PAST_LEN, jnp.int32))
    c, sp, sm = (jnp.broadcast_to(z, (8, LANES)) for z in tabs)
    cc, spc, smc = (jnp.broadcast_to(jnp.tile(z, (1, KV_WIDTH // LANES)).T, (KV_WIDTH, n)) for z in tabs)
    small = lambda shape: pl.BlockSpec(shape, lambda i: (0, 0))
    cache_spec = pl.BlockSpec((bt, KV_WIDTH, win), lambda i: (i, 0, 0))
    return pl.pallas_call(
        _attn_sample_kernel,
        grid=(n // bt,),
        in_specs=[pl.BlockSpec((bt, ATTN_WIDTH), lambda i: (i, 0)),
                  small((2 * KV_WIDTH, n)),
                  cache_spec, cache_spec,
                  small((8, LANES)), small((8, LANES)), small((8, LANES)),
                  small((KV_WIDTH, n)), small((KV_WIDTH, n)), small((KV_WIDTH, n)),
                  small((N_Q_HEADS, LANES))],
        out_specs=[pl.BlockSpec((bt, ATTN_WIDTH), lambda i: (i, 0)), cache_spec, cache_spec],
        out_shape=[jax.ShapeDtypeStruct((n, ATTN_WIDTH), BF16),
                   jax.ShapeDtypeStruct(cache_kt.shape, F32),
                   jax.ShapeDtypeStruct(cache_vt.shape, F32)],
        compiler_params=_cparams(1),
        name="attn_sample",
    )(q_perm, kv_new_t, cache_kt, cache_vt, c, sp, sm, cc, spc, smc, sink_mat)


def _rwkv_sample_kernel(r_ref, k_ref, v_ref, l_ref, pr_ref, pk_ref, pv_ref, pl_ref, st_ref,
                        mur_ref, muk_ref, muv_ref, mul_ref, w0_ref, a0_ref, kk_ref, ka_ref,
                        rk_ref, gnw_ref, gnb_ref, wd_ref, wi_ref, wg_ref,
                        o_ref, ns_ref, vec_s, keep_s, y_s):
    h = pl.program_id(0)
    hd = HEAD_DIM
    q_a, q_w, q_b, q_k, q_r, q_v = range(6)

    @pl.when(h == 0)
    def _():
        bd2 = _head_block_ones()
        prm = tuple(ref[...] for ref in (mur_ref, muk_ref, muv_ref, mul_ref, w0_ref, a0_ref, kk_ref, ka_ref))
        r, k, v, logw, a_s, b_s, gate = _rwkv_prepare(
            r_ref[...].astype(F32), k_ref[...].astype(F32), v_ref[...].astype(F32), l_ref[...].astype(F32),
            pr_ref[...], pk_ref[...], pv_ref[...], pl_ref[...],
            prm, wd_ref, wi_ref, wg_ref, bd2)
        for qi, x in enumerate((a_s, jnp.exp(logw), b_s, k, r, v)):
            xt = x.T
            for hh in range(N_RWKV_HEADS):
                vec_s[qi, hh] = xt[hd * hh:hd * (hh + 1), :]
        for qi, x in enumerate((r, k, v, gate)):
            keep_s[qi] = x

    a_h, w_h, b_h, k_h, r_h = (vec_s[qi, h] for qi in (q_a, q_w, q_b, q_k, q_r))
    for i in range(hd):
        s = st_ref[0, i]
        sa = jnp.sum(s * a_h, axis=0, keepdims=True)
        s_new = s * w_h + sa * b_h + vec_s[q_v, h, i:i + 1, :] * k_h
        ns_ref[0, i] = s_new
        y_s[h, i:i + 1, :] = jnp.sum(s_new * r_h, axis=0, keepdims=True)

    @pl.when(h == pl.num_programs(0) - 1)
    def _():
        y = jnp.concatenate([y_s[hh] for hh in range(N_RWKV_HEADS)], axis=0).T
        out = _rwkv_finish(y, keep_s[0], keep_s[1], keep_s[2], keep_s[3], rk_ref[...], gnw_ref[...], gnb_ref[...],
                           _head_block_ones())
        o_ref[...] = out.astype(o_ref.dtype)


def _rwkv_sample(p_all, shift, shift_l, state_t, prm):
    n = state_t.shape[-1]
    wide = lambda col0: pl.BlockSpec((n, RWKV_WIDTH), lambda h: (0, col0 // RWKV_WIDTH))
    proj = lambda col0: pl.BlockSpec((pl.Element(n), pl.Element(RWKV_WIDTH)), lambda h: (0, pl.multiple_of(col0, LANES)))
    vec = pl.BlockSpec((1, RWKV_WIDTH), lambda h: (0, 0))
    vec_l = pl.BlockSpec((1, LORA_PAD), lambda h: (0, 0))
    st_spec = pl.BlockSpec((1, HEAD_DIM, HEAD_DIM, n), lambda h: (h, 0, 0, 0))
    return pl.pallas_call(
        _rwkv_sample_kernel,
        grid=(N_RWKV_HEADS,),
        in_specs=[proj(COL_R), proj(COL_KR), proj(COL_VR),
                  pl.BlockSpec((n, LORA_PAD), lambda h: (0, COL_LORA // LORA_PAD)),
                  wide(0), wide(RWKV_WIDTH), wide(2 * RWKV_WIDTH),
                  pl.BlockSpec((n, LORA_PAD), lambda h: (0, 0)),
                  st_spec,
                  vec, vec, vec, vec_l, vec, vec, vec, vec, vec, vec, vec,
                  pl.BlockSpec((LANES, RWKV_WIDTH), lambda h: (0, 0)),
                  pl.BlockSpec((LANES, RWKV_WIDTH), lambda h: (0, 0)),
                  pl.BlockSpec((2 * LANES, RWKV_WIDTH), lambda h: (0, 0))],
        out_specs=[pl.BlockSpec((n, RWKV_WIDTH), lambda h: (0, 0)), st_spec],
        out_shape=[jax.ShapeDtypeStruct((n, RWKV_WIDTH), BF16), jax.ShapeDtypeStruct(state_t.shape, F32)],
        scratch_shapes=[pltpu.VMEM((6, N_RWKV_HEADS, HEAD_DIM, n), F32),
                        pltpu.VMEM((4, n, RWKV_WIDTH), F32),
                        pltpu.VMEM((N_RWKV_HEADS, HEAD_DIM, n), F32)],
        compiler_params=_cparams(1),
        name="rwkv_sample",
    )(p_all, p_all, p_all, p_all, shift, shift, shift, shift_l, state_t,
      prm["mu_r"], prm["mu_k"], prm["mu_v"], prm["mu_l"], prm["w0"], prm["a0"], prm["k_k"], prm["k_a"],
      prm["r_k"], prm["gn_w"], prm["gn_b"], prm["wd"], prm["wi"], prm["wg"])


def _swap_head_order(z, outer, inner):
    n = z.shape[0]
    return z.reshape(n, outer, inner, HEAD_DIM).transpose(0, 2, 1, 3).reshape(n, outer * inner * HEAD_DIM)


def _shift_columns(p_rows):
    return p_rows[:, COL_R:COL_R + RWKV_PROJ_WIDTH].astype(F32)


def _forward(x_prompt, x_sample, cache_k_win, cache_v_win, state_shift, state_wkv, w, cfg):
    b, t, _ = x_prompt.shape
    n_s = x_sample.shape[0]
    row = lambda z: z.reshape(1, -1).astype(F32)
    mu = w["mu_shift"]
    wd, wi, wg = _rwkv_lora_weights(w["w_decay_up"], w["w_iclr_up"], w["w_gate_up"])
    prm = dict(
        mu_r=row(mu[:RWKV_WIDTH]), mu_k=row(mu[RWKV_WIDTH:2 * RWKV_WIDTH]), mu_v=row(mu[2 * RWKV_WIDTH:3 * RWKV_WIDTH]),
        mu_l=row(jnp.pad(mu[3 * RWKV_WIDTH:], (0, LORA_PAD - LORA_WIDTH))),
        w0=row(w["w0"]), a0=row(w["a0"]), k_k=row(w["k_k"]), k_a=row(w["k_a"]), r_k=row(w["r_k"]),
        gn_w=row(w["gn_w"]), gn_b=row(w["gn_b"]), wd=wd, wi=wi, wg=wg)
    wpa = w["w_proj_attn"].astype(BF16)
    wpr = w["w_proj_rwkv"].astype(BF16)
    wo = w["w_out"].astype(BF16)
    wu = w["w_up"].astype(BF16)
    wdn = w["w_down"].astype(BF16)
    ln1g, ln1b, ln2g, ln2b = row(w["ln1_g"]), row(w["ln1_b"]), row(w["ln2_g"]), row(w["ln2_b"])
    sinks = w["attn_sinks"].astype(F32)

    xp = x_prompt.reshape(b * t, D_MODEL)
    w_in_t = w["w_in"].T.astype(BF16)
    pp = _inproj(xp, w_in_t, cfg["tm_in"], cfg["tn_in"])
    attn_p, kwin_p, vwin_p = _attn_prompt(pp, sinks, b, t)
    rwkv_p, wkv_p = _rwkv_prompt(pp, prm, b, t, cfg["tb_rwkv"], cfg["lw_rwkv"])
    hp = _merge(xp, attn_p, rwkv_p, w_in_t, wpa, wpr, wo, ln1g, ln1b, cfg["tm_merge"], cfg["tj_merge"])
    yp = _ffn(hp, wu, wdn, ln2g, ln2b, cfg["tm_ffn"], cfg["tf_ffn"])
    shift_p = _shift_columns(pp.reshape(b, t, PACK_WIDTH)[:, t - 1])

    group = N_Q_HEADS // N_KV_HEADS
    xs = x_sample.reshape(n_s, D_MODEL)
    ps = _inproj(xs, w_in_t, n_s, cfg["tn_in"])
    q_perm = _swap_head_order(ps[:, COL_Q:COL_Q + ATTN_WIDTH].astype(F32), N_KV_HEADS, group)
    sink_mat = jnp.broadcast_to(sinks.reshape(N_KV_HEADS, group).T.reshape(N_Q_HEADS, 1), (N_Q_HEADS, LANES))
    win = cache_k_win.shape[1]
    to_t = lambda z: jnp.transpose(z, (0, 2, 3, 1)).reshape(n_s, KV_WIDTH, win)
    from_t = lambda z: jnp.transpose(z.reshape(n_s, N_KV_HEADS, HEAD_DIM, win), (0, 3, 1, 2))[None]
    attn_s, nk_t, nv_t = _attn_sample(q_perm, ps[:, COL_K:COL_K + 2 * KV_WIDTH].T.astype(F32), to_t(cache_k_win), to_t(cache_v_win),
                                      sink_mat, cfg["bt_sample"])
    attn_s = _swap_head_order(attn_s, group, N_KV_HEADS)
    shift_l = jnp.pad(state_shift[:, 3 * RWKV_WIDTH:], ((0, 0), (0, LORA_PAD - LORA_WIDTH)))
    rwkv_s, wkv_t = _rwkv_sample(ps, state_shift, shift_l, jnp.transpose(state_wkv, (1, 2, 3, 0)), prm)
    wkv_s = jnp.transpose(wkv_t, (3, 0, 1, 2))
    hs = _merge(xs, attn_s, rwkv_s, w_in_t, wpa, wpr, wo, ln1g, ln1b, n_s, cfg["tj_merge"])
    ys = _ffn(hs, wu, wdn, ln2g, ln2b, n_s, cfg["tf_ffn"])
    shift_s = _shift_columns(ps)

    kv5 = lambda z: z.reshape(1, z.shape[0], z.shape[1], N_KV_HEADS, HEAD_DIM)
    return (yp.reshape(b, t, D_MODEL), ys.reshape(n_s, 1, D_MODEL),
            kv5(kwin_p), kv5(vwin_p), shift_p[None], wkv_p[None],
            from_t(nk_t), from_t(nv_t), shift_s[None], wkv_s[None])


_CFG = dict(tm_in=1024, tn_in=1024, tb_rwkv=256, lw_rwkv=1024, tm_merge=512, tj_merge=512, tm_ffn=512, tf_ffn=1024, bt_sample=16)


def kernel(x_prompt, x_sample, cache_k_win, cache_v_win, state_shift, state_wkv, w_in, attn_sinks, mu_shift, w0,
           w_decay_up, a0, w_iclr_up, w_gate_up, k_k, k_a, r_k, gn_w, gn_b, w_proj_attn, w_proj_rwkv, w_out,
           ln1_g, ln1_b, w_up, w_down, ln2_g, ln2_b):
    w = dict(w_in=w_in[0], attn_sinks=attn_sinks[0], mu_shift=mu_shift[0], w0=w0[0], w_decay_up=w_decay_up[0],
             a0=a0[0], w_iclr_up=w_iclr_up[0], w_gate_up=w_gate_up[0], k_k=k_k[0], k_a=k_a[0], r_k=r_k[0],
             gn_w=gn_w[0], gn_b=gn_b[0], w_proj_attn=w_proj_attn[0], w_proj_rwkv=w_proj_rwkv[0], w_out=w_out[0],
             ln1_g=ln1_g[0], ln1_b=ln1_b[0], w_up=w_up[0], w_down=w_down[0], ln2_g=ln2_g[0], ln2_b=ln2_b[0])
    return _forward(x_prompt, x_sample, cache_k_win[0], cache_v_win[0], state_shift[0], state_wkv[0], w, _CFG)
```

```python
import jax
import jax.numpy as jnp
from jax import lax
from jax.experimental import pallas as pl
from jax.experimental.pallas import tpu as pltpu

F32 = jnp.float32
BF16 = jnp.bfloat16

D_MODEL = 2048
HEAD_DIM = 64
N_Q_HEADS = 16
N_KV_HEADS = 4
ATTN_WIDTH = N_Q_HEADS * HEAD_DIM
KV_WIDTH = N_KV_HEADS * HEAD_DIM
WINDOW = 128
ROPE_THETA = 500000.0
ROT_DIM = HEAD_DIM // 4
N_RWKV_HEADS = 16
RWKV_WIDTH = N_RWKV_HEADS * HEAD_DIM
DECAY_LORA = 64
ICLR_LORA = 64
GATE_LORA = 160
LORA_WIDTH = DECAY_LORA + ICLR_LORA + GATE_LORA
RWKV_PROJ_WIDTH = 3 * RWKV_WIDTH + LORA_WIDTH
D_FF = 4 * D_MODEL
PAST_LEN = 16384
DEEPNORM_ALPHA = 2.0 ** 0.25
LN_EPS = 1e-5
GN_EPS = HEAD_DIM * 1e-5
NEG_BIG = -1e30
DECAY_SCALE = 0.6065306597126334
NORM_FLOOR_SQ = 1e-24

LANES = 128
BF16_ROWS_PER_TILE = 16
HEADS_PER_GROUP = 8
CHUNK = 64
VMEM_LIMIT = 56 * 1024 * 1024

COL_Q = 0
COL_K = 1024
COL_V = 1280
COL_R = 1536
COL_KR = 2560
COL_VR = 3584
COL_LORA = 4608
LORA_PAD = 512
PACK_WIDTH = COL_LORA + LORA_PAD
GATE_ROW0 = COL_LORA + LORA_WIDTH


def _cparams(n_axes):
    return pltpu.CompilerParams(dimension_semantics=("arbitrary",) * n_axes, vmem_limit_bytes=VMEM_LIMIT)


def _sigmoid(x):
    return 1.0 / (1.0 + jnp.exp(-x))


def _layer_norm_rows(z, g, b):
    mu = jnp.mean(z, axis=-1, keepdims=True)
    d = z - mu
    var = jnp.mean(d * d, axis=-1, keepdims=True)
    return d * lax.rsqrt(var + LN_EPS) * g + b


def _dot_t(a, b):
    return lax.dot_general(a, b, (((1,), (1,)), ((), ())), preferred_element_type=F32)


def _head_block_ones():
    r = lax.broadcasted_iota(jnp.int32, (LANES, LANES), 0)
    c = lax.broadcasted_iota(jnp.int32, (LANES, LANES), 1)
    return jnp.where(r // HEAD_DIM == c // HEAD_DIM, 1.0, 0.0).astype(BF16)


def _head_sum(x, bd):
    outs = [jnp.dot(x[:, LANES * t:LANES * (t + 1)].astype(BF16), bd, preferred_element_type=F32)
            for t in range(x.shape[1] // LANES)]
    return outs[0] if len(outs) == 1 else jnp.concatenate(outs, axis=1)


def _inproj_kernel(x_ref, w_ref, o_ref, xb_ref):
    @pl.when(pl.program_id(1) == 0)
    def _():
        xb_ref[...] = x_ref[...].astype(BF16)

    o_ref[...] = _dot_t(xb_ref[...], w_ref[...]).astype(o_ref.dtype)


def _inproj(x, w_t, tm, tn):
    m, k = x.shape
    n = PACK_WIDTH
    return pl.pallas_call(
        _inproj_kernel,
        grid=(m // tm, n // tn),
        in_specs=[pl.BlockSpec((tm, k), lambda i, j: (i, 0)),
                  pl.BlockSpec((tn, k), lambda i, j: (j, 0))],
        out_specs=pl.BlockSpec((tm, tn), lambda i, j: (i, j)),
        out_shape=jax.ShapeDtypeStruct((m, n), BF16),
        scratch_shapes=[pltpu.VMEM((tm, k), BF16)],
        compiler_params=_cparams(2),
        name="inproj",
    )(x, w_t)


def _rope_tables(pos):
    inv = ROPE_THETA ** (-jnp.arange(0, ROT_DIM, 2, dtype=F32) / ROT_DIM)
    ang = pos.astype(F32)[:, None] * inv[None, :]
    cos, sin = jnp.cos(ang), jnp.sin(ang)
    t = pos.shape[0]
    half = ROT_DIM // 2
    pad = HEAD_DIM - ROT_DIM
    c_head = jnp.concatenate([cos, cos, jnp.ones((t, pad), F32)], axis=1)
    sp_head = jnp.concatenate([-sin, jnp.zeros((t, half + pad), F32)], axis=1)
    sm_head = jnp.concatenate([jnp.zeros((t, half), F32), sin, jnp.zeros((t, pad), F32)], axis=1)
    rep = LANES // HEAD_DIM
    return tuple(jnp.tile(z, (1, rep)) for z in (c_head, sp_head, sm_head))


def _rope_tile(x, c, sp, sm):
    half = ROT_DIM // 2
    return x * c + pltpu.roll(x, LANES - half, 1) * sp + pltpu.roll(x, half, 1) * sm


def _attn_prompt_kernel(q_ref, k_ref, v_ref, c_ref, sp_ref, sm_ref, sink_ref,
                        o_ref, kwin_ref, vwin_ref, kp_ref, vp_ref):
    i = pl.program_id(1)
    blk = WINDOW

    @pl.when(i == 0)
    def _():
        kp_ref[...] = jnp.zeros_like(kp_ref)
        vp_ref[...] = jnp.zeros_like(vp_ref)

    c, sp, sm = c_ref[...], sp_ref[...], sm_ref[...]
    lane = lax.broadcasted_iota(jnp.int32, (2 * blk, LANES), 1)
    lo2 = lane < HEAD_DIM
    lo1 = lax.broadcasted_iota(jnp.int32, (blk, LANES), 1) < HEAD_DIM

    own = lax.broadcasted_iota(jnp.int32, (blk, blk), 1) <= lax.broadcasted_iota(jnp.int32, (blk, blk), 0)
    no_prev = jnp.where(i == 0, NEG_BIG, 0.0)
    zero_p = jnp.zeros((blk, blk), BF16)

    n_kv_tiles = KV_WIDTH // LANES
    kk_g, va_g, vb_g = [], [], []
    for t in range(n_kv_tiles):
        sl = slice(LANES * t, LANES * (t + 1))
        kcur = _rope_tile(k_ref[:, sl].astype(F32), c, sp, sm)
        vcur = v_ref[:, sl].astype(F32)
        kwin_ref[0, :, sl] = kcur
        vwin_ref[0, :, sl] = vcur
        kall = jnp.concatenate([kp_ref[:, sl], kcur], axis=0)
        vall = jnp.concatenate([vp_ref[:, sl], vcur], axis=0)
        kp_ref[:, sl] = kcur
        vp_ref[:, sl] = vcur
        kswap = pltpu.roll(kall, HEAD_DIM, 1)
        vswap = pltpu.roll(vall, HEAD_DIM, 1)
        kk_g += [jnp.where(lo2, kall, kswap).astype(BF16), jnp.where(lo2, kswap, kall).astype(BF16)]
        va_g += [jnp.where(lo2, vall, 0.0).astype(BF16), jnp.where(lo2, vswap, 0.0).astype(BF16)]
        vb_g += [jnp.where(lo2, 0.0, vswap).astype(BF16), jnp.where(lo2, 0.0, vall).astype(BF16)]

    group = N_Q_HEADS // N_KV_HEADS
    tiles = range(ATTN_WIDTH // LANES)
    half = ROT_DIM // 2
    pr = lax.broadcasted_iota(jnp.int32, (LANES, LANES), 0)
    pc = lax.broadcasted_iota(jnp.int32, (LANES, LANES), 1)
    dcol = pc % HEAD_DIM
    partner = jnp.where(dcol < half, pc + half, jnp.where(dcol < ROT_DIM, pc - half, -1))
    perm = jnp.where(pr == partner, 1.0, 0.0).astype(BF16)
    ssum = sp + sm
    qt = []
    for j in tiles:
        xq = q_ref[:, LANES * j:LANES * (j + 1)]
        xp = jnp.dot(xq.astype(BF16), perm, preferred_element_type=F32)
        qt.append((xq.astype(F32) * c + xp * ssum) * (HEAD_DIM ** -0.5))
    for h0 in range(0, N_Q_HEADS, HEADS_PER_GROUP):
        heads = range(h0, h0 + HEADS_PER_GROUP)
        qm = {h: (jnp.where(lo1, qt[h // 2], 0.0) if h % 2 == 0 else jnp.where(lo1, 0.0, qt[h // 2])).astype(BF16) for h in heads}
        sinks = {h: sink_ref[h] for h in heads}
        s2 = {h: _dot_t(qm[h], kk_g[h // group]) for h in heads}
        s = {h: jnp.where(own, s2[h][:, blk:], s2[h][:, :blk] + no_prev) for h in heads}
        m = {h: jnp.maximum(jnp.max(s[h], axis=1, keepdims=True), sinks[h]) for h in heads}
        p = {h: jnp.exp(s[h] - m[h]) for h in heads}
        den = {h: jnp.sum(p[h], axis=1, keepdims=True) + jnp.exp(sinks[h] - m[h]) for h in heads}
        pb = {h: p[h].astype(BF16) for h in heads}
        p2 = {h: jnp.concatenate([jnp.where(own, zero_p, pb[h]), jnp.where(own, pb[h], zero_p)], axis=1) for h in heads}
        o = {h: jnp.dot(p2[h], (va_g if h % 2 == 0 else vb_g)[h // group], preferred_element_type=F32) / den[h]
             for h in heads}
        for j in range(h0 // 2, (h0 + HEADS_PER_GROUP) // 2):
            o_ref[:, LANES * j:LANES * (j + 1)] = (o[2 * j] + o[2 * j + 1]).astype(o_ref.dtype)


def _attn_prompt(p_all, sinks, b, t):
    blk = WINDOW
    nb = t // blk
    c, sp, sm = _rope_tables(jnp.arange(t, dtype=jnp.int32))
    tab_spec = pl.BlockSpec((blk, LANES), lambda bi, i: (i, 0))
    row = lambda bi, i: bi * nb + i
    return pl.pallas_call(
        _attn_prompt_kernel,
        grid=(b, nb),
        in_specs=[pl.BlockSpec((blk, ATTN_WIDTH), lambda bi, i: (row(bi, i), COL_Q // ATTN_WIDTH)),
                  pl.BlockSpec((blk, KV_WIDTH), lambda bi, i: (row(bi, i), COL_K // KV_WIDTH)),
                  pl.BlockSpec((blk, KV_WIDTH), lambda bi, i: (row(bi, i), COL_V // KV_WIDTH)),
                  tab_spec, tab_spec, tab_spec,
                  pl.BlockSpec(memory_space=pltpu.SMEM)],
        out_specs=[pl.BlockSpec((blk, ATTN_WIDTH), lambda bi, i: (row(bi, i), 0)),
                   pl.BlockSpec((1, blk, KV_WIDTH), lambda bi, i: (bi, 0, 0)),
                   pl.BlockSpec((1, blk, KV_WIDTH), lambda bi, i: (bi, 0, 0))],
        out_shape=[jax.ShapeDtypeStruct((b * t, ATTN_WIDTH), BF16),
                   jax.ShapeDtypeStruct((b, blk, KV_WIDTH), F32),
                   jax.ShapeDtypeStruct((b, blk, KV_WIDTH), F32)],
        scratch_shapes=[pltpu.VMEM((blk, KV_WIDTH), F32), pltpu.VMEM((blk, KV_WIDTH), F32)],
        compiler_params=_cparams(2),
        name="attn_prompt",
    )(p_all, p_all, p_all, c, sp, sm, sinks)


def _rwkv_prepare(r_in, k_in, v_in, l_in, pr, pk, pv, plr, prm, wd_ref, wi_ref, wg_ref, bd2):
    mur, muk, muv, mul, w0, a0, k_k, k_a = prm
    r = r_in + (pr - r_in) * mur
    k = k_in + (pk - k_in) * muk
    v = v_in + (pv - v_in) * muv
    ls = l_in + (plr - l_in) * mul
    l0 = ls[:, :LANES]
    lane = lax.broadcasted_iota(jnp.int32, l0.shape, 1)
    z0 = jnp.where(lane < DECAY_LORA, jnp.tanh(l0), l0).astype(BF16)
    zg = _sigmoid(ls[:, LANES:3 * LANES]).astype(BF16)
    dec_up = jnp.dot(z0, wd_ref[...], preferred_element_type=F32)
    icl_up = jnp.dot(z0, wi_ref[...], preferred_element_type=F32)
    gate = jnp.dot(zg, wg_ref[...], preferred_element_type=F32)
    logw = -DECAY_SCALE * _sigmoid(w0 + dec_up)
    a_sig = _sigmoid(a0 + icl_up)
    kk = k * k_k
    kk = kk * lax.rsqrt(jnp.maximum(_head_sum(kk * kk, bd2), NORM_FLOOR_SQ))
    k = k * (1.0 + (a_sig - 1.0) * k_a)
    return r, k, v, logw, -kk, kk * a_sig, gate


def _rwkv_finish(y, r, k, v, gate, r_k, gn_w, gn_b, bd2):
    mu = _head_sum(y, bd2) * (1.0 / HEAD_DIM)
    d = y - mu
    var = _head_sum(d * d, bd2) * (1.0 / HEAD_DIM)
    yn = d * lax.rsqrt(var + GN_EPS) * gn_w + gn_b
    bonus = _head_sum(r * k * r_k, bd2) * v
    return (yn + bonus) * gate


def _rwkv_prompt_kernel(r_ref, k_ref, v_ref, l_ref,
                        mur_ref, muk_ref, muv_ref, mul_ref, w0_ref, a0_ref, kk_ref, ka_ref,
                        rk_ref, gnw_ref, gnb_ref, wd_ref, wi_ref, wg_ref,
                        o_ref, sout_ref,
                        cr_ref, ck_ref, cv_ref, cl_ref, st_ref, y_ref):
    ti = pl.program_id(2)
    tb, lw = r_ref.shape
    n_pair = lw // LANES
    n_chunk = tb // CHUNK

    @pl.when(ti == 0)
    def _():
        for ref in (cr_ref, ck_ref, cv_ref, cl_ref, st_ref):
            ref[...] = jnp.zeros_like(ref)

    def shifted(x, carry_ref):
        rolled = pltpu.roll(x, 1, 0)
        row = lax.broadcasted_iota(jnp.int32, x.shape, 0)
        prev = jnp.where(row == 0, carry_ref[0:1, :], rolled)
        carry_ref[0:1, :] = x[tb - 1:tb, :]
        return prev

    bd2 = _head_block_ones()
    r_in, k_in, v_in, l_in = (ref[...].astype(F32) for ref in (r_ref, k_ref, v_ref, l_ref))
    prm = tuple(ref[...] for ref in (mur_ref, muk_ref, muv_ref, mul_ref, w0_ref, a0_ref, kk_ref, ka_ref))
    r, k, v, logw, a_s, b_s, gate = _rwkv_prepare(
        r_in, k_in, v_in, l_in,
        shifted(r_in, cr_ref), shifted(k_in, ck_ref), shifted(v_in, cv_ref), shifted(l_in, cl_ref),
        prm, wd_ref, wi_ref, wg_ref, bd2)

    c = CHUNK
    ri = lax.broadcasted_iota(jnp.int32, (c, 3 * c), 0)
    ci = lax.broadcasted_iota(jnp.int32, (c, 3 * c), 1) % c
    tri3 = jnp.where(ri >= ci, 1.0, 0.0).astype(BF16)
    r2 = lax.broadcasted_iota(jnp.int32, (2 * c, LANES), 0)
    l2 = lax.broadcasted_iota(jnp.int32, (2 * c, LANES), 1)
    tt, ss = r2 % c, l2 % c
    causal = ss < tt + r2 // c
    lo1 = lax.broadcasted_iota(jnp.int32, (c, LANES), 1) < HEAD_DIM
    diag_blocks = (r2 // HEAD_DIM) == (l2 // HEAD_DIM)
    eye_side = jnp.where(lax.broadcasted_iota(jnp.int32, (c, LANES), 0) == lax.broadcasted_iota(jnp.int32, (c, LANES), 1) % c,
                         1.0, 0.0)

    def stack_heads(x):
        return jnp.concatenate([jnp.where(lo1, x, 0.0), jnp.where(lo1, 0.0, x)], axis=0)

    def dot_t(a, b):
        return lax.dot_general(a, b, (((1,), (1,)), ((), ())), preferred_element_type=F32)

    def dot_tt(a, b):
        return lax.dot_general(a, b, (((0,), (0,)), ((), ())), preferred_element_type=F32)

    def dot(a, b):
        return jnp.dot(a.astype(BF16), b.astype(BF16), preferred_element_type=F32)

    lanes = [slice(LANES * pi, LANES * (pi + 1)) for pi in range(n_pair)]
    units = [(ch, pi) for ch in range(n_chunk) for pi in range(n_pair)]
    un = range(len(units))
    el = []
    for ch in range(n_chunk):
        rows = slice(c * ch, c * (ch + 1))
        lw_c = logw[rows]
        hi = lw_c.astype(BF16)
        rem = lw_c - hi.astype(F32)
        mid = rem.astype(BF16)
        low = (rem - mid.astype(F32)).astype(BF16)
        lcum = jnp.dot(tri3, jnp.concatenate([hi, mid, low], axis=0), preferred_element_type=F32)
        ltot = lcum[c - 1:c, :]
        p_inv = jnp.exp(-lcum)
        p_tail = jnp.exp(ltot - lcum)
        el.append(dict(aq=a_s[rows] * jnp.exp(lcum - lw_c), rq=r[rows] * jnp.exp(lcum), bk=b_s[rows] * p_inv,
                       kq=k[rows] * p_inv, bt=b_s[rows] * p_tail, kt=k[rows] * p_tail, v=v[rows],
                       p_end=jnp.exp(ltot)))
    op = lambda name, u: el[units[u][0]][name][:, lanes[units[u][1]]]

    ar = [jnp.concatenate([op("aq", u), op("rq", u)], axis=0).astype(BF16) for u in un]
    gb = [jnp.where(causal, dot_t(ar[u], stack_heads(op("bk", u)).astype(BF16)), 0.0) for u in un]
    gk = [jnp.where(causal, dot_t(ar[u], stack_heads(op("kq", u)).astype(BF16)), 0.0) for u in un]
    gv = [dot(gk[u], stack_heads(op("v", u))) for u in un]
    pw = [gb[u][:c] for u in un]
    tm = [eye_side + pw[u] for u in un]
    for level in range(1, 6):
        bd = [stack_heads(pw[u]).astype(BF16) for u in un]
        if level == 1:
            pw = [jnp.dot(pw[u].astype(BF16), bd[u], preferred_element_type=F32) for u in un]
            bd = [stack_heads(pw[u]).astype(BF16) for u in un]
        if level < 5:
            both = [jnp.dot(jnp.concatenate([tm[u], pw[u]], axis=0).astype(BF16), bd[u], preferred_element_type=F32)
                    for u in un]
            tm = [tm[u] + both[u][:c] for u in un]
            pw = [both[u][c:] for u in un]
        else:
            tm = [tm[u] + jnp.dot(tm[u].astype(BF16), bd[u], preferred_element_type=F32) for u in un]
    tax = [dot(tm[u], jnp.concatenate([stack_heads(op("aq", u)), stack_heads(gv[u][:c])], axis=1)) for u in un]
    taq = [tax[u][:, :LANES] for u in un]
    txv = [tax[u][:, LANES:] for u in un]
    arx = [dot(gb[u][c:], jnp.concatenate([stack_heads(taq[u]), stack_heads(txv[u])], axis=1)) for u in un]
    mb = [jnp.where(diag_blocks, dot_tt(op("bt", u).astype(BF16), taq[u].astype(BF16)), 0.0).astype(BF16) for u in un]
    cct = [jnp.where(diag_blocks,
                     dot_tt(jnp.concatenate([txv[u], op("v", u)], axis=0).astype(BF16),
                            jnp.concatenate([op("bt", u), op("kt", u)], axis=0).astype(BF16)), 0.0) for u in un]
    rqp = [(op("rq", u) + arx[u][:, :LANES]).astype(BF16) for u in un]
    yc = [gv[u][c:] + arx[u][:, LANES:] for u in un]
    for u, (ch, pi) in enumerate(units):
        s_old = st_ref[pi]
        sb = s_old.astype(BF16)
        y_ref[c * ch:c * (ch + 1), lanes[pi]] = dot_t(rqp[u], sb) + yc[u]
        st_ref[pi] = s_old * op("p_end", u) + dot_t(sb, mb[u]) + cct[u]

    out = _rwkv_finish(y_ref[...], r, k, v, gate, rk_ref[...], gnw_ref[...], gnb_ref[...], bd2)
    o_ref[...] = out.astype(o_ref.dtype)

    @pl.when(ti == pl.num_programs(2) - 1)
    def _():
        for pi in range(n_pair):
            s = st_ref[pi]
            sout_ref[0, 2 * pi] = s[:HEAD_DIM, :HEAD_DIM]
            sout_ref[0, 2 * pi + 1] = s[HEAD_DIM:, HEAD_DIM:]


def _rwkv_lora_weights(w_decay_up, w_iclr_up, w_gate_up):
    z64 = jnp.zeros((DECAY_LORA, RWKV_WIDTH), F32)
    wd = jnp.concatenate([w_decay_up, z64], axis=0).astype(BF16)
    wi = jnp.concatenate([z64, w_iclr_up], axis=0).astype(BF16)
    wg = jnp.concatenate([w_gate_up, jnp.zeros((2 * LANES - GATE_LORA, RWKV_WIDTH), F32)], axis=0).astype(BF16)
    return wd, wi, wg


def _rwkv_prompt(p_all, prm, b, t, tb, lw):
    nt = t // tb
    ns = RWKV_WIDTH // lw
    row = lambda bi, si, ti: bi * nt + ti
    col_spec = lambda col0: pl.BlockSpec(
        (pl.Element(tb), pl.Element(lw)),
        lambda bi, si, ti: (pl.multiple_of(row(bi, si, ti) * tb, tb), pl.multiple_of(col0 + si * lw, LANES)))
    vec = pl.BlockSpec((1, lw), lambda bi, si, ti: (0, si))
    vec_l = pl.BlockSpec((1, LORA_PAD), lambda bi, si, ti: (0, 0))
    return pl.pallas_call(
        _rwkv_prompt_kernel,
        grid=(b, ns, nt),
        in_specs=[col_spec(COL_R), col_spec(COL_KR), col_spec(COL_VR),
                  pl.BlockSpec((tb, LORA_PAD), lambda bi, si, ti: (row(bi, si, ti), COL_LORA // LORA_PAD)),
                  vec, vec, vec, vec_l, vec, vec, vec, vec, vec, vec, vec,
                  pl.BlockSpec((LANES, lw), lambda bi, si, ti: (0, si)),
                  pl.BlockSpec((LANES, lw), lambda bi, si, ti: (0, si)),
                  pl.BlockSpec((2 * LANES, lw), lambda bi, si, ti: (0, si))],
        out_specs=[pl.BlockSpec((tb, lw), lambda bi, si, ti: (row(bi, si, ti), si)),
                   pl.BlockSpec((1, 2 * (lw // LANES), HEAD_DIM, HEAD_DIM), lambda bi, si, ti: (bi, si, 0, 0))],
        out_shape=[jax.ShapeDtypeStruct((b * t, RWKV_WIDTH), BF16),
                   jax.ShapeDtypeStruct((b, N_RWKV_HEADS, HEAD_DIM, HEAD_DIM), F32)],
        scratch_shapes=[pltpu.VMEM((8, lw), F32), pltpu.VMEM((8, lw), F32), pltpu.VMEM((8, lw), F32),
                        pltpu.VMEM((8, LORA_PAD), F32),
                        pltpu.VMEM((lw // LANES, LANES, LANES), F32),
                        pltpu.VMEM((tb, lw), F32)],
        compiler_params=_cparams(3),
        name="rwkv_prompt",
    )(p_all, p_all, p_all, p_all,
      prm["mu_r"], prm["mu_k"], prm["mu_v"], prm["mu_l"], prm["w0"], prm["a0"], prm["k_k"], prm["k_a"],
      prm["r_k"], prm["gn_w"], prm["gn_b"], prm["wd"], prm["wi"], prm["wg"])


def _merge_kernel(x_ref, ao_ref, ro_ref, wga_ref, wgb_ref, wpa_ref, wpr_ref, wo_ref, g_ref, b_ref, h_ref,
                  xb_ref, acc_ref):
    j = pl.program_id(1)

    @pl.when(j == 0)
    def _():
        xb_ref[...] = x_ref[...].astype(BF16)
        acc_ref[...] = jnp.zeros_like(acc_ref)

    xb = xb_ref[...]
    ga = _dot_t(xb, wga_ref[...])
    gb = _dot_t(xb, wgb_ref[...])
    a = jnp.dot(ao_ref[...], wpa_ref[...], preferred_element_type=F32)
    r = jnp.dot(ro_ref[...], wpr_ref[...], preferred_element_type=F32)
    m = _sigmoid(ga) * a + _sigmoid(gb) * r
    acc_ref[...] += jnp.dot(m.astype(BF16), wo_ref[...], preferred_element_type=F32)

    @pl.when(j == pl.num_programs(1) - 1)
    def _():
        z = DEEPNORM_ALPHA * x_ref[...] + acc_ref[...]
        h_ref[...] = _layer_norm_rows(z, g_ref[...], b_ref[...])


def _merge(x, attn_o, rwkv_o, w_in_t, wpa, wpr, wo, ln_g, ln_b, tm, tj):
    m = x.shape[0]
    nj = D_MODEL // tj
    gate_spec = lambda row0: pl.BlockSpec(
        (pl.Element(tj), pl.Element(D_MODEL)),
        lambda i, j: (pl.multiple_of(row0 + j * tj, BF16_ROWS_PER_TILE), 0))
    return pl.pallas_call(
        _merge_kernel,
        grid=(m // tm, nj),
        in_specs=[pl.BlockSpec((tm, D_MODEL), lambda i, j: (i, 0)),
                  pl.BlockSpec((tm, ATTN_WIDTH), lambda i, j: (i, 0)),
                  pl.BlockSpec((tm, RWKV_WIDTH), lambda i, j: (i, 0)),
                  gate_spec(GATE_ROW0), gate_spec(GATE_ROW0 + D_MODEL),
                  pl.BlockSpec((ATTN_WIDTH, tj), lambda i, j: (0, j)),
                  pl.BlockSpec((RWKV_WIDTH, tj), lambda i, j: (0, j)),
                  pl.BlockSpec((tj, D_MODEL), lambda i, j: (j, 0)),
                  pl.BlockSpec((1, D_MODEL), lambda i, j: (0, 0)),
                  pl.BlockSpec((1, D_MODEL), lambda i, j: (0, 0))],
        out_specs=pl.BlockSpec((tm, D_MODEL), lambda i, j: (i, 0)),
        out_shape=jax.ShapeDtypeStruct((m, D_MODEL), F32),
        scratch_shapes=[pltpu.VMEM((tm, D_MODEL), BF16), pltpu.VMEM((tm, D_MODEL), F32)],
        compiler_params=_cparams(2),
        name="merge_ln1",
    )(x, attn_o, rwkv_o, w_in_t, w_in_t, wpa, wpr, wo, ln_g, ln_b)


def _ffn_kernel(h_ref, wu_ref, wd_ref, g_ref, b_ref, y_ref, hb_ref, acc_ref):
    f = pl.program_id(1)

    @pl.when(f == 0)
    def _():
        hb_ref[...] = h_ref[...].astype(BF16)
        acc_ref[...] = jnp.zeros_like(acc_ref)

    u = jnp.dot(hb_ref[...], wu_ref[...], preferred_element_type=F32)
    u = jnp.square(jnp.maximum(u, 0.0))
    acc_ref[...] += jnp.dot(u.astype(BF16), wd_ref[...], preferred_element_type=F32)

    @pl.when(f == pl.num_programs(1) - 1)
    def _():
        z = DEEPNORM_ALPHA * h_ref[...] + acc_ref[...]
        y_ref[...] = _layer_norm_rows(z, g_ref[...], b_ref[...])


def _ffn(h, wu, wd, ln_g, ln_b, tm, tf):
    m = h.shape[0]
    return pl.pallas_call(
        _ffn_kernel,
        grid=(m // tm, D_FF // tf),
        in_specs=[pl.BlockSpec((tm, D_MODEL), lambda i, f: (i, 0)),
                  pl.BlockSpec((D_MODEL, tf), lambda i, f: (0, f)),
                  pl.BlockSpec((tf, D_MODEL), lambda i, f: (f, 0)),
                  pl.BlockSpec((1, D_MODEL), lambda i, f: (0, 0)),
                  pl.BlockSpec((1, D_MODEL), lambda i, f: (0, 0))],
        out_specs=pl.BlockSpec((tm, D_MODEL), lambda i, f: (i, 0)),
        out_shape=jax.ShapeDtypeStruct((m, D_MODEL), F32),
        scratch_shapes=[pltpu.VMEM((tm, D_MODEL), BF16), pltpu.VMEM((tm, D_MODEL), F32)],
        compiler_params=_cparams(2),
        name="ffn_ln2",
    )(h, wu, wd, ln_g, ln_b)


def _attn_sample_kernel(q_ref, kvt_ref, ck_ref, cv_ref, c_ref, sp_ref, sm_ref, cc_ref, spc_ref, smc_ref, sink_ref,
                        o_ref, nk_ref, nv_ref):
    bt = q_ref.shape[0]
    win = ck_ref.shape[2]
    n = kvt_ref.shape[1]
    half = ROT_DIM // 2
    group = N_Q_HEADS // N_KV_HEADS
    c, sp, sm = c_ref[0:1, :], sp_ref[0:1, :], sm_ref[0:1, :]
    sink = sink_ref[:, 0:1]
    row16 = lax.broadcasted_iota(jnp.int32, (N_Q_HEADS, KV_WIDTH), 0)
    lane16 = lax.broadcasted_iota(jnp.int32, (N_Q_HEADS, KV_WIDTH), 1)
    own_kv = (lane16 // HEAD_DIM) == (row16 % N_KV_HEADS)
    urow = row16 // N_KV_HEADS
    pos = lax.broadcasted_iota(jnp.int32, (KV_WIDTH, win), 1)
    seq = lax.broadcasted_iota(jnp.int32, (KV_WIDTH, n), 1)

    kt = kvt_ref[0:KV_WIDTH, :]
    kt = kt * cc_ref[...] + pltpu.roll(kt, KV_WIDTH - half, 0) * spc_ref[...] + pltpu.roll(kt, half, 0) * smc_ref[...]
    vt = kvt_ref[KV_WIDTH:2 * KV_WIDTH, :]

    def rope_row(x):
        return jnp.concatenate([_rope_tile(x[:, LANES * t:LANES * (t + 1)], c, sp, sm)
                                for t in range(x.shape[1] // LANES)], axis=1)

    rng = range(bt)
    me = [seq == pl.program_id(0) * bt + b for b in rng]
    k_col = [jnp.sum(jnp.where(me[b], kt, 0.0), axis=1, keepdims=True) for b in rng]
    v_col = [jnp.sum(jnp.where(me[b], vt, 0.0), axis=1, keepdims=True) for b in rng]
    nk = [jnp.where(pos == win - 1, k_col[b], pltpu.roll(ck_ref[b], win - 1, 1)) for b in rng]
    nv = [jnp.where(pos == win - 1, v_col[b], pltpu.roll(cv_ref[b], win - 1, 1)) for b in rng]
    for b in rng:
        nk_ref[b] = nk[b]
        nv_ref[b] = nv[b]
    qmat = []
    for b in rng:
        q = rope_row(q_ref[b:b + 1, :]) * (HEAD_DIM ** -0.5)
        qb = [jnp.broadcast_to(q[:, KV_WIDTH * u:KV_WIDTH * (u + 1)], (N_Q_HEADS, KV_WIDTH)) for u in range(group)]
        qsel = jnp.where(urow == 0, qb[0], jnp.where(urow == 1, qb[1], jnp.where(urow == 2, qb[2], qb[3])))
        qmat.append(jnp.where(own_kv, qsel, 0.0).astype(BF16))
    s = [jnp.dot(qmat[b], nk[b].astype(BF16), preferred_element_type=F32) for b in rng]
    m = [jnp.maximum(jnp.max(s[b], axis=1, keepdims=True), sink) for b in rng]
    p = [jnp.exp(s[b] - m[b]) for b in rng]
    den = [jnp.sum(p[b], axis=1, keepdims=True) + jnp.exp(sink - m[b]) for b in rng]
    o = [_dot_t(p[b].astype(BF16), nv[b].astype(BF16)) / den[b] for b in rng]
    out_rows = []
    for b in rng:
        ob = jnp.where(own_kv, o[b], 0.0)
        chunks = [jnp.sum(jnp.where(urow == u, ob, 0.0), axis=0, keepdims=True) for u in range(group)]
        out_rows.append(jnp.concatenate(chunks, axis=1))
    o_ref[...] = jnp.concatenate(out_rows, axis=0).astype(o_ref.dtype)


def _attn_sample(q_perm, kv_new_t, cache_kt, cache_vt, sink_mat, bt):
    n, win = cache_kt.shape[0], cache_kt.shape[2]
    tabs = _rope_tables(jnp.full((1,), PAST_LEN, jnp.int32))
    c, sp, sm = (jnp.broadcast_to(z, (8, LANES)) for z in tabs)
    cc, spc, smc = (jnp.broadcast_to(jnp.tile(z, (1, KV_WIDTH // LANES)).T, (KV_WIDTH, n)) for z in tabs)
    small = lambda shape: pl.BlockSpec(shape, lambda i: (0, 0))
    cache_spec = pl.BlockSpec((bt, KV_WIDTH, win), lambda i: (i, 0, 0))
    return pl.pallas_call(
        _attn_sample_kernel,
        grid=(n // bt,),
        in_specs=[pl.BlockSpec((bt, ATTN_WIDTH), lambda i: (i, 0)),
                  small((2 * KV_WIDTH, n)),
                  cache_spec, cache_spec,
                  small((8, LANES)), small((8, LANES)), small((8, LANES)),
                  small((KV_WIDTH, n)), small((KV_WIDTH, n)), small((KV_WIDTH, n)),
                  small((N_Q_HEADS, LANES))],
        out_specs=[pl.BlockSpec((bt, ATTN_WIDTH), lambda i: (i, 0)), cache_spec, cache_spec],
        out_shape=[jax.ShapeDtypeStruct((n, ATTN_WIDTH), BF16),
                   jax.ShapeDtypeStruct(cache_kt.shape, F32),
                   jax.ShapeDtypeStruct(cache_vt.shape, F32)],
        compiler_params=_cparams(1),
        name="attn_sample",
    )(q_perm, kv_new_t, cache_kt, cache_vt, c, sp, sm, cc, spc, smc, sink_mat)


def _rwkv_sample_kernel(r_ref, k_ref, v_ref, l_ref, pr_ref, pk_ref, pv_ref, pl_ref, st_ref,
                        mur_ref, muk_ref, muv_ref, mul_ref, w0_ref, a0_ref, kk_ref, ka_ref,
                        rk_ref, gnw_ref, gnb_ref, wd_ref, wi_ref, wg_ref,
                        o_ref, ns_ref, vec_s, keep_s, y_s):
    h = pl.program_id(0)
    hd = HEAD_DIM
    q_a, q_w, q_b, q_k, q_r, q_v = range(6)

    @pl.when(h == 0)
    def _():
        bd2 = _head_block_ones()
        prm = tuple(ref[...] for ref in (mur_ref, muk_ref, muv_ref, mul_ref, w0_ref, a0_ref, kk_ref, ka_ref))
        r, k, v, logw, a_s, b_s, gate = _rwkv_prepare(
            r_ref[...].astype(F32), k_ref[...].astype(F32), v_ref[...].astype(F32), l_ref[...].astype(F32),
            pr_ref[...], pk_ref[...], pv_ref[...], pl_ref[...],
            prm, wd_ref, wi_ref, wg_ref, bd2)
        for qi, x in enumerate((a_s, jnp.exp(logw), b_s, k, r, v)):
            xt = x.T
            for hh in range(N_RWKV_HEADS):
                vec_s[qi, hh] = xt[hd * hh:hd * (hh + 1), :]
        for qi, x in enumerate((r, k, v, gate)):
            keep_s[qi] = x

    a_h, w_h, b_h, k_h, r_h = (vec_s[qi, h] for qi in (q_a, q_w, q_b, q_k, q_r))
    for i in range(hd):
        s = st_ref[0, i]
        sa = jnp.sum(s * a_h, axis=0, keepdims=True)
        s_new = s * w_h + sa * b_h + vec_s[q_v, h, i:i + 1, :] * k_h
        ns_ref[0, i] = s_new
        y_s[h, i:i + 1, :] = jnp.sum(s_new * r_h, axis=0, keepdims=True)

    @pl.when(h == pl.num_programs(0) - 1)
    def _():
        y = jnp.concatenate([y_s[hh] for hh in range(N_RWKV_HEADS)], axis=0).T
        out = _rwkv_finish(y, keep_s[0], keep_s[1], keep_s[2], keep_s[3], rk_ref[...], gnw_ref[...], gnb_ref[...],
                           _head_block_ones())
        o_ref[...] = out.astype(o_ref.dtype)


def _rwkv_sample(p_all, shift, shift_l, state_t, prm):
    n = state_t.shape[-1]
    wide = lambda col0: pl.BlockSpec((n, RWKV_WIDTH), lambda h: (0, col0 // RWKV_WIDTH))
    proj = lambda col0: pl.BlockSpec((pl.Element(n), pl.Element(RWKV_WIDTH)), lambda h: (0, pl.multiple_of(col0, LANES)))
    vec = pl.BlockSpec((1, RWKV_WIDTH), lambda h: (0, 0))
    vec_l = pl.BlockSpec((1, LORA_PAD), lambda h: (0, 0))
    st_spec = pl.BlockSpec((1, HEAD_DIM, HEAD_DIM, n), lambda h: (h, 0, 0, 0))
    return pl.pallas_call(
        _rwkv_sample_kernel,
        grid=(N_RWKV_HEADS,),
        in_specs=[proj(COL_R), proj(COL_KR), proj(COL_VR),
                  pl.BlockSpec((n, LORA_PAD), lambda h: (0, COL_LORA // LORA_PAD)),
                  wide(0), wide(RWKV_WIDTH), wide(2 * RWKV_WIDTH),
                  pl.BlockSpec((n, LORA_PAD), lambda h: (0, 0)),
                  st_spec,
                  vec, vec, vec, vec_l, vec, vec, vec, vec, vec, vec, vec,
                  pl.BlockSpec((LANES, RWKV_WIDTH), lambda h: (0, 0)),
                  pl.BlockSpec((LANES, RWKV_WIDTH), lambda h: (0, 0)),
                  pl.BlockSpec((2 * LANES, RWKV_WIDTH), lambda h: (0, 0))],
        out_specs=[pl.BlockSpec((n, RWKV_WIDTH), lambda h: (0, 0)), st_spec],
        out_shape=[jax.ShapeDtypeStruct((n, RWKV_WIDTH), BF16), jax.ShapeDtypeStruct(state_t.shape, F32)],
        scratch_shapes=[pltpu.VMEM((6, N_RWKV_HEADS, HEAD_DIM, n), F32),
                        pltpu.VMEM((4, n, RWKV_WIDTH), F32),
                        pltpu.VMEM((N_RWKV_HEADS, HEAD_DIM, n), F32)],
        compiler_params=_cparams(1),
        name="rwkv_sample",
    )(p_all, p_all, p_all, p_all, shift, shift, shift, shift_l, state_t,
      prm["mu_r"], prm["mu_k"], prm["mu_v"], prm["mu_l"], prm["w0"], prm["a0"], prm["k_k"], prm["k_a"],
      prm["r_k"], prm["gn_w"], prm["gn_b"], prm["wd"], prm["wi"], prm["wg"])


def _swap_head_order(z, outer, inner):
    n = z.shape[0]
    return z.reshape(n, outer, inner, HEAD_DIM).transpose(0, 2, 1, 3).reshape(n, outer * inner * HEAD_DIM)


def _shift_columns(p_rows):
    return p_rows[:, COL_R:COL_R + RWKV_PROJ_WIDTH].astype(F32)


def _forward(x_prompt, x_sample, cache_k_win, cache_v_win, state_shift, state_wkv, w, cfg):
    b, t, _ = x_prompt.shape
    n_s = x_sample.shape[0]
    row = lambda z: z.reshape(1, -1).astype(F32)
    mu = w["mu_shift"]
    wd, wi, wg = _rwkv_lora_weights(w["w_decay_up"], w["w_iclr_up"], w["w_gate_up"])
    prm = dict(
        mu_r=row(mu[:RWKV_WIDTH]), mu_k=row(mu[RWKV_WIDTH:2 * RWKV_WIDTH]), mu_v=row(mu[2 * RWKV_WIDTH:3 * RWKV_WIDTH]),
        mu_l=row(jnp.pad(mu[3 * RWKV_WIDTH:], (0, LORA_PAD - LORA_WIDTH))),
        w0=row(w["w0"]), a0=row(w["a0"]), k_k=row(w["k_k"]), k_a=row(w["k_a"]), r_k=row(w["r_k"]),
        gn_w=row(w["gn_w"]), gn_b=row(w["gn_b"]), wd=wd, wi=wi, wg=wg)
    wpa = w["w_proj_attn"].astype(BF16)
    wpr = w["w_proj_rwkv"].astype(BF16)
    wo = w["w_out"].astype(BF16)
    wu = w["w_up"].astype(BF16)
    wdn = w["w_down"].astype(BF16)
    ln1g, ln1b, ln2g, ln2b = row(w["ln1_g"]), row(w["ln1_b"]), row(w["ln2_g"]), row(w["ln2_b"])
    sinks = w["attn_sinks"].astype(F32)

    xp = x_prompt.reshape(b * t, D_MODEL)
    w_in_t = w["w_in"].T.astype(BF16)
    pp = _inproj(xp, w_in_t, cfg["tm_in"], cfg["tn_in"])
    attn_p, kwin_p, vwin_p = _attn_prompt(pp, sinks, b, t)
    rwkv_p, wkv_p = _rwkv_prompt(pp, prm, b, t, cfg["tb_rwkv"], cfg["lw_rwkv"])
    hp = _merge(xp, attn_p, rwkv_p, w_in_t, wpa, wpr, wo, ln1g, ln1b, cfg["tm_merge"], cfg["tj_merge"])
    yp = _ffn(hp, wu, wdn, ln2g, ln2b, cfg["tm_ffn"], cfg["tf_ffn"])
    shift_p = _shift_columns(pp.reshape(b, t, PACK_WIDTH)[:, t - 1])

    group = N_Q_HEADS // N_KV_HEADS
    xs = x_sample.reshape(n_s, D_MODEL)
    ps = _inproj(xs, w_in_t, n_s, cfg["tn_in"])
    q_perm = _swap_head_order(ps[:, COL_Q:COL_Q + ATTN_WIDTH].astype(F32), N_KV_HEADS, group)
    sink_mat = jnp.broadcast_to(sinks.reshape(N_KV_HEADS, group).T.reshape(N_Q_HEADS, 1), (N_Q_HEADS, LANES))
    win = cache_k_win.shape[1]
    to_t = lambda z: jnp.transpose(z, (0, 2, 3, 1)).reshape(n_s, KV_WIDTH, win)
    from_t = lambda z: jnp.transpose(z.reshape(n_s, N_KV_HEADS, HEAD_DIM, win), (0, 3, 1, 2))[None]
    attn_s, nk_t, nv_t = _attn_sample(q_perm, ps[:, COL_K:COL_K + 2 * KV_WIDTH].T.astype(F32), to_t(cache_k_win), to_t(cache_v_win),
                                      sink_mat, cfg["bt_sample"])
    attn_s = _swap_head_order(attn_s, group, N_KV_HEADS)
    shift_l = jnp.pad(state_shift[:, 3 * RWKV_WIDTH:], ((0, 0), (0, LORA_PAD - LORA_WIDTH)))
    rwkv_s, wkv_t = _rwkv_sample(ps, state_shift, shift_l, jnp.transpose(state_wkv, (1, 2, 3, 0)), prm)
    wkv_s = jnp.transpose(wkv_t, (3, 0, 1, 2))
    hs = _merge(xs, attn_s, rwkv_s, w_in_t, wpa, wpr, wo, ln1g, ln1b, n_s, cfg["tj_merge"])
    ys = _ffn(hs, wu, wdn, ln2g, ln2b, n_s, cfg["tf_ffn"])
    shift_s = _shift_columns(ps)

    kv5 = lambda z: z.reshape(1, z.shape[0], z.shape[1], N_KV_HEADS, HEAD_DIM)
    return (yp.reshape(b, t, D_MODEL), ys.reshape(n_s, 1, D_MODEL),
            kv5(kwin_p), kv5(vwin_p), shift_p[None], wkv_p[None],
            from_t(nk_t), from_t(nv_t), shift_s[None], wkv_s[None])


_CFG = dict(tm_in=1024, tn_in=1280, tb_rwkv=256, lw_rwkv=1024, tm_merge=512, tj_merge=512, tm_ffn=512, tf_ffn=1024, bt_sample=16)


def kernel(x_prompt, x_sample, cache_k_win, cache_v_win, state_shift, state_wkv, w_in, attn_sinks, mu_shift, w0,
           w_decay_up, a0, w_iclr_up, w_gate_up, k_k, k_a, r_k, gn_w, gn_b, w_proj_attn, w_proj_rwkv, w_out,
           ln1_g, ln1_b, w_up, w_down, ln2_g, ln2_b):
    w = dict(w_in=w_in[0], attn_sinks=attn_sinks[0], mu_shift=mu_shift[0], w0=w0[0], w_decay_up=w_decay_up[0],
             a0=a0[0], w_iclr_up=w_iclr_up[0], w_gate_up=w_gate_up[0], k_k=k_k[0], k_a=k_a[0], r_k=r_k[0],
             gn_w=gn_w[0], gn_b=gn_b[0], w_proj_attn=w_proj_attn[0], w_proj_rwkv=w_proj_rwkv[0], w_out=w_out[0],
             ln1_g=ln1_g[0], ln1_b=ln1_b[0], w_up=w_up[0], w_down=w_down[0], ln2_g=ln2_g[0], ln2_b=ln2_b[0])
    return _forward(x_prompt, x_sample, cache_k_win[0], cache_v_win[0], state_shift[0], state_wkv[0], w, _CFG)
```
